```python
import math
import jax, jax.numpy as jnp
from jax import lax
import numpy as np

D_MODEL = 1024
BATCH = 8
SEQ = 4096
DEPTH = 1

CHUNK = 64
MEM_LEN = 256
CONV_WIDTH = D_MODEL // 2
CONV_KERNEL = 31
SSM_WIDTH = D_MODEL // 2
SSM_GROUP = 16
SSM_GROUPS = SSM_WIDTH // SSM_GROUP
SSM_STATE = 64
XATTN_HEADS = 4
XATTN_HEAD_DIM = 128
XATTN_WIDTH = XATTN_HEADS * XATTN_HEAD_DIM
N_BRANCH = 3
MOE_GROUPS = 4
EXPERTS_PER_GROUP = 8
N_EXPERTS = MOE_GROUPS * EXPERTS_PER_GROUP
TOP_K = 2
D_EXPERT = D_MODEL // 4
EPS = 1e-6
IN_COLS = 2 * CONV_WIDTH + SSM_WIDTH + XATTN_WIDTH + N_BRANCH * D_MODEL

kernel_name = "hybrid_conv_s5_memxattn_hmoe_block"


def rmsnorm(x, g):
    x32 = x.astype(jnp.float32)
    y = x32 * lax.rsqrt(jnp.mean(x32 * x32, axis=-1, keepdims=True) + EPS)
    return (y * g.astype(jnp.float32)).astype(x.dtype)


def layernorm(x, g, b):
    x32 = x.astype(jnp.float32)
    mu = jnp.mean(x32, axis=-1, keepdims=True)
    xc = x32 - mu
    var = jnp.mean(xc * xc, axis=-1, keepdims=True)
    y = xc * lax.rsqrt(var + EPS) * g.astype(jnp.float32) + b.astype(jnp.float32)
    return y.astype(x.dtype)


def conformer_conv(u2, w_dw, b_dw, ln_g, ln_b, w_pw):
    a, gate = jnp.split(u2, 2, axis=-1)
    v = a * jax.nn.sigmoid(gate)
    v = lax.conv_general_dilated(
        v, w_dw[:, None, :], window_strides=(1,),
        padding=[(CONV_KERNEL - 1, 0)],
        dimension_numbers=("NWC", "WIO", "NWC"),
        feature_group_count=CONV_WIDTH) + b_dw
    v = jax.nn.silu(layernorm(v, ln_g, ln_b))
    return v @ w_pw


def _ssm_combine(left, right):
    a_l, b_l = left
    a_r, b_r = right
    return a_l * a_r, a_r * b_l + b_r


def s5_ssm(u, lam_re, lam_im, log_dt, b_re, b_im, c_re, c_im, d, w_glu):
    bsz, seq, _ = u.shape
    ug = u.reshape(bsz, seq, SSM_GROUPS, SSM_GROUP).astype(jnp.float32)
    lam = lax.complex(lam_re.astype(jnp.float32), lam_im.astype(jnp.float32))
    dt = jnp.exp(log_dt.astype(jnp.float32))
    a_bar = jnp.exp(lam * dt[:, None])
    bmat = lax.complex(b_re.astype(jnp.float32), b_im.astype(jnp.float32))
    b_bar = ((a_bar - 1.0) / lam)[:, :, None] * bmat
    cmat = lax.complex(c_re.astype(jnp.float32), c_im.astype(jnp.float32))
    bu = jnp.einsum("bsgc,gpc->bsgp", ug, b_bar)
    a_elems = jnp.broadcast_to(a_bar[None, None], (1, seq, SSM_GROUPS, SSM_STATE))
    _, states = lax.associative_scan(_ssm_combine, (a_elems, bu), axis=1)
    y = jnp.real(jnp.einsum("bsgp,gcp->bsgc", states, cmat))
    y = y + d.astype(jnp.float32).reshape(SSM_GROUPS, SSM_GROUP) * ug
    y = y.reshape(bsz, seq, SSM_WIDTH).astype(u.dtype)
    za, zb = jnp.split(jax.nn.gelu(y) @ w_glu, 2, axis=-1)
    return za * jax.nn.sigmoid(zb)


def memory_cross_attention(q, mem_n, w_kv, w_o):
    bsz, seq, _ = q.shape
    qh = q.reshape(bsz, seq, XATTN_HEADS, XATTN_HEAD_DIM)
    k, v = jnp.split(mem_n @ w_kv, 2, axis=-1)
    kh = k.reshape(bsz, -1, XATTN_HEADS, XATTN_HEAD_DIM)
    vh = v.reshape(bsz, -1, XATTN_HEADS, XATTN_HEAD_DIM)
    s = jnp.einsum("bshd,bmhd->bhsm", qh, kh).astype(jnp.float32) * (XATTN_HEAD_DIM ** -0.5)
    p = jax.nn.softmax(s, axis=-1).astype(vh.dtype)
    o = jnp.einsum("bhsm,bmhd->bshd", p, vh).reshape(bsz, seq, XATTN_WIDTH)
    return o @ w_o


def hier_moe(h, w_rg, b_rg, w_re, b_re, w_gate, w_up, w_down):
    bsz, seq, dm = h.shape
    t = bsz * seq
    hf = h.reshape(t, dm)
    p_group = jax.nn.softmax((hf @ w_rg + b_rg).astype(jnp.float32), axis=-1)
    p_top, g_idx = lax.top_k(p_group, 1)
    logits_e = (hf @ w_re + b_re).astype(jnp.float32).reshape(t, MOE_GROUPS, EXPERTS_PER_GROUP)
    sel = jnp.take_along_axis(logits_e, g_idx[:, :, None], axis=1)[:, 0]
    p_in = jax.nn.softmax(sel, axis=-1)
    vals, e_idx = lax.top_k(p_in, TOP_K)
    weights = p_top * vals / jnp.sum(vals, axis=-1, keepdims=True)
    expert_id = (g_idx * EXPERTS_PER_GROUP + e_idx).reshape(-1)
    order = jnp.argsort(expert_id)
    tok = order // TOP_K
    xs = hf[tok]
    sizes = jnp.bincount(expert_id, length=N_EXPERTS).astype(jnp.int32)
    a = jax.nn.silu(lax.ragged_dot(xs, w_gate, sizes)) * lax.ragged_dot(xs, w_up, sizes)
    ys = lax.ragged_dot(a, w_down, sizes)
    ys = ys * weights.reshape(-1)[order][:, None].astype(ys.dtype)
    out = jax.ops.segment_sum(ys, tok, num_segments=t)
    return out.reshape(bsz, seq, dm)


def setup_inputs(seed: int = 0) -> dict:
    key = jax.random.key(seed)
    ks = jax.random.split(key, 32)
    f32 = jnp.float32
    nrm = lambda k, shape, s: jax.random.normal(k, shape, f32) * s
    L = DEPTH
    return {
        "x": nrm(ks[0], (BATCH, SEQ, D_MODEL), 1.0),
        "mem": nrm(ks[1], (BATCH, MEM_LEN, D_MODEL), 1.0),
        "g_mix": 1.0 + nrm(ks[2], (L, D_MODEL), 0.02),
        "w_in": nrm(ks[3], (L, D_MODEL, IN_COLS), D_MODEL ** -0.5),
        "conv_dw": nrm(ks[4], (L, CONV_KERNEL, CONV_WIDTH), CONV_KERNEL ** -0.5),
        "conv_dw_bias": nrm(ks[5], (L, CONV_WIDTH), 0.02),
        "conv_ln_g": 1.0 + nrm(ks[6], (L, CONV_WIDTH), 0.02),
        "conv_ln_b": nrm(ks[7], (L, CONV_WIDTH), 0.02),
        "w_conv_out": nrm(ks[8], (L, CONV_WIDTH, D_MODEL), CONV_WIDTH ** -0.5),
        "ssm_lambda_re": -0.5 + nrm(ks[9], (L, SSM_GROUPS, SSM_STATE), 0.01),
        "ssm_lambda_im": math.pi * jnp.arange(SSM_STATE, dtype=f32)[None, None, :]
                         + nrm(ks[10], (L, SSM_GROUPS, SSM_STATE), 0.01),
        "ssm_log_dt": jax.random.uniform(ks[11], (L, SSM_GROUPS), f32,
                                         math.log(1e-3), math.log(1e-1)),
        "ssm_b_re": nrm(ks[12], (L, SSM_GROUPS, SSM_STATE, SSM_GROUP), (2 * SSM_GROUP) ** -0.5),
        "ssm_b_im": nrm(ks[13], (L, SSM_GROUPS, SSM_STATE, SSM_GROUP), (2 * SSM_GROUP) ** -0.5),
        "ssm_c_re": nrm(ks[14], (L, SSM_GROUPS, SSM_GROUP, SSM_STATE), SSM_STATE ** -0.5),
        "ssm_c_im": nrm(ks[15], (L, SSM_GROUPS, SSM_GROUP, SSM_STATE), SSM_STATE ** -0.5),
        "ssm_d": nrm(ks[16], (L, SSM_WIDTH), 1.0),
        "w_ssm_glu": nrm(ks[17], (L, SSM_WIDTH, 2 * D_MODEL), SSM_WIDTH ** -0.5),
        "g_mem": 1.0 + nrm(ks[18], (L, D_MODEL), 0.02),
        "w_mem_kv": nrm(ks[19], (L, D_MODEL, 2 * XATTN_WIDTH), D_MODEL ** -0.5),
        "w_mem_out": nrm(ks[20], (L, XATTN_WIDTH, D_MODEL), XATTN_WIDTH ** -0.5),
        "w_out": nrm(ks[21], (L, D_MODEL, D_MODEL), D_MODEL ** -0.5),
        "g_ffn": 1.0 + nrm(ks[22], (L, D_MODEL), 0.02),
        "w_router_group": nrm(ks[23], (L, D_MODEL, MOE_GROUPS), D_MODEL ** -0.5),
        "b_router_group": nrm(ks[24], (L, MOE_GROUPS), 0.01),
        "w_router_expert": nrm(ks[25], (L, D_MODEL, N_EXPERTS), D_MODEL ** -0.5),
        "b_router_expert": nrm(ks[26], (L, N_EXPERTS), 0.01),
        "w_exp_gate": nrm(ks[27], (L, N_EXPERTS, D_MODEL, D_EXPERT), D_MODEL ** -0.5),
        "w_exp_up": nrm(ks[28], (L, N_EXPERTS, D_MODEL, D_EXPERT), D_MODEL ** -0.5),
        "w_exp_down": nrm(ks[29], (L, N_EXPERTS, D_EXPERT, D_MODEL), D_EXPERT ** -0.5),
        "g_final": 1.0 + nrm(ks[30], (D_MODEL,), 0.02),
    }


def reference(x, mem, g_mix, w_in, conv_dw, conv_dw_bias, conv_ln_g, conv_ln_b, w_conv_out,
              ssm_lambda_re, ssm_lambda_im, ssm_log_dt, ssm_b_re, ssm_b_im, ssm_c_re, ssm_c_im,
              ssm_d, w_ssm_glu, g_mem, w_mem_kv, w_mem_out, w_out, g_ffn,
              w_router_group, b_router_group, w_router_expert, b_router_expert,
              w_exp_gate, w_exp_up, w_exp_down, g_final):
    split_pts = [2 * CONV_WIDTH, 2 * CONV_WIDTH + SSM_WIDTH, 2 * CONV_WIDTH + SSM_WIDTH + XATTN_WIDTH]
    for l in range(DEPTH):
        h = rmsnorm(x, g_mix[l])
        proj = h @ w_in[l]
        conv_in, ssm_in, q, gates = jnp.split(proj, split_pts, axis=-1)
        y_conv = conformer_conv(conv_in, conv_dw[l], conv_dw_bias[l], conv_ln_g[l], conv_ln_b[l],
                                w_conv_out[l])
        y_ssm = s5_ssm(ssm_in, ssm_lambda_re[l], ssm_lambda_im[l], ssm_log_dt[l], ssm_b_re[l],
                       ssm_b_im[l], ssm_c_re[l], ssm_c_im[l], ssm_d[l], w_ssm_glu[l])
        mem_n = rmsnorm(mem, g_mem[l])
        y_mem = memory_cross_attention(q, mem_n, w_mem_kv[l], w_mem_out[l])
        g_a, g_b, g_c = jnp.split(jax.nn.sigmoid(gates), N_BRANCH, axis=-1)
        merged = g_a * y_conv + g_b * y_ssm + g_c * y_mem
        x = x + merged @ w_out[l]
        h2 = rmsnorm(x, g_ffn[l])
        x = x + hier_moe(h2, w_router_group[l], b_router_group[l], w_router_expert[l],
                         b_router_expert[l], w_exp_gate[l], w_exp_up[l], w_exp_down[l])
    return rmsnorm(x, g_final)
```

```python
import functools
import math

import jax
import jax.numpy as jnp
from jax import lax
from jax.experimental import pallas as pl
from jax.experimental.pallas import tpu as pltpu

F32 = jnp.float32
BF16 = jnp.bfloat16
EPS = 1e-6

LANES = 128
CHUNK_ROWS = 16
SSM_GROUP = 16
SSM_STATE = 64
SSM_BLOCK = LANES
CONV_KERNEL = 31
CONV_HALO = 32
HEADS = 4
HEAD_DIM = 128
MOE_GROUPS = 4
EXPERTS_PER_GROUP = 8
N_EXPERTS = MOE_GROUPS * EXPERTS_PER_GROUP
TOKEN_TILE = 512
EXPERT_TILE = 256
VMEM_LIMIT = 56 * 1024 * 1024


def _rms(x, g):
    return x * lax.rsqrt(jnp.mean(x * x, axis=-1, keepdims=True) + EPS) * g


def _sigmoid(x):
    return jax.nn.sigmoid(x)


def _const_spec(shape):
    zeros = (0,) * len(shape)
    return pl.BlockSpec(shape, lambda *_: zeros, pipeline_mode=pl.Buffered(1))


def _kv_kernel(mem_ref, g_ref, w_ref, k_ref, v_ref):
    width = k_ref.shape[-1]
    mn = _rms(mem_ref[0], g_ref[...]).astype(BF16)
    kv = jnp.dot(mn, w_ref[...], preferred_element_type=F32)
    k_ref[0] = kv[:, :width].astype(BF16)
    v_ref[0] = kv[:, width:].astype(BF16)


def _kv_call(mem, g_mem, w_kv):
    b, m, d = mem.shape
    width = w_kv.shape[1] // 2
    return pl.pallas_call(
        _kv_kernel,
        grid=(b,),
        in_specs=[pl.BlockSpec((1, m, d), lambda i: (i, 0, 0)),
                  pl.BlockSpec((1, d), lambda i: (0, 0)),
                  pl.BlockSpec(w_kv.shape, lambda i: (0, 0))],
        out_specs=[pl.BlockSpec((1, m, width), lambda i: (i, 0, 0)),
                   pl.BlockSpec((1, m, width), lambda i: (i, 0, 0))],
        out_shape=[jax.ShapeDtypeStruct((b, m, width), BF16)] * 2,
        name="kv",
    )(mem, g_mem, w_kv)


def _ut_kernel(x_ref, g_ref, wt_ref, o_ref):
    h = _rms(x_ref[...], g_ref[...]).astype(BF16)
    ut = lax.dot_general(wt_ref[...], h, (((1,), (1,)), ((), ())), preferred_element_type=F32)
    o_ref[...] = ut.astype(BF16)


def _ut_call(x2, g_mix, w_ssm_t):
    t, d = x2.shape
    c = w_ssm_t.shape[0]
    ts = 1024
    return pl.pallas_call(
        _ut_kernel,
        grid=(t // ts,),
        in_specs=[pl.BlockSpec((ts, d), lambda i: (i, 0)),
                  pl.BlockSpec((1, d), lambda i: (0, 0)),
                  pl.BlockSpec((c, d), lambda i: (0, 0))],
        out_specs=pl.BlockSpec((c, ts), lambda i: (0, i)),
        out_shape=jax.ShapeDtypeStruct((c, t), BF16),
        name="ut",
    )(x2, g_mix, w_ssm_t)


def _ssm_kernel(u_ref, kt_ref, w_ref, v_ref, a_ref, y_ref, acc_ref, z_ref, s_ref, *, n_batch):
    rows = u_ref.shape[2]
    n_blocks = rows // n_batch
    width = SSM_GROUP * SSM_BLOCK
    ri = lax.broadcasted_iota(jnp.int32, (SSM_BLOCK, SSM_BLOCK), 0)
    ci = lax.broadcasted_iota(jnp.int32, (SSM_BLOCK, SSM_BLOCK), 1)
    causal = ci >= ri

    acc_ref[...] = jnp.zeros_like(acc_ref)
    z_ref[...] = jnp.zeros_like(z_ref)

    def pair(cp, carry):
        slabs = []
        for half in range(2):
            kt = kt_ref[0, 2 * cp + half]
            blocks = []
            for c in range(SSM_GROUP):
                xb = jnp.broadcast_to(kt[c:c + 1, :], (SSM_BLOCK, SSM_BLOCK))
                toe = pltpu.roll(xb, 0, 1, stride=1, stride_axis=0)
                blocks.append(jnp.where(causal, toe, 0.0).astype(BF16))
            slabs.append(jnp.concatenate(blocks, axis=1))
        slab = jnp.concatenate(slabs, axis=0)
        x2 = jnp.concatenate([u_ref[0, 2 * cp], u_ref[0, 2 * cp + 1]], axis=1)
        acc_ref[...] += jnp.dot(x2, slab, preferred_element_type=F32)
        wrow = pl.multiple_of(cp * 2 * SSM_BLOCK, 2 * SSM_BLOCK)
        z_ref[...] += jnp.dot(x2, w_ref[0, pl.ds(wrow, 2 * SSM_BLOCK), :], preferred_element_type=F32)
        return carry

    lax.fori_loop(0, SSM_GROUP // 2, pair, 0)

    a_full = a_ref[0, 0:1, :]
    a_swap = a_ref[0, 1:2, :]
    st = jnp.zeros((n_batch, 2 * SSM_STATE), F32)
    for blk in range(n_blocks):
        sl = pl.ds(blk, n_batch, stride=n_blocks)
        s_ref[sl, :] = st
        st = a_full * st + a_swap * pltpu.roll(st, SSM_STATE, 1) + z_ref[sl, :]

    y = acc_ref[...] + jnp.dot(s_ref[...], v_ref[0], preferred_element_type=F32,
                               precision=lax.Precision.HIGHEST)
    for c in range(SSM_GROUP):
        y_ref[0, c] = y[:, c * SSM_BLOCK:(c + 1) * SSM_BLOCK]


def _ssm_call(u4, kt, wz, vy, acoef, n_batch):
    g, c, rows, blk = u4.shape
    width = c * blk
    return pl.pallas_call(
        functools.partial(_ssm_kernel, n_batch=n_batch),
        grid=(g,),
        in_specs=[pl.BlockSpec((1, c, rows, blk), lambda i: (i, 0, 0, 0)),
                  pl.BlockSpec((1, c, c, blk), lambda i: (i, 0, 0, 0)),
                  pl.BlockSpec((1, width, 2 * SSM_STATE), lambda i: (i, 0, 0)),
                  pl.BlockSpec((1, 2 * SSM_STATE, width), lambda i: (i, 0, 0)),
                  pl.BlockSpec((1, 2, 2 * SSM_STATE), lambda i: (i, 0, 0))],
        out_specs=pl.BlockSpec((1, c, rows, blk), lambda i: (i, 0, 0, 0)),
        out_shape=jax.ShapeDtypeStruct((g, c, rows, blk), F32),
        scratch_shapes=[pltpu.VMEM((rows, width), F32),
                        pltpu.VMEM((rows, 2 * SSM_STATE), F32),
                        pltpu.VMEM((rows, 2 * SSM_STATE), F32)],
        name="ssm",
    )(u4, kt, wz, vy, acoef)


def _ssm_tables(lam_re, lam_im, log_dt, b_re, b_im, c_re, c_im, d):
    hi = lax.Precision.HIGHEST
    g = lam_re.shape[0]
    dt = jnp.exp(log_dt)[:, None]
    er, ei = lam_re * dt, lam_im * dt
    k = jnp.arange(SSM_BLOCK + 1, dtype=F32)[None, :, None]
    mag = jnp.exp(k * er[:, None, :])
    ang = k * ei[:, None, :]
    pr, pi = mag * jnp.cos(ang), mag * jnp.sin(ang)
    nr, ni = pr[:, 1] - 1.0, pi[:, 1]
    den = lam_re * lam_re + lam_im * lam_im
    fr = (nr * lam_re + ni * lam_im) / den
    fi = (ni * lam_re - nr * lam_im) / den
    bbr = fr[:, :, None] * b_re - fi[:, :, None] * b_im
    bbi = fr[:, :, None] * b_im + fi[:, :, None] * b_re
    car = c_re[:, None] * pr[:, :, None, :] - c_im[:, None] * pi[:, :, None, :]
    cai = c_re[:, None] * pi[:, :, None, :] + c_im[:, None] * pr[:, :, None, :]
    kt = (jnp.einsum("gkcp,gpd->gdck", car[:, :SSM_BLOCK], bbr, precision=hi)
          - jnp.einsum("gkcp,gpd->gdck", cai[:, :SSM_BLOCK], bbi, precision=hi))
    dmat = jnp.eye(SSM_GROUP, dtype=F32)[None] * d.reshape(g, 1, SSM_GROUP)
    kt = kt.at[:, :, :, 0].add(dmat)
    prr, pir = pr[:, SSM_BLOCK - 1::-1][:, :SSM_BLOCK], pi[:, SSM_BLOCK - 1::-1][:, :SSM_BLOCK]
    wzr = prr[:, None, :, :] * jnp.swapaxes(bbr, 1, 2)[:, :, None, :] \
        - pir[:, None, :, :] * jnp.swapaxes(bbi, 1, 2)[:, :, None, :]
    wzi = prr[:, None, :, :] * jnp.swapaxes(bbi, 1, 2)[:, :, None, :] \
        + pir[:, None, :, :] * jnp.swapaxes(bbr, 1, 2)[:, :, None, :]
    wz = jnp.concatenate([wzr, wzi], axis=-1).reshape(g, SSM_GROUP * SSM_BLOCK, 2 * SSM_STATE)
    vr = jnp.transpose(car[:, 1:], (0, 3, 2, 1)).reshape(g, SSM_STATE, SSM_GROUP * SSM_BLOCK)
    vi = jnp.transpose(cai[:, 1:], (0, 3, 2, 1)).reshape(g, SSM_STATE, SSM_GROUP * SSM_BLOCK)
    vy = jnp.concatenate([vr, -vi], axis=1)
    ar, ai = pr[:, SSM_BLOCK], pi[:, SSM_BLOCK]
    acoef = jnp.stack([jnp.concatenate([ar, ar], -1), jnp.concatenate([-ai, ai], -1)], axis=1)
    return kt, wz.astype(BF16), vy, acoef


def _mix_kernel(x_ref, yt_ref, k_ref, v_ref, g_ref, wc_ref, wq_ref, wg_ref, dw_ref, dwb_ref,
                lng_ref, lnb_ref, wpw_ref, wglu_ref, wo_ref, wout_ref, o_ref, vext_ref):
    ts = x_ref.shape[1]
    d = x_ref.shape[2]
    cw = dw_ref.shape[1]
    x = x_ref[0]
    h = _rms(x, g_ref[...]).astype(BF16)

    ci = jnp.dot(h, wc_ref[...], preferred_element_type=F32)
    v = ci[:, :cw] * _sigmoid(ci[:, cw:])

    @pl.when(pl.program_id(1) == 0)
    def _():
        vext_ref[0:CONV_HALO, :] = jnp.zeros((CONV_HALO, cw), F32)

    vext_ref[CONV_HALO:CONV_HALO + ts, :] = v
    acc = jnp.broadcast_to(dwb_ref[...], (ts, cw))
    for k in range(CONV_KERNEL):
        off = CONV_HALO - (CONV_KERNEL - 1) + k
        acc = acc + dw_ref[k:k + 1, :] * vext_ref[off:off + ts, :]
    vext_ref[0:CONV_HALO, :] = vext_ref[ts:ts + CONV_HALO, :]
    mu = jnp.mean(acc, axis=-1, keepdims=True)
    xc = acc - mu
    var = jnp.mean(xc * xc, axis=-1, keepdims=True)
    ln = xc * lax.rsqrt(var + EPS) * lng_ref[...] + lnb_ref[...]
    sw = ln * _sigmoid(ln)
    y_conv = jnp.dot(sw.astype(BF16), wpw_ref[...], preferred_element_type=F32)
    merged = _sigmoid(jnp.dot(h, wg_ref[:, 0:d], preferred_element_type=F32)) * y_conv

    gy = jax.nn.gelu(yt_ref[...]).astype(BF16)
    z = lax.dot_general(gy, wglu_ref[...], (((0,), (0,)), ((), ())), preferred_element_type=F32)
    y_ssm = z[:, :d] * _sigmoid(z[:, d:])
    merged = merged + _sigmoid(jnp.dot(h, wg_ref[:, d:2 * d], preferred_element_type=F32)) * y_ssm

    q = jnp.dot(h, wq_ref[...], preferred_element_type=F32)
    kk = k_ref[0]
    vv = v_ref[0]
    outs = []
    for hd in range(HEADS):
        sl = slice(hd * HEAD_DIM, (hd + 1) * HEAD_DIM)
        s = lax.dot_general(q[:, sl].astype(BF16), kk[:, sl], (((1,), (1,)), ((), ())),
                            preferred_element_type=F32) * (HEAD_DIM ** -0.5)
        p = jnp.exp(s - jnp.max(s, axis=-1, keepdims=True))
        den = jnp.sum(p, axis=-1, keepdims=True)
        o = jnp.dot(p.astype(BF16), vv[:, sl], preferred_element_type=F32) / den
        outs.append(o.astype(BF16))
    y_mem = jnp.dot(jnp.concatenate(outs, axis=1), wo_ref[...], preferred_element_type=F32)
    merged = merged + _sigmoid(jnp.dot(h, wg_ref[:, 2 * d:3 * d], preferred_element_type=F32)) * y_mem

    o_ref[0] = x + jnp.dot(merged.astype(BF16), wout_ref[...], preferred_element_type=F32)


def _mix_call(x, yt, kmem, vmem, g_mix, wc, wq, wg, dw, dwb, lng, lnb, wpw, wglu, wo, wout):
    b, s, d = x.shape
    ts = TOKEN_TILE
    nst = s // ts
    cw = dw.shape[1]
    m = kmem.shape[1]
    consts = [g_mix, wc, wq, wg, dw, dwb, lng, lnb, wpw, wglu, wo, wout]
    return pl.pallas_call(
        _mix_kernel,
        grid=(b, nst),
        in_specs=[pl.BlockSpec((1, ts, d), lambda i, j: (i, j, 0)),
                  pl.BlockSpec((yt.shape[0], ts), lambda i, j: (0, i * nst + j)),
                  pl.BlockSpec((1, m, kmem.shape[2]), lambda i, j: (i, 0, 0)),
                  pl.BlockSpec((1, m, vmem.shape[2]), lambda i, j: (i, 0, 0))]
                 + [_const_spec(c.shape) for c in consts],
        out_specs=pl.BlockSpec((1, ts, d), lambda i, j: (i, j, 0)),
        out_shape=jax.ShapeDtypeStruct((b, s, d), F32),
        scratch_shapes=[pltpu.VMEM((ts + CONV_HALO, cw), F32)],
        compiler_params=pltpu.CompilerParams(
            dimension_semantics=("arbitrary", "arbitrary"), vmem_limit_bytes=VMEM_LIMIT),
        name="mix",
    )(x, yt, kmem, vmem, *consts)


def _route_kernel(x_ref, g_ref, wr_ref, br_ref, xs_ref, info_ref, nch_ref):
    ts = x_ref.shape[0]
    cap = xs_ref.shape[1]
    h2 = _rms(x_ref[...], g_ref[...])
    logits = jnp.dot(h2, wr_ref[...], preferred_element_type=F32,
                     precision=lax.Precision.HIGHEST) + br_ref[...]
    lane_i = lax.broadcasted_iota(jnp.int32, (ts, LANES), 1)
    lane = lane_i.astype(F32)
    neg = jnp.float32(-1e30)
    big = jnp.float32(1e9)

    gmask = (lane_i >= N_EXPERTS) & (lane_i < N_EXPERTS + MOE_GROUPS)
    gmax = jnp.max(jnp.where(gmask, logits, neg), axis=-1, keepdims=True)
    gidx = jnp.min(jnp.where(gmask & (logits == gmax), lane, big), axis=-1, keepdims=True) - N_EXPERTS
    gsum = jnp.sum(jnp.where(gmask, jnp.exp(jnp.minimum(logits - gmax, 0.0)), 0.0), axis=-1, keepdims=True)
    p_top = 1.0 / gsum
    emask = (lane_i < N_EXPERTS) & (jnp.floor(lane * (1.0 / EXPERTS_PER_GROUP)) == gidx)
    m1 = jnp.max(jnp.where(emask, logits, neg), axis=-1, keepdims=True)
    i1 = jnp.min(jnp.where(emask & (logits == m1), lane, big), axis=-1, keepdims=True)
    emask2 = emask & (lane != i1)
    m2 = jnp.max(jnp.where(emask2, logits, neg), axis=-1, keepdims=True)
    i2 = jnp.min(jnp.where(emask2 & (logits == m2), lane, big), axis=-1, keepdims=True)
    r = jnp.exp(m2 - m1)
    w1 = p_top / (1.0 + r)
    w2 = p_top * r / (1.0 + r)

    sel1 = lane == i1
    sel2 = lane == i2
    occ = jnp.where(sel1 | sel2, 1.0, 0.0)
    tr = lax.broadcasted_iota(jnp.int32, (ts, ts), 0)
    tc = lax.broadcasted_iota(jnp.int32, (ts, ts), 1)
    before = jnp.where(tc < tr, 1.0, 0.0).astype(BF16)
    rank = jnp.dot(before, occ.astype(BF16), preferred_element_type=F32)
    cnt = jnp.sum(occ, axis=0, keepdims=True)
    nch = jnp.floor((cnt + (CHUNK_ROWS - 1)) * (1.0 / CHUNK_ROWS))
    er = lax.broadcasted_iota(jnp.int32, (LANES, LANES), 0)
    ec = lax.broadcasted_iota(jnp.int32, (LANES, LANES), 1)
    upper = jnp.where(er < ec, 1.0, 0.0).astype(BF16)
    start = jnp.dot(jnp.broadcast_to(nch, (8, LANES)).astype(BF16), upper,
                    preferred_element_type=F32)[0:1, :] * CHUNK_ROWS
    slot = start + rank
    pos1 = jnp.sum(jnp.where(sel1, slot, 0.0), axis=-1, keepdims=True)
    pos2 = jnp.sum(jnp.where(sel2, slot, 0.0), axis=-1, keepdims=True)

    rowid = lax.broadcasted_iota(jnp.int32, (ts, cap), 1)
    pt = jnp.where((rowid == pos1.astype(jnp.int32)) | (rowid == pos2.astype(jnp.int32)), 1.0, 0.0)
    xs = lax.dot_general(pt.astype(BF16), h2.astype(BF16), (((0,), (0,)), ((), ())),
                         preferred_element_type=F32)
    xs_ref[0] = xs.astype(BF16)
    info_ref[...] = jnp.where(lane_i == 0, pos1, jnp.where(lane_i == 1, pos2,
                              jnp.where(lane_i == 2, w1, jnp.where(lane_i == 3, w2, 0.0))))
    nch_ref[0] = nch


def _route_call(x1, g_ffn, wr, br, cap):
    t, d = x1.shape
    ts = TOKEN_TILE
    nt = t // ts
    return pl.pallas_call(
        _route_kernel,
        grid=(nt,),
        in_specs=[pl.BlockSpec((ts, d), lambda i: (i, 0)),
                  pl.BlockSpec((1, d), lambda i: (0, 0)),
                  pl.BlockSpec(wr.shape, lambda i: (0, 0)),
                  pl.BlockSpec((1, LANES), lambda i: (0, 0))],
        out_specs=[pl.BlockSpec((1, cap, d), lambda i: (i, 0, 0)),
                   pl.BlockSpec((ts, LANES), lambda i: (i, 0)),
                   pl.BlockSpec((1, 1, LANES), lambda i: (i, 0, 0))],
        out_shape=[jax.ShapeDtypeStruct((nt, cap, d), BF16),
                   jax.ShapeDtypeStruct((t, LANES), F32),
                   jax.ShapeDtypeStruct((nt, 1, LANES), F32)],
        compiler_params=pltpu.CompilerParams(vmem_limit_bytes=VMEM_LIMIT),
        name="route",
    )(x1, g_ffn, wr, br)


def _shuffle_kernel(nch_ref, toff_ref, eoff_ref, ntot_ref, zoff_ref, zcnt_ref, nused_ref,
                    src_ref, dst_ref, zbuf_ref, sem_ref, *, to_sorted):
    i = pl.program_id(0)
    n_tiles = pl.num_programs(0)
    zchunk_ref = zbuf_ref.at[pl.ds(0, CHUNK_ROWS)]

    def tile_rows(off):
        return pl.ds(pl.multiple_of(off * CHUNK_ROWS, CHUNK_ROWS), CHUNK_ROWS)

    def run_copy(tile_off, exp_off):
        if to_sorted:
            return pltpu.make_async_copy(src_ref.at[i, tile_rows(tile_off)],
                                         dst_ref.at[tile_rows(exp_off)], sem_ref.at[0])
        return pltpu.make_async_copy(src_ref.at[tile_rows(exp_off)],
                                     dst_ref.at[i, tile_rows(tile_off)], sem_ref.at[0])

    def zero_copy(off):
        if to_sorted:
            return pltpu.make_async_copy(zchunk_ref, dst_ref.at[tile_rows(off)], sem_ref.at[1])
        return pltpu.make_async_copy(zchunk_ref, dst_ref.at[i, tile_rows(off)], sem_ref.at[1])

    def tail_copy(etile):
        rows = pl.ds(pl.multiple_of(etile * EXPERT_TILE, EXPERT_TILE), EXPERT_TILE)
        return pltpu.make_async_copy(zbuf_ref, dst_ref.at[rows], sem_ref.at[2])

    @pl.when(i == 0)
    def _():
        zbuf_ref[...] = jnp.zeros_like(zbuf_ref)

    def per_expert(e, carry):
        idx = i * N_EXPERTS + e
        n = nch_ref[idx]
        t0 = toff_ref[idx]
        e0 = eoff_ref[idx]

        def chunk(c, cc):
            run_copy(t0 + c, e0 + c).start()
            return cc
        lax.fori_loop(0, n, chunk, 0)
        return carry
    lax.fori_loop(0, N_EXPERTS, per_expert, 0)

    if to_sorted:
        @pl.when(i == 0)
        def _():
            def per_gap(e, carry):
                z0 = zoff_ref[e]

                def chunk(c, cc):
                    zero_copy(z0 + c).start()
                    return cc
                lax.fori_loop(0, zcnt_ref[e], chunk, 0)
                return carry
            lax.fori_loop(0, N_EXPERTS, per_gap, 0)

            def per_gap_wait(e, carry):
                def chunk(c, cc):
                    zero_copy(0).wait()
                    return cc
                lax.fori_loop(0, zcnt_ref[e], chunk, 0)
                return carry
            lax.fori_loop(0, N_EXPERTS, per_gap_wait, 0)

            n_etiles = dst_ref.shape[0] // EXPERT_TILE

            def tail_start(c, cc):
                tail_copy(c).start()
                return cc
            lax.fori_loop(nused_ref[0], n_etiles, tail_start, 0)

            def tail_wait(c, cc):
                tail_copy(0).wait()
                return cc
            lax.fori_loop(nused_ref[0], n_etiles, tail_wait, 0)
    else:
        z0 = zoff_ref[i]

        def zchunk(c, cc):
            zero_copy(z0 + c).start()
            return cc
        lax.fori_loop(0, zcnt_ref[i], zchunk, 0)

        def zwait(c, cc):
            zero_copy(0).wait()
            return cc
        lax.fori_loop(0, zcnt_ref[i], zwait, 0)

    def wait_one(c, cc):
        run_copy(0, 0).wait()
        return cc

    @pl.when(i > 0)
    def _():
        lax.fori_loop(0, ntot_ref[i - 1], wait_one, 0)

    @pl.when(i == n_tiles - 1)
    def _():
        lax.fori_loop(0, ntot_ref[i], wait_one, 0)


def _shuffle_call(src, tables, out_shape, to_sorted):
    n_tiles = tables[3].shape[0]
    d = src.shape[-1]
    return pl.pallas_call(
        functools.partial(_shuffle_kernel, to_sorted=to_sorted),
        grid_spec=pltpu.PrefetchScalarGridSpec(
            num_scalar_prefetch=7,
            grid=(n_tiles,),
            in_specs=[pl.BlockSpec(memory_space=pl.ANY)],
            out_specs=pl.BlockSpec(memory_space=pl.ANY),
            scratch_shapes=[pltpu.VMEM((EXPERT_TILE, d), src.dtype),
                            pltpu.SemaphoreType.DMA((3,))]),
        out_shape=jax.ShapeDtypeStruct(out_shape, src.dtype),
        compiler_params=pltpu.CompilerParams(dimension_semantics=("arbitrary",)),
        name="shuffle_to_sorted" if to_sorted else "shuffle_to_tiles",
    )(*tables, src)


def _expert_kernel(te_ref, nu_ref, x_ref, wg_ref, wu_ref, wd_ref, o_ref):
    used = pl.program_id(0) < nu_ref[0]

    @pl.when(used)
    def _():
        x = x_ref[...]
        gate = jnp.dot(x, wg_ref[0], preferred_element_type=F32)
        up = jnp.dot(x, wu_ref[0], preferred_element_type=F32)
        act = (gate * _sigmoid(gate) * up).astype(BF16)
        o_ref[...] = jnp.dot(act, wd_ref[0], preferred_element_type=F32).astype(BF16)

    @pl.when(jnp.logical_not(used))
    def _():
        o_ref[...] = jnp.zeros_like(o_ref)


def _expert_call(tile_expert, n_used, xs, wg, wu, wd):
    rows, d = xs.shape
    de = wg.shape[2]
    tm = EXPERT_TILE
    nt = rows // tm

    def row_map(i, te, nu):
        return (jnp.minimum(i, nu[0] - 1), 0)

    def w_map(i, te, nu):
        return (te[jnp.minimum(i, nu[0] - 1)], 0, 0)

    return pl.pallas_call(
        _expert_kernel,
        grid_spec=pltpu.PrefetchScalarGridSpec(
            num_scalar_prefetch=2,
            grid=(nt,),
            in_specs=[pl.BlockSpec((tm, d), row_map),
                      pl.BlockSpec((1, d, de), w_map),
                      pl.BlockSpec((1, d, de), w_map),
                      pl.BlockSpec((1, de, d), w_map)],
            out_specs=pl.BlockSpec((tm, d), lambda i, te, nu: (i, 0))),
        out_shape=jax.ShapeDtypeStruct((rows, d), BF16),
        compiler_params=pltpu.CompilerParams(dimension_semantics=("arbitrary",)),
        name="expert",
    )(tile_expert, n_used, xs, wg, wu, wd)


def _combine_kernel(x_ref, ys_ref, info_ref, g_ref, o_ref):
    ts = x_ref.shape[0]
    cap = ys_ref.shape[1]
    info = info_ref[...]
    pos1 = info[:, 0:1].astype(jnp.int32)
    pos2 = info[:, 1:2].astype(jnp.int32)
    w1 = info[:, 2:3]
    w2 = info[:, 3:4]
    rowid = lax.broadcasted_iota(jnp.int32, (ts, cap), 1)
    ys = ys_ref[0]
    y1 = jnp.dot(jnp.where(rowid == pos1, 1.0, 0.0).astype(BF16), ys, preferred_element_type=F32)
    y2 = jnp.dot(jnp.where(rowid == pos2, 1.0, 0.0).astype(BF16), ys, preferred_element_type=F32)
    o_ref[...] = _rms(x_ref[...] + w1 * y1 + w2 * y2, g_ref[...])


def _combine_call(x1, ys_tiles, info, g_final):
    t, d = x1.shape
    ts = TOKEN_TILE
    cap = ys_tiles.shape[1]
    return pl.pallas_call(
        _combine_kernel,
        grid=(t // ts,),
        in_specs=[pl.BlockSpec((ts, d), lambda i: (i, 0)),
                  pl.BlockSpec((1, cap, d), lambda i: (i, 0, 0)),
                  pl.BlockSpec((ts, LANES), lambda i: (i, 0)),
                  pl.BlockSpec((1, d), lambda i: (0, 0))],
        out_specs=pl.BlockSpec((ts, d), lambda i: (i, 0)),
        out_shape=jax.ShapeDtypeStruct((t, d), F32),
        compiler_params=pltpu.CompilerParams(vmem_limit_bytes=VMEM_LIMIT),
        name="combine",
    )(x1, ys_tiles, info, g_final)


def _layer(x, mem, g_mix, w_in, conv_dw, conv_dw_bias, conv_ln_g, conv_ln_b, w_conv_out,
           lam_re, lam_im, log_dt, b_re, b_im, c_re, c_im, ssm_d, w_ssm_glu, g_mem, w_mem_kv,
           w_mem_out, w_out, g_ffn, w_rg, b_rg, w_re, b_re_, w_eg, w_eu, w_ed):
    b, s, d = x.shape
    t = b * s
    cw = conv_dw.shape[1]
    sw = ssm_d.shape[0]
    qw = w_mem_out.shape[0]
    n_groups = sw // SSM_GROUP
    row = lambda a: a.reshape(1, -1)

    o0, o1, o2 = 2 * cw, 2 * cw + sw, 2 * cw + sw + qw
    wc = w_in[:, :o0].astype(BF16)
    w_ssm_t = w_in[:, o0:o1].T.astype(BF16)
    wq = w_in[:, o1:o2].astype(BF16)
    wg = w_in[:, o2:].astype(BF16)

    kmem, vmem = _kv_call(mem, row(g_mem), w_mem_kv.astype(BF16))

    ut = _ut_call(x.reshape(t, d), row(g_mix), w_ssm_t)
    n_rows = t // SSM_BLOCK
    kt, wz, vy, acoef = _ssm_tables(lam_re, lam_im, log_dt, b_re, b_im, c_re, c_im, ssm_d)
    yt = _ssm_call(ut.reshape(n_groups, SSM_GROUP, n_rows, SSM_BLOCK), kt, wz, vy, acoef, b)
    yt = yt.reshape(sw, t)

    dw = jnp.concatenate([conv_dw, jnp.zeros((CONV_HALO - CONV_KERNEL, cw), F32)], axis=0)
    x1 = _mix_call(x, yt, kmem, vmem, row(g_mix), wc, wq, wg, dw, row(conv_dw_bias),
                   row(conv_ln_g), row(conv_ln_b), w_conv_out.astype(BF16),
                   w_ssm_glu.astype(BF16), w_mem_out.astype(BF16), w_out.astype(BF16))
    x1 = x1.reshape(t, d)

    pad = LANES - N_EXPERTS - MOE_GROUPS
    wr = jnp.concatenate([w_re, w_rg, jnp.zeros((d, pad), F32)], axis=1)
    br = jnp.concatenate([b_re_, b_rg, jnp.zeros((pad,), F32)]).reshape(1, LANES)
    n_tiles = t // TOKEN_TILE
    chunks_per_tile = (2 * TOKEN_TILE + N_EXPERTS * (CHUNK_ROWS - 1)) // CHUNK_ROWS
    chunks_per_tile = -(-chunks_per_tile // 8) * 8
    cap = chunks_per_tile * CHUNK_ROWS
    xs_tiles, info, nch_f = _route_call(x1, row(g_ffn), wr, br, cap)

    per_etile = EXPERT_TILE // CHUNK_ROWS
    nch = nch_f[:, 0, :N_EXPERTS].astype(jnp.int32)
    toff = jnp.cumsum(nch, axis=1) - nch
    etot = jnp.sum(nch, axis=0)
    eseg = -(-etot // per_etile) * per_etile
    ebase = jnp.cumsum(eseg) - eseg
    eoff = ebase[None, :] + jnp.cumsum(nch, axis=0) - nch
    ntot = jnp.sum(nch, axis=1)
    n_used = (jnp.sum(eseg) // per_etile).astype(jnp.int32).reshape(1)
    max_chunks = n_tiles * ((2 * TOKEN_TILE + N_EXPERTS * (CHUNK_ROWS - 1)) // CHUNK_ROWS) \
        + N_EXPERTS * (per_etile - 1)
    n_etiles = -(-max_chunks // per_etile)
    tile_expert = jnp.searchsorted(jnp.cumsum(eseg) // per_etile,
                                   jnp.arange(n_etiles, dtype=jnp.int32), side="right")
    tile_expert = jnp.minimum(tile_expert, N_EXPERTS - 1).astype(jnp.int32)
    flat = lambda a: a.reshape(-1).astype(jnp.int32)
    fwd_tables = (flat(nch), flat(toff), flat(eoff), flat(ntot), flat(ebase + etot),
                  flat(eseg - etot), n_used)
    bwd_tables = (flat(nch), flat(toff), flat(eoff), flat(ntot), flat(ntot),
                  flat(chunks_per_tile - ntot), n_used)

    xs_sorted = _shuffle_call(xs_tiles, fwd_tables, (n_etiles * EXPERT_TILE, d), True)
    ys_sorted = _expert_call(tile_expert, n_used, xs_sorted,
                             w_eg.astype(BF16), w_eu.astype(BF16), w_ed.astype(BF16))
    ys_tiles = _shuffle_call(ys_sorted, bwd_tables, (n_tiles, cap, d), False)
    return x1, ys_tiles, info


def kernel(x, mem, g_mix, w_in, conv_dw, conv_dw_bias, conv_ln_g, conv_ln_b, w_conv_out, ssm_lambda_re, ssm_lambda_im, ssm_log_dt, ssm_b_re, ssm_b_im, ssm_c_re, ssm_c_im, ssm_d, w_ssm_glu, g_mem, w_mem_kv, w_mem_out, w_out, g_ffn, w_router_group, b_router_group, w_router_expert, b_router_expert, w_exp_gate, w_exp_up, w_exp_down, g_final):
    b, s, d = x.shape
    assert g_mix.shape[0] == 1, "the problem fixes one layer"
    l = 0
    x1, ys_tiles, info = _layer(
        x, mem, g_mix[l], w_in[l], conv_dw[l], conv_dw_bias[l], conv_ln_g[l], conv_ln_b[l],
        w_conv_out[l], ssm_lambda_re[l], ssm_lambda_im[l], ssm_log_dt[l], ssm_b_re[l],
        ssm_b_im[l], ssm_c_re[l], ssm_c_im[l], ssm_d[l], w_ssm_glu[l], g_mem[l], w_mem_kv[l],
        w_mem_out[l], w_out[l], g_ffn[l], w_router_group[l], b_router_group[l],
        w_router_expert[l], b_router_expert[l], w_exp_gate[l], w_exp_up[l], w_exp_down[l])
    return _combine_call(x1, ys_tiles, info, g_final.reshape(1, d)).reshape(b, s, d)
```

```python
import functools

import jax
import jax.numpy as jnp
from jax import lax
from jax.experimental import pallas as pl
from jax.experimental.pallas import tpu as pltpu

F32 = jnp.float32
BF16 = jnp.bfloat16
EPS = 1e-6

LANES = 128
CHUNK_ROWS = 16
SSM_GROUP = 16
SSM_STATE = 64
SSM_BLOCK = LANES
CONV_KERNEL = 31
CONV_HALO = 32
HEADS = 4
HEAD_DIM = 128
MOE_GROUPS = 4
EXPERTS_PER_GROUP = 8
N_EXPERTS = MOE_GROUPS * EXPERTS_PER_GROUP
TOKEN_TILE = 512
EXPERT_TILE = 256
CHUNKS_PER_ETILE = EXPERT_TILE // CHUNK_ROWS
VMEM_LIMIT = 56 * 1024 * 1024


def _rms(x, g):
    return x * lax.rsqrt(jnp.mean(x * x, axis=-1, keepdims=True) + EPS) * g


def _sigmoid(x):
    return jax.nn.sigmoid(x)


def _const_spec(shape):
    zeros = (0,) * len(shape)
    return pl.BlockSpec(shape, lambda *_: zeros, pipeline_mode=pl.Buffered(1))


def _chunk_rows(chunk):
    return pl.ds(pl.multiple_of(chunk * CHUNK_ROWS, CHUNK_ROWS), CHUNK_ROWS)


def _kv_kernel(mem_ref, g_ref, w_ref, k_ref, v_ref):
    width = k_ref.shape[-1]
    mn = _rms(mem_ref[0], g_ref[...]).astype(BF16)
    kv = jnp.dot(mn, w_ref[...], preferred_element_type=F32)
    k_ref[0] = kv[:, :width].astype(BF16)
    v_ref[0] = kv[:, width:].astype(BF16)


def _kv_call(mem, g_mem, w_kv):
    b, m, d = mem.shape
    width = w_kv.shape[1] // 2
    return pl.pallas_call(
        _kv_kernel,
        grid=(b,),
        in_specs=[pl.BlockSpec((1, m, d), lambda i: (i, 0, 0)),
                  pl.BlockSpec((1, d), lambda i: (0, 0)),
                  pl.BlockSpec(w_kv.shape, lambda i: (0, 0))],
        out_specs=[pl.BlockSpec((1, m, width), lambda i: (i, 0, 0)),
                   pl.BlockSpec((1, m, width), lambda i: (i, 0, 0))],
        out_shape=[jax.ShapeDtypeStruct((b, m, width), BF16)] * 2,
        name="kv",
    )(mem, g_mem, w_kv)


def _ut_kernel(x_ref, g_ref, wt_ref, o_ref):
    h = _rms(x_ref[...], g_ref[...]).astype(BF16)
    ut = lax.dot_general(wt_ref[...], h, (((1,), (1,)), ((), ())), preferred_element_type=F32)
    o_ref[...] = ut.astype(BF16)


def _ut_call(x2, g_mix, w_ssm_t):
    t, d = x2.shape
    c = w_ssm_t.shape[0]
    ts = 1024
    return pl.pallas_call(
        _ut_kernel,
        grid=(t // ts,),
        in_specs=[pl.BlockSpec((ts, d), lambda i: (i, 0)),
                  pl.BlockSpec((1, d), lambda i: (0, 0)),
                  pl.BlockSpec((c, d), lambda i: (0, 0))],
        out_specs=pl.BlockSpec((c, ts), lambda i: (0, i)),
        out_shape=jax.ShapeDtypeStruct((c, t), BF16),
        name="ut",
    )(x2, g_mix, w_ssm_t)


def _ssm_kernel(u_ref, kt_ref, w_ref, v_ref, a_ref, y_ref, acc_ref, z_ref, s_ref, *, n_batch):
    rows = u_ref.shape[2]
    n_blocks = rows // n_batch
    ri = lax.broadcasted_iota(jnp.int32, (SSM_BLOCK, SSM_BLOCK), 0)
    ci = lax.broadcasted_iota(jnp.int32, (SSM_BLOCK, SSM_BLOCK), 1)
    causal = ci >= ri

    acc_ref[...] = jnp.zeros_like(acc_ref)
    z_ref[...] = jnp.zeros_like(z_ref)

    def pair(cp, carry):
        slabs = []
        for half in range(2):
            kt = kt_ref[0, 2 * cp + half]
            blocks = []
            for c in range(SSM_GROUP):
                xb = jnp.broadcast_to(kt[c:c + 1, :], (SSM_BLOCK, SSM_BLOCK))
                toe = pltpu.roll(xb, 0, 1, stride=1, stride_axis=0)
                blocks.append(jnp.where(causal, toe, 0.0).astype(BF16))
            slabs.append(jnp.concatenate(blocks, axis=1))
        slab = jnp.concatenate(slabs, axis=0)
        x2 = jnp.concatenate([u_ref[0, 2 * cp], u_ref[0, 2 * cp + 1]], axis=1)
        acc_ref[...] += jnp.dot(x2, slab, preferred_element_type=F32)
        wrow = pl.multiple_of(cp * 2 * SSM_BLOCK, 2 * SSM_BLOCK)
        z_ref[...] += jnp.dot(x2, w_ref[0, pl.ds(wrow, 2 * SSM_BLOCK), :], preferred_element_type=F32)
        return carry

    lax.fori_loop(0, SSM_GROUP // 2, pair, 0)

    a_full = a_ref[0, 0:1, :]
    a_swap = a_ref[0, 1:2, :]
    st = jnp.zeros((n_batch, 2 * SSM_STATE), F32)
    for blk in range(n_blocks):
        sl = pl.ds(blk, n_batch, stride=n_blocks)
        s_ref[sl, :] = st
        st = a_full * st + a_swap * pltpu.roll(st, SSM_STATE, 1) + z_ref[sl, :]

    y = acc_ref[...] + jnp.dot(s_ref[...], v_ref[0], preferred_element_type=F32,
                               precision=lax.Precision.HIGHEST)
    for c in range(SSM_GROUP):
        y_ref[0, c] = y[:, c * SSM_BLOCK:(c + 1) * SSM_BLOCK]


def _ssm_call(u4, kt, wz, vy, acoef, n_batch):
    g, c, rows, blk = u4.shape
    width = c * blk
    return pl.pallas_call(
        functools.partial(_ssm_kernel, n_batch=n_batch),
        grid=(g,),
        in_specs=[pl.BlockSpec((1, c, rows, blk), lambda i: (i, 0, 0, 0)),
                  pl.BlockSpec((1, c, c, blk), lambda i: (i, 0, 0, 0)),
                  pl.BlockSpec((1, width, 2 * SSM_STATE), lambda i: (i, 0, 0)),
                  pl.BlockSpec((1, 2 * SSM_STATE, width), lambda i: (i, 0, 0)),
                  pl.BlockSpec((1, 2, 2 * SSM_STATE), lambda i: (i, 0, 0))],
        out_specs=pl.BlockSpec((1, c, rows, blk), lambda i: (i, 0, 0, 0)),
        out_shape=jax.ShapeDtypeStruct((g, c, rows, blk), F32),
        scratch_shapes=[pltpu.VMEM((rows, width), F32),
                        pltpu.VMEM((rows, 2 * SSM_STATE), F32),
                        pltpu.VMEM((rows, 2 * SSM_STATE), F32)],
        name="ssm",
    )(u4, kt, wz, vy, acoef)


def _ssm_tables(lam_re, lam_im, log_dt, b_re, b_im, c_re, c_im, d):
    hi = lax.Precision.HIGHEST
    g = lam_re.shape[0]
    dt = jnp.exp(log_dt)[:, None]
    er, ei = lam_re * dt, lam_im * dt
    k = jnp.arange(SSM_BLOCK + 1, dtype=F32)[None, :, None]
    mag = jnp.exp(k * er[:, None, :])
    ang = k * ei[:, None, :]
    pr, pi = mag * jnp.cos(ang), mag * jnp.sin(ang)
    nr, ni = pr[:, 1] - 1.0, pi[:, 1]
    den = lam_re * lam_re + lam_im * lam_im
    fr = (nr * lam_re + ni * lam_im) / den
    fi = (ni * lam_re - nr * lam_im) / den
    bbr = fr[:, :, None] * b_re - fi[:, :, None] * b_im
    bbi = fr[:, :, None] * b_im + fi[:, :, None] * b_re
    car = c_re[:, None] * pr[:, :, None, :] - c_im[:, None] * pi[:, :, None, :]
    cai = c_re[:, None] * pi[:, :, None, :] + c_im[:, None] * pr[:, :, None, :]
    kt = (jnp.einsum("gkcp,gpd->gdck", car[:, :SSM_BLOCK], bbr, precision=hi)
          - jnp.einsum("gkcp,gpd->gdck", cai[:, :SSM_BLOCK], bbi, precision=hi))
    dmat = jnp.eye(SSM_GROUP, dtype=F32)[None] * d.reshape(g, 1, SSM_GROUP)
    kt = kt.at[:, :, :, 0].add(dmat)
    prr, pir = pr[:, SSM_BLOCK - 1::-1][:, :SSM_BLOCK], pi[:, SSM_BLOCK - 1::-1][:, :SSM_BLOCK]
    wzr = prr[:, None, :, :] * jnp.swapaxes(bbr, 1, 2)[:, :, None, :] \
        - pir[:, None, :, :] * jnp.swapaxes(bbi, 1, 2)[:, :, None, :]
    wzi = prr[:, None, :, :] * jnp.swapaxes(bbi, 1, 2)[:, :, None, :] \
        + pir[:, None, :, :] * jnp.swapaxes(bbr, 1, 2)[:, :, None, :]
    wz = jnp.concatenate([wzr, wzi], axis=-1).reshape(g, SSM_GROUP * SSM_BLOCK, 2 * SSM_STATE)
    vr = jnp.transpose(car[:, 1:], (0, 3, 2, 1)).reshape(g, SSM_STATE, SSM_GROUP * SSM_BLOCK)
    vi = jnp.transpose(cai[:, 1:], (0, 3, 2, 1)).reshape(g, SSM_STATE, SSM_GROUP * SSM_BLOCK)
    vy = jnp.concatenate([vr, -vi], axis=1)
    ar, ai = pr[:, SSM_BLOCK], pi[:, SSM_BLOCK]
    acoef = jnp.stack([jnp.concatenate([ar, ar], -1), jnp.concatenate([-ai, ai], -1)], axis=1)
    return kt, wz.astype(BF16), vy, acoef


def _mix_kernel(x_ref, yt_ref, k_ref, v_ref, g_ref, wc_ref, wq_ref, wg_ref, dw_ref, dwb_ref,
                lng_ref, lnb_ref, wpw_ref, wglu_ref, wo_ref, wout_ref, o_ref, vext_ref):
    ts = x_ref.shape[1]
    d = x_ref.shape[2]
    cw = dw_ref.shape[1]
    x = x_ref[0]
    h = _rms(x, g_ref[...]).astype(BF16)

    ci = jnp.dot(h, wc_ref[...], preferred_element_type=F32)
    v = ci[:, :cw] * _sigmoid(ci[:, cw:])

    @pl.when(pl.program_id(1) == 0)
    def _():
        vext_ref[0:CONV_HALO, :] = jnp.zeros((CONV_HALO, cw), F32)

    vext_ref[CONV_HALO:CONV_HALO + ts, :] = v
    acc = jnp.broadcast_to(dwb_ref[...], (ts, cw))
    for k in range(CONV_KERNEL):
        off = CONV_HALO - (CONV_KERNEL - 1) + k
        acc = acc + dw_ref[k:k + 1, :] * vext_ref[off:off + ts, :]
    vext_ref[0:CONV_HALO, :] = vext_ref[ts:ts + CONV_HALO, :]
    mu = jnp.mean(acc, axis=-1, keepdims=True)
    xc = acc - mu
    var = jnp.mean(xc * xc, axis=-1, keepdims=True)
    ln = xc * lax.rsqrt(var + EPS) * lng_ref[...] + lnb_ref[...]
    sw = ln * _sigmoid(ln)
    y_conv = jnp.dot(sw.astype(BF16), wpw_ref[...], preferred_element_type=F32)
    merged = _sigmoid(jnp.dot(h, wg_ref[:, 0:d], preferred_element_type=F32)) * y_conv

    gy = jax.nn.gelu(yt_ref[...]).astype(BF16)
    z = lax.dot_general(gy, wglu_ref[...], (((0,), (0,)), ((), ())), preferred_element_type=F32)
    y_ssm = z[:, :d] * _sigmoid(z[:, d:])
    merged = merged + _sigmoid(jnp.dot(h, wg_ref[:, d:2 * d], preferred_element_type=F32)) * y_ssm

    q = jnp.dot(h, wq_ref[...], preferred_element_type=F32)
    kk = k_ref[0]
    vv = v_ref[0]
    outs = []
    for hd in range(HEADS):
        sl = slice(hd * HEAD_DIM, (hd + 1) * HEAD_DIM)
        s = lax.dot_general(q[:, sl].astype(BF16), kk[:, sl], (((1,), (1,)), ((), ())),
                            preferred_element_type=F32) * (HEAD_DIM ** -0.5)
        p = jnp.exp(s - jnp.max(s, axis=-1, keepdims=True))
        den = jnp.sum(p, axis=-1, keepdims=True)
        o = jnp.dot(p.astype(BF16), vv[:, sl], preferred_element_type=F32) / den
        outs.append(o.astype(BF16))
    y_mem = jnp.dot(jnp.concatenate(outs, axis=1), wo_ref[...], preferred_element_type=F32)
    merged = merged + _sigmoid(jnp.dot(h, wg_ref[:, 2 * d:3 * d], preferred_element_type=F32)) * y_mem

    o_ref[0] = x + jnp.dot(merged.astype(BF16), wout_ref[...], preferred_element_type=F32)


def _mix_call(x, yt, kmem, vmem, g_mix, wc, wq, wg, dw, dwb, lng, lnb, wpw, wglu, wo, wout):
    b, s, d = x.shape
    ts = TOKEN_TILE
    nst = s // ts
    cw = dw.shape[1]
    m = kmem.shape[1]
    consts = [g_mix, wc, wq, wg, dw, dwb, lng, lnb, wpw, wglu, wo, wout]
    return pl.pallas_call(
        _mix_kernel,
        grid=(b, nst),
        in_specs=[pl.BlockSpec((1, ts, d), lambda i, j: (i, j, 0)),
                  pl.BlockSpec((yt.shape[0], ts), lambda i, j: (0, i * nst + j)),
                  pl.BlockSpec((1, m, kmem.shape[2]), lambda i, j: (i, 0, 0)),
                  pl.BlockSpec((1, m, vmem.shape[2]), lambda i, j: (i, 0, 0))]
                 + [_const_spec(c.shape) for c in consts],
        out_specs=pl.BlockSpec((1, ts, d), lambda i, j: (i, j, 0)),
        out_shape=jax.ShapeDtypeStruct((b, s, d), F32),
        scratch_shapes=[pltpu.VMEM((ts + CONV_HALO, cw), F32)],
        compiler_params=pltpu.CompilerParams(
            dimension_semantics=("arbitrary", "arbitrary"), vmem_limit_bytes=VMEM_LIMIT),
        name="mix",
    )(x, yt, kmem, vmem, *consts)


def _route_kernel(x_ref, g_ref, wr_ref, br_ref, xs_ref, info_ref, nch_ref):
    ts = x_ref.shape[0]
    cap = xs_ref.shape[0]
    h2 = _rms(x_ref[...], g_ref[...])
    logits = jnp.dot(h2, wr_ref[...], preferred_element_type=F32,
                     precision=lax.Precision.HIGHEST) + br_ref[...]
    lane_i = lax.broadcasted_iota(jnp.int32, (ts, LANES), 1)
    lane = lane_i.astype(F32)
    neg = jnp.float32(-1e30)
    big = jnp.float32(1e9)

    gmask = (lane_i >= N_EXPERTS) & (lane_i < N_EXPERTS + MOE_GROUPS)
    gmax = jnp.max(jnp.where(gmask, logits, neg), axis=-1, keepdims=True)
    gidx = jnp.min(jnp.where(gmask & (logits == gmax), lane, big), axis=-1, keepdims=True) - N_EXPERTS
    gsum = jnp.sum(jnp.where(gmask, jnp.exp(jnp.minimum(logits - gmax, 0.0)), 0.0), axis=-1, keepdims=True)
    p_top = 1.0 / gsum
    emask = (lane_i < N_EXPERTS) & (jnp.floor(lane * (1.0 / EXPERTS_PER_GROUP)) == gidx)
    m1 = jnp.max(jnp.where(emask, logits, neg), axis=-1, keepdims=True)
    i1 = jnp.min(jnp.where(emask & (logits == m1), lane, big), axis=-1, keepdims=True)
    emask2 = emask & (lane != i1)
    m2 = jnp.max(jnp.where(emask2, logits, neg), axis=-1, keepdims=True)
    i2 = jnp.min(jnp.where(emask2 & (logits == m2), lane, big), axis=-1, keepdims=True)
    r = jnp.exp(m2 - m1)
    w1 = p_top / (1.0 + r)
    w2 = p_top * r / (1.0 + r)

    sel1 = lane == i1
    sel2 = lane == i2
    occ = jnp.where(sel1 | sel2, 1.0, 0.0)
    tr = lax.broadcasted_iota(jnp.int32, (ts, ts), 0)
    tc = lax.broadcasted_iota(jnp.int32, (ts, ts), 1)
    before = jnp.where(tc < tr, 1.0, 0.0).astype(BF16)
    rank = jnp.dot(before, occ.astype(BF16), preferred_element_type=F32)
    cnt = jnp.sum(occ, axis=0, keepdims=True)
    nch = jnp.floor((cnt + (CHUNK_ROWS - 1)) * (1.0 / CHUNK_ROWS))
    er = lax.broadcasted_iota(jnp.int32, (LANES, LANES), 0)
    ec = lax.broadcasted_iota(jnp.int32, (LANES, LANES), 1)
    upper = jnp.where(er < ec, 1.0, 0.0).astype(BF16)
    start = jnp.dot(jnp.broadcast_to(nch, (8, LANES)).astype(BF16), upper,
                    preferred_element_type=F32)[0:1, :] * CHUNK_ROWS
    slot = start + rank
    pos1 = jnp.sum(jnp.where(sel1, slot, 0.0), axis=-1, keepdims=True)
    pos2 = jnp.sum(jnp.where(sel2, slot, 0.0), axis=-1, keepdims=True)

    rowid = lax.broadcasted_iota(jnp.int32, (ts, cap), 1)
    pt = jnp.where((rowid == pos1.astype(jnp.int32)) | (rowid == pos2.astype(jnp.int32)), 1.0, 0.0)
    xs = lax.dot_general(pt.astype(BF16), h2.astype(BF16), (((0,), (0,)), ((), ())),
                         preferred_element_type=F32)
    xs_ref[...] = xs.astype(BF16)
    info_ref[...] = jnp.where(lane_i == 0, pos1, jnp.where(lane_i == 1, pos2,
                              jnp.where(lane_i == 2, w1, jnp.where(lane_i == 3, w2, 0.0))))
    nch_ref[0] = nch


def _route_call(x1, g_ffn, wr, br, cap):
    t, d = x1.shape
    ts = TOKEN_TILE
    nt = t // ts
    return pl.pallas_call(
        _route_kernel,
        grid=(nt,),
        in_specs=[pl.BlockSpec((ts, d), lambda i: (i, 0)),
                  pl.BlockSpec((1, d), lambda i: (0, 0)),
                  pl.BlockSpec(wr.shape, lambda i: (0, 0)),
                  pl.BlockSpec((1, LANES), lambda i: (0, 0))],
        out_specs=[pl.BlockSpec((cap, d), lambda i: (i, 0)),
                   pl.BlockSpec((ts, LANES), lambda i: (i, 0)),
                   pl.BlockSpec((1, 1, LANES), lambda i: (i, 0, 0))],
        out_shape=[jax.ShapeDtypeStruct((nt * cap, d), BF16),
                   jax.ShapeDtypeStruct((t, LANES), F32),
                   jax.ShapeDtypeStruct((nt, 1, LANES), F32)],
        compiler_params=pltpu.CompilerParams(vmem_limit_bytes=VMEM_LIMIT),
        name="route",
    )(x1, g_ffn, wr, br)


def _expert_kernel(te_ref, nu_ref, nv_ref, src_ref, xs_ref, wg_ref, wu_ref, wd_ref, o_ref,
                   xbuf_ref, sem_ref):
    i = pl.program_id(0)
    n_used = nu_ref[0]
    slot = i % 2

    def chunk_copy(tile, c, buf):
        return pltpu.make_async_copy(
            xs_ref.at[_chunk_rows(src_ref[tile * CHUNKS_PER_ETILE + c])],
            xbuf_ref.at[buf, pl.ds(c * CHUNK_ROWS, CHUNK_ROWS)], sem_ref.at[buf])

    def gather(tile, buf):
        for c in range(CHUNKS_PER_ETILE):
            @pl.when(c < nv_ref[tile])
            def _():
                chunk_copy(tile, c, buf).start()

    @pl.when(i == 0)
    def _():
        gather(0, 0)

    @pl.when(i + 1 < n_used)
    def _():
        gather(i + 1, 1 - slot)

    @pl.when(i < n_used)
    def _():
        for c in range(CHUNKS_PER_ETILE):
            @pl.when(c < nv_ref[i])
            def _():
                chunk_copy(i, c, slot).wait()

            @pl.when(c >= nv_ref[i])
            def _():
                xbuf_ref[slot, c * CHUNK_ROWS:(c + 1) * CHUNK_ROWS, :] = jnp.zeros(
                    (CHUNK_ROWS, xbuf_ref.shape[2]), xbuf_ref.dtype)
        x = xbuf_ref[slot]
        gate = jnp.dot(x, wg_ref[0], preferred_element_type=F32)
        up = jnp.dot(x, wu_ref[0], preferred_element_type=F32)
        act = (gate * _sigmoid(gate) * up).astype(BF16)
        o_ref[...] = jnp.dot(act, wd_ref[0], preferred_element_type=F32).astype(BF16)

    @pl.when(i >= n_used)
    def _():
        o_ref[...] = jnp.zeros_like(o_ref)


def _expert_call(tile_expert, n_used, n_valid, chunk_src, xs, wg, wu, wd):
    d = xs.shape[1]
    de = wg.shape[2]
    tm = EXPERT_TILE
    nt = tile_expert.shape[0]

    def w_map(i, te, nu, nv, src):
        return (te[jnp.minimum(i, nu[0] - 1)], 0, 0)

    return pl.pallas_call(
        _expert_kernel,
        grid_spec=pltpu.PrefetchScalarGridSpec(
            num_scalar_prefetch=4,
            grid=(nt,),
            in_specs=[pl.BlockSpec(memory_space=pl.ANY),
                      pl.BlockSpec((1, d, de), w_map),
                      pl.BlockSpec((1, d, de), w_map),
                      pl.BlockSpec((1, de, d), w_map)],
            out_specs=pl.BlockSpec((tm, d), lambda i, te, nu, nv, src: (i, 0)),
            scratch_shapes=[pltpu.VMEM((2, tm, d), BF16),
                            pltpu.SemaphoreType.DMA((2,))]),
        out_shape=jax.ShapeDtypeStruct((nt * tm, d), BF16),
        compiler_params=pltpu.CompilerParams(dimension_semantics=("arbitrary",)),
        name="expert",
    )(tile_expert, n_used, n_valid, chunk_src, xs, wg, wu, wd)


def _combine_kernel(nt_ref, dst_ref, x_ref, info_ref, g_ref, ys_ref, o_ref, ybuf_ref, sem_ref):
    i = pl.program_id(0)
    n_tiles = pl.num_programs(0)
    ts = x_ref.shape[0]
    cap = ybuf_ref.shape[1]
    cpt = cap // CHUNK_ROWS
    slot = i % 2

    def chunk_copy(tile, k, buf):
        return pltpu.make_async_copy(ys_ref.at[_chunk_rows(dst_ref[tile * cpt + k])],
                                     ybuf_ref.at[buf, _chunk_rows(k)], sem_ref.at[buf])

    def gather(tile, buf):
        def body(k, carry):
            chunk_copy(tile, k, buf).start()
            return carry
        lax.fori_loop(0, nt_ref[tile], body, 0)

    @pl.when(i == 0)
    def _():
        gather(0, 0)

    @pl.when(i + 1 < n_tiles)
    def _():
        gather(i + 1, 1 - slot)

    def wait_body(k, carry):
        chunk_copy(i, k, slot).wait()
        return carry
    lax.fori_loop(0, nt_ref[i], wait_body, 0)

    def zero_body(k, carry):
        ybuf_ref[slot, _chunk_rows(k), :] = jnp.zeros((CHUNK_ROWS, ybuf_ref.shape[2]), ybuf_ref.dtype)
        return carry
    lax.fori_loop(nt_ref[i], cpt, zero_body, 0)

    info = info_ref[...]
    pos1 = info[:, 0:1].astype(jnp.int32)
    pos2 = info[:, 1:2].astype(jnp.int32)
    w1 = info[:, 2:3]
    w2 = info[:, 3:4]
    rowid = lax.broadcasted_iota(jnp.int32, (ts, cap), 1)
    ys = ybuf_ref[slot]
    y1 = jnp.dot(jnp.where(rowid == pos1, 1.0, 0.0).astype(BF16), ys, preferred_element_type=F32)
    y2 = jnp.dot(jnp.where(rowid == pos2, 1.0, 0.0).astype(BF16), ys, preferred_element_type=F32)
    o_ref[...] = _rms(x_ref[...] + w1 * y1 + w2 * y2, g_ref[...])


def _combine_call(ntot, chunk_dst, x1, info, g_final, ys_sorted, cap):
    t, d = x1.shape
    ts = TOKEN_TILE
    return pl.pallas_call(
        _combine_kernel,
        grid_spec=pltpu.PrefetchScalarGridSpec(
            num_scalar_prefetch=2,
            grid=(t // ts,),
            in_specs=[pl.BlockSpec((ts, d), lambda i, nt, dst: (i, 0)),
                      pl.BlockSpec((ts, LANES), lambda i, nt, dst: (i, 0)),
                      pl.BlockSpec((1, d), lambda i, nt, dst: (0, 0)),
                      pl.BlockSpec(memory_space=pl.ANY)],
            out_specs=pl.BlockSpec((ts, d), lambda i, nt, dst: (i, 0)),
            scratch_shapes=[pltpu.VMEM((2, cap, d), BF16),
                            pltpu.SemaphoreType.DMA((2,))]),
        out_shape=jax.ShapeDtypeStruct((t, d), F32),
        compiler_params=pltpu.CompilerParams(dimension_semantics=("arbitrary",),
                                             vmem_limit_bytes=VMEM_LIMIT),
        name="combine",
    )(ntot, chunk_dst, x1, info, g_final, ys_sorted)


def _run_tables(nch, cpt, n_etiles):
    n_tiles, n_exp = nch.shape
    per = CHUNKS_PER_ETILE
    tcum = jnp.cumsum(nch, axis=1)
    toff = tcum - nch
    ntot = tcum[:, -1]
    ecum = jnp.cumsum(nch, axis=0)
    etot = ecum[-1]
    eseg = -(-etot // per) * per
    segcum = jnp.cumsum(eseg)
    ebase = segcum - eseg
    eoff = ebase[None, :] + ecum - nch
    n_used = (segcum[-1] // per).reshape(1)

    k = jnp.arange(cpt, dtype=jnp.int32)
    e_of = jnp.minimum(jnp.sum(k[None, :, None] >= tcum[:, None, :], axis=-1), n_exp - 1)
    chunk_dst = jnp.take_along_axis(eoff, e_of, axis=1) + k[None, :] - jnp.take_along_axis(toff, e_of, axis=1)

    c = jnp.arange(n_etiles * per, dtype=jnp.int32)
    e_c = jnp.minimum(jnp.sum(c[:, None] >= segcum[None, :], axis=-1), n_exp - 1)
    q = c - ebase[e_c]
    cum_e = ecum.T[e_c]
    i_c = jnp.minimum(jnp.sum(q[:, None] >= cum_e, axis=-1), n_tiles - 1)
    excl = jnp.take_along_axis(cum_e, i_c[:, None], axis=1)[:, 0] - nch[i_c, e_c]
    chunk_src = jnp.where(q < etot[e_c], i_c * cpt + toff[i_c, e_c] + q - excl, 0)
    tile_expert = e_c[::per]
    n_valid = jnp.clip(etot[tile_expert] - q[::per], 0, per)
    i32 = lambda a: a.reshape(-1).astype(jnp.int32)
    return (i32(ntot), i32(chunk_dst), i32(tile_expert), i32(n_used), i32(n_valid), i32(chunk_src))


def _layer(x, mem, g_mix, w_in, conv_dw, conv_dw_bias, conv_ln_g, conv_ln_b, w_conv_out,
           lam_re, lam_im, log_dt, b_re, b_im, c_re, c_im, ssm_d, w_ssm_glu, g_mem, w_mem_kv,
           w_mem_out, w_out, g_ffn, w_rg, b_rg, w_re, b_re_, w_eg, w_eu, w_ed, g_final):
    b, s, d = x.shape
    t = b * s
    cw = conv_dw.shape[1]
    sw = ssm_d.shape[0]
    qw = w_mem_out.shape[0]
    n_groups = sw // SSM_GROUP
    row = lambda a: a.reshape(1, -1)

    o0, o1, o2 = 2 * cw, 2 * cw + sw, 2 * cw + sw + qw
    wc = w_in[:, :o0].astype(BF16)
    w_ssm_t = w_in[:, o0:o1].T.astype(BF16)
    wq = w_in[:, o1:o2].astype(BF16)
    wg = w_in[:, o2:].astype(BF16)

    kmem, vmem = _kv_call(mem, row(g_mem), w_mem_kv.astype(BF16))

    ut = _ut_call(x.reshape(t, d), row(g_mix), w_ssm_t)
    n_rows = t // SSM_BLOCK
    kt, wz, vy, acoef = _ssm_tables(lam_re, lam_im, log_dt, b_re, b_im, c_re, c_im, ssm_d)
    yt = _ssm_call(ut.reshape(n_groups, SSM_GROUP, n_rows, SSM_BLOCK), kt, wz, vy, acoef, b)
    yt = yt.reshape(sw, t)

    dw = jnp.concatenate([conv_dw, jnp.zeros((CONV_HALO - CONV_KERNEL, cw), F32)], axis=0)
    x1 = _mix_call(x, yt, kmem, vmem, row(g_mix), wc, wq, wg, dw, row(conv_dw_bias),
                   row(conv_ln_g), row(conv_ln_b), w_conv_out.astype(BF16),
                   w_ssm_glu.astype(BF16), w_mem_out.astype(BF16), w_out.astype(BF16))
    x1 = x1.reshape(t, d)

    pad = LANES - N_EXPERTS - MOE_GROUPS
    wr = jnp.concatenate([w_re, w_rg, jnp.zeros((d, pad), F32)], axis=1)
    br = jnp.concatenate([b_re_, b_rg, jnp.zeros((pad,), F32)]).reshape(1, LANES)
    n_tiles = t // TOKEN_TILE
    max_tile_chunks = (2 * TOKEN_TILE + N_EXPERTS * (CHUNK_ROWS - 1)) // CHUNK_ROWS
    cpt = -(-max_tile_chunks // 8) * 8
    cap = cpt * CHUNK_ROWS
    xs_tiles, info, nch_f = _route_call(x1, row(g_ffn), wr, br, cap)

    max_chunks = n_tiles * max_tile_chunks + N_EXPERTS * (CHUNKS_PER_ETILE - 1)
    n_etiles = -(-max_chunks // CHUNKS_PER_ETILE)
    nch = nch_f[:, 0, :N_EXPERTS].astype(jnp.int32)
    ntot, chunk_dst, tile_expert, n_used, n_valid, chunk_src = _run_tables(nch, cpt, n_etiles)

    ys_sorted = _expert_call(tile_expert, n_used, n_valid, chunk_src, xs_tiles,
                             w_eg.astype(BF16), w_eu.astype(BF16), w_ed.astype(BF16))
    out = _combine_call(ntot, chunk_dst, x1, info, row(g_final), ys_sorted, cap)
    return out.reshape(b, s, d)


def kernel(x, mem, g_mix, w_in, conv_dw, conv_dw_bias, conv_ln_g, conv_ln_b, w_conv_out, ssm_lambda_re, ssm_lambda_im, ssm_log_dt, ssm_b_re, ssm_b_im, ssm_c_re, ssm_c_im, ssm_d, w_ssm_glu, g_mem, w_mem_kv, w_mem_out, w_out, g_ffn, w_router_group, b_router_group, w_router_expert, b_router_expert, w_exp_gate, w_exp_up, w_exp_down, g_final):
    assert g_mix.shape[0] == 1, "the problem fixes one layer"
    l = 0
    return _layer(
        x, mem, g_mix[l], w_in[l], conv_dw[l], conv_dw_bias[l], conv_ln_g[l], conv_ln_b[l],
        w_conv_out[l], ssm_lambda_re[l], ssm_lambda_im[l], ssm_log_dt[l], ssm_b_re[l],
        ssm_b_im[l], ssm_c_re[l], ssm_c_im[l], ssm_d[l], w_ssm_glu[l], g_mem[l], w_mem_kv[l],
        w_mem_out[l], w_out[l], g_ffn[l], w_router_group[l], b_router_group[l],
        w_router_expert[l], b_router_expert[l], w_exp_gate[l], w_exp_up[l], w_exp_down[l], g_final)
```

```python
import functools

import jax
import jax.numpy as jnp
from jax import lax
from jax.experimental import pallas as pl
from jax.experimental.pallas import tpu as pltpu

F32 = jnp.float32
BF16 = jnp.bfloat16
EPS = 1e-6

LANES = 128
CHUNK_ROWS = 16
SSM_GROUP = 16
SSM_STATE = 64
SSM_BLOCK = LANES
CONV_KERNEL = 31
CONV_HALO = 32
HEADS = 4
HEAD_DIM = 128
MOE_GROUPS = 4
EXPERTS_PER_GROUP = 8
N_EXPERTS = MOE_GROUPS * EXPERTS_PER_GROUP
TOKEN_TILE = 512
EXPERT_TILE = 256
CHUNKS_PER_ETILE = EXPERT_TILE // CHUNK_ROWS
VMEM_LIMIT = 56 * 1024 * 1024


def _rms(x, g):
    return x * lax.rsqrt(jnp.mean(x * x, axis=-1, keepdims=True) + EPS) * g


def _sigmoid(x):
    return 0.5 * jnp.tanh(0.5 * x) + 0.5


def _const_spec(shape):
    zeros = (0,) * len(shape)
    return pl.BlockSpec(shape, lambda *_: zeros, pipeline_mode=pl.Buffered(1))


def _chunk_rows(chunk):
    return pl.ds(pl.multiple_of(chunk * CHUNK_ROWS, CHUNK_ROWS), CHUNK_ROWS)


def _kv_kernel(mem_ref, g_ref, w_ref, k_ref, v_ref):
    width = k_ref.shape[-1]
    mn = _rms(mem_ref[0], g_ref[...]).astype(BF16)
    kv = jnp.dot(mn, w_ref[...], preferred_element_type=F32)
    k_ref[0] = kv[:, :width].astype(BF16)
    v_ref[0] = kv[:, width:].astype(BF16)


def _kv_call(mem, g_mem, w_kv):
    b, m, d = mem.shape
    width = w_kv.shape[1] // 2
    return pl.pallas_call(
        _kv_kernel,
        grid=(b,),
        in_specs=[pl.BlockSpec((1, m, d), lambda i: (i, 0, 0)),
                  pl.BlockSpec((1, d), lambda i: (0, 0)),
                  pl.BlockSpec(w_kv.shape, lambda i: (0, 0))],
        out_specs=[pl.BlockSpec((1, m, width), lambda i: (i, 0, 0)),
                   pl.BlockSpec((1, m, width), lambda i: (i, 0, 0))],
        out_shape=[jax.ShapeDtypeStruct((b, m, width), BF16)] * 2,
        name="kv",
    )(mem, g_mem, w_kv)


def _ut_kernel(x_ref, g_ref, wt_ref, o_ref):
    h = _rms(x_ref[...], g_ref[...]).astype(BF16)
    ut = lax.dot_general(wt_ref[...], h, (((1,), (1,)), ((), ())), preferred_element_type=F32)
    ut = ut.reshape(ut.shape[0], ut.shape[1] // SSM_BLOCK, SSM_BLOCK)
    o_ref[...] = ut.reshape(o_ref.shape).astype(BF16)


def _ut_call(x2, g_mix, w_ssm_t):
    t, d = x2.shape
    c = w_ssm_t.shape[0]
    ts = 8 * SSM_BLOCK
    n_groups = c // SSM_GROUP
    return pl.pallas_call(
        _ut_kernel,
        grid=(t // ts,),
        in_specs=[pl.BlockSpec((ts, d), lambda i: (i, 0)),
                  pl.BlockSpec((1, d), lambda i: (0, 0)),
                  pl.BlockSpec((c, d), lambda i: (0, 0))],
        out_specs=pl.BlockSpec((n_groups, SSM_GROUP, ts // SSM_BLOCK, SSM_BLOCK),
                               lambda i: (0, 0, i, 0)),
        out_shape=jax.ShapeDtypeStruct((n_groups, SSM_GROUP, t // SSM_BLOCK, SSM_BLOCK), BF16),
        name="ut",
    )(x2, g_mix, w_ssm_t)


def _ssm_kernel(u_ref, kt_ref, w_ref, v_ref, a_ref, y_ref, acc_ref, z_ref, s_ref, *, n_batch):
    rows = u_ref.shape[2]
    n_blocks = rows // n_batch
    ri = lax.broadcasted_iota(jnp.int32, (SSM_BLOCK, SSM_BLOCK), 0)
    ci = lax.broadcasted_iota(jnp.int32, (SSM_BLOCK, SSM_BLOCK), 1)
    causal = ci >= ri

    acc_ref[...] = jnp.zeros_like(acc_ref)
    z_ref[...] = jnp.zeros_like(z_ref)

    def pair(cp, carry):
        slabs = []
        for half in range(2):
            kt = kt_ref[0, 2 * cp + half]
            blocks = []
            for c in range(SSM_GROUP):
                xb = jnp.broadcast_to(kt[c:c + 1, :], (SSM_BLOCK, SSM_BLOCK))
                toe = pltpu.roll(xb, 0, 1, stride=1, stride_axis=0)
                blocks.append(jnp.where(causal, toe, 0.0).astype(BF16))
            slabs.append(jnp.concatenate(blocks, axis=1))
        slab = jnp.concatenate(slabs, axis=0)
        x2 = jnp.concatenate([u_ref[0, 2 * cp], u_ref[0, 2 * cp + 1]], axis=1)
        acc_ref[...] += jnp.dot(x2, slab, preferred_element_type=F32)
        wrow = pl.multiple_of(cp * 2 * SSM_BLOCK, 2 * SSM_BLOCK)
        z_ref[...] += jnp.dot(x2, w_ref[0, pl.ds(wrow, 2 * SSM_BLOCK), :], preferred_element_type=F32)
        return carry

    lax.fori_loop(0, SSM_GROUP // 2, pair, 0)

    a_full = a_ref[0, 0:1, :]
    a_swap = a_ref[0, 1:2, :]
    st = jnp.zeros((n_batch, 2 * SSM_STATE), F32)
    for blk in range(n_blocks):
        sl = pl.ds(blk, n_batch, stride=n_blocks)
        s_ref[sl, :] = st
        st = a_full * st + a_swap * pltpu.roll(st, SSM_STATE, 1) + z_ref[sl, :]

    y = acc_ref[...] + jnp.dot(s_ref[...], v_ref[0], preferred_element_type=F32,
                               precision=lax.Precision.HIGHEST)
    y3 = jnp.stack([y[:, c * SSM_BLOCK:(c + 1) * SSM_BLOCK] for c in range(SSM_GROUP)], axis=0)
    y_ref[...] = y3.reshape(y_ref.shape)


def _ssm_call(u4, kt, wz, vy, acoef, n_batch):
    g, c, rows, blk = u4.shape
    width = c * blk
    return pl.pallas_call(
        functools.partial(_ssm_kernel, n_batch=n_batch),
        grid=(g,),
        in_specs=[pl.BlockSpec((1, c, rows, blk), lambda i: (i, 0, 0, 0)),
                  pl.BlockSpec((1, c, c, blk), lambda i: (i, 0, 0, 0)),
                  pl.BlockSpec((1, width, 2 * SSM_STATE), lambda i: (i, 0, 0)),
                  pl.BlockSpec((1, 2 * SSM_STATE, width), lambda i: (i, 0, 0)),
                  pl.BlockSpec((1, 2, 2 * SSM_STATE), lambda i: (i, 0, 0))],
        out_specs=pl.BlockSpec((c, rows * blk), lambda i: (i, 0)),
        out_shape=jax.ShapeDtypeStruct((g * c, rows * blk), F32),
        scratch_shapes=[pltpu.VMEM((rows, width), F32),
                        pltpu.VMEM((rows, 2 * SSM_STATE), F32),
                        pltpu.VMEM((rows, 2 * SSM_STATE), F32)],
        name="ssm",
    )(u4, kt, wz, vy, acoef)


def _ssm_tables(lam_re, lam_im, log_dt, b_re, b_im, c_re, c_im, d):
    hi = lax.Precision.HIGHEST
    g = lam_re.shape[0]
    dt = jnp.exp(log_dt)[:, None]
    er, ei = lam_re * dt, lam_im * dt
    k = jnp.arange(SSM_BLOCK + 1, dtype=F32)[None, :, None]
    mag = jnp.exp(k * er[:, None, :])
    ang = k * ei[:, None, :]
    pr, pi = mag * jnp.cos(ang), mag * jnp.sin(ang)
    nr, ni = pr[:, 1] - 1.0, pi[:, 1]
    den = lam_re * lam_re + lam_im * lam_im
    fr = (nr * lam_re + ni * lam_im) / den
    fi = (ni * lam_re - nr * lam_im) / den
    bbr = fr[:, :, None] * b_re - fi[:, :, None] * b_im
    bbi = fr[:, :, None] * b_im + fi[:, :, None] * b_re
    car = c_re[:, None] * pr[:, :, None, :] - c_im[:, None] * pi[:, :, None, :]
    cai = c_re[:, None] * pi[:, :, None, :] + c_im[:, None] * pr[:, :, None, :]
    kt = (jnp.einsum("gkcp,gpd->gdck", car[:, :SSM_BLOCK], bbr, precision=hi)
          - jnp.einsum("gkcp,gpd->gdck", cai[:, :SSM_BLOCK], bbi, precision=hi))
    dmat = jnp.eye(SSM_GROUP, dtype=F32)[None] * d.reshape(g, 1, SSM_GROUP)
    kt = kt.at[:, :, :, 0].add(dmat)
    prr, pir = pr[:, SSM_BLOCK - 1::-1][:, :SSM_BLOCK], pi[:, SSM_BLOCK - 1::-1][:, :SSM_BLOCK]
    wzr = prr[:, None, :, :] * jnp.swapaxes(bbr, 1, 2)[:, :, None, :] \
        - pir[:, None, :, :] * jnp.swapaxes(bbi, 1, 2)[:, :, None, :]
    wzi = prr[:, None, :, :] * jnp.swapaxes(bbi, 1, 2)[:, :, None, :] \
        + pir[:, None, :, :] * jnp.swapaxes(bbr, 1, 2)[:, :, None, :]
    wz = jnp.concatenate([wzr, wzi], axis=-1).reshape(g, SSM_GROUP * SSM_BLOCK, 2 * SSM_STATE)
    vr = jnp.transpose(car[:, 1:], (0, 3, 2, 1)).reshape(g, SSM_STATE, SSM_GROUP * SSM_BLOCK)
    vi = jnp.transpose(cai[:, 1:], (0, 3, 2, 1)).reshape(g, SSM_STATE, SSM_GROUP * SSM_BLOCK)
    vy = jnp.concatenate([vr, -vi], axis=1)
    ar, ai = pr[:, SSM_BLOCK], pi[:, SSM_BLOCK]
    acoef = jnp.stack([jnp.concatenate([ar, ar], -1), jnp.concatenate([-ai, ai], -1)], axis=1)
    return kt, wz.astype(BF16), vy, acoef


def _mix_kernel(x_ref, yt_ref, k_ref, v_ref, g_ref, wc_ref, wq_ref, wg_ref, dw_ref, dwb_ref,
                lng_ref, lnb_ref, wpw_ref, wglu_ref, wo_ref, wout_ref, o_ref, vext_ref, vsh_ref):
    ts = x_ref.shape[1]
    d = x_ref.shape[2]
    cw = dw_ref.shape[1]
    x = x_ref[0]
    h = _rms(x, g_ref[...]).astype(BF16)

    ci = jnp.dot(h, wc_ref[...], preferred_element_type=F32)
    v = ci[:, :cw] * _sigmoid(ci[:, cw:])

    @pl.when(pl.program_id(1) == 0)
    def _():
        vext_ref[0:CONV_HALO, :] = jnp.zeros((CONV_HALO, cw), F32)

    vext_ref[CONV_HALO:CONV_HALO + ts, :] = v
    acc = jnp.broadcast_to(dwb_ref[...], (ts, cw))
    first = CONV_HALO - (CONV_KERNEL - 1)
    for phase in range(8):
        offs = [first + k for k in range(CONV_KERNEL) if (first + k) % 8 == phase]
        if not offs:
            continue
        span = offs[-1] - offs[0] + ts
        vsh_ref[0:span, :] = vext_ref[offs[0]:offs[0] + span, :]
        for off in offs:
            acc = acc + dw_ref[off - first:off - first + 1, :] * vsh_ref[off - offs[0]:off - offs[0] + ts, :]
    vext_ref[0:CONV_HALO, :] = vext_ref[ts:ts + CONV_HALO, :]
    mu = jnp.mean(acc, axis=-1, keepdims=True)
    xc = acc - mu
    var = jnp.mean(xc * xc, axis=-1, keepdims=True)
    ln = xc * lax.rsqrt(var + EPS) * lng_ref[...] + lnb_ref[...]
    sw = ln * _sigmoid(ln)
    y_conv = jnp.dot(sw.astype(BF16), wpw_ref[...], preferred_element_type=F32)
    merged = _sigmoid(jnp.dot(h, wg_ref[:, 0:d], preferred_element_type=F32)) * y_conv

    gy = jax.nn.gelu(yt_ref[...]).astype(BF16)
    z = lax.dot_general(gy, wglu_ref[...], (((0,), (0,)), ((), ())), preferred_element_type=F32)
    y_ssm = z[:, :d] * _sigmoid(z[:, d:])
    merged = merged + _sigmoid(jnp.dot(h, wg_ref[:, d:2 * d], preferred_element_type=F32)) * y_ssm

    q = jnp.dot(h, wq_ref[...], preferred_element_type=F32)
    kk = k_ref[0]
    vv = v_ref[0]
    outs = []
    for hd in range(HEADS):
        sl = slice(hd * HEAD_DIM, (hd + 1) * HEAD_DIM)
        s = lax.dot_general(q[:, sl].astype(BF16), kk[:, sl], (((1,), (1,)), ((), ())),
                            preferred_element_type=F32) * (HEAD_DIM ** -0.5)
        p = jnp.exp(s - jnp.max(s, axis=-1, keepdims=True))
        den = jnp.sum(p, axis=-1, keepdims=True)
        o = jnp.dot(p.astype(BF16), vv[:, sl], preferred_element_type=F32) / den
        outs.append(o.astype(BF16))
    y_mem = jnp.dot(jnp.concatenate(outs, axis=1), wo_ref[...], preferred_element_type=F32)
    merged = merged + _sigmoid(jnp.dot(h, wg_ref[:, 2 * d:3 * d], preferred_element_type=F32)) * y_mem

    o_ref[0] = x + jnp.dot(merged.astype(BF16), wout_ref[...], preferred_element_type=F32)


def _mix_call(x, yt, kmem, vmem, g_mix, wc, wq, wg, dw, dwb, lng, lnb, wpw, wglu, wo, wout):
    b, s, d = x.shape
    ts = TOKEN_TILE
    nst = s // ts
    cw = dw.shape[1]
    m = kmem.shape[1]
    consts = [g_mix, wc, wq, wg, dw, dwb, lng, lnb, wpw, wglu, wo, wout]
    return pl.pallas_call(
        _mix_kernel,
        grid=(b, nst),
        in_specs=[pl.BlockSpec((1, ts, d), lambda i, j: (i, j, 0)),
                  pl.BlockSpec((yt.shape[0], ts), lambda i, j: (0, i * nst + j)),
                  pl.BlockSpec((1, m, kmem.shape[2]), lambda i, j: (i, 0, 0)),
                  pl.BlockSpec((1, m, vmem.shape[2]), lambda i, j: (i, 0, 0))]
                 + [_const_spec(c.shape) for c in consts],
        out_specs=pl.BlockSpec((1, ts, d), lambda i, j: (i, j, 0)),
        out_shape=jax.ShapeDtypeStruct((b, s, d), F32),
        scratch_shapes=[pltpu.VMEM((ts + CONV_HALO, cw), F32),
                        pltpu.VMEM((ts + CONV_HALO, cw), F32)],
        compiler_params=pltpu.CompilerParams(
            dimension_semantics=("arbitrary", "arbitrary"), vmem_limit_bytes=VMEM_LIMIT),
        name="mix",
    )(x, yt, kmem, vmem, *consts)


def _route_kernel(x_ref, g_ref, wr_ref, br_ref, xs_ref, info_ref, nch_ref):
    ts = x_ref.shape[0]
    cap = xs_ref.shape[0]
    h2 = _rms(x_ref[...], g_ref[...])
    logits = jnp.dot(h2, wr_ref[...], preferred_element_type=F32,
                     precision=lax.Precision.HIGHEST) + br_ref[...]
    lane_i = lax.broadcasted_iota(jnp.int32, (ts, LANES), 1)
    lane = lane_i.astype(F32)
    neg = jnp.float32(-1e30)
    big = jnp.float32(1e9)

    gmask = (lane_i >= N_EXPERTS) & (lane_i < N_EXPERTS + MOE_GROUPS)
    gmax = jnp.max(jnp.where(gmask, logits, neg), axis=-1, keepdims=True)
    gidx = jnp.min(jnp.where(gmask & (logits == gmax), lane, big), axis=-1, keepdims=True) - N_EXPERTS
    gsum = jnp.sum(jnp.where(gmask, jnp.exp(jnp.minimum(logits - gmax, 0.0)), 0.0), axis=-1, keepdims=True)
    p_top = 1.0 / gsum
    emask = (lane_i < N_EXPERTS) & (jnp.floor(lane * (1.0 / EXPERTS_PER_GROUP)) == gidx)
    m1 = jnp.max(jnp.where(emask, logits, neg), axis=-1, keepdims=True)
    i1 = jnp.min(jnp.where(emask & (logits == m1), lane, big), axis=-1, keepdims=True)
    emask2 = emask & (lane != i1)
    m2 = jnp.max(jnp.where(emask2, logits, neg), axis=-1, keepdims=True)
    i2 = jnp.min(jnp.where(emask2 & (logits == m2), lane, big), axis=-1, keepdims=True)
    r = jnp.exp(m2 - m1)
    w1 = p_top / (1.0 + r)
    w2 = p_top * r / (1.0 + r)

    sel1 = lane == i1
    sel2 = lane == i2
    occ = jnp.where(sel1 | sel2, 1.0, 0.0)
    tr = lax.broadcasted_iota(jnp.int32, (ts, ts), 0)
    tc = lax.broadcasted_iota(jnp.int32, (ts, ts), 1)
    before = jnp.where(tc < tr, 1.0, 0.0).astype(BF16)
    rank = jnp.dot(before, occ.astype(BF16), preferred_element_type=F32)
    cnt = jnp.sum(occ, axis=0, keepdims=True)
    nch = jnp.floor((cnt + (CHUNK_ROWS - 1)) * (1.0 / CHUNK_ROWS))
    er = lax.broadcasted_iota(jnp.int32, (LANES, LANES), 0)
    ec = lax.broadcasted_iota(jnp.int32, (LANES, LANES), 1)
    upper = jnp.where(er < ec, 1.0, 0.0).astype(BF16)
    start = jnp.dot(jnp.broadcast_to(nch, (8, LANES)).astype(BF16), upper,
                    preferred_element_type=F32)[0:1, :] * CHUNK_ROWS
    slot = start + rank
    pos1 = jnp.sum(jnp.where(sel1, slot, 0.0), axis=-1, keepdims=True)
    pos2 = jnp.sum(jnp.where(sel2, slot, 0.0), axis=-1, keepdims=True)

    rowid = lax.broadcasted_iota(jnp.int32, (ts, cap), 1)
    pt = jnp.where((rowid == pos1.astype(jnp.int32)) | (rowid == pos2.astype(jnp.int32)), 1.0, 0.0)
    xs = lax.dot_general(pt.astype(BF16), h2.astype(BF16), (((0,), (0,)), ((), ())),
                         preferred_element_type=F32)
    xs_ref[...] = xs.astype(BF16)
    info_ref[...] = jnp.where(lane_i == 0, pos1, jnp.where(lane_i == 1, pos2,
                              jnp.where(lane_i == 2, w1, jnp.where(lane_i == 3, w2, 0.0))))
    nch_ref[0] = nch


def _route_call(x1, g_ffn, wr, br, cap):
    t, d = x1.shape
    ts = TOKEN_TILE
    nt = t // ts
    return pl.pallas_call(
        _route_kernel,
        grid=(nt,),
        in_specs=[pl.BlockSpec((ts, d), lambda i: (i, 0)),
                  pl.BlockSpec((1, d), lambda i: (0, 0)),
                  pl.BlockSpec(wr.shape, lambda i: (0, 0)),
                  pl.BlockSpec((1, LANES), lambda i: (0, 0))],
        out_specs=[pl.BlockSpec((cap, d), lambda i: (i, 0)),
                   pl.BlockSpec((ts, LANES), lambda i: (i, 0)),
                   pl.BlockSpec((1, 1, LANES), lambda i: (i, 0, 0))],
        out_shape=[jax.ShapeDtypeStruct((nt * cap, d), BF16),
                   jax.ShapeDtypeStruct((t, LANES), F32),
                   jax.ShapeDtypeStruct((nt, 1, LANES), F32)],
        compiler_params=pltpu.CompilerParams(vmem_limit_bytes=VMEM_LIMIT),
        name="route",
    )(x1, g_ffn, wr, br)


def _expert_kernel(te_ref, nu_ref, nv_ref, src_ref, xs_ref, wg_ref, wu_ref, wd_ref, o_ref,
                   xbuf_ref, wgb_ref, wub_ref, wdb_ref, sem_ref):
    i = pl.program_id(0)
    n_used = nu_ref[0]
    slot = i % 2

    def chunk_copy(tile, c, buf):
        return pltpu.make_async_copy(
            xs_ref.at[_chunk_rows(src_ref[tile * CHUNKS_PER_ETILE + c])],
            xbuf_ref.at[buf, pl.ds(c * CHUNK_ROWS, CHUNK_ROWS)], sem_ref.at[buf])

    def gather(tile, buf):
        for c in range(CHUNKS_PER_ETILE):
            @pl.when(c < nv_ref[tile])
            def _():
                chunk_copy(tile, c, buf).start()

    @pl.when(i == 0)
    def _():
        gather(0, 0)

    @pl.when(i + 1 < n_used)
    def _():
        gather(i + 1, 1 - slot)

    @pl.when(i < n_used)
    def _():
        for c in range(CHUNKS_PER_ETILE):
            @pl.when(c < nv_ref[i])
            def _():
                chunk_copy(i, c, slot).wait()

            @pl.when(c >= nv_ref[i])
            def _():
                xbuf_ref[slot, c * CHUNK_ROWS:(c + 1) * CHUNK_ROWS, :] = jnp.zeros(
                    (CHUNK_ROWS, xbuf_ref.shape[2]), xbuf_ref.dtype)
        @pl.when((i == 0) | (te_ref[i] != te_ref[jnp.maximum(i - 1, 0)]))
        def _():
            wgb_ref[...] = wg_ref[0].astype(BF16)
            wub_ref[...] = wu_ref[0].astype(BF16)
            wdb_ref[...] = wd_ref[0].astype(BF16)

        x = xbuf_ref[slot]
        gate = jnp.dot(x, wgb_ref[...], preferred_element_type=F32)
        up = jnp.dot(x, wub_ref[...], preferred_element_type=F32)
        act = (gate * _sigmoid(gate) * up).astype(BF16)
        o_ref[...] = jnp.dot(act, wdb_ref[...], preferred_element_type=F32).astype(BF16)

    @pl.when(i >= n_used)
    def _():
        o_ref[...] = jnp.zeros_like(o_ref)


def _expert_call(tile_expert, n_used, n_valid, chunk_src, xs, wg, wu, wd):
    d = xs.shape[1]
    de = wg.shape[2]
    tm = EXPERT_TILE
    nt = tile_expert.shape[0]

    def w_map(i, te, nu, nv, src):
        return (te[jnp.maximum(jnp.minimum(i, nu[0] - 1), 0)], 0, 0)

    return pl.pallas_call(
        _expert_kernel,
        grid_spec=pltpu.PrefetchScalarGridSpec(
            num_scalar_prefetch=4,
            grid=(nt,),
            in_specs=[pl.BlockSpec(memory_space=pl.ANY),
                      pl.BlockSpec((1, d, de), w_map),
                      pl.BlockSpec((1, d, de), w_map),
                      pl.BlockSpec((1, de, d), w_map)],
            out_specs=pl.BlockSpec((tm, d), lambda i, te, nu, nv, src: (i, 0)),
            scratch_shapes=[pltpu.VMEM((2, tm, d), BF16),
                            pltpu.VMEM((d, de), BF16),
                            pltpu.VMEM((d, de), BF16),
                            pltpu.VMEM((de, d), BF16),
                            pltpu.SemaphoreType.DMA((2,))]),
        out_shape=jax.ShapeDtypeStruct((nt * tm, d), BF16),
        compiler_params=pltpu.CompilerParams(dimension_semantics=("arbitrary",)),
        name="expert",
    )(tile_expert, n_used, n_valid, chunk_src, xs, wg, wu, wd)


def _combine_kernel(nt_ref, dst_ref, x_ref, info_ref, g_ref, ys_ref, o_ref, ybuf_ref, sem_ref):
    i = pl.program_id(0)
    n_tiles = pl.num_programs(0)
    ts = x_ref.shape[0]
    cap = ybuf_ref.shape[1]
    cpt = cap // CHUNK_ROWS
    slot = i % 2

    def chunk_copy(tile, k, buf):
        return pltpu.make_async_copy(ys_ref.at[_chunk_rows(dst_ref[tile * cpt + k])],
                                     ybuf_ref.at[buf, _chunk_rows(k)], sem_ref.at[buf])

    def gather(tile, buf):
        def body(k, carry):
            chunk_copy(tile, k, buf).start()
            return carry
        lax.fori_loop(0, nt_ref[tile], body, 0)

    @pl.when(i == 0)
    def _():
        gather(0, 0)

    @pl.when(i + 1 < n_tiles)
    def _():
        gather(i + 1, 1 - slot)

    def wait_body(k, carry):
        chunk_copy(i, k, slot).wait()
        return carry
    lax.fori_loop(0, nt_ref[i], wait_body, 0)

    def zero_body(k, carry):
        ybuf_ref[slot, _chunk_rows(k), :] = jnp.zeros((CHUNK_ROWS, ybuf_ref.shape[2]), ybuf_ref.dtype)
        return carry
    lax.fori_loop(nt_ref[i], cpt, zero_body, 0)

    info = info_ref[...]
    pos1 = info[:, 0:1].astype(jnp.int32)
    pos2 = info[:, 1:2].astype(jnp.int32)
    w1 = info[:, 2:3]
    w2 = info[:, 3:4]
    rowid = lax.broadcasted_iota(jnp.int32, (ts, cap), 1)
    ys = ybuf_ref[slot]
    y1 = jnp.dot(jnp.where(rowid == pos1, 1.0, 0.0).astype(BF16), ys, preferred_element_type=F32)
    y2 = jnp.dot(jnp.where(rowid == pos2, 1.0, 0.0).astype(BF16), ys, preferred_element_type=F32)
    o_ref[...] = _rms(x_ref[...] + w1 * y1 + w2 * y2, g_ref[...])


def _combine_call(ntot, chunk_dst, x1, info, g_final, ys_sorted, cap):
    t, d = x1.shape
    ts = TOKEN_TILE
    return pl.pallas_call(
        _combine_kernel,
        grid_spec=pltpu.PrefetchScalarGridSpec(
            num_scalar_prefetch=2,
            grid=(t // ts,),
            in_specs=[pl.BlockSpec((ts, d), lambda i, nt, dst: (i, 0)),
                      pl.BlockSpec((ts, LANES), lambda i, nt, dst: (i, 0)),
                      pl.BlockSpec((1, d), lambda i, nt, dst: (0, 0)),
                      pl.BlockSpec(memory_space=pl.ANY)],
            out_specs=pl.BlockSpec((ts, d), lambda i, nt, dst: (i, 0)),
            scratch_shapes=[pltpu.VMEM((2, cap, d), BF16),
                            pltpu.SemaphoreType.DMA((2,))]),
        out_shape=jax.ShapeDtypeStruct((t, d), F32),
        compiler_params=pltpu.CompilerParams(dimension_semantics=("arbitrary",),
                                             vmem_limit_bytes=VMEM_LIMIT),
        name="combine",
    )(ntot, chunk_dst, x1, info, g_final, ys_sorted)


def _run_tables(nch, cpt, n_etiles):
    n_tiles, n_exp = nch.shape
    per = CHUNKS_PER_ETILE
    tcum = jnp.cumsum(nch, axis=1)
    toff = tcum - nch
    ntot = tcum[:, -1]
    ecum = jnp.cumsum(nch, axis=0)
    etot = ecum[-1]
    eseg = -(-etot // per) * per
    segcum = jnp.cumsum(eseg)
    ebase = segcum - eseg
    eoff = ebase[None, :] + ecum - nch
    n_used = (segcum[-1] // per).reshape(1)

    k = jnp.arange(cpt, dtype=jnp.int32)[None, :, None]
    in_run = (k >= toff[:, None, :]) & (k < tcum[:, None, :])
    chunk_dst = jnp.sum(jnp.where(in_run, (eoff - toff)[:, None, :] + k, 0), axis=-1)

    c = jnp.arange(n_etiles * per, dtype=jnp.int32)[:, None]
    lo = eoff.reshape(1, -1)
    hit = (c >= lo) & (c < lo + nch.reshape(1, -1))
    shift = (jnp.arange(n_tiles, dtype=jnp.int32)[:, None] * cpt + toff - eoff).reshape(1, -1)
    chunk_src = jnp.sum(jnp.where(hit, shift + c, 0), axis=-1)

    first = jnp.arange(n_etiles, dtype=jnp.int32)[:, None] * per
    owner = (first >= ebase[None, :]) & (first < segcum[None, :])
    tile_expert = jnp.sum(jnp.where(owner, jnp.arange(n_exp, dtype=jnp.int32)[None, :], 0), axis=-1)
    n_valid = jnp.sum(jnp.where(owner, jnp.clip(etot[None, :] - (first - ebase[None, :]), 0, per), 0), axis=-1)
    i32 = lambda a: a.reshape(-1).astype(jnp.int32)
    return (i32(ntot), i32(chunk_dst), i32(tile_expert), i32(n_used), i32(n_valid), i32(chunk_src))


def _layer(x, mem, g_mix, w_in, conv_dw, conv_dw_bias, conv_ln_g, conv_ln_b, w_conv_out,
           lam_re, lam_im, log_dt, b_re, b_im, c_re, c_im, ssm_d, w_ssm_glu, g_mem, w_mem_kv,
           w_mem_out, w_out, g_ffn, w_rg, b_rg, w_re, b_re_, w_eg, w_eu, w_ed, g_final):
    b, s, d = x.shape
    t = b * s
    cw = conv_dw.shape[1]
    sw = ssm_d.shape[0]
    qw = w_mem_out.shape[0]
    n_groups = sw // SSM_GROUP
    row = lambda a: a.reshape(1, -1)

    o0, o1, o2 = 2 * cw, 2 * cw + sw, 2 * cw + sw + qw
    wc = w_in[:, :o0].astype(BF16)
    w_ssm_t = w_in[:, o0:o1].T.astype(BF16)
    wq = w_in[:, o1:o2].astype(BF16)
    wg = w_in[:, o2:].astype(BF16)

    kmem, vmem = _kv_call(mem, row(g_mem), w_mem_kv.astype(BF16))

    ut = _ut_call(x.reshape(t, d), row(g_mix), w_ssm_t)
    kt, wz, vy, acoef = _ssm_tables(lam_re, lam_im, log_dt, b_re, b_im, c_re, c_im, ssm_d)
    yt = _ssm_call(ut, kt, wz, vy, acoef, b)

    dw = jnp.concatenate([conv_dw, jnp.zeros((CONV_HALO - CONV_KERNEL, cw), F32)], axis=0)
    x1 = _mix_call(x, yt, kmem, vmem, row(g_mix), wc, wq, wg, dw, row(conv_dw_bias),
                   row(conv_ln_g), row(conv_ln_b), w_conv_out.astype(BF16),
                   w_ssm_glu.astype(BF16), w_mem_out.astype(BF16), w_out.astype(BF16))
    x1 = x1.reshape(t, d)

    pad = LANES - N_EXPERTS - MOE_GROUPS
    wr = jnp.concatenate([w_re, w_rg, jnp.zeros((d, pad), F32)], axis=1)
    br = jnp.concatenate([b_re_, b_rg, jnp.zeros((pad,), F32)]).reshape(1, LANES)
    n_tiles = t // TOKEN_TILE
    max_tile_chunks = (2 * TOKEN_TILE + N_EXPERTS * (CHUNK_ROWS - 1)) // CHUNK_ROWS
    cpt = -(-max_tile_chunks // 8) * 8
    cap = cpt * CHUNK_ROWS
    xs_tiles, info, nch_f = _route_call(x1, row(g_ffn), wr, br, cap)

    max_chunks = n_tiles * max_tile_chunks + N_EXPERTS * (CHUNKS_PER_ETILE - 1)
    n_etiles = -(-max_chunks // CHUNKS_PER_ETILE)
    nch = nch_f[:, 0, :N_EXPERTS].astype(jnp.int32)
    ntot, chunk_dst, tile_expert, n_used, n_valid, chunk_src = _run_tables(nch, cpt, n_etiles)

    ys_sorted = _expert_call(tile_expert, n_used, n_valid, chunk_src, xs_tiles, w_eg, w_eu, w_ed)
    out = _combine_call(ntot, chunk_dst, x1, info, row(g_final), ys_sorted, cap)
    return out.reshape(b, s, d)


def kernel(x, mem, g_mix, w_in, conv_dw, conv_dw_bias, conv_ln_g, conv_ln_b, w_conv_out, ssm_lambda_re, ssm_lambda_im, ssm_log_dt, ssm_b_re, ssm_b_im, ssm_c_re, ssm_c_im, ssm_d, w_ssm_glu, g_mem, w_mem_kv, w_mem_out, w_out, g_ffn, w_router_group, b_router_group, w_router_expert, b_router_expert, w_exp_gate, w_exp_up, w_exp_down, g_final):
    assert g_mix.shape[0] == 1, "the problem fixes one layer"
    l = 0
    return _layer(
        x, mem, g_mix[l], w_in[l], conv_dw[l], conv_dw_bias[l], conv_ln_g[l], conv_ln_b[l],
        w_conv_out[l], ssm_lambda_re[l], ssm_lambda_im[l], ssm_log_dt[l], ssm_b_re[l],
        ssm_b_im[l], ssm_c_re[l], ssm_c_im[l], ssm_d[l], w_ssm_glu[l], g_mem[l], w_mem_kv[l],
        w_mem_out[l], w_out[l], g_ffn[l], w_router_group[l], b_router_group[l],
        w_router_expert[l], b_router_expert[l], w_exp_gate[l], w_exp_up[l], w_exp_down[l], g_final)
```

```python
import functools

import jax
import jax.numpy as jnp
from jax import lax
from jax.experimental import pallas as pl
from jax.experimental.pallas import tpu as pltpu

F32 = jnp.float32
BF16 = jnp.bfloat16
EPS = 1e-6

LANES = 128
CHUNK_ROWS = 16
SSM_GROUP = 16
SSM_STATE = 64
SSM_BLOCK = LANES
CONV_KERNEL = 31
CONV_HALO = 32
HEADS = 4
HEAD_DIM = 128
MOE_GROUPS = 4
EXPERTS_PER_GROUP = 8
N_EXPERTS = MOE_GROUPS * EXPERTS_PER_GROUP
TOKEN_TILE = 512
EXPERT_TILE = 512
CHUNKS_PER_ETILE = EXPERT_TILE // CHUNK_ROWS
VMEM_LIMIT = 56 * 1024 * 1024


def _rms(x, g):
    return x * lax.rsqrt(jnp.mean(x * x, axis=-1, keepdims=True) + EPS) * g


def _sigmoid(x):
    return 0.5 * jnp.tanh(0.5 * x) + 0.5


def _const_spec(shape):
    zeros = (0,) * len(shape)
    return pl.BlockSpec(shape, lambda *_: zeros, pipeline_mode=pl.Buffered(1))


def _chunk_rows(chunk):
    return pl.ds(pl.multiple_of(chunk * CHUNK_ROWS, CHUNK_ROWS), CHUNK_ROWS)


def _kv_kernel(mem_ref, g_ref, w_ref, k_ref, v_ref):
    width = k_ref.shape[-1]
    mn = _rms(mem_ref[0], g_ref[...]).astype(BF16)
    kv = jnp.dot(mn, w_ref[...], preferred_element_type=F32)
    k_ref[0] = kv[:, :width].astype(BF16)
    v_ref[0] = kv[:, width:].astype(BF16)


def _kv_call(mem, g_mem, w_kv):
    b, m, d = mem.shape
    width = w_kv.shape[1] // 2
    return pl.pallas_call(
        _kv_kernel,
        grid=(b,),
        in_specs=[pl.BlockSpec((1, m, d), lambda i: (i, 0, 0)),
                  pl.BlockSpec((1, d), lambda i: (0, 0)),
                  pl.BlockSpec(w_kv.shape, lambda i: (0, 0))],
        out_specs=[pl.BlockSpec((1, m, width), lambda i: (i, 0, 0)),
                   pl.BlockSpec((1, m, width), lambda i: (i, 0, 0))],
        out_shape=[jax.ShapeDtypeStruct((b, m, width), BF16)] * 2,
        name="kv",
    )(mem, g_mem, w_kv)


def _ut_kernel(x_ref, g_ref, wt_ref, o_ref):
    h = _rms(x_ref[...], g_ref[...]).astype(BF16)
    ut = lax.dot_general(wt_ref[...], h, (((1,), (1,)), ((), ())), preferred_element_type=F32)
    ut = ut.reshape(ut.shape[0], ut.shape[1] // SSM_BLOCK, SSM_BLOCK)
    o_ref[...] = ut.reshape(o_ref.shape).astype(BF16)


def _ut_call(x2, g_mix, w_ssm_t):
    t, d = x2.shape
    c = w_ssm_t.shape[0]
    ts = 8 * SSM_BLOCK
    n_groups = c // SSM_GROUP
    return pl.pallas_call(
        _ut_kernel,
        grid=(t // ts,),
        in_specs=[pl.BlockSpec((ts, d), lambda i: (i, 0)),
                  pl.BlockSpec((1, d), lambda i: (0, 0)),
                  pl.BlockSpec((c, d), lambda i: (0, 0))],
        out_specs=pl.BlockSpec((n_groups, SSM_GROUP, ts // SSM_BLOCK, SSM_BLOCK),
                               lambda i: (0, 0, i, 0)),
        out_shape=jax.ShapeDtypeStruct((n_groups, SSM_GROUP, t // SSM_BLOCK, SSM_BLOCK), BF16),
        name="ut",
    )(x2, g_mix, w_ssm_t)


def _ssm_kernel(u_ref, kt_ref, w_ref, v_ref, a_ref, y_ref, acc_ref, z_ref, s_ref, *, n_batch):
    rows = u_ref.shape[2]
    n_blocks = rows // n_batch
    ri = lax.broadcasted_iota(jnp.int32, (SSM_BLOCK, SSM_BLOCK), 0)
    ci = lax.broadcasted_iota(jnp.int32, (SSM_BLOCK, SSM_BLOCK), 1)
    causal = ci >= ri

    acc_ref[...] = jnp.zeros_like(acc_ref)
    z_ref[...] = jnp.zeros_like(z_ref)

    def pair(cp, carry):
        slabs = []
        for half in range(2):
            kt = kt_ref[0, 2 * cp + half]
            blocks = []
            for c in range(SSM_GROUP):
                xb = jnp.broadcast_to(kt[c:c + 1, :], (SSM_BLOCK, SSM_BLOCK))
                toe = pltpu.roll(xb, 0, 1, stride=1, stride_axis=0)
                blocks.append(jnp.where(causal, toe, 0.0).astype(BF16))
            slabs.append(jnp.concatenate(blocks, axis=1))
        slab = jnp.concatenate(slabs, axis=0)
        x2 = jnp.concatenate([u_ref[0, 2 * cp], u_ref[0, 2 * cp + 1]], axis=1)
        acc_ref[...] += jnp.dot(x2, slab, preferred_element_type=F32)
        wrow = pl.multiple_of(cp * 2 * SSM_BLOCK, 2 * SSM_BLOCK)
        z_ref[...] += jnp.dot(x2, w_ref[0, pl.ds(wrow, 2 * SSM_BLOCK), :], preferred_element_type=F32)
        return carry

    lax.fori_loop(0, SSM_GROUP // 2, pair, 0)

    a_full = a_ref[0, 0:1, :]
    a_swap = a_ref[0, 1:2, :]
    st = jnp.zeros((n_batch, 2 * SSM_STATE), F32)
    for blk in range(n_blocks):
        sl = pl.ds(blk, n_batch, stride=n_blocks)
        s_ref[sl, :] = st
        st = a_full * st + a_swap * pltpu.roll(st, SSM_STATE, 1) + z_ref[sl, :]

    y = acc_ref[...] + jnp.dot(s_ref[...], v_ref[0], preferred_element_type=F32,
                               precision=lax.Precision.HIGHEST)
    y3 = jnp.stack([y[:, c * SSM_BLOCK:(c + 1) * SSM_BLOCK] for c in range(SSM_GROUP)], axis=0)
    y_ref[...] = y3.reshape(y_ref.shape)


def _ssm_call(u4, kt, wz, vy, acoef, n_batch):
    g, c, rows, blk = u4.shape
    width = c * blk
    return pl.pallas_call(
        functools.partial(_ssm_kernel, n_batch=n_batch),
        grid=(g,),
        in_specs=[pl.BlockSpec((1, c, rows, blk), lambda i: (i, 0, 0, 0)),
                  pl.BlockSpec((1, c, c, blk), lambda i: (i, 0, 0, 0)),
                  pl.BlockSpec((1, width, 2 * SSM_STATE), lambda i: (i, 0, 0)),
                  pl.BlockSpec((1, 2 * SSM_STATE, width), lambda i: (i, 0, 0)),
                  pl.BlockSpec((1, 2, 2 * SSM_STATE), lambda i: (i, 0, 0))],
        out_specs=pl.BlockSpec((c, rows * blk), lambda i: (i, 0)),
        out_shape=jax.ShapeDtypeStruct((g * c, rows * blk), F32),
        scratch_shapes=[pltpu.VMEM((rows, width), F32),
                        pltpu.VMEM((rows, 2 * SSM_STATE), F32),
                        pltpu.VMEM((rows, 2 * SSM_STATE), F32)],
        name="ssm",
    )(u4, kt, wz, vy, acoef)


def _ssm_tables(lam_re, lam_im, log_dt, b_re, b_im, c_re, c_im, d):
    hi = lax.Precision.HIGHEST
    g = lam_re.shape[0]
    dt = jnp.exp(log_dt)[:, None]
    er, ei = lam_re * dt, lam_im * dt
    k = jnp.arange(SSM_BLOCK + 1, dtype=F32)[None, :, None]
    mag = jnp.exp(k * er[:, None, :])
    ang = k * ei[:, None, :]
    pr, pi = mag * jnp.cos(ang), mag * jnp.sin(ang)
    nr, ni = pr[:, 1] - 1.0, pi[:, 1]
    den = lam_re * lam_re + lam_im * lam_im
    fr = (nr * lam_re + ni * lam_im) / den
    fi = (ni * lam_re - nr * lam_im) / den
    bbr = fr[:, :, None] * b_re - fi[:, :, None] * b_im
    bbi = fr[:, :, None] * b_im + fi[:, :, None] * b_re
    car = c_re[:, None] * pr[:, :, None, :] - c_im[:, None] * pi[:, :, None, :]
    cai = c_re[:, None] * pi[:, :, None, :] + c_im[:, None] * pr[:, :, None, :]
    kt = (jnp.einsum("gkcp,gpd->gdck", car[:, :SSM_BLOCK], bbr, precision=hi)
          - jnp.einsum("gkcp,gpd->gdck", cai[:, :SSM_BLOCK], bbi, precision=hi))
    dmat = jnp.eye(SSM_GROUP, dtype=F32)[None] * d.reshape(g, 1, SSM_GROUP)
    kt = kt.at[:, :, :, 0].add(dmat)
    prr, pir = pr[:, SSM_BLOCK - 1::-1][:, :SSM_BLOCK], pi[:, SSM_BLOCK - 1::-1][:, :SSM_BLOCK]
    wzr = prr[:, None, :, :] * jnp.swapaxes(bbr, 1, 2)[:, :, None, :] \
        - pir[:, None, :, :] * jnp.swapaxes(bbi, 1, 2)[:, :, None, :]
    wzi = prr[:, None, :, :] * jnp.swapaxes(bbi, 1, 2)[:, :, None, :] \
        + pir[:, None, :, :] * jnp.swapaxes(bbr, 1, 2)[:, :, None, :]
    wz = jnp.concatenate([wzr, wzi], axis=-1).reshape(g, SSM_GROUP * SSM_BLOCK, 2 * SSM_STATE)
    vr = jnp.transpose(car[:, 1:], (0, 3, 2, 1)).reshape(g, SSM_STATE, SSM_GROUP * SSM_BLOCK)
    vi = jnp.transpose(cai[:, 1:], (0, 3, 2, 1)).reshape(g, SSM_STATE, SSM_GROUP * SSM_BLOCK)
    vy = jnp.concatenate([vr, -vi], axis=1)
    ar, ai = pr[:, SSM_BLOCK], pi[:, SSM_BLOCK]
    acoef = jnp.stack([jnp.concatenate([ar, ar], -1), jnp.concatenate([-ai, ai], -1)], axis=1)
    return kt, wz.astype(BF16), vy, acoef


def _mix_kernel(x_ref, yt_ref, k_ref, v_ref, g_ref, wc_ref, wq_ref, wg_ref, dw_ref, dwb_ref,
                lng_ref, lnb_ref, wpw_ref, wglu_ref, wo_ref, wout_ref, o_ref, vext_ref, vsh_ref):
    ts = x_ref.shape[1]
    d = x_ref.shape[2]
    cw = dw_ref.shape[1]
    x = x_ref[0]
    h = _rms(x, g_ref[...]).astype(BF16)

    ci = jnp.dot(h, wc_ref[...], preferred_element_type=F32)
    v = ci[:, :cw] * _sigmoid(ci[:, cw:])

    @pl.when(pl.program_id(1) == 0)
    def _():
        vext_ref[0:CONV_HALO, :] = jnp.zeros((CONV_HALO, cw), F32)

    vext_ref[CONV_HALO:CONV_HALO + ts, :] = v
    acc = jnp.broadcast_to(dwb_ref[...], (ts, cw))
    first = CONV_HALO - (CONV_KERNEL - 1)
    for phase in range(8):
        offs = [first + k for k in range(CONV_KERNEL) if (first + k) % 8 == phase]
        if not offs:
            continue
        span = offs[-1] - offs[0] + ts
        vsh_ref[0:span, :] = vext_ref[offs[0]:offs[0] + span, :]
        for off in offs:
            acc = acc + dw_ref[off - first:off - first + 1, :] * vsh_ref[off - offs[0]:off - offs[0] + ts, :]
    vext_ref[0:CONV_HALO, :] = vext_ref[ts:ts + CONV_HALO, :]
    mu = jnp.mean(acc, axis=-1, keepdims=True)
    xc = acc - mu
    var = jnp.mean(xc * xc, axis=-1, keepdims=True)
    ln = xc * lax.rsqrt(var + EPS) * lng_ref[...] + lnb_ref[...]
    sw = ln * _sigmoid(ln)
    y_conv = jnp.dot(sw.astype(BF16), wpw_ref[...], preferred_element_type=F32)
    merged = _sigmoid(jnp.dot(h, wg_ref[:, 0:d], preferred_element_type=F32)) * y_conv

    gy = jax.nn.gelu(yt_ref[...]).astype(BF16)
    z = lax.dot_general(gy, wglu_ref[...], (((0,), (0,)), ((), ())), preferred_element_type=F32)
    y_ssm = z[:, :d] * _sigmoid(z[:, d:])
    merged = merged + _sigmoid(jnp.dot(h, wg_ref[:, d:2 * d], preferred_element_type=F32)) * y_ssm

    q = jnp.dot(h, wq_ref[...], preferred_element_type=F32)
    kk = k_ref[0]
    vv = v_ref[0]
    outs = []
    for hd in range(HEADS):
        sl = slice(hd * HEAD_DIM, (hd + 1) * HEAD_DIM)
        s = lax.dot_general(q[:, sl].astype(BF16), kk[:, sl], (((1,), (1,)), ((), ())),
                            preferred_element_type=F32) * (HEAD_DIM ** -0.5)
        p = jnp.exp(s - jnp.max(s, axis=-1, keepdims=True))
        den = jnp.sum(p, axis=-1, keepdims=True)
        o = jnp.dot(p.astype(BF16), vv[:, sl], preferred_element_type=F32) / den
        outs.append(o.astype(BF16))
    y_mem = jnp.dot(jnp.concatenate(outs, axis=1), wo_ref[...], preferred_element_type=F32)
    merged = merged + _sigmoid(jnp.dot(h, wg_ref[:, 2 * d:3 * d], preferred_element_type=F32)) * y_mem

    o_ref[0] = x + jnp.dot(merged.astype(BF16), wout_ref[...], preferred_element_type=F32)


def _mix_call(x, yt, kmem, vmem, g_mix, wc, wq, wg, dw, dwb, lng, lnb, wpw, wglu, wo, wout):
    b, s, d = x.shape
    ts = TOKEN_TILE
    nst = s // ts
    cw = dw.shape[1]
    m = kmem.shape[1]
    consts = [g_mix, wc, wq, wg, dw, dwb, lng, lnb, wpw, wglu, wo, wout]
    return pl.pallas_call(
        _mix_kernel,
        grid=(b, nst),
        in_specs=[pl.BlockSpec((1, ts, d), lambda i, j: (i, j, 0)),
                  pl.BlockSpec((yt.shape[0], ts), lambda i, j: (0, i * nst + j)),
                  pl.BlockSpec((1, m, kmem.shape[2]), lambda i, j: (i, 0, 0)),
                  pl.BlockSpec((1, m, vmem.shape[2]), lambda i, j: (i, 0, 0))]
                 + [_const_spec(c.shape) for c in consts],
        out_specs=pl.BlockSpec((1, ts, d), lambda i, j: (i, j, 0)),
        out_shape=jax.ShapeDtypeStruct((b, s, d), F32),
        scratch_shapes=[pltpu.VMEM((ts + CONV_HALO, cw), F32),
                        pltpu.VMEM((ts + CONV_HALO, cw), F32)],
        compiler_params=pltpu.CompilerParams(
            dimension_semantics=("arbitrary", "arbitrary"), vmem_limit_bytes=VMEM_LIMIT),
        name="mix",
    )(x, yt, kmem, vmem, *consts)


def _route_kernel(x_ref, g_ref, wr_ref, br_ref, xs_ref, info_ref, nch_ref):
    ts = x_ref.shape[0]
    cap = xs_ref.shape[0]
    h2 = _rms(x_ref[...], g_ref[...])
    hb = h2.astype(BF16)
    hl = (h2 - hb.astype(F32)).astype(BF16)
    wr = wr_ref[...]
    wb = wr.astype(BF16)
    wl = (wr - wb.astype(F32)).astype(BF16)
    logits = (jnp.dot(hb, wb, preferred_element_type=F32) + jnp.dot(hl, wb, preferred_element_type=F32)
              + jnp.dot(hb, wl, preferred_element_type=F32)) + br_ref[...]
    lt = logits.T
    le = lt[0:N_EXPERTS]
    lg = lt[N_EXPERTS:N_EXPERTS + 8]
    neg = jnp.float32(-1e30)
    big = jnp.float32(1e9)
    g_f = lax.broadcasted_iota(jnp.int32, (8, ts), 0).astype(F32)
    e_f = lax.broadcasted_iota(jnp.int32, (N_EXPERTS, ts), 0).astype(F32)

    gmask = g_f < MOE_GROUPS
    gmax = jnp.max(jnp.where(gmask, lg, neg), axis=0, keepdims=True)
    gidx = jnp.min(jnp.where(gmask & (lg == gmax), g_f, big), axis=0, keepdims=True)
    gsum = jnp.sum(jnp.where(gmask, jnp.exp(jnp.minimum(lg - gmax, 0.0)), 0.0), axis=0, keepdims=True)
    p_top = 1.0 / gsum
    emask = jnp.floor(e_f * (1.0 / EXPERTS_PER_GROUP)) == gidx
    m1 = jnp.max(jnp.where(emask, le, neg), axis=0, keepdims=True)
    i1 = jnp.min(jnp.where(emask & (le == m1), e_f, big), axis=0, keepdims=True)
    emask2 = emask & (e_f != i1)
    m2 = jnp.max(jnp.where(emask2, le, neg), axis=0, keepdims=True)
    i2 = jnp.min(jnp.where(emask2 & (le == m2), e_f, big), axis=0, keepdims=True)
    r = jnp.exp(m2 - m1)
    w1 = p_top / (1.0 + r)
    w2 = p_top * r / (1.0 + r)

    sel1 = e_f == i1
    sel2 = e_f == i2
    occ = jnp.where(sel1 | sel2, 1.0, 0.0)
    tr = lax.broadcasted_iota(jnp.int32, (ts, ts), 0)
    tc = lax.broadcasted_iota(jnp.int32, (ts, ts), 1)
    earlier = jnp.where(tr < tc, 1.0, 0.0).astype(BF16)
    rank = jnp.dot(occ.astype(BF16), earlier, preferred_element_type=F32)
    cnt = jnp.sum(occ, axis=1, keepdims=True)
    nch = jnp.floor((cnt + (CHUNK_ROWS - 1)) * (1.0 / CHUNK_ROWS))
    er = lax.broadcasted_iota(jnp.int32, (N_EXPERTS, N_EXPERTS), 0)
    ec = lax.broadcasted_iota(jnp.int32, (N_EXPERTS, N_EXPERTS), 1)
    lower = jnp.where(ec < er, 1.0, 0.0).astype(BF16)
    nch_b = jnp.broadcast_to(nch, (N_EXPERTS, LANES))
    start = jnp.dot(lower, nch_b.astype(BF16), preferred_element_type=F32)[:, 0:1] * CHUNK_ROWS
    slot = start + rank
    pos1 = jnp.sum(jnp.where(sel1, slot, 0.0), axis=0, keepdims=True)
    pos2 = jnp.sum(jnp.where(sel2, slot, 0.0), axis=0, keepdims=True)

    rowid = lax.broadcasted_iota(jnp.int32, (cap, ts), 0)
    p = jnp.where((rowid == pos1.astype(jnp.int32)) | (rowid == pos2.astype(jnp.int32)), 1.0, 0.0)
    xs_ref[...] = jnp.dot(p.astype(BF16), hb, preferred_element_type=F32).astype(BF16)
    sub = lax.broadcasted_iota(jnp.int32, (LANES, ts), 0)
    info_t = jnp.where(sub == 0, pos1, jnp.where(sub == 1, pos2,
                       jnp.where(sub == 2, w1, jnp.where(sub == 3, w2, 0.0))))
    info_ref[...] = info_t.T
    nch_ref[0] = jnp.concatenate([nch_b, jnp.zeros((LANES - N_EXPERTS, LANES), F32)], axis=0).T[0:8, :]


def _route_call(x1, g_ffn, wr, br, cap):
    t, d = x1.shape
    ts = TOKEN_TILE
    nt = t // ts
    return pl.pallas_call(
        _route_kernel,
        grid=(nt,),
        in_specs=[pl.BlockSpec((ts, d), lambda i: (i, 0)),
                  pl.BlockSpec((1, d), lambda i: (0, 0)),
                  pl.BlockSpec(wr.shape, lambda i: (0, 0)),
                  pl.BlockSpec((1, LANES), lambda i: (0, 0))],
        out_specs=[pl.BlockSpec((cap, d), lambda i: (i, 0)),
                   pl.BlockSpec((ts, LANES), lambda i: (i, 0)),
                   pl.BlockSpec((1, 8, LANES), lambda i: (i, 0, 0))],
        out_shape=[jax.ShapeDtypeStruct((nt * cap, d), BF16),
                   jax.ShapeDtypeStruct((t, LANES), F32),
                   jax.ShapeDtypeStruct((nt, 8, LANES), F32)],
        compiler_params=pltpu.CompilerParams(vmem_limit_bytes=VMEM_LIMIT),
        name="route",
    )(x1, g_ffn, wr, br)


def _expert_kernel(te_ref, nu_ref, nv_ref, src_ref, xs_ref, wg_ref, wu_ref, wd_ref, o_ref,
                   xbuf_ref, wgb_ref, wub_ref, wdb_ref, sem_ref):
    i = pl.program_id(0)
    n_used = nu_ref[0]
    slot = i % 2

    def chunk_copy(tile, c, buf):
        return pltpu.make_async_copy(
            xs_ref.at[_chunk_rows(src_ref[tile * CHUNKS_PER_ETILE + c])],
            xbuf_ref.at[buf, pl.ds(c * CHUNK_ROWS, CHUNK_ROWS)], sem_ref.at[buf])

    def gather(tile, buf):
        for c in range(CHUNKS_PER_ETILE):
            @pl.when(c < nv_ref[tile])
            def _():
                chunk_copy(tile, c, buf).start()

    @pl.when(i == 0)
    def _():
        gather(0, 0)

    @pl.when(i + 1 < n_used)
    def _():
        gather(i + 1, 1 - slot)

    @pl.when(i < n_used)
    def _():
        for c in range(CHUNKS_PER_ETILE):
            @pl.when(c < nv_ref[i])
            def _():
                chunk_copy(i, c, slot).wait()

            @pl.when(c >= nv_ref[i])
            def _():
                xbuf_ref[slot, c * CHUNK_ROWS:(c + 1) * CHUNK_ROWS, :] = jnp.zeros(
                    (CHUNK_ROWS, xbuf_ref.shape[2]), xbuf_ref.dtype)
        @pl.when((i == 0) | (te_ref[i] != te_ref[jnp.maximum(i - 1, 0)]))
        def _():
            wgb_ref[...] = wg_ref[0].astype(BF16)
            wub_ref[...] = wu_ref[0].astype(BF16)
            wdb_ref[...] = wd_ref[0].astype(BF16)

        x = xbuf_ref[slot]
        gate = jnp.dot(x, wgb_ref[...], preferred_element_type=F32)
        up = jnp.dot(x, wub_ref[...], preferred_element_type=F32)
        act = (gate * _sigmoid(gate) * up).astype(BF16)
        o_ref[...] = jnp.dot(act, wdb_ref[...], preferred_element_type=F32).astype(BF16)

    @pl.when(i >= n_used)
    def _():
        o_ref[...] = jnp.zeros_like(o_ref)


def _expert_call(tile_expert, n_used, n_valid, chunk_src, xs, wg, wu, wd):
    d = xs.shape[1]
    de = wg.shape[2]
    tm = EXPERT_TILE
    nt = tile_expert.shape[0]

    def w_map(i, te, nu, nv, src):
        return (te[jnp.maximum(jnp.minimum(i, nu[0] - 1), 0)], 0, 0)

    return pl.pallas_call(
        _expert_kernel,
        grid_spec=pltpu.PrefetchScalarGridSpec(
            num_scalar_prefetch=4,
            grid=(nt,),
            in_specs=[pl.BlockSpec(memory_space=pl.ANY),
                      pl.BlockSpec((1, d, de), w_map),
                      pl.BlockSpec((1, d, de), w_map),
                      pl.BlockSpec((1, de, d), w_map)],
            out_specs=pl.BlockSpec((tm, d), lambda i, te, nu, nv, src: (i, 0)),
            scratch_shapes=[pltpu.VMEM((2, tm, d), BF16),
                            pltpu.VMEM((d, de), BF16),
                            pltpu.VMEM((d, de), BF16),
                            pltpu.VMEM((de, d), BF16),
                            pltpu.SemaphoreType.DMA((2,))]),
        out_shape=jax.ShapeDtypeStruct((nt * tm, d), BF16),
        compiler_params=pltpu.CompilerParams(dimension_semantics=("arbitrary",)),
        name="expert",
    )(tile_expert, n_used, n_valid, chunk_src, xs, wg, wu, wd)


def _combine_kernel(nt_ref, dst_ref, x_ref, info_ref, g_ref, ys_ref, o_ref, ybuf_ref, sem_ref):
    i = pl.program_id(0)
    n_tiles = pl.num_programs(0)
    ts = x_ref.shape[0]
    cap = ybuf_ref.shape[1]
    cpt = cap // CHUNK_ROWS
    slot = i % 2

    def chunk_copy(tile, k, buf):
        return pltpu.make_async_copy(ys_ref.at[_chunk_rows(dst_ref[tile * cpt + k])],
                                     ybuf_ref.at[buf, _chunk_rows(k)], sem_ref.at[buf])

    def gather(tile, buf):
        def body(k, carry):
            chunk_copy(tile, k, buf).start()
            return carry
        lax.fori_loop(0, nt_ref[tile], body, 0)

    @pl.when(i == 0)
    def _():
        gather(0, 0)

    @pl.when(i + 1 < n_tiles)
    def _():
        gather(i + 1, 1 - slot)

    def wait_body(k, carry):
        chunk_copy(i, k, slot).wait()
        return carry
    lax.fori_loop(0, nt_ref[i], wait_body, 0)

    def zero_body(k, carry):
        ybuf_ref[slot, _chunk_rows(k), :] = jnp.zeros((CHUNK_ROWS, ybuf_ref.shape[2]), ybuf_ref.dtype)
        return carry
    lax.fori_loop(nt_ref[i], cpt, zero_body, 0)

    info = info_ref[...]
    pos1 = info[:, 0:1].astype(jnp.int32)
    pos2 = info[:, 1:2].astype(jnp.int32)
    w1 = info[:, 2:3]
    w2 = info[:, 3:4]
    rowid = lax.broadcasted_iota(jnp.int32, (ts, cap), 1)
    ys = ybuf_ref[slot]
    pw = jnp.where(rowid == pos1, w1, jnp.where(rowid == pos2, w2, 0.0)).astype(BF16)
    y = jnp.dot(pw, ys, preferred_element_type=F32)
    o_ref[...] = _rms(x_ref[...] + y, g_ref[...])


def _combine_call(ntot, chunk_dst, x1, info, g_final, ys_sorted, cap):
    t, d = x1.shape
    ts = TOKEN_TILE
    return pl.pallas_call(
        _combine_kernel,
        grid_spec=pltpu.PrefetchScalarGridSpec(
            num_scalar_prefetch=2,
            grid=(t // ts,),
            in_specs=[pl.BlockSpec((ts, d), lambda i, nt, dst: (i, 0)),
                      pl.BlockSpec((ts, LANES), lambda i, nt, dst: (i, 0)),
                      pl.BlockSpec((1, d), lambda i, nt, dst: (0, 0)),
                      pl.BlockSpec(memory_space=pl.ANY)],
            out_specs=pl.BlockSpec((ts, d), lambda i, nt, dst: (i, 0)),
            scratch_shapes=[pltpu.VMEM((2, cap, d), BF16),
                            pltpu.SemaphoreType.DMA((2,))]),
        out_shape=jax.ShapeDtypeStruct((t, d), F32),
        compiler_params=pltpu.CompilerParams(dimension_semantics=("arbitrary",),
                                             vmem_limit_bytes=VMEM_LIMIT),
        name="combine",
    )(ntot, chunk_dst, x1, info, g_final, ys_sorted)


def _run_tables(nch, cpt, n_etiles):
    n_tiles, n_exp = nch.shape
    per = CHUNKS_PER_ETILE
    tcum = jnp.cumsum(nch, axis=1)
    toff = tcum - nch
    ntot = tcum[:, -1]
    ecum = jnp.cumsum(nch, axis=0)
    etot = ecum[-1]
    eseg = -(-etot // per) * per
    segcum = jnp.cumsum(eseg)
    ebase = segcum - eseg
    eoff = ebase[None, :] + ecum - nch
    n_used = (segcum[-1] // per).reshape(1)

    k = jnp.arange(cpt, dtype=jnp.int32)[None, :, None]
    in_run = (k >= toff[:, None, :]) & (k < tcum[:, None, :])
    chunk_dst = jnp.sum(jnp.where(in_run, (eoff - toff)[:, None, :] + k, 0), axis=-1)

    c = jnp.arange(n_etiles * per, dtype=jnp.int32)[:, None]
    lo = eoff.reshape(1, -1)
    hit = (c >= lo) & (c < lo + nch.reshape(1, -1))
    shift = (jnp.arange(n_tiles, dtype=jnp.int32)[:, None] * cpt + toff - eoff).reshape(1, -1)
    chunk_src = jnp.sum(jnp.where(hit, shift + c, 0), axis=-1)

    first = jnp.arange(n_etiles, dtype=jnp.int32)[:, None] * per
    owner = (first >= ebase[None, :]) & (first < segcum[None, :])
    tile_expert = jnp.sum(jnp.where(owner, jnp.arange(n_exp, dtype=jnp.int32)[None, :], 0), axis=-1)
    n_valid = jnp.sum(jnp.where(owner, jnp.clip(etot[None, :] - (first - ebase[None, :]), 0, per), 0), axis=-1)
    i32 = lambda a: a.reshape(-1).astype(jnp.int32)
    return (i32(ntot), i32(chunk_dst), i32(tile_expert), i32(n_used), i32(n_valid), i32(chunk_src))


def _layer(x, mem, g_mix, w_in, conv_dw, conv_dw_bias, conv_ln_g, conv_ln_b, w_conv_out,
           lam_re, lam_im, log_dt, b_re, b_im, c_re, c_im, ssm_d, w_ssm_glu, g_mem, w_mem_kv,
           w_mem_out, w_out, g_ffn, w_rg, b_rg, w_re, b_re_, w_eg, w_eu, w_ed, g_final):
    b, s, d = x.shape
    t = b * s
    cw = conv_dw.shape[1]
    sw = ssm_d.shape[0]
    qw = w_mem_out.shape[0]
    n_groups = sw // SSM_GROUP
    row = lambda a: a.reshape(1, -1)

    o0, o1, o2 = 2 * cw, 2 * cw + sw, 2 * cw + sw + qw
    wc = w_in[:, :o0].astype(BF16)
    w_ssm_t = w_in[:, o0:o1].T.astype(BF16)
    wq = w_in[:, o1:o2].astype(BF16)
    wg = w_in[:, o2:].astype(BF16)

    kmem, vmem = _kv_call(mem, row(g_mem), w_mem_kv.astype(BF16))

    ut = _ut_call(x.reshape(t, d), row(g_mix), w_ssm_t)
    kt, wz, vy, acoef = _ssm_tables(lam_re, lam_im, log_dt, b_re, b_im, c_re, c_im, ssm_d)
    yt = _ssm_call(ut, kt, wz, vy, acoef, b)

    dw = jnp.concatenate([conv_dw, jnp.zeros((CONV_HALO - CONV_KERNEL, cw), F32)], axis=0)
    x1 = _mix_call(x, yt, kmem, vmem, row(g_mix), wc, wq, wg, dw, row(conv_dw_bias),
                   row(conv_ln_g), row(conv_ln_b), w_conv_out.astype(BF16),
                   w_ssm_glu.astype(BF16), w_mem_out.astype(BF16), w_out.astype(BF16))
    x1 = x1.reshape(t, d)

    pad = LANES - N_EXPERTS - MOE_GROUPS
    wr = jnp.concatenate([w_re, w_rg, jnp.zeros((d, pad), F32)], axis=1)
    br = jnp.concatenate([b_re_, b_rg, jnp.zeros((pad,), F32)]).reshape(1, LANES)
    n_tiles = t // TOKEN_TILE
    max_tile_chunks = (2 * TOKEN_TILE + N_EXPERTS * (CHUNK_ROWS - 1)) // CHUNK_ROWS
    cpt = -(-max_tile_chunks // 8) * 8
    cap = cpt * CHUNK_ROWS
    xs_tiles, info, nch_f = _route_call(x1, row(g_ffn), wr, br, cap)

    max_chunks = n_tiles * max_tile_chunks + N_EXPERTS * (CHUNKS_PER_ETILE - 1)
    n_etiles = -(-max_chunks // CHUNKS_PER_ETILE)
    nch = nch_f[:, 0, :N_EXPERTS].astype(jnp.int32)
    ntot, chunk_dst, tile_expert, n_used, n_valid, chunk_src = _run_tables(nch, cpt, n_etiles)

    ys_sorted = _expert_call(tile_expert, n_used, n_valid, chunk_src, xs_tiles, w_eg, w_eu, w_ed)
    out = _combine_call(ntot, chunk_dst, x1, info, row(g_final), ys_sorted, cap)
    return out.reshape(b, s, d)


def kernel(x, mem, g_mix, w_in, conv_dw, conv_dw_bias, conv_ln_g, conv_ln_b, w_conv_out, ssm_lambda_re, ssm_lambda_im, ssm_log_dt, ssm_b_re, ssm_b_im, ssm_c_re, ssm_c_im, ssm_d, w_ssm_glu, g_mem, w_mem_kv, w_mem_out, w_out, g_ffn, w_router_group, b_router_group, w_router_expert, b_router_expert, w_exp_gate, w_exp_up, w_exp_down, g_final):
    assert g_mix.shape[0] == 1, "the problem fixes one layer"
    l = 0
    return _layer(
        x, mem, g_mix[l], w_in[l], conv_dw[l], conv_dw_bias[l], conv_ln_g[l], conv_ln_b[l],
        w_conv_out[l], ssm_lambda_re[l], ssm_lambda_im[l], ssm_log_dt[l], ssm_b_re[l],
        ssm_b_im[l], ssm_c_re[l], ssm_c_im[l], ssm_d[l], w_ssm_glu[l], g_mem[l], w_mem_kv[l],
        w_mem_out[l], w_out[l], g_ffn[l], w_router_group[l], b_router_group[l],
        w_router_expert[l], b_router_expert[l], w_exp_gate[l], w_exp_up[l], w_exp_down[l], g_final)
```

```python
import functools

import jax
import jax.numpy as jnp
from jax import lax
from jax.experimental import pallas as pl
from jax.experimental.pallas import tpu as pltpu

F32 = jnp.float32
BF16 = jnp.bfloat16
EPS = 1e-6

LANES = 128
CHUNK_ROWS = 16
SSM_GROUP = 16
SSM_STATE = 64
SSM_BLOCK = LANES
CONV_KERNEL = 31
CONV_HALO = 32
HEADS = 4
HEAD_DIM = 128
MOE_GROUPS = 4
EXPERTS_PER_GROUP = 8
N_EXPERTS = MOE_GROUPS * EXPERTS_PER_GROUP
TOKEN_TILE = 512
EXPERT_TILE = 512
CHUNKS_PER_ETILE = EXPERT_TILE // CHUNK_ROWS
VMEM_LIMIT = 56 * 1024 * 1024


def _rms(x, g):
    return x * lax.rsqrt(jnp.mean(x * x, axis=-1, keepdims=True) + EPS) * g


def _sigmoid(x):
    return 0.5 * jnp.tanh(0.5 * x) + 0.5


def _const_spec(shape):
    zeros = (0,) * len(shape)
    return pl.BlockSpec(shape, lambda *_: zeros, pipeline_mode=pl.Buffered(1))


def _chunk_rows(chunk):
    return pl.ds(pl.multiple_of(chunk * CHUNK_ROWS, CHUNK_ROWS), CHUNK_ROWS)


def _kv_kernel(mem_ref, g_ref, w_ref, k_ref, v_ref):
    width = k_ref.shape[-1]
    mn = _rms(mem_ref[0], g_ref[...]).astype(BF16)
    kv = jnp.dot(mn, w_ref[...], preferred_element_type=F32)
    k_ref[0] = kv[:, :width].astype(BF16)
    v_ref[0] = kv[:, width:].astype(BF16)


def _kv_call(mem, g_mem, w_kv):
    b, m, d = mem.shape
    width = w_kv.shape[1] // 2
    return pl.pallas_call(
        _kv_kernel,
        grid=(b,),
        in_specs=[pl.BlockSpec((1, m, d), lambda i: (i, 0, 0)),
                  pl.BlockSpec((1, d), lambda i: (0, 0)),
                  pl.BlockSpec(w_kv.shape, lambda i: (0, 0))],
        out_specs=[pl.BlockSpec((1, m, width), lambda i: (i, 0, 0)),
                   pl.BlockSpec((1, m, width), lambda i: (i, 0, 0))],
        out_shape=[jax.ShapeDtypeStruct((b, m, width), BF16)] * 2,
        name="kv",
    )(mem, g_mem, w_kv)


def _ut_kernel(x_ref, g_ref, wt_ref, o_ref):
    h = _rms(x_ref[...], g_ref[...]).astype(BF16)
    ut = lax.dot_general(wt_ref[...], h, (((1,), (1,)), ((), ())), preferred_element_type=F32)
    ut = ut.reshape(ut.shape[0], ut.shape[1] // SSM_BLOCK, SSM_BLOCK)
    o_ref[...] = ut.reshape(o_ref.shape).astype(BF16)


def _ut_call(x2, g_mix, w_ssm_t):
    t, d = x2.shape
    c = w_ssm_t.shape[0]
    ts = 8 * SSM_BLOCK
    n_groups = c // SSM_GROUP
    return pl.pallas_call(
        _ut_kernel,
        grid=(t // ts,),
        in_specs=[pl.BlockSpec((ts, d), lambda i: (i, 0)),
                  pl.BlockSpec((1, d), lambda i: (0, 0)),
                  pl.BlockSpec((c, d), lambda i: (0, 0))],
        out_specs=pl.BlockSpec((n_groups, SSM_GROUP, ts // SSM_BLOCK, SSM_BLOCK),
                               lambda i: (0, 0, i, 0)),
        out_shape=jax.ShapeDtypeStruct((n_groups, SSM_GROUP, t // SSM_BLOCK, SSM_BLOCK), BF16),
        name="ut",
    )(x2, g_mix, w_ssm_t)


def _ssm_kernel(u_ref, kt_ref, w_ref, v_ref, a_ref, y_ref, acc_ref, z_ref, zs_ref, s_ref, slab_ref,
                *, n_batch):
    rows = u_ref.shape[2]
    n_blocks = rows // n_batch
    ri = lax.broadcasted_iota(jnp.int32, (SSM_BLOCK, SSM_BLOCK), 0)
    ci = lax.broadcasted_iota(jnp.int32, (SSM_BLOCK, SSM_BLOCK), 1)
    causal = ci >= ri

    n_pairs = SSM_GROUP // 2

    def build(cp, slot):
        for half in range(2):
            kt = kt_ref[0, 2 * cp + half]
            for c in range(SSM_GROUP):
                xb = jnp.broadcast_to(kt[c:c + 1, :], (SSM_BLOCK, SSM_BLOCK))
                toe = pltpu.roll(xb, 0, 1, stride=1, stride_axis=0)
                slab_ref[slot, half * SSM_BLOCK:(half + 1) * SSM_BLOCK, c * SSM_BLOCK:(c + 1) * SSM_BLOCK] = (
                    jnp.where(causal, toe, 0.0).astype(BF16))

    def apply(cp, slot, first):
        x2 = jnp.concatenate([u_ref[0, 2 * cp], u_ref[0, 2 * cp + 1]], axis=1)
        part = jnp.dot(x2, slab_ref[slot], preferred_element_type=F32)
        wrow = pl.multiple_of(cp * 2 * SSM_BLOCK, 2 * SSM_BLOCK)
        zpart = jnp.dot(x2, w_ref[0, pl.ds(wrow, 2 * SSM_BLOCK), :], preferred_element_type=F32)
        if first:
            acc_ref[...] = part
            z_ref[...] = zpart
        else:
            acc_ref[...] += part
            z_ref[...] += zpart

    build(0, 0)
    build(1, 1)
    apply(0, 0, True)

    def two(it, carry):
        cp = 2 * it + 1
        build(cp + 1, 0)
        apply(cp, 1, False)
        build(cp + 2, 1)
        apply(cp + 1, 0, False)
        return carry

    lax.fori_loop(0, (n_pairs - 2) // 2, two, 0)
    apply(n_pairs - 1, 1, False)

    a_full = a_ref[0, 0:1, :]
    a_swap = a_ref[0, 1:2, :]
    zs_ref[...] = pltpu.roll(z_ref[...], SSM_STATE, 1)
    st = jnp.zeros((n_batch, 2 * SSM_STATE), F32)
    sw = jnp.zeros((n_batch, 2 * SSM_STATE), F32)
    for blk in range(n_blocks):
        sl = pl.ds(blk, n_batch, stride=n_blocks)
        s_ref[sl, :] = st
        st, sw = (a_full * st + a_swap * sw + z_ref[sl, :],
                  a_full * sw - a_swap * st + zs_ref[sl, :])

    y = acc_ref[...] + jnp.dot(s_ref[...].astype(BF16), v_ref[0], preferred_element_type=F32)
    y3 = jnp.stack([y[:, c * SSM_BLOCK:(c + 1) * SSM_BLOCK] for c in range(SSM_GROUP)], axis=0)
    y_ref[...] = y3.reshape(y_ref.shape)


def _ssm_call(u4, kt, wz, vy, acoef, n_batch):
    g, c, rows, blk = u4.shape
    width = c * blk
    return pl.pallas_call(
        functools.partial(_ssm_kernel, n_batch=n_batch),
        grid=(g,),
        in_specs=[pl.BlockSpec((1, c, rows, blk), lambda i: (i, 0, 0, 0)),
                  pl.BlockSpec((1, c, c, blk), lambda i: (i, 0, 0, 0)),
                  pl.BlockSpec((1, width, 2 * SSM_STATE), lambda i: (i, 0, 0)),
                  pl.BlockSpec((1, 2 * SSM_STATE, width), lambda i: (i, 0, 0)),
                  pl.BlockSpec((1, 2, 2 * SSM_STATE), lambda i: (i, 0, 0))],
        out_specs=pl.BlockSpec((c, rows * blk), lambda i: (i, 0)),
        out_shape=jax.ShapeDtypeStruct((g * c, rows * blk), F32),
        scratch_shapes=[pltpu.VMEM((rows, width), F32),
                        pltpu.VMEM((rows, 2 * SSM_STATE), F32),
                        pltpu.VMEM((rows, 2 * SSM_STATE), F32),
                        pltpu.VMEM((rows, 2 * SSM_STATE), F32),
                        pltpu.VMEM((2, 2 * blk, width), BF16)],
        name="ssm",
    )(u4, kt, wz, vy, acoef)


def _ssm_tables(lam_re, lam_im, log_dt, b_re, b_im, c_re, c_im, d):
    hi = lax.Precision.HIGHEST
    g = lam_re.shape[0]
    dt = jnp.exp(log_dt)[:, None]
    er, ei = lam_re * dt, lam_im * dt
    k = jnp.arange(SSM_BLOCK + 1, dtype=F32)[None, :, None]
    mag = jnp.exp(k * er[:, None, :])
    ang = k * ei[:, None, :]
    pr, pi = mag * jnp.cos(ang), mag * jnp.sin(ang)
    nr, ni = pr[:, 1] - 1.0, pi[:, 1]
    den = lam_re * lam_re + lam_im * lam_im
    fr = (nr * lam_re + ni * lam_im) / den
    fi = (ni * lam_re - nr * lam_im) / den
    bbr = fr[:, :, None] * b_re - fi[:, :, None] * b_im
    bbi = fr[:, :, None] * b_im + fi[:, :, None] * b_re
    car = c_re[:, None] * pr[:, :, None, :] - c_im[:, None] * pi[:, :, None, :]
    cai = c_re[:, None] * pi[:, :, None, :] + c_im[:, None] * pr[:, :, None, :]
    kt = (jnp.einsum("gkcp,gpd->gdck", car[:, :SSM_BLOCK], bbr, precision=hi)
          - jnp.einsum("gkcp,gpd->gdck", cai[:, :SSM_BLOCK], bbi, precision=hi))
    dmat = jnp.eye(SSM_GROUP, dtype=F32)[None] * d.reshape(g, 1, SSM_GROUP)
    kt = kt.at[:, :, :, 0].add(dmat)
    prr, pir = pr[:, SSM_BLOCK - 1::-1][:, :SSM_BLOCK], pi[:, SSM_BLOCK - 1::-1][:, :SSM_BLOCK]
    wzr = prr[:, None, :, :] * jnp.swapaxes(bbr, 1, 2)[:, :, None, :] \
        - pir[:, None, :, :] * jnp.swapaxes(bbi, 1, 2)[:, :, None, :]
    wzi = prr[:, None, :, :] * jnp.swapaxes(bbi, 1, 2)[:, :, None, :] \
        + pir[:, None, :, :] * jnp.swapaxes(bbr, 1, 2)[:, :, None, :]
    wz = jnp.concatenate([wzr, wzi], axis=-1).reshape(g, SSM_GROUP * SSM_BLOCK, 2 * SSM_STATE)
    vr = jnp.transpose(car[:, 1:], (0, 3, 2, 1)).reshape(g, SSM_STATE, SSM_GROUP * SSM_BLOCK)
    vi = jnp.transpose(cai[:, 1:], (0, 3, 2, 1)).reshape(g, SSM_STATE, SSM_GROUP * SSM_BLOCK)
    vy = jnp.concatenate([vr, -vi], axis=1)
    ar, ai = pr[:, SSM_BLOCK], pi[:, SSM_BLOCK]
    acoef = jnp.stack([jnp.concatenate([ar, ar], -1), jnp.concatenate([-ai, ai], -1)], axis=1)
    return kt, wz.astype(BF16), vy.astype(BF16), acoef


def _mix_kernel(x_ref, yt_ref, k_ref, v_ref, g_ref, wc_ref, wq_ref, wg_ref, dw_ref, dwb_ref,
                lng_ref, lnb_ref, wpw_ref, wglu_ref, wo_ref, wout_ref, o_ref, vext_ref, vsh_ref):
    ts = x_ref.shape[1]
    d = x_ref.shape[2]
    cw = dw_ref.shape[1]
    x = x_ref[0]
    h = _rms(x, g_ref[...]).astype(BF16)

    ci = jnp.dot(h, wc_ref[...], preferred_element_type=F32)
    v = ci[:, :cw] * _sigmoid(ci[:, cw:])
    q = jnp.dot(h, wq_ref[...], preferred_element_type=F32)

    @pl.when(pl.program_id(1) == 0)
    def _():
        vext_ref[0:CONV_HALO, :] = jnp.zeros((CONV_HALO, cw), F32)

    vext_ref[CONV_HALO:CONV_HALO + ts, :] = v
    acc = jnp.broadcast_to(dwb_ref[...], (ts, cw))
    first = CONV_HALO - (CONV_KERNEL - 1)
    for phase in range(8):
        offs = [first + k for k in range(CONV_KERNEL) if (first + k) % 8 == phase]
        if not offs:
            continue
        span = offs[-1] - offs[0] + ts
        vsh_ref[0:span, :] = vext_ref[offs[0]:offs[0] + span, :]
        for off in offs:
            acc = acc + dw_ref[off - first:off - first + 1, :] * vsh_ref[off - offs[0]:off - offs[0] + ts, :]
    vext_ref[0:CONV_HALO, :] = vext_ref[ts:ts + CONV_HALO, :]
    mu = jnp.mean(acc, axis=-1, keepdims=True)
    xc = acc - mu
    var = jnp.mean(xc * xc, axis=-1, keepdims=True)
    ln = xc * lax.rsqrt(var + EPS) * lng_ref[...] + lnb_ref[...]
    sw = ln * _sigmoid(ln)
    y_conv = jnp.dot(sw.astype(BF16), wpw_ref[...], preferred_element_type=F32)
    merged = _sigmoid(jnp.dot(h, wg_ref[:, 0:d], preferred_element_type=F32)) * y_conv

    gy = jax.nn.gelu(yt_ref[...]).astype(BF16)
    z = lax.dot_general(gy, wglu_ref[...], (((0,), (0,)), ((), ())), preferred_element_type=F32)
    y_ssm = z[:, :d] * _sigmoid(z[:, d:])
    merged = merged + _sigmoid(jnp.dot(h, wg_ref[:, d:2 * d], preferred_element_type=F32)) * y_ssm

    kk = k_ref[0]
    vv = v_ref[0]
    outs = []
    for hd in range(HEADS):
        sl = slice(hd * HEAD_DIM, (hd + 1) * HEAD_DIM)
        s = lax.dot_general(q[:, sl].astype(BF16), kk[:, sl], (((1,), (1,)), ((), ())),
                            preferred_element_type=F32) * (HEAD_DIM ** -0.5)
        p = jnp.exp(s - jnp.max(s, axis=-1, keepdims=True))
        den = jnp.sum(p, axis=-1, keepdims=True)
        o = jnp.dot(p.astype(BF16), vv[:, sl], preferred_element_type=F32) / den
        outs.append(o.astype(BF16))
    y_mem = jnp.dot(jnp.concatenate(outs, axis=1), wo_ref[...], preferred_element_type=F32)
    merged = merged + _sigmoid(jnp.dot(h, wg_ref[:, 2 * d:3 * d], preferred_element_type=F32)) * y_mem

    o_ref[0] = x + jnp.dot(merged.astype(BF16), wout_ref[...], preferred_element_type=F32)


def _mix_call(x, yt, kmem, vmem, g_mix, wc, wq, wg, dw, dwb, lng, lnb, wpw, wglu, wo, wout):
    b, s, d = x.shape
    ts = TOKEN_TILE
    nst = s // ts
    cw = dw.shape[1]
    m = kmem.shape[1]
    consts = [g_mix, wc, wq, wg, dw, dwb, lng, lnb, wpw, wglu, wo, wout]
    return pl.pallas_call(
        _mix_kernel,
        grid=(b, nst),
        in_specs=[pl.BlockSpec((1, ts, d), lambda i, j: (i, j, 0)),
                  pl.BlockSpec((yt.shape[0], ts), lambda i, j: (0, i * nst + j)),
                  pl.BlockSpec((1, m, kmem.shape[2]), lambda i, j: (i, 0, 0)),
                  pl.BlockSpec((1, m, vmem.shape[2]), lambda i, j: (i, 0, 0))]
                 + [_const_spec(c.shape) for c in consts],
        out_specs=pl.BlockSpec((1, ts, d), lambda i, j: (i, j, 0)),
        out_shape=jax.ShapeDtypeStruct((b, s, d), F32),
        scratch_shapes=[pltpu.VMEM((ts + CONV_HALO, cw), F32),
                        pltpu.VMEM((ts + CONV_HALO, cw), F32)],
        compiler_params=pltpu.CompilerParams(
            dimension_semantics=("arbitrary", "arbitrary"), vmem_limit_bytes=VMEM_LIMIT),
        name="mix",
    )(x, yt, kmem, vmem, *consts)


def _route_kernel(x_ref, g_ref, wr_ref, br_ref, xs_ref, info_ref, nch_ref):
    ts = x_ref.shape[0]
    cap = xs_ref.shape[0]
    h2 = _rms(x_ref[...], g_ref[...])
    hb = h2.astype(BF16)
    hl = (h2 - hb.astype(F32)).astype(BF16)
    wr = wr_ref[...]
    wb = wr.astype(BF16)
    wl = (wr - wb.astype(F32)).astype(BF16)
    logits = (jnp.dot(hb, wb, preferred_element_type=F32) + jnp.dot(hl, wb, preferred_element_type=F32)
              + jnp.dot(hb, wl, preferred_element_type=F32)) + br_ref[...]
    lt = logits.T
    le = lt[0:N_EXPERTS]
    lg = lt[N_EXPERTS:N_EXPERTS + 8]
    neg = jnp.float32(-1e30)
    big = jnp.float32(1e9)
    g_f = lax.broadcasted_iota(jnp.int32, (8, ts), 0).astype(F32)
    e_f = lax.broadcasted_iota(jnp.int32, (N_EXPERTS, ts), 0).astype(F32)

    gmask = g_f < MOE_GROUPS
    gmax = jnp.max(jnp.where(gmask, lg, neg), axis=0, keepdims=True)
    gidx = jnp.min(jnp.where(gmask & (lg == gmax), g_f, big), axis=0, keepdims=True)
    gsum = jnp.sum(jnp.where(gmask, jnp.exp(jnp.minimum(lg - gmax, 0.0)), 0.0), axis=0, keepdims=True)
    p_top = 1.0 / gsum
    emask = jnp.floor(e_f * (1.0 / EXPERTS_PER_GROUP)) == gidx
    m1 = jnp.max(jnp.where(emask, le, neg), axis=0, keepdims=True)
    i1 = jnp.min(jnp.where(emask & (le == m1), e_f, big), axis=0, keepdims=True)
    emask2 = emask & (e_f != i1)
    m2 = jnp.max(jnp.where(emask2, le, neg), axis=0, keepdims=True)
    i2 = jnp.min(jnp.where(emask2 & (le == m2), e_f, big), axis=0, keepdims=True)
    r = jnp.exp(m2 - m1)
    w1 = p_top / (1.0 + r)
    w2 = p_top * r / (1.0 + r)

    sel1 = e_f == i1
    sel2 = e_f == i2
    occ = jnp.where(sel1 | sel2, 1.0, 0.0)
    tr = lax.broadcasted_iota(jnp.int32, (ts, ts), 0)
    tc = lax.broadcasted_iota(jnp.int32, (ts, ts), 1)
    earlier = jnp.where(tr < tc, 1.0, 0.0).astype(BF16)
    rank = jnp.dot(occ.astype(BF16), earlier, preferred_element_type=F32)
    cnt = jnp.sum(occ, axis=1, keepdims=True)
    nch = jnp.floor((cnt + (CHUNK_ROWS - 1)) * (1.0 / CHUNK_ROWS))
    er = lax.broadcasted_iota(jnp.int32, (N_EXPERTS, N_EXPERTS), 0)
    ec = lax.broadcasted_iota(jnp.int32, (N_EXPERTS, N_EXPERTS), 1)
    lower = jnp.where(ec < er, 1.0, 0.0).astype(BF16)
    nch_b = jnp.broadcast_to(nch, (N_EXPERTS, LANES))
    start = jnp.dot(lower, nch_b.astype(BF16), preferred_element_type=F32)[:, 0:1] * CHUNK_ROWS
    slot = start + rank
    pos1 = jnp.sum(jnp.where(sel1, slot, 0.0), axis=0, keepdims=True)
    pos2 = jnp.sum(jnp.where(sel2, slot, 0.0), axis=0, keepdims=True)

    rowid = lax.broadcasted_iota(jnp.int32, (cap, ts), 0)
    p = jnp.where((rowid == pos1.astype(jnp.int32)) | (rowid == pos2.astype(jnp.int32)), 1.0, 0.0)
    xs_ref[...] = jnp.dot(p.astype(BF16), hb, preferred_element_type=F32).astype(BF16)
    sub = lax.broadcasted_iota(jnp.int32, (LANES, ts), 0)
    info_t = jnp.where(sub == 0, pos1, jnp.where(sub == 1, pos2,
                       jnp.where(sub == 2, w1, jnp.where(sub == 3, w2, 0.0))))
    info_ref[...] = info_t.T
    nch_ref[0] = jnp.concatenate([nch_b, jnp.zeros((LANES - N_EXPERTS, LANES), F32)], axis=0).T[0:8, :]


def _route_call(x1, g_ffn, wr, br, cap):
    t, d = x1.shape
    ts = TOKEN_TILE
    nt = t // ts
    return pl.pallas_call(
        _route_kernel,
        grid=(nt,),
        in_specs=[pl.BlockSpec((ts, d), lambda i: (i, 0)),
                  pl.BlockSpec((1, d), lambda i: (0, 0)),
                  pl.BlockSpec(wr.shape, lambda i: (0, 0)),
                  pl.BlockSpec((1, LANES), lambda i: (0, 0))],
        out_specs=[pl.BlockSpec((cap, d), lambda i: (i, 0)),
                   pl.BlockSpec((ts, LANES), lambda i: (i, 0)),
                   pl.BlockSpec((1, 8, LANES), lambda i: (i, 0, 0))],
        out_shape=[jax.ShapeDtypeStruct((nt * cap, d), BF16),
                   jax.ShapeDtypeStruct((t, LANES), F32),
                   jax.ShapeDtypeStruct((nt, 8, LANES), F32)],
        compiler_params=pltpu.CompilerParams(vmem_limit_bytes=VMEM_LIMIT),
        name="route",
    )(x1, g_ffn, wr, br)


def _expert_kernel(te_ref, nu_ref, src_ref, xs_ref, wg_ref, wu_ref, wd_ref, o_ref, xbuf_ref, sem_ref):
    i = pl.program_id(0)
    n_used = nu_ref[0]
    slot = i % 2

    def chunk_copy(tile, c, buf):
        return pltpu.make_async_copy(
            xs_ref.at[_chunk_rows(src_ref[tile * CHUNKS_PER_ETILE + c])],
            xbuf_ref.at[buf, pl.ds(c * CHUNK_ROWS, CHUNK_ROWS)], sem_ref.at[buf])

    def gather(tile, buf):
        for c in range(CHUNKS_PER_ETILE):
            chunk_copy(tile, c, buf).start()

    def drain(tile, buf):
        for c in range(CHUNKS_PER_ETILE):
            chunk_copy(tile, c, buf).wait()

    @pl.when(i == 0)
    def _():
        gather(0, 0)

    @pl.when(i < n_used)
    def _():
        nxt = jnp.minimum(i + 1, n_used - 1)
        gather(nxt, 1 - slot)
        drain(i, slot)
        x = xbuf_ref[slot]
        gate = jnp.dot(x, wg_ref[0].astype(BF16), preferred_element_type=F32)
        up = jnp.dot(x, wu_ref[0].astype(BF16), preferred_element_type=F32)
        act = (gate * _sigmoid(gate) * up).astype(BF16)
        o_ref[...] = jnp.dot(act, wd_ref[0].astype(BF16), preferred_element_type=F32).astype(BF16)

        @pl.when(i == n_used - 1)
        def _():
            drain(nxt, 1 - slot)

    @pl.when(i >= n_used)
    def _():
        o_ref[...] = jnp.zeros_like(o_ref)


def _expert_call(tile_expert, n_used, chunk_src, xs, wg, wu, wd):
    d = xs.shape[1]
    de = wg.shape[2]
    tm = EXPERT_TILE
    nt = tile_expert.shape[0]

    def w_map(i, te, nu, src):
        return (te[jnp.maximum(jnp.minimum(i, nu[0] - 1), 0)], 0, 0)

    return pl.pallas_call(
        _expert_kernel,
        grid_spec=pltpu.PrefetchScalarGridSpec(
            num_scalar_prefetch=3,
            grid=(nt,),
            in_specs=[pl.BlockSpec(memory_space=pl.ANY),
                      pl.BlockSpec((1, d, de), w_map),
                      pl.BlockSpec((1, d, de), w_map),
                      pl.BlockSpec((1, de, d), w_map)],
            out_specs=pl.BlockSpec((tm, d), lambda i, te, nu, src: (i, 0)),
            scratch_shapes=[pltpu.VMEM((2, tm, d), BF16),
                            pltpu.SemaphoreType.DMA((2,))]),
        out_shape=jax.ShapeDtypeStruct((nt * tm, d), BF16),
        compiler_params=pltpu.CompilerParams(dimension_semantics=("arbitrary",)),
        name="expert",
    )(tile_expert, n_used, chunk_src, xs, wg, wu, wd)


def _combine_kernel(dst_ref, x_ref, info_ref, g_ref, ys_ref, o_ref, ybuf_ref, sem_ref):
    i = pl.program_id(0)
    n_tiles = pl.num_programs(0)
    ts = x_ref.shape[0]
    cap = ybuf_ref.shape[1]
    cpt = cap // CHUNK_ROWS
    slot = i % 2

    def chunk_copy(tile, k, buf):
        return pltpu.make_async_copy(ys_ref.at[_chunk_rows(dst_ref[tile * cpt + k])],
                                     ybuf_ref.at[buf, pl.ds(k * CHUNK_ROWS, CHUNK_ROWS)], sem_ref.at[buf])

    def gather(tile, buf):
        for k in range(cpt):
            chunk_copy(tile, k, buf).start()

    def drain(tile, buf):
        for k in range(cpt):
            chunk_copy(tile, k, buf).wait()

    @pl.when(i == 0)
    def _():
        gather(0, 0)

    nxt = jnp.minimum(i + 1, n_tiles - 1)
    gather(nxt, 1 - slot)
    drain(i, slot)

    info = info_ref[...]
    pos1 = info[:, 0:1].astype(jnp.int32)
    pos2 = info[:, 1:2].astype(jnp.int32)
    w1 = info[:, 2:3]
    w2 = info[:, 3:4]
    rowid = lax.broadcasted_iota(jnp.int32, (ts, cap), 1)
    ys = ybuf_ref[slot]
    pw = jnp.where(rowid == pos1, w1, jnp.where(rowid == pos2, w2, 0.0)).astype(BF16)
    y = jnp.dot(pw, ys, preferred_element_type=F32)
    o_ref[...] = _rms(x_ref[...] + y, g_ref[...])

    @pl.when(i == n_tiles - 1)
    def _():
        drain(nxt, 1 - slot)


def _combine_call(chunk_dst, x1, info, g_final, ys_sorted, cap):
    t, d = x1.shape
    ts = TOKEN_TILE
    return pl.pallas_call(
        _combine_kernel,
        grid_spec=pltpu.PrefetchScalarGridSpec(
            num_scalar_prefetch=1,
            grid=(t // ts,),
            in_specs=[pl.BlockSpec((ts, d), lambda i, dst: (i, 0)),
                      pl.BlockSpec((ts, LANES), lambda i, dst: (i, 0)),
                      pl.BlockSpec((1, d), lambda i, dst: (0, 0)),
                      pl.BlockSpec(memory_space=pl.ANY)],
            out_specs=pl.BlockSpec((ts, d), lambda i, dst: (i, 0)),
            scratch_shapes=[pltpu.VMEM((2, cap, d), BF16),
                            pltpu.SemaphoreType.DMA((2,))]),
        out_shape=jax.ShapeDtypeStruct((t, d), F32),
        compiler_params=pltpu.CompilerParams(dimension_semantics=("arbitrary",),
                                             vmem_limit_bytes=VMEM_LIMIT),
        name="combine",
    )(chunk_dst, x1, info, g_final, ys_sorted)


def _run_tables(nch, cpt, n_etiles):
    n_tiles, n_exp = nch.shape
    per = CHUNKS_PER_ETILE
    tcum = jnp.cumsum(nch, axis=1)
    toff = tcum - nch
    ecum = jnp.cumsum(nch, axis=0)
    etot = ecum[-1]
    eseg = -(-etot // per) * per
    segcum = jnp.cumsum(eseg)
    ebase = segcum - eseg
    eoff = ebase[None, :] + ecum - nch
    n_used = (segcum[-1] // per).reshape(1)

    k = jnp.arange(cpt, dtype=jnp.int32)[None, :, None]
    in_run = (k >= toff[:, None, :]) & (k < tcum[:, None, :])
    chunk_dst = jnp.sum(jnp.where(in_run, (eoff - toff)[:, None, :] + k, 0), axis=-1)

    c = jnp.arange(n_etiles * per, dtype=jnp.int32)[:, None]
    lo = eoff.reshape(1, -1)
    hit = (c >= lo) & (c < lo + nch.reshape(1, -1))
    shift = (jnp.arange(n_tiles, dtype=jnp.int32)[:, None] * cpt + toff - eoff).reshape(1, -1)
    chunk_src = jnp.sum(jnp.where(hit, shift + c, 0), axis=-1)

    first = jnp.arange(n_etiles, dtype=jnp.int32)[:, None] * per
    owner = (first >= ebase[None, :]) & (first < segcum[None, :])
    tile_expert = jnp.sum(jnp.where(owner, jnp.arange(n_exp, dtype=jnp.int32)[None, :], 0), axis=-1)
    i32 = lambda a: a.reshape(-1).astype(jnp.int32)
    return i32(chunk_dst), i32(tile_expert), i32(n_used), i32(chunk_src)


def _layer(x, mem, g_mix, w_in, conv_dw, conv_dw_bias, conv_ln_g, conv_ln_b, w_conv_out,
           lam_re, lam_im, log_dt, b_re, b_im, c_re, c_im, ssm_d, w_ssm_glu, g_mem, w_mem_kv,
           w_mem_out, w_out, g_ffn, w_rg, b_rg, w_re, b_re_, w_eg, w_eu, w_ed, g_final):
    b, s, d = x.shape
    t = b * s
    cw = conv_dw.shape[1]
    sw = ssm_d.shape[0]
    qw = w_mem_out.shape[0]
    n_groups = sw // SSM_GROUP
    row = lambda a: a.reshape(1, -1)

    o0, o1, o2 = 2 * cw, 2 * cw + sw, 2 * cw + sw + qw
    wc = w_in[:, :o0].astype(BF16)
    w_ssm_t = w_in[:, o0:o1].T.astype(BF16)
    wq = w_in[:, o1:o2].astype(BF16)
    wg = w_in[:, o2:].astype(BF16)

    kmem, vmem = _kv_call(mem, row(g_mem), w_mem_kv.astype(BF16))

    ut = _ut_call(x.reshape(t, d), row(g_mix), w_ssm_t)
    kt, wz, vy, acoef = _ssm_tables(lam_re, lam_im, log_dt, b_re, b_im, c_re, c_im, ssm_d)
    yt = _ssm_call(ut, kt, wz, vy, acoef, b)

    dw = jnp.concatenate([conv_dw, jnp.zeros((CONV_HALO - CONV_KERNEL, cw), F32)], axis=0)
    x1 = _mix_call(x, yt, kmem, vmem, row(g_mix), wc, wq, wg, dw, row(conv_dw_bias),
                   row(conv_ln_g), row(conv_ln_b), w_conv_out.astype(BF16),
                   w_ssm_glu.astype(BF16), w_mem_out.astype(BF16), w_out.astype(BF16))
    x1 = x1.reshape(t, d)

    pad = LANES - N_EXPERTS - MOE_GROUPS
    wr = jnp.concatenate([w_re, w_rg, jnp.zeros((d, pad), F32)], axis=1)
    br = jnp.concatenate([b_re_, b_rg, jnp.zeros((pad,), F32)]).reshape(1, LANES)
    n_tiles = t // TOKEN_TILE
    max_tile_chunks = (2 * TOKEN_TILE + N_EXPERTS * (CHUNK_ROWS - 1)) // CHUNK_ROWS
    cpt = -(-max_tile_chunks // 8) * 8
    cap = cpt * CHUNK_ROWS
    xs_tiles, info, nch_f = _route_call(x1, row(g_ffn), wr, br, cap)

    max_chunks = n_tiles * max_tile_chunks + N_EXPERTS * (CHUNKS_PER_ETILE - 1)
    n_etiles = -(-max_chunks // CHUNKS_PER_ETILE)
    nch = nch_f[:, 0, :N_EXPERTS].astype(jnp.int32)
    chunk_dst, tile_expert, n_used, chunk_src = _run_tables(nch, cpt, n_etiles)

    ys_sorted = _expert_call(tile_expert, n_used, chunk_src, xs_tiles, w_eg, w_eu, w_ed)
    out = _combine_call(chunk_dst, x1, info, row(g_final), ys_sorted, cap)
    return out.reshape(b, s, d)


def kernel(x, mem, g_mix, w_in, conv_dw, conv_dw_bias, conv_ln_g, conv_ln_b, w_conv_out, ssm_lambda_re, ssm_lambda_im, ssm_log_dt, ssm_b_re, ssm_b_im, ssm_c_re, ssm_c_im, ssm_d, w_ssm_glu, g_mem, w_mem_kv, w_mem_out, w_out, g_ffn, w_router_group, b_router_group, w_router_expert, b_router_expert, w_exp_gate, w_exp_up, w_exp_down, g_final):
    assert g_mix.shape[0] == 1, "the problem fixes one layer"
    l = 0
    return _layer(
        x, mem, g_mix[l], w_in[l], conv_dw[l], conv_dw_bias[l], conv_ln_g[l], conv_ln_b[l],
        w_conv_out[l], ssm_lambda_re[l], ssm_lambda_im[l], ssm_log_dt[l], ssm_b_re[l],
        ssm_b_im[l], ssm_c_re[l], ssm_c_im[l], ssm_d[l], w_ssm_glu[l], g_mem[l], w_mem_kv[l],
        w_mem_out[l], w_out[l], g_ffn[l], w_router_group[l], b_router_group[l],
        w_router_expert[l], b_router_expert[l], w_exp_gate[l], w_exp_up[l], w_exp_down[l], g_final)
```

```python
import functools

import jax
import jax.numpy as jnp
from jax import lax
from jax.experimental import pallas as pl
from jax.experimental.pallas import tpu as pltpu

F32 = jnp.float32
BF16 = jnp.bfloat16
EPS = 1e-6

LANES = 128
CHUNK_ROWS = 16
SSM_GROUP = 16
SSM_STATE = 64
SSM_BLOCK = LANES
CONV_KERNEL = 31
CONV_HALO = 32
HEADS = 4
HEAD_DIM = 128
MOE_GROUPS = 4
EXPERTS_PER_GROUP = 8
N_EXPERTS = MOE_GROUPS * EXPERTS_PER_GROUP
TOKEN_TILE = 512
EXPERT_TILE = 512
CHUNKS_PER_ETILE = EXPERT_TILE // CHUNK_ROWS
VMEM_LIMIT = 56 * 1024 * 1024


def _rms(x, g):
    return x * lax.rsqrt(jnp.mean(x * x, axis=-1, keepdims=True) + EPS) * g


def _sigmoid(x):
    return 0.5 * jnp.tanh(0.5 * x) + 0.5


def _const_spec(shape):
    zeros = (0,) * len(shape)
    return pl.BlockSpec(shape, lambda *_: zeros, pipeline_mode=pl.Buffered(1))


def _chunk_rows(chunk):
    return pl.ds(pl.multiple_of(chunk * CHUNK_ROWS, CHUNK_ROWS), CHUNK_ROWS)


def _kv_kernel(mem_ref, g_ref, w_ref, k_ref, v_ref):
    width = k_ref.shape[-1]
    mn = _rms(mem_ref[0], g_ref[...]).astype(BF16)
    kv = jnp.dot(mn, w_ref[...], preferred_element_type=F32)
    k_ref[0] = kv[:, :width].astype(BF16)
    v_ref[0] = kv[:, width:].astype(BF16)


def _kv_call(mem, g_mem, w_kv):
    b, m, d = mem.shape
    width = w_kv.shape[1] // 2
    return pl.pallas_call(
        _kv_kernel,
        grid=(b,),
        in_specs=[pl.BlockSpec((1, m, d), lambda i: (i, 0, 0)),
                  pl.BlockSpec((1, d), lambda i: (0, 0)),
                  pl.BlockSpec(w_kv.shape, lambda i: (0, 0))],
        out_specs=[pl.BlockSpec((1, m, width), lambda i: (i, 0, 0)),
                   pl.BlockSpec((1, m, width), lambda i: (i, 0, 0))],
        out_shape=[jax.ShapeDtypeStruct((b, m, width), BF16)] * 2,
        name="kv",
    )(mem, g_mem, w_kv)


def _ut_kernel(x_ref, g_ref, wt_ref, o_ref):
    h = _rms(x_ref[...], g_ref[...]).astype(BF16)
    ut = lax.dot_general(wt_ref[...], h, (((1,), (1,)), ((), ())), preferred_element_type=F32)
    ut = ut.reshape(ut.shape[0], ut.shape[1] // SSM_BLOCK, SSM_BLOCK)
    o_ref[...] = ut.reshape(o_ref.shape).astype(BF16)


def _ut_call(x2, g_mix, w_ssm_t):
    t, d = x2.shape
    c = w_ssm_t.shape[0]
    ts = 8 * SSM_BLOCK
    n_groups = c // SSM_GROUP
    return pl.pallas_call(
        _ut_kernel,
        grid=(t // ts,),
        in_specs=[pl.BlockSpec((ts, d), lambda i: (i, 0)),
                  pl.BlockSpec((1, d), lambda i: (0, 0)),
                  pl.BlockSpec((c, d), lambda i: (0, 0))],
        out_specs=pl.BlockSpec((n_groups, SSM_GROUP, ts // SSM_BLOCK, SSM_BLOCK),
                               lambda i: (0, 0, i, 0)),
        out_shape=jax.ShapeDtypeStruct((n_groups, SSM_GROUP, t // SSM_BLOCK, SSM_BLOCK), BF16),
        name="ut",
    )(x2, g_mix, w_ssm_t)


def _ssm_kernel(u_ref, kt_ref, w_ref, v_ref, a_ref, y_ref, acc_ref, z_ref, zs_ref, s_ref, slab_ref,
                *, n_batch):
    rows = u_ref.shape[2]
    n_blocks = rows // n_batch
    ri = lax.broadcasted_iota(jnp.int32, (SSM_BLOCK, SSM_BLOCK), 0)
    ci = lax.broadcasted_iota(jnp.int32, (SSM_BLOCK, SSM_BLOCK), 1)
    causal = ci >= ri

    n_pairs = SSM_GROUP // 2

    def build(cp, slot):
        for half in range(2):
            kt = kt_ref[0, 2 * cp + half]
            for c in range(SSM_GROUP):
                xb = jnp.broadcast_to(kt[c:c + 1, :], (SSM_BLOCK, SSM_BLOCK))
                toe = pltpu.roll(xb, 0, 1, stride=1, stride_axis=0)
                slab_ref[slot, half * SSM_BLOCK:(half + 1) * SSM_BLOCK, c * SSM_BLOCK:(c + 1) * SSM_BLOCK] = (
                    jnp.where(causal, toe, 0.0).astype(BF16))

    def apply(cp, slot, first):
        x2 = jnp.concatenate([u_ref[0, 2 * cp], u_ref[0, 2 * cp + 1]], axis=1)
        part = jnp.dot(x2, slab_ref[slot], preferred_element_type=F32)
        wrow = pl.multiple_of(cp * 2 * SSM_BLOCK, 2 * SSM_BLOCK)
        zpart = jnp.dot(x2, w_ref[0, pl.ds(wrow, 2 * SSM_BLOCK), :], preferred_element_type=F32)
        if first:
            acc_ref[...] = part
            z_ref[...] = zpart
        else:
            acc_ref[...] += part
            z_ref[...] += zpart

    build(0, 0)
    build(1, 1)
    apply(0, 0, True)

    def two(it, carry):
        cp = 2 * it + 1
        build(cp + 1, 0)
        apply(cp, 1, False)
        build(cp + 2, 1)
        apply(cp + 1, 0, False)
        return carry

    lax.fori_loop(0, (n_pairs - 2) // 2, two, 0)
    apply(n_pairs - 1, 1, False)

    a_full = a_ref[0, 0:1, :]
    a_swap = a_ref[0, 1:2, :]
    zs_ref[...] = pltpu.roll(z_ref[...], SSM_STATE, 1)
    st = jnp.zeros((n_batch, 2 * SSM_STATE), F32)
    sw = jnp.zeros((n_batch, 2 * SSM_STATE), F32)
    for blk in range(n_blocks):
        sl = pl.ds(blk, n_batch, stride=n_blocks)
        s_ref[sl, :] = st
        st, sw = (a_full * st + a_swap * sw + z_ref[sl, :],
                  a_full * sw - a_swap * st + zs_ref[sl, :])

    y = acc_ref[...] + jnp.dot(s_ref[...].astype(BF16), v_ref[0], preferred_element_type=F32)
    y3 = jnp.stack([y[:, c * SSM_BLOCK:(c + 1) * SSM_BLOCK] for c in range(SSM_GROUP)], axis=0)
    y_ref[...] = y3.reshape(y_ref.shape)


def _ssm_call(u4, kt, wz, vy, acoef, n_batch):
    g, c, rows, blk = u4.shape
    width = c * blk
    return pl.pallas_call(
        functools.partial(_ssm_kernel, n_batch=n_batch),
        grid=(g,),
        in_specs=[pl.BlockSpec((1, c, rows, blk), lambda i: (i, 0, 0, 0)),
                  pl.BlockSpec((1, c, c, blk), lambda i: (i, 0, 0, 0)),
                  pl.BlockSpec((1, width, 2 * SSM_STATE), lambda i: (i, 0, 0)),
                  pl.BlockSpec((1, 2 * SSM_STATE, width), lambda i: (i, 0, 0)),
                  pl.BlockSpec((1, 2, 2 * SSM_STATE), lambda i: (i, 0, 0))],
        out_specs=pl.BlockSpec((c, rows * blk), lambda i: (i, 0)),
        out_shape=jax.ShapeDtypeStruct((g * c, rows * blk), F32),
        scratch_shapes=[pltpu.VMEM((rows, width), F32),
                        pltpu.VMEM((rows, 2 * SSM_STATE), F32),
                        pltpu.VMEM((rows, 2 * SSM_STATE), F32),
                        pltpu.VMEM((rows, 2 * SSM_STATE), F32),
                        pltpu.VMEM((2, 2 * blk, width), BF16)],
        name="ssm",
    )(u4, kt, wz, vy, acoef)


def _ssm_tables(lam_re, lam_im, log_dt, b_re, b_im, c_re, c_im, d):
    hi = lax.Precision.HIGHEST
    g = lam_re.shape[0]
    dt = jnp.exp(log_dt)[:, None]
    er, ei = lam_re * dt, lam_im * dt
    k = jnp.arange(SSM_BLOCK + 1, dtype=F32)[None, :, None]
    mag = jnp.exp(k * er[:, None, :])
    ang = k * ei[:, None, :]
    pr, pi = mag * jnp.cos(ang), mag * jnp.sin(ang)
    nr, ni = pr[:, 1] - 1.0, pi[:, 1]
    den = lam_re * lam_re + lam_im * lam_im
    fr = (nr * lam_re + ni * lam_im) / den
    fi = (ni * lam_re - nr * lam_im) / den
    bbr = fr[:, :, None] * b_re - fi[:, :, None] * b_im
    bbi = fr[:, :, None] * b_im + fi[:, :, None] * b_re
    car = c_re[:, None] * pr[:, :, None, :] - c_im[:, None] * pi[:, :, None, :]
    cai = c_re[:, None] * pi[:, :, None, :] + c_im[:, None] * pr[:, :, None, :]
    kt = (jnp.einsum("gkcp,gpd->gdck", car[:, :SSM_BLOCK], bbr, precision=hi)
          - jnp.einsum("gkcp,gpd->gdck", cai[:, :SSM_BLOCK], bbi, precision=hi))
    dmat = jnp.eye(SSM_GROUP, dtype=F32)[None] * d.reshape(g, 1, SSM_GROUP)
    kt = kt.at[:, :, :, 0].add(dmat)
    prr, pir = pr[:, SSM_BLOCK - 1::-1][:, :SSM_BLOCK], pi[:, SSM_BLOCK - 1::-1][:, :SSM_BLOCK]
    wzr = prr[:, None, :, :] * jnp.swapaxes(bbr, 1, 2)[:, :, None, :] \
        - pir[:, None, :, :] * jnp.swapaxes(bbi, 1, 2)[:, :, None, :]
    wzi = prr[:, None, :, :] * jnp.swapaxes(bbi, 1, 2)[:, :, None, :] \
        + pir[:, None, :, :] * jnp.swapaxes(bbr, 1, 2)[:, :, None, :]
    wz = jnp.concatenate([wzr, wzi], axis=-1).reshape(g, SSM_GROUP * SSM_BLOCK, 2 * SSM_STATE)
    vr = jnp.transpose(car[:, 1:], (0, 3, 2, 1)).reshape(g, SSM_STATE, SSM_GROUP * SSM_BLOCK)
    vi = jnp.transpose(cai[:, 1:], (0, 3, 2, 1)).reshape(g, SSM_STATE, SSM_GROUP * SSM_BLOCK)
    vy = jnp.concatenate([vr, -vi], axis=1)
    ar, ai = pr[:, SSM_BLOCK], pi[:, SSM_BLOCK]
    acoef = jnp.stack([jnp.concatenate([ar, ar], -1), jnp.concatenate([-ai, ai], -1)], axis=1)
    return kt, wz.astype(BF16), vy.astype(BF16), acoef


def _mix_kernel(x_ref, yt_ref, k_ref, v_ref, g_ref, wc_ref, wq_ref, wg_ref, dw_ref, dwb_ref,
                lng_ref, lnb_ref, wpw_ref, wglu_ref, wo_ref, wout_ref, o_ref, vext_ref, vsh_ref):
    ts = x_ref.shape[1]
    d = x_ref.shape[2]
    cw = dw_ref.shape[1]
    x = x_ref[0]
    h = _rms(x, g_ref[...]).astype(BF16)

    ci = jnp.dot(h, wc_ref[...], preferred_element_type=F32)
    v = ci[:, :cw] * _sigmoid(ci[:, cw:])
    q = jnp.dot(h, wq_ref[...], preferred_element_type=F32)

    @pl.when(pl.program_id(1) == 0)
    def _():
        vext_ref[0:CONV_HALO, :] = jnp.zeros((CONV_HALO, cw), F32)

    vext_ref[CONV_HALO:CONV_HALO + ts, :] = v
    acc = jnp.broadcast_to(dwb_ref[...], (ts, cw))
    first = CONV_HALO - (CONV_KERNEL - 1)
    for phase in range(8):
        offs = [first + k for k in range(CONV_KERNEL) if (first + k) % 8 == phase]
        if not offs:
            continue
        span = offs[-1] - offs[0] + ts
        vsh_ref[0:span, :] = vext_ref[offs[0]:offs[0] + span, :]
        for off in offs:
            acc = acc + dw_ref[off - first:off - first + 1, :] * vsh_ref[off - offs[0]:off - offs[0] + ts, :]
    vext_ref[0:CONV_HALO, :] = vext_ref[ts:ts + CONV_HALO, :]
    mu = jnp.mean(acc, axis=-1, keepdims=True)
    xc = acc - mu
    var = jnp.mean(xc * xc, axis=-1, keepdims=True)
    ln = xc * lax.rsqrt(var + EPS) * lng_ref[...] + lnb_ref[...]
    sw = ln * _sigmoid(ln)
    y_conv = jnp.dot(sw.astype(BF16), wpw_ref[...], preferred_element_type=F32)
    merged = _sigmoid(jnp.dot(h, wg_ref[:, 0:d], preferred_element_type=F32)) * y_conv

    gy = jax.nn.gelu(yt_ref[...]).astype(BF16)
    z = lax.dot_general(gy, wglu_ref[...], (((0,), (0,)), ((), ())), preferred_element_type=F32)
    y_ssm = z[:, :d] * _sigmoid(z[:, d:])
    merged = merged + _sigmoid(jnp.dot(h, wg_ref[:, d:2 * d], preferred_element_type=F32)) * y_ssm

    kk = k_ref[0]
    vv = v_ref[0]
    outs = []
    for hd in range(HEADS):
        sl = slice(hd * HEAD_DIM, (hd + 1) * HEAD_DIM)
        s = lax.dot_general(q[:, sl].astype(BF16), kk[:, sl], (((1,), (1,)), ((), ())),
                            preferred_element_type=F32) * (HEAD_DIM ** -0.5)
        p = jnp.exp(s - jnp.max(s, axis=-1, keepdims=True))
        den = jnp.sum(p, axis=-1, keepdims=True)
        o = jnp.dot(p.astype(BF16), vv[:, sl], preferred_element_type=F32) / den
        outs.append(o.astype(BF16))
    y_mem = jnp.dot(jnp.concatenate(outs, axis=1), wo_ref[...], preferred_element_type=F32)
    merged = merged + _sigmoid(jnp.dot(h, wg_ref[:, 2 * d:3 * d], preferred_element_type=F32)) * y_mem

    o_ref[0] = x + jnp.dot(merged.astype(BF16), wout_ref[...], preferred_element_type=F32)


def _mix_call(x, yt, kmem, vmem, g_mix, wc, wq, wg, dw, dwb, lng, lnb, wpw, wglu, wo, wout):
    b, s, d = x.shape
    ts = TOKEN_TILE
    nst = s // ts
    cw = dw.shape[1]
    m = kmem.shape[1]
    consts = [g_mix, wc, wq, wg, dw, dwb, lng, lnb, wpw, wglu, wo, wout]
    return pl.pallas_call(
        _mix_kernel,
        grid=(b, nst),
        in_specs=[pl.BlockSpec((1, ts, d), lambda i, j: (i, j, 0)),
                  pl.BlockSpec((yt.shape[0], ts), lambda i, j: (0, i * nst + j)),
                  pl.BlockSpec((1, m, kmem.shape[2]), lambda i, j: (i, 0, 0)),
                  pl.BlockSpec((1, m, vmem.shape[2]), lambda i, j: (i, 0, 0))]
                 + [_const_spec(c.shape) for c in consts],
        out_specs=pl.BlockSpec((1, ts, d), lambda i, j: (i, j, 0)),
        out_shape=jax.ShapeDtypeStruct((b, s, d), F32),
        scratch_shapes=[pltpu.VMEM((ts + CONV_HALO, cw), F32),
                        pltpu.VMEM((ts + CONV_HALO, cw), F32)],
        compiler_params=pltpu.CompilerParams(
            dimension_semantics=("arbitrary", "arbitrary"), vmem_limit_bytes=VMEM_LIMIT),
        name="mix",
    )(x, yt, kmem, vmem, *consts)


def _route_kernel(x_ref, g_ref, wr_ref, br_ref, xs_ref, info_ref, nch_ref):
    ts = x_ref.shape[0]
    cap = xs_ref.shape[0]
    h2 = _rms(x_ref[...], g_ref[...])
    hb = h2.astype(BF16)
    hl = (h2 - hb.astype(F32)).astype(BF16)
    wr = wr_ref[...]
    wb = wr.astype(BF16)
    wl = (wr - wb.astype(F32)).astype(BF16)
    logits = (jnp.dot(hb, wb, preferred_element_type=F32) + jnp.dot(hl, wb, preferred_element_type=F32)
              + jnp.dot(hb, wl, preferred_element_type=F32)) + br_ref[...]
    lt = logits.T
    le = lt[0:N_EXPERTS]
    lg = lt[N_EXPERTS:N_EXPERTS + 8]
    neg = jnp.float32(-1e30)
    big = jnp.float32(1e9)
    g_f = lax.broadcasted_iota(jnp.int32, (8, ts), 0).astype(F32)
    e_f = lax.broadcasted_iota(jnp.int32, (N_EXPERTS, ts), 0).astype(F32)

    gmask = g_f < MOE_GROUPS
    gmax = jnp.max(jnp.where(gmask, lg, neg), axis=0, keepdims=True)
    gidx = jnp.min(jnp.where(gmask & (lg == gmax), g_f, big), axis=0, keepdims=True)
    gsum = jnp.sum(jnp.where(gmask, jnp.exp(jnp.minimum(lg - gmax, 0.0)), 0.0), axis=0, keepdims=True)
    p_top = 1.0 / gsum
    emask = jnp.floor(e_f * (1.0 / EXPERTS_PER_GROUP)) == gidx
    m1 = jnp.max(jnp.where(emask, le, neg), axis=0, keepdims=True)
    i1 = jnp.min(jnp.where(emask & (le == m1), e_f, big), axis=0, keepdims=True)
    emask2 = emask & (e_f != i1)
    m2 = jnp.max(jnp.where(emask2, le, neg), axis=0, keepdims=True)
    i2 = jnp.min(jnp.where(emask2 & (le == m2), e_f, big), axis=0, keepdims=True)
    r = jnp.exp(m2 - m1)
    w1 = p_top / (1.0 + r)
    w2 = p_top * r / (1.0 + r)

    sel1 = e_f == i1
    sel2 = e_f == i2
    occ = jnp.where(sel1 | sel2, 1.0, 0.0)
    tr = lax.broadcasted_iota(jnp.int32, (ts, ts), 0)
    tc = lax.broadcasted_iota(jnp.int32, (ts, ts), 1)
    earlier = jnp.where(tr < tc, 1.0, 0.0).astype(BF16)
    rank = jnp.dot(occ.astype(BF16), earlier, preferred_element_type=F32)
    cnt = jnp.sum(occ, axis=1, keepdims=True)
    nch = jnp.floor((cnt + (CHUNK_ROWS - 1)) * (1.0 / CHUNK_ROWS))
    er = lax.broadcasted_iota(jnp.int32, (N_EXPERTS, N_EXPERTS), 0)
    ec = lax.broadcasted_iota(jnp.int32, (N_EXPERTS, N_EXPERTS), 1)
    lower = jnp.where(ec < er, 1.0, 0.0).astype(BF16)
    nch_b = jnp.broadcast_to(nch, (N_EXPERTS, LANES))
    start = jnp.dot(lower, nch_b.astype(BF16), preferred_element_type=F32)[:, 0:1] * CHUNK_ROWS
    slot = start + rank
    pos1 = jnp.sum(jnp.where(sel1, slot, 0.0), axis=0, keepdims=True)
    pos2 = jnp.sum(jnp.where(sel2, slot, 0.0), axis=0, keepdims=True)

    rowid = lax.broadcasted_iota(jnp.int32, (cap, ts), 0)
    p = jnp.where((rowid == pos1.astype(jnp.int32)) | (rowid == pos2.astype(jnp.int32)), 1.0, 0.0)
    xs_ref[...] = jnp.dot(p.astype(BF16), hb, preferred_element_type=F32).astype(BF16)
    sub = lax.broadcasted_iota(jnp.int32, (LANES, ts), 0)
    info_t = jnp.where(sub == 0, pos1, jnp.where(sub == 1, pos2,
                       jnp.where(sub == 2, w1, jnp.where(sub == 3, w2, 0.0))))
    info_ref[...] = info_t.T
    nch_ref[0] = jnp.concatenate([nch_b, jnp.zeros((LANES - N_EXPERTS, LANES), F32)], axis=0).T[0:8, :]


def _route_call(x1, g_ffn, wr, br, cap):
    t, d = x1.shape
    ts = TOKEN_TILE
    nt = t // ts
    return pl.pallas_call(
        _route_kernel,
        grid=(nt,),
        in_specs=[pl.BlockSpec((ts, d), lambda i: (i, 0)),
                  pl.BlockSpec((1, d), lambda i: (0, 0)),
                  pl.BlockSpec(wr.shape, lambda i: (0, 0)),
                  pl.BlockSpec((1, LANES), lambda i: (0, 0))],
        out_specs=[pl.BlockSpec((cap, d), lambda i: (i, 0)),
                   pl.BlockSpec((ts, LANES), lambda i: (i, 0)),
                   pl.BlockSpec((1, 8, LANES), lambda i: (i, 0, 0))],
        out_shape=[jax.ShapeDtypeStruct((nt * cap, d), BF16),
                   jax.ShapeDtypeStruct((t, LANES), F32),
                   jax.ShapeDtypeStruct((nt, 8, LANES), F32)],
        compiler_params=pltpu.CompilerParams(vmem_limit_bytes=VMEM_LIMIT),
        name="route",
    )(x1, g_ffn, wr, br)


def _expert_kernel(te_ref, nu_ref, src_ref, xs_ref, wg_ref, wu_ref, wd_ref, o_ref, xbuf_ref, sem_ref):
    i = pl.program_id(0)
    n_used = nu_ref[0]
    slot = i % 2

    def chunk_copy(tile, c, buf):
        return pltpu.make_async_copy(
            xs_ref.at[_chunk_rows(src_ref[tile * CHUNKS_PER_ETILE + c])],
            xbuf_ref.at[buf, pl.ds(c * CHUNK_ROWS, CHUNK_ROWS)], sem_ref.at[buf])

    def gather(tile, buf):
        for c in range(CHUNKS_PER_ETILE):
            chunk_copy(tile, c, buf).start(priority=c % 2)

    def drain(tile, buf):
        for c in range(CHUNKS_PER_ETILE):
            chunk_copy(tile, c, buf).wait()

    @pl.when(i == 0)
    def _():
        gather(0, 0)

    @pl.when(i < n_used)
    def _():
        nxt = jnp.minimum(i + 1, n_used - 1)
        gather(nxt, 1 - slot)
        drain(i, slot)
        x = xbuf_ref[slot]
        gate = jnp.dot(x, wg_ref[0].astype(BF16), preferred_element_type=F32)
        up = jnp.dot(x, wu_ref[0].astype(BF16), preferred_element_type=F32)
        act = (gate * _sigmoid(gate) * up).astype(BF16)
        o_ref[...] = jnp.dot(act, wd_ref[0].astype(BF16), preferred_element_type=F32).astype(BF16)

        @pl.when(i == n_used - 1)
        def _():
            drain(nxt, 1 - slot)

    @pl.when(i >= n_used)
    def _():
        o_ref[...] = jnp.zeros_like(o_ref)


def _expert_call(tile_expert, n_used, chunk_src, xs, wg, wu, wd):
    d = xs.shape[1]
    de = wg.shape[2]
    tm = EXPERT_TILE
    nt = tile_expert.shape[0]

    def w_map(i, te, nu, src):
        return (te[jnp.maximum(jnp.minimum(i, nu[0] - 1), 0)], 0, 0)

    return pl.pallas_call(
        _expert_kernel,
        grid_spec=pltpu.PrefetchScalarGridSpec(
            num_scalar_prefetch=3,
            grid=(nt,),
            in_specs=[pl.BlockSpec(memory_space=pl.ANY),
                      pl.BlockSpec((1, d, de), w_map),
                      pl.BlockSpec((1, d, de), w_map),
                      pl.BlockSpec((1, de, d), w_map)],
            out_specs=pl.BlockSpec((tm, d), lambda i, te, nu, src: (i, 0)),
            scratch_shapes=[pltpu.VMEM((2, tm, d), BF16),
                            pltpu.SemaphoreType.DMA((2,))]),
        out_shape=jax.ShapeDtypeStruct((nt * tm, d), BF16),
        compiler_params=pltpu.CompilerParams(dimension_semantics=("arbitrary",)),
        name="expert",
    )(tile_expert, n_used, chunk_src, xs, wg, wu, wd)


def _combine_kernel(dst_ref, x_ref, info_ref, g_ref, ys_ref, o_ref, ybuf_ref, sem_ref):
    i = pl.program_id(0)
    n_tiles = pl.num_programs(0)
    ts = x_ref.shape[0]
    cap = ybuf_ref.shape[1]
    cpt = cap // CHUNK_ROWS
    slot = i % 2

    def chunk_copy(tile, k, buf):
        return pltpu.make_async_copy(ys_ref.at[_chunk_rows(dst_ref[tile * cpt + k])],
                                     ybuf_ref.at[buf, pl.ds(k * CHUNK_ROWS, CHUNK_ROWS)], sem_ref.at[buf])

    def gather(tile, buf):
        for k in range(cpt):
            chunk_copy(tile, k, buf).start(priority=k % 2)

    def drain(tile, buf):
        for k in range(cpt):
            chunk_copy(tile, k, buf).wait()

    @pl.when(i == 0)
    def _():
        gather(0, 0)

    nxt = jnp.minimum(i + 1, n_tiles - 1)
    gather(nxt, 1 - slot)
    drain(i, slot)

    info = info_ref[...]
    pos1 = info[:, 0:1].astype(jnp.int32)
    pos2 = info[:, 1:2].astype(jnp.int32)
    w1 = info[:, 2:3]
    w2 = info[:, 3:4]
    rowid = lax.broadcasted_iota(jnp.int32, (ts, cap), 1)
    ys = ybuf_ref[slot]
    pw = jnp.where(rowid == pos1, w1, jnp.where(rowid == pos2, w2, 0.0)).astype(BF16)
    y = jnp.dot(pw, ys, preferred_element_type=F32)
    o_ref[...] = _rms(x_ref[...] + y, g_ref[...])

    @pl.when(i == n_tiles - 1)
    def _():
        drain(nxt, 1 - slot)


def _combine_call(chunk_dst, x1, info, g_final, ys_sorted, cap):
    t, d = x1.shape
    ts = TOKEN_TILE
    return pl.pallas_call(
        _combine_kernel,
        grid_spec=pltpu.PrefetchScalarGridSpec(
            num_scalar_prefetch=1,
            grid=(t // ts,),
            in_specs=[pl.BlockSpec((ts, d), lambda i, dst: (i, 0)),
                      pl.BlockSpec((ts, LANES), lambda i, dst: (i, 0)),
                      pl.BlockSpec((1, d), lambda i, dst: (0, 0)),
                      pl.BlockSpec(memory_space=pl.ANY)],
            out_specs=pl.BlockSpec((ts, d), lambda i, dst: (i, 0)),
            scratch_shapes=[pltpu.VMEM((2, cap, d), BF16),
                            pltpu.SemaphoreType.DMA((2,))]),
        out_shape=jax.ShapeDtypeStruct((t, d), F32),
        compiler_params=pltpu.CompilerParams(dimension_semantics=("arbitrary",),
                                             vmem_limit_bytes=VMEM_LIMIT),
        name="combine",
    )(chunk_dst, x1, info, g_final, ys_sorted)


def _run_tables(nch, cpt, n_etiles):
    n_tiles, n_exp = nch.shape
    per = CHUNKS_PER_ETILE
    tcum = jnp.cumsum(nch, axis=1)
    toff = tcum - nch
    ecum = jnp.cumsum(nch, axis=0)
    etot = ecum[-1]
    eseg = -(-etot // per) * per
    segcum = jnp.cumsum(eseg)
    ebase = segcum - eseg
    eoff = ebase[None, :] + ecum - nch
    n_used = (segcum[-1] // per).reshape(1)

    k = jnp.arange(cpt, dtype=jnp.int32)[None, :, None]
    in_run = (k >= toff[:, None, :]) & (k < tcum[:, None, :])
    chunk_dst = jnp.sum(jnp.where(in_run, (eoff - toff)[:, None, :] + k, 0), axis=-1)

    c = jnp.arange(n_etiles * per, dtype=jnp.int32)[:, None]
    lo = eoff.reshape(1, -1)
    hit = (c >= lo) & (c < lo + nch.reshape(1, -1))
    shift = (jnp.arange(n_tiles, dtype=jnp.int32)[:, None] * cpt + toff - eoff).reshape(1, -1)
    chunk_src = jnp.sum(jnp.where(hit, shift + c, 0), axis=-1)

    first = jnp.arange(n_etiles, dtype=jnp.int32)[:, None] * per
    owner = (first >= ebase[None, :]) & (first < segcum[None, :])
    tile_expert = jnp.sum(jnp.where(owner, jnp.arange(n_exp, dtype=jnp.int32)[None, :], 0), axis=-1)
    i32 = lambda a: a.reshape(-1).astype(jnp.int32)
    return i32(chunk_dst), i32(tile_expert), i32(n_used), i32(chunk_src)


def _layer(x, mem, g_mix, w_in, conv_dw, conv_dw_bias, conv_ln_g, conv_ln_b, w_conv_out,
           lam_re, lam_im, log_dt, b_re, b_im, c_re, c_im, ssm_d, w_ssm_glu, g_mem, w_mem_kv,
           w_mem_out, w_out, g_ffn, w_rg, b_rg, w_re, b_re_, w_eg, w_eu, w_ed, g_final):
    b, s, d = x.shape
    t = b * s
    cw = conv_dw.shape[1]
    sw = ssm_d.shape[0]
    qw = w_mem_out.shape[0]
    n_groups = sw // SSM_GROUP
    row = lambda a: a.reshape(1, -1)

    o0, o1, o2 = 2 * cw, 2 * cw + sw, 2 * cw + sw + qw
    wc = w_in[:, :o0].astype(BF16)
    w_ssm_t = w_in[:, o0:o1].T.astype(BF16)
    wq = w_in[:, o1:o2].astype(BF16)
    wg = w_in[:, o2:].astype(BF16)

    kmem, vmem = _kv_call(mem, row(g_mem), w_mem_kv.astype(BF16))

    ut = _ut_call(x.reshape(t, d), row(g_mix), w_ssm_t)
    kt, wz, vy, acoef = _ssm_tables(lam_re, lam_im, log_dt, b_re, b_im, c_re, c_im, ssm_d)
    yt = _ssm_call(ut, kt, wz, vy, acoef, b)

    dw = jnp.concatenate([conv_dw, jnp.zeros((CONV_HALO - CONV_KERNEL, cw), F32)], axis=0)
    x1 = _mix_call(x, yt, kmem, vmem, row(g_mix), wc, wq, wg, dw, row(conv_dw_bias),
                   row(conv_ln_g), row(conv_ln_b), w_conv_out.astype(BF16),
                   w_ssm_glu.astype(BF16), w_mem_out.astype(BF16), w_out.astype(BF16))
    x1 = x1.reshape(t, d)

    pad = LANES - N_EXPERTS - MOE_GROUPS
    wr = jnp.concatenate([w_re, w_rg, jnp.zeros((d, pad), F32)], axis=1)
    br = jnp.concatenate([b_re_, b_rg, jnp.zeros((pad,), F32)]).reshape(1, LANES)
    n_tiles = t // TOKEN_TILE
    max_tile_chunks = (2 * TOKEN_TILE + N_EXPERTS * (CHUNK_ROWS - 1)) // CHUNK_ROWS
    cpt = -(-max_tile_chunks // 8) * 8
    cap = cpt * CHUNK_ROWS
    xs_tiles, info, nch_f = _route_call(x1, row(g_ffn), wr, br, cap)

    max_chunks = n_tiles * max_tile_chunks + N_EXPERTS * (CHUNKS_PER_ETILE - 1)
    n_etiles = -(-max_chunks // CHUNKS_PER_ETILE)
    nch = nch_f[:, 0, :N_EXPERTS].astype(jnp.int32)
    chunk_dst, tile_expert, n_used, chunk_src = _run_tables(nch, cpt, n_etiles)

    ys_sorted = _expert_call(tile_expert, n_used, chunk_src, xs_tiles, w_eg, w_eu, w_ed)
    out = _combine_call(chunk_dst, x1, info, row(g_final), ys_sorted, cap)
    return out.reshape(b, s, d)


def kernel(x, mem, g_mix, w_in, conv_dw, conv_dw_bias, conv_ln_g, conv_ln_b, w_conv_out, ssm_lambda_re, ssm_lambda_im, ssm_log_dt, ssm_b_re, ssm_b_im, ssm_c_re, ssm_c_im, ssm_d, w_ssm_glu, g_mem, w_mem_kv, w_mem_out, w_out, g_ffn, w_router_group, b_router_group, w_router_expert, b_router_expert, w_exp_gate, w_exp_up, w_exp_down, g_final):
    assert g_mix.shape[0] == 1, "the problem fixes one layer"
    l = 0
    return _layer(
        x, mem, g_mix[l], w_in[l], conv_dw[l], conv_dw_bias[l], conv_ln_g[l], conv_ln_b[l],
        w_conv_out[l], ssm_lambda_re[l], ssm_lambda_im[l], ssm_log_dt[l], ssm_b_re[l],
        ssm_b_im[l], ssm_c_re[l], ssm_c_im[l], ssm_d[l], w_ssm_glu[l], g_mem[l], w_mem_kv[l],
        w_mem_out[l], w_out[l], g_ffn[l], w_router_group[l], b_router_group[l],
        w_router_expert[l], b_router_expert[l], w_exp_gate[l], w_exp_up[l], w_exp_down[l], g_final)
```

```python
import functools

import jax
import jax.numpy as jnp
from jax import lax
from jax.experimental import pallas as pl
from jax.experimental.pallas import tpu as pltpu

F32 = jnp.float32
BF16 = jnp.bfloat16
EPS = 1e-6

LANES = 128
CHUNK_ROWS = 16
SSM_GROUP = 16
SSM_STATE = 64
SSM_BLOCK = LANES
CONV_KERNEL = 31
CONV_HALO = 32
HEADS = 4
HEAD_DIM = 128
MOE_GROUPS = 4
EXPERTS_PER_GROUP = 8
N_EXPERTS = MOE_GROUPS * EXPERTS_PER_GROUP
TOKEN_TILE = 512
EXPERT_TILE = 512
CHUNKS_PER_ETILE = EXPERT_TILE // CHUNK_ROWS
VMEM_LIMIT = 56 * 1024 * 1024


def _rms(x, g):
    return x * lax.rsqrt(jnp.mean(x * x, axis=-1, keepdims=True) + EPS) * g


def _sigmoid(x):
    return 0.5 * jnp.tanh(0.5 * x) + 0.5


def _const_spec(shape):
    zeros = (0,) * len(shape)
    return pl.BlockSpec(shape, lambda *_: zeros, pipeline_mode=pl.Buffered(1))


def _chunk_rows(chunk):
    return pl.ds(pl.multiple_of(chunk * CHUNK_ROWS, CHUNK_ROWS), CHUNK_ROWS)


def _kv_kernel(mem_ref, g_ref, w_ref, k_ref, v_ref):
    width = k_ref.shape[-1]
    mn = _rms(mem_ref[0], g_ref[...]).astype(BF16)
    kv = jnp.dot(mn, w_ref[...], preferred_element_type=F32)
    k_ref[0] = kv[:, :width].astype(BF16)
    v_ref[0] = kv[:, width:].astype(BF16)


def _kv_call(mem, g_mem, w_kv):
    b, m, d = mem.shape
    width = w_kv.shape[1] // 2
    return pl.pallas_call(
        _kv_kernel,
        grid=(b,),
        in_specs=[pl.BlockSpec((1, m, d), lambda i: (i, 0, 0)),
                  pl.BlockSpec((1, d), lambda i: (0, 0)),
                  pl.BlockSpec(w_kv.shape, lambda i: (0, 0))],
        out_specs=[pl.BlockSpec((1, m, width), lambda i: (i, 0, 0)),
                   pl.BlockSpec((1, m, width), lambda i: (i, 0, 0))],
        out_shape=[jax.ShapeDtypeStruct((b, m, width), BF16)] * 2,
        name="kv",
    )(mem, g_mem, w_kv)


def _ut_kernel(x_ref, g_ref, wt_ref, o_ref):
    h = _rms(x_ref[...], g_ref[...]).astype(BF16)
    ut = lax.dot_general(wt_ref[...], h, (((1,), (1,)), ((), ())), preferred_element_type=F32)
    ut = ut.reshape(ut.shape[0], ut.shape[1] // SSM_BLOCK, SSM_BLOCK)
    o_ref[...] = ut.reshape(o_ref.shape).astype(BF16)


def _ut_call(x2, g_mix, w_ssm_t):
    t, d = x2.shape
    c = w_ssm_t.shape[0]
    ts = 8 * SSM_BLOCK
    n_groups = c // SSM_GROUP
    return pl.pallas_call(
        _ut_kernel,
        grid=(t // ts,),
        in_specs=[pl.BlockSpec((ts, d), lambda i: (i, 0)),
                  pl.BlockSpec((1, d), lambda i: (0, 0)),
                  pl.BlockSpec((c, d), lambda i: (0, 0))],
        out_specs=pl.BlockSpec((n_groups, SSM_GROUP, ts // SSM_BLOCK, SSM_BLOCK),
                               lambda i: (0, 0, i, 0)),
        out_shape=jax.ShapeDtypeStruct((n_groups, SSM_GROUP, t // SSM_BLOCK, SSM_BLOCK), BF16),
        name="ut",
    )(x2, g_mix, w_ssm_t)


def _ssm_kernel(u_ref, kt_ref, w_ref, v_ref, a_ref, y_ref, acc_ref, z_ref, zs_ref, s_ref, slab_ref,
                *, n_batch):
    rows = u_ref.shape[2]
    n_blocks = rows // n_batch
    ri = lax.broadcasted_iota(jnp.int32, (SSM_BLOCK, SSM_BLOCK), 0)
    ci = lax.broadcasted_iota(jnp.int32, (SSM_BLOCK, SSM_BLOCK), 1)
    causal = ci >= ri

    n_pairs = SSM_GROUP // 2

    def build(cp, slot):
        for half in range(2):
            kt = kt_ref[0, 2 * cp + half]
            for c in range(SSM_GROUP):
                xb = jnp.broadcast_to(kt[c:c + 1, :], (SSM_BLOCK, SSM_BLOCK))
                toe = pltpu.roll(xb, 0, 1, stride=1, stride_axis=0)
                slab_ref[slot, half * SSM_BLOCK:(half + 1) * SSM_BLOCK, c * SSM_BLOCK:(c + 1) * SSM_BLOCK] = (
                    jnp.where(causal, toe, 0.0).astype(BF16))

    def apply(cp, slot, first):
        x2 = jnp.concatenate([u_ref[0, 2 * cp], u_ref[0, 2 * cp + 1]], axis=1)
        part = jnp.dot(x2, slab_ref[slot], preferred_element_type=F32)
        wrow = pl.multiple_of(cp * 2 * SSM_BLOCK, 2 * SSM_BLOCK)
        zpart = jnp.dot(x2, w_ref[0, pl.ds(wrow, 2 * SSM_BLOCK), :], preferred_element_type=F32)
        if first:
            acc_ref[...] = part
            z_ref[...] = zpart
        else:
            acc_ref[...] += part
            z_ref[...] += zpart

    build(0, 0)
    build(1, 1)
    apply(0, 0, True)

    def two(it, carry):
        cp = 2 * it + 1
        build(cp + 1, 0)
        apply(cp, 1, False)
        build(cp + 2, 1)
        apply(cp + 1, 0, False)
        return carry

    lax.fori_loop(0, (n_pairs - 2) // 2, two, 0)
    apply(n_pairs - 1, 1, False)

    a_full = a_ref[0, 0:1, :]
    a_swap = a_ref[0, 1:2, :]
    zs_ref[...] = pltpu.roll(z_ref[...], SSM_STATE, 1)
    st = jnp.zeros((n_batch, 2 * SSM_STATE), F32)
    sw = jnp.zeros((n_batch, 2 * SSM_STATE), F32)
    for blk in range(n_blocks):
        sl = pl.ds(blk, n_batch, stride=n_blocks)
        s_ref[sl, :] = st
        st, sw = (a_full * st + a_swap * sw + z_ref[sl, :],
                  a_full * sw - a_swap * st + zs_ref[sl, :])

    y = acc_ref[...] + jnp.dot(s_ref[...].astype(BF16), v_ref[0], preferred_element_type=F32)
    y3 = jnp.stack([y[:, c * SSM_BLOCK:(c + 1) * SSM_BLOCK] for c in range(SSM_GROUP)], axis=0)
    y_ref[...] = y3.reshape(y_ref.shape)


def _ssm_call(u4, kt, wz, vy, acoef, n_batch):
    g, c, rows, blk = u4.shape
    width = c * blk
    return pl.pallas_call(
        functools.partial(_ssm_kernel, n_batch=n_batch),
        grid=(g,),
        in_specs=[pl.BlockSpec((1, c, rows, blk), lambda i: (i, 0, 0, 0)),
                  pl.BlockSpec((1, c, c, blk), lambda i: (i, 0, 0, 0)),
                  pl.BlockSpec((1, width, 2 * SSM_STATE), lambda i: (i, 0, 0)),
                  pl.BlockSpec((1, 2 * SSM_STATE, width), lambda i: (i, 0, 0)),
                  pl.BlockSpec((1, 2, 2 * SSM_STATE), lambda i: (i, 0, 0))],
        out_specs=pl.BlockSpec((c, rows * blk), lambda i: (i, 0)),
        out_shape=jax.ShapeDtypeStruct((g * c, rows * blk), F32),
        scratch_shapes=[pltpu.VMEM((rows, width), F32),
                        pltpu.VMEM((rows, 2 * SSM_STATE), F32),
                        pltpu.VMEM((rows, 2 * SSM_STATE), F32),
                        pltpu.VMEM((rows, 2 * SSM_STATE), F32),
                        pltpu.VMEM((2, 2 * blk, width), BF16)],
        name="ssm",
    )(u4, kt, wz, vy, acoef)


def _ssm_tables(lam_re, lam_im, log_dt, b_re, b_im, c_re, c_im, d):
    hi = lax.Precision.HIGHEST
    g = lam_re.shape[0]
    dt = jnp.exp(log_dt)[:, None]
    er, ei = lam_re * dt, lam_im * dt
    cat = jnp.concatenate
    kk = jnp.arange(SSM_BLOCK + 1, dtype=F32)
    kdesc = jnp.arange(SSM_BLOCK - 1, -1, -1, dtype=F32)[None, :, None]
    ppk_m, ppk_a = jnp.exp(er[:, :, None] * kk), ei[:, :, None] * kk
    ppk_r, ppk_i = ppk_m * jnp.cos(ppk_a), ppk_m * jnp.sin(ppk_a)
    prev_m, prev_a = jnp.exp(kdesc * er[:, None, :]), kdesc * ei[:, None, :]
    prr, pir = prev_m * jnp.cos(prev_a), prev_m * jnp.sin(prev_a)
    nr, ni = ppk_r[:, :, 1] - 1.0, ppk_i[:, :, 1]
    den = lam_re * lam_re + lam_im * lam_im
    fr = (nr * lam_re + ni * lam_im) / den
    fi = (ni * lam_re - nr * lam_im) / den
    bbr = jnp.swapaxes(fr[:, :, None] * b_re - fi[:, :, None] * b_im, 1, 2)
    bbi = jnp.swapaxes(fr[:, :, None] * b_im + fi[:, :, None] * b_re, 1, 2)
    cb_r = c_re[:, None] * bbr[:, :, None, :] - c_im[:, None] * bbi[:, :, None, :]
    cb_i = c_re[:, None] * bbi[:, :, None, :] + c_im[:, None] * bbr[:, :, None, :]
    cb = cat([cb_r, -cb_i], axis=-1).reshape(g, SSM_GROUP * SSM_GROUP, 2 * SSM_STATE)
    pk = cat([ppk_r[:, :, :SSM_BLOCK], ppk_i[:, :, :SSM_BLOCK]], axis=1)
    kt = jnp.einsum("gmq,gqk->gmk", cb, pk, precision=hi).reshape(g, SSM_GROUP, SSM_GROUP, SSM_BLOCK)
    dmat = jnp.eye(SSM_GROUP, dtype=F32)[None] * d.reshape(g, 1, SSM_GROUP)
    kt = kt + dmat[..., None] * (jnp.arange(SSM_BLOCK) == 0).astype(F32)
    wz = (cat([prr, prr], -1)[:, None] * cat([bbr, bbi], -1)[:, :, None, :]
          + cat([pir, pir], -1)[:, None] * cat([-bbi, bbr], -1)[:, :, None, :])
    wz = wz.astype(BF16).reshape(g, SSM_GROUP * SSM_BLOCK, 2 * SSM_STATE)
    pt_r, pt_i = ppk_r[:, :, 1:], ppk_i[:, :, 1:]
    cq_a = jnp.swapaxes(cat([c_re, -c_re], -1), 1, 2)[..., None]
    cq_b = jnp.swapaxes(cat([-c_im, -c_im], -1), 1, 2)[..., None]
    vy = cq_a * cat([pt_r, pt_i], axis=1)[:, :, None, :] + cq_b * cat([pt_i, pt_r], axis=1)[:, :, None, :]
    vy = vy.astype(BF16).reshape(g, 2 * SSM_STATE, SSM_GROUP * SSM_BLOCK)
    ar, ai = ppk_r[:, :, SSM_BLOCK], ppk_i[:, :, SSM_BLOCK]
    acoef = jnp.stack([cat([ar, ar], -1), cat([-ai, ai], -1)], axis=1)
    return kt, wz, vy, acoef


def _mix_kernel(x_ref, yt_ref, k_ref, v_ref, g_ref, wc_ref, wq_ref, wg_ref, dw_ref, dwb_ref,
                lng_ref, lnb_ref, wpw_ref, wglu_ref, wo_ref, wout_ref, o_ref, vext_ref, vsh_ref):
    ts = x_ref.shape[1]
    d = x_ref.shape[2]
    cw = dw_ref.shape[1]
    x = x_ref[0]
    h = _rms(x, g_ref[...]).astype(BF16)

    ci = jnp.dot(h, wc_ref[...], preferred_element_type=F32)
    v = ci[:, :cw] * _sigmoid(ci[:, cw:])
    q = jnp.dot(h, wq_ref[...], preferred_element_type=F32)

    @pl.when(pl.program_id(1) == 0)
    def _():
        vext_ref[0:CONV_HALO, :] = jnp.zeros((CONV_HALO, cw), F32)

    vext_ref[CONV_HALO:CONV_HALO + ts, :] = v
    acc = jnp.broadcast_to(dwb_ref[...], (ts, cw))
    first = CONV_HALO - (CONV_KERNEL - 1)
    for phase in range(8):
        offs = [first + k for k in range(CONV_KERNEL) if (first + k) % 8 == phase]
        if not offs:
            continue
        span = offs[-1] - offs[0] + ts
        vsh_ref[0:span, :] = vext_ref[offs[0]:offs[0] + span, :]
        for off in offs:
            acc = acc + dw_ref[off - first:off - first + 1, :] * vsh_ref[off - offs[0]:off - offs[0] + ts, :]
    vext_ref[0:CONV_HALO, :] = vext_ref[ts:ts + CONV_HALO, :]
    mu = jnp.mean(acc, axis=-1, keepdims=True)
    xc = acc - mu
    var = jnp.mean(xc * xc, axis=-1, keepdims=True)
    ln = xc * lax.rsqrt(var + EPS) * lng_ref[...] + lnb_ref[...]
    sw = ln * _sigmoid(ln)
    y_conv = jnp.dot(sw.astype(BF16), wpw_ref[...], preferred_element_type=F32)
    merged = _sigmoid(jnp.dot(h, wg_ref[:, 0:d], preferred_element_type=F32)) * y_conv

    gy = jax.nn.gelu(yt_ref[...]).astype(BF16)
    z = lax.dot_general(gy, wglu_ref[...], (((0,), (0,)), ((), ())), preferred_element_type=F32)
    y_ssm = z[:, :d] * _sigmoid(z[:, d:])
    merged = merged + _sigmoid(jnp.dot(h, wg_ref[:, d:2 * d], preferred_element_type=F32)) * y_ssm

    kk = k_ref[0]
    vv = v_ref[0]
    outs = []
    for hd in range(HEADS):
        sl = slice(hd * HEAD_DIM, (hd + 1) * HEAD_DIM)
        s = lax.dot_general(q[:, sl].astype(BF16), kk[:, sl], (((1,), (1,)), ((), ())),
                            preferred_element_type=F32) * (HEAD_DIM ** -0.5)
        p = jnp.exp(s - jnp.max(s, axis=-1, keepdims=True))
        den = jnp.sum(p, axis=-1, keepdims=True)
        o = jnp.dot(p.astype(BF16), vv[:, sl], preferred_element_type=F32) / den
        outs.append(o.astype(BF16))
    y_mem = jnp.dot(jnp.concatenate(outs, axis=1), wo_ref[...], preferred_element_type=F32)
    merged = merged + _sigmoid(jnp.dot(h, wg_ref[:, 2 * d:3 * d], preferred_element_type=F32)) * y_mem

    o_ref[0] = x + jnp.dot(merged.astype(BF16), wout_ref[...], preferred_element_type=F32)


def _mix_call(x, yt, kmem, vmem, g_mix, wc, wq, wg, dw, dwb, lng, lnb, wpw, wglu, wo, wout):
    b, s, d = x.shape
    ts = TOKEN_TILE
    nst = s // ts
    cw = dw.shape[1]
    m = kmem.shape[1]
    consts = [g_mix, wc, wq, wg, dw, dwb, lng, lnb, wpw, wglu, wo, wout]
    return pl.pallas_call(
        _mix_kernel,
        grid=(b, nst),
        in_specs=[pl.BlockSpec((1, ts, d), lambda i, j: (i, j, 0)),
                  pl.BlockSpec((yt.shape[0], ts), lambda i, j: (0, i * nst + j)),
                  pl.BlockSpec((1, m, kmem.shape[2]), lambda i, j: (i, 0, 0)),
                  pl.BlockSpec((1, m, vmem.shape[2]), lambda i, j: (i, 0, 0))]
                 + [_const_spec(c.shape) for c in consts],
        out_specs=pl.BlockSpec((1, ts, d), lambda i, j: (i, j, 0)),
        out_shape=jax.ShapeDtypeStruct((b, s, d), F32),
        scratch_shapes=[pltpu.VMEM((ts + CONV_HALO, cw), F32),
                        pltpu.VMEM((ts + CONV_HALO, cw), F32)],
        compiler_params=pltpu.CompilerParams(
            dimension_semantics=("arbitrary", "arbitrary"), vmem_limit_bytes=VMEM_LIMIT),
        name="mix",
    )(x, yt, kmem, vmem, *consts)


def _route_kernel(x_ref, g_ref, wr_ref, br_ref, xs_ref, info_ref, nch_ref):
    ts = x_ref.shape[0]
    cap = xs_ref.shape[0]
    h2 = _rms(x_ref[...], g_ref[...])
    hb = h2.astype(BF16)
    hl = (h2 - hb.astype(F32)).astype(BF16)
    wr = wr_ref[...]
    wb = wr.astype(BF16)
    wl = (wr - wb.astype(F32)).astype(BF16)
    logits = (jnp.dot(hb, wb, preferred_element_type=F32) + jnp.dot(hl, wb, preferred_element_type=F32)
              + jnp.dot(hb, wl, preferred_element_type=F32)) + br_ref[...]
    lt = logits.T
    le = lt[0:N_EXPERTS]
    lg = lt[N_EXPERTS:N_EXPERTS + 8]
    neg = jnp.float32(-1e30)
    big = jnp.float32(1e9)
    g_f = lax.broadcasted_iota(jnp.int32, (8, ts), 0).astype(F32)
    e_f = lax.broadcasted_iota(jnp.int32, (N_EXPERTS, ts), 0).astype(F32)

    gmask = g_f < MOE_GROUPS
    gmax = jnp.max(jnp.where(gmask, lg, neg), axis=0, keepdims=True)
    gidx = jnp.min(jnp.where(gmask & (lg == gmax), g_f, big), axis=0, keepdims=True)
    gsum = jnp.sum(jnp.where(gmask, jnp.exp(jnp.minimum(lg - gmax, 0.0)), 0.0), axis=0, keepdims=True)
    p_top = 1.0 / gsum
    emask = jnp.floor(e_f * (1.0 / EXPERTS_PER_GROUP)) == gidx
    m1 = jnp.max(jnp.where(emask, le, neg), axis=0, keepdims=True)
    i1 = jnp.min(jnp.where(emask & (le == m1), e_f, big), axis=0, keepdims=True)
    emask2 = emask & (e_f != i1)
    m2 = jnp.max(jnp.where(emask2, le, neg), axis=0, keepdims=True)
    i2 = jnp.min(jnp.where(emask2 & (le == m2), e_f, big), axis=0, keepdims=True)
    r = jnp.exp(m2 - m1)
    w1 = p_top / (1.0 + r)
    w2 = p_top * r / (1.0 + r)

    sel1 = e_f == i1
    sel2 = e_f == i2
    occ = jnp.where(sel1 | sel2, 1.0, 0.0)
    tr = lax.broadcasted_iota(jnp.int32, (ts, ts), 0)
    tc = lax.broadcasted_iota(jnp.int32, (ts, ts), 1)
    earlier = jnp.where(tr < tc, 1.0, 0.0).astype(BF16)
    rank = jnp.dot(occ.astype(BF16), earlier, preferred_element_type=F32)
    cnt = jnp.sum(occ, axis=1, keepdims=True)
    nch = jnp.floor((cnt + (CHUNK_ROWS - 1)) * (1.0 / CHUNK_ROWS))
    er = lax.broadcasted_iota(jnp.int32, (N_EXPERTS, N_EXPERTS), 0)
    ec = lax.broadcasted_iota(jnp.int32, (N_EXPERTS, N_EXPERTS), 1)
    lower = jnp.where(ec < er, 1.0, 0.0).astype(BF16)
    nch_b = jnp.broadcast_to(nch, (N_EXPERTS, LANES))
    start = jnp.dot(lower, nch_b.astype(BF16), preferred_element_type=F32)[:, 0:1] * CHUNK_ROWS
    slot = start + rank
    pos1 = jnp.sum(jnp.where(sel1, slot, 0.0), axis=0, keepdims=True)
    pos2 = jnp.sum(jnp.where(sel2, slot, 0.0), axis=0, keepdims=True)

    rowid = lax.broadcasted_iota(jnp.int32, (cap, ts), 0)
    p = jnp.where((rowid == pos1.astype(jnp.int32)) | (rowid == pos2.astype(jnp.int32)), 1.0, 0.0)
    xs_ref[...] = jnp.dot(p.astype(BF16), hb, preferred_element_type=F32).astype(BF16)
    sub = lax.broadcasted_iota(jnp.int32, (LANES, ts), 0)
    info_t = jnp.where(sub == 0, pos1, jnp.where(sub == 1, pos2,
                       jnp.where(sub == 2, w1, jnp.where(sub == 3, w2, 0.0))))
    info_ref[...] = info_t.T
    nch_ref[0] = jnp.concatenate([nch_b, jnp.zeros((LANES - N_EXPERTS, LANES), F32)], axis=0).T[0:8, :]


def _route_call(x1, g_ffn, wr, br, cap):
    t, d = x1.shape
    ts = TOKEN_TILE
    nt = t // ts
    return pl.pallas_call(
        _route_kernel,
        grid=(nt,),
        in_specs=[pl.BlockSpec((ts, d), lambda i: (i, 0)),
                  pl.BlockSpec((1, d), lambda i: (0, 0)),
                  pl.BlockSpec(wr.shape, lambda i: (0, 0)),
                  pl.BlockSpec((1, LANES), lambda i: (0, 0))],
        out_specs=[pl.BlockSpec((cap, d), lambda i: (i, 0)),
                   pl.BlockSpec((ts, LANES), lambda i: (i, 0)),
                   pl.BlockSpec((1, 8, LANES), lambda i: (i, 0, 0))],
        out_shape=[jax.ShapeDtypeStruct((nt * cap, d), BF16),
                   jax.ShapeDtypeStruct((t, LANES), F32),
                   jax.ShapeDtypeStruct((nt, 8, LANES), F32)],
        compiler_params=pltpu.CompilerParams(vmem_limit_bytes=VMEM_LIMIT),
        name="route",
    )(x1, g_ffn, wr, br)


def _expert_kernel(te_ref, nu_ref, src_ref, xs_ref, wg_ref, wu_ref, wd_ref, o_ref, xbuf_ref, sem_ref):
    i = pl.program_id(0)
    n_used = nu_ref[0]
    slot = i % 2

    def chunk_copy(tile, c, buf):
        return pltpu.make_async_copy(
            xs_ref.at[_chunk_rows(src_ref[tile * CHUNKS_PER_ETILE + c])],
            xbuf_ref.at[buf, pl.ds(c * CHUNK_ROWS, CHUNK_ROWS)], sem_ref.at[buf])

    def gather(tile, buf):
        for c in range(CHUNKS_PER_ETILE):
            chunk_copy(tile, c, buf).start(priority=c % 2)

    def drain(tile, buf):
        for c in range(CHUNKS_PER_ETILE):
            chunk_copy(tile, c, buf).wait()

    @pl.when(i == 0)
    def _():
        gather(0, 0)

    @pl.when(i < n_used)
    def _():
        nxt = jnp.minimum(i + 1, n_used - 1)
        gather(nxt, 1 - slot)
        drain(i, slot)
        x = xbuf_ref[slot]
        gate = jnp.dot(x, wg_ref[0].astype(BF16), preferred_element_type=F32)
        up = jnp.dot(x, wu_ref[0].astype(BF16), preferred_element_type=F32)
        act = (gate * _sigmoid(gate) * up).astype(BF16)
        o_ref[...] = jnp.dot(act, wd_ref[0].astype(BF16), preferred_element_type=F32).astype(BF16)

        @pl.when(i == n_used - 1)
        def _():
            drain(nxt, 1 - slot)

    @pl.when(i >= n_used)
    def _():
        o_ref[...] = jnp.zeros_like(o_ref)


def _expert_call(tile_expert, n_used, chunk_src, xs, wg, wu, wd):
    d = xs.shape[1]
    de = wg.shape[2]
    tm = EXPERT_TILE
    nt = tile_expert.shape[0]

    def w_map(i, te, nu, src):
        return (te[jnp.maximum(jnp.minimum(i, nu[0] - 1), 0)], 0, 0)

    return pl.pallas_call(
        _expert_kernel,
        grid_spec=pltpu.PrefetchScalarGridSpec(
            num_scalar_prefetch=3,
            grid=(nt,),
            in_specs=[pl.BlockSpec(memory_space=pl.ANY),
                      pl.BlockSpec((1, d, de), w_map),
                      pl.BlockSpec((1, d, de), w_map),
                      pl.BlockSpec((1, de, d), w_map)],
            out_specs=pl.BlockSpec((tm, d), lambda i, te, nu, src: (i, 0)),
            scratch_shapes=[pltpu.VMEM((2, tm, d), BF16),
                            pltpu.SemaphoreType.DMA((2,))]),
        out_shape=jax.ShapeDtypeStruct((nt * tm, d), BF16),
        compiler_params=pltpu.CompilerParams(dimension_semantics=("arbitrary",)),
        name="expert",
    )(tile_expert, n_used, chunk_src, xs, wg, wu, wd)


def _combine_kernel(dst_ref, x_ref, info_ref, g_ref, ys_ref, o_ref, ybuf_ref, sem_ref):
    i = pl.program_id(0)
    n_tiles = pl.num_programs(0)
    ts = x_ref.shape[0]
    cap = ybuf_ref.shape[1]
    cpt = cap // CHUNK_ROWS
    slot = i % 2

    def chunk_copy(tile, k, buf):
        return pltpu.make_async_copy(ys_ref.at[_chunk_rows(dst_ref[tile * cpt + k])],
                                     ybuf_ref.at[buf, pl.ds(k * CHUNK_ROWS, CHUNK_ROWS)], sem_ref.at[buf])

    def gather(tile, buf):
        for k in range(cpt):
            chunk_copy(tile, k, buf).start(priority=k % 2)

    def drain(tile, buf):
        for k in range(cpt):
            chunk_copy(tile, k, buf).wait()

    @pl.when(i == 0)
    def _():
        gather(0, 0)

    nxt = jnp.minimum(i + 1, n_tiles - 1)
    gather(nxt, 1 - slot)
    drain(i, slot)

    info = info_ref[...]
    pos1 = info[:, 0:1].astype(jnp.int32)
    pos2 = info[:, 1:2].astype(jnp.int32)
    w1 = info[:, 2:3]
    w2 = info[:, 3:4]
    rowid = lax.broadcasted_iota(jnp.int32, (ts, cap), 1)
    ys = ybuf_ref[slot]
    pw = jnp.where(rowid == pos1, w1, jnp.where(rowid == pos2, w2, 0.0)).astype(BF16)
    y = jnp.dot(pw, ys, preferred_element_type=F32)
    o_ref[...] = _rms(x_ref[...] + y, g_ref[...])

    @pl.when(i == n_tiles - 1)
    def _():
        drain(nxt, 1 - slot)


def _combine_call(chunk_dst, x1, info, g_final, ys_sorted, cap):
    t, d = x1.shape
    ts = TOKEN_TILE
    return pl.pallas_call(
        _combine_kernel,
        grid_spec=pltpu.PrefetchScalarGridSpec(
            num_scalar_prefetch=1,
            grid=(t // ts,),
            in_specs=[pl.BlockSpec((ts, d), lambda i, dst: (i, 0)),
                      pl.BlockSpec((ts, LANES), lambda i, dst: (i, 0)),
                      pl.BlockSpec((1, d), lambda i, dst: (0, 0)),
                      pl.BlockSpec(memory_space=pl.ANY)],
            out_specs=pl.BlockSpec((ts, d), lambda i, dst: (i, 0)),
            scratch_shapes=[pltpu.VMEM((2, cap, d), BF16),
                            pltpu.SemaphoreType.DMA((2,))]),
        out_shape=jax.ShapeDtypeStruct((t, d), F32),
        compiler_params=pltpu.CompilerParams(dimension_semantics=("arbitrary",),
                                             vmem_limit_bytes=VMEM_LIMIT),
        name="combine",
    )(chunk_dst, x1, info, g_final, ys_sorted)


def _run_tables(nch, cpt, n_etiles):
    n_tiles, n_exp = nch.shape
    per = CHUNKS_PER_ETILE
    tcum = jnp.cumsum(nch, axis=1)
    toff = tcum - nch
    ecum = jnp.cumsum(nch, axis=0)
    etot = ecum[-1]
    eseg = -(-etot // per) * per
    segcum = jnp.cumsum(eseg)
    ebase = segcum - eseg
    eoff = ebase[None, :] + ecum - nch
    n_used = (segcum[-1] // per).reshape(1)

    k = jnp.arange(cpt, dtype=jnp.int32)[None, :, None]
    in_run = (k >= toff[:, None, :]) & (k < tcum[:, None, :])
    chunk_dst = jnp.sum(jnp.where(in_run, (eoff - toff)[:, None, :] + k, 0), axis=-1)

    first = jnp.arange(n_etiles, dtype=jnp.int32)[:, None] * per
    owner = (first >= ebase[None, :]) & (first < segcum[None, :])
    tile_expert = jnp.sum(jnp.where(owner, jnp.arange(n_exp, dtype=jnp.int32)[None, :], 0), axis=-1)

    pick = lambda a: jnp.sum(jnp.where(owner[:, None, :], a[None, :, :], 0), axis=-1)
    run_lo, run_n = pick(eoff), pick(nch)
    shift = pick(jnp.arange(n_tiles, dtype=jnp.int32)[:, None] * cpt + toff - eoff)
    c = (first + jnp.arange(per, dtype=jnp.int32)[None, :])[:, :, None]
    hit = (c >= run_lo[:, None, :]) & (c < (run_lo + run_n)[:, None, :])
    chunk_src = jnp.sum(jnp.where(hit, shift[:, None, :] + c, 0), axis=-1)
    i32 = lambda a: a.reshape(-1).astype(jnp.int32)
    return i32(chunk_dst), i32(tile_expert), i32(n_used), i32(chunk_src)


def _layer(x, mem, g_mix, w_in, conv_dw, conv_dw_bias, conv_ln_g, conv_ln_b, w_conv_out,
           lam_re, lam_im, log_dt, b_re, b_im, c_re, c_im, ssm_d, w_ssm_glu, g_mem, w_mem_kv,
           w_mem_out, w_out, g_ffn, w_rg, b_rg, w_re, b_re_, w_eg, w_eu, w_ed, g_final):
    b, s, d = x.shape
    t = b * s
    cw = conv_dw.shape[1]
    sw = ssm_d.shape[0]
    qw = w_mem_out.shape[0]
    n_groups = sw // SSM_GROUP
    row = lambda a: a.reshape(1, -1)

    o0, o1, o2 = 2 * cw, 2 * cw + sw, 2 * cw + sw + qw
    wc = w_in[:, :o0].astype(BF16)
    w_ssm_t = w_in[:, o0:o1].T.astype(BF16)
    wq = w_in[:, o1:o2].astype(BF16)
    wg = w_in[:, o2:].astype(BF16)

    kmem, vmem = _kv_call(mem, row(g_mem), w_mem_kv.astype(BF16))

    ut = _ut_call(x.reshape(t, d), row(g_mix), w_ssm_t)
    kt, wz, vy, acoef = _ssm_tables(lam_re, lam_im, log_dt, b_re, b_im, c_re, c_im, ssm_d)
    yt = _ssm_call(ut, kt, wz, vy, acoef, b)

    dw = jnp.concatenate([conv_dw, jnp.zeros((CONV_HALO - CONV_KERNEL, cw), F32)], axis=0)
    x1 = _mix_call(x, yt, kmem, vmem, row(g_mix), wc, wq, wg, dw, row(conv_dw_bias),
                   row(conv_ln_g), row(conv_ln_b), w_conv_out.astype(BF16),
                   w_ssm_glu.astype(BF16), w_mem_out.astype(BF16), w_out.astype(BF16))
    x1 = x1.reshape(t, d)

    pad = LANES - N_EXPERTS - MOE_GROUPS
    wr = jnp.concatenate([w_re, w_rg, jnp.zeros((d, pad), F32)], axis=1)
    br = jnp.concatenate([b_re_, b_rg, jnp.zeros((pad,), F32)]).reshape(1, LANES)
    n_tiles = t // TOKEN_TILE
    max_tile_chunks = (2 * TOKEN_TILE + N_EXPERTS * (CHUNK_ROWS - 1)) // CHUNK_ROWS
    cpt = -(-max_tile_chunks // 8) * 8
    cap = cpt * CHUNK_ROWS
    xs_tiles, info, nch_f = _route_call(x1, row(g_ffn), wr, br, cap)

    max_chunks = n_tiles * max_tile_chunks + N_EXPERTS * (CHUNKS_PER_ETILE - 1)
    n_etiles = -(-max_chunks // CHUNKS_PER_ETILE)
    nch = nch_f[:, 0, :N_EXPERTS].astype(jnp.int32)
    chunk_dst, tile_expert, n_used, chunk_src = _run_tables(nch, cpt, n_etiles)

    ys_sorted = _expert_call(tile_expert, n_used, chunk_src, xs_tiles, w_eg, w_eu, w_ed)
    out = _combine_call(chunk_dst, x1, info, row(g_final), ys_sorted, cap)
    return out.reshape(b, s, d)


def kernel(x, mem, g_mix, w_in, conv_dw, conv_dw_bias, conv_ln_g, conv_ln_b, w_conv_out, ssm_lambda_re, ssm_lambda_im, ssm_log_dt, ssm_b_re, ssm_b_im, ssm_c_re, ssm_c_im, ssm_d, w_ssm_glu, g_mem, w_mem_kv, w_mem_out, w_out, g_ffn, w_router_group, b_router_group, w_router_expert, b_router_expert, w_exp_gate, w_exp_up, w_exp_down, g_final):
    assert g_mix.shape[0] == 1, "the problem fixes one layer"
    l = 0
    return _layer(
        x, mem, g_mix[l], w_in[l], conv_dw[l], conv_dw_bias[l], conv_ln_g[l], conv_ln_b[l],
        w_conv_out[l], ssm_lambda_re[l], ssm_lambda_im[l], ssm_log_dt[l], ssm_b_re[l],
        ssm_b_im[l], ssm_c_re[l], ssm_c_im[l], ssm_d[l], w_ssm_glu[l], g_mem[l], w_mem_kv[l],
        w_mem_out[l], w_out[l], g_ffn[l], w_router_group[l], b_router_group[l],
        w_router_expert[l], b_router_expert[l], w_exp_gate[l], w_exp_up[l], w_exp_down[l], g_final)
```

```python
import functools

import jax
import jax.numpy as jnp
from jax import lax
from jax.experimental import pallas as pl
from jax.experimental.pallas import tpu as pltpu

F32 = jnp.float32
BF16 = jnp.bfloat16
EPS = 1e-6

LANES = 128
CHUNK_ROWS = 16
SSM_GROUP = 16
SSM_STATE = 64
SSM_BLOCK = LANES
CONV_KERNEL = 31
CONV_HALO = 32
HEADS = 4
HEAD_DIM = 128
MOE_GROUPS = 4
EXPERTS_PER_GROUP = 8
N_EXPERTS = MOE_GROUPS * EXPERTS_PER_GROUP
TOKEN_TILE = 512
EXPERT_TILE = 512
CHUNKS_PER_ETILE = EXPERT_TILE // CHUNK_ROWS
VMEM_LIMIT = 56 * 1024 * 1024


def _rms(x, g):
    return x * lax.rsqrt(jnp.mean(x * x, axis=-1, keepdims=True) + EPS) * g


def _sigmoid(x):
    return 0.5 * jnp.tanh(0.5 * x) + 0.5


def _const_spec(shape):
    zeros = (0,) * len(shape)
    return pl.BlockSpec(shape, lambda *_: zeros, pipeline_mode=pl.Buffered(1))


def _chunk_rows(chunk):
    return pl.ds(pl.multiple_of(chunk * CHUNK_ROWS, CHUNK_ROWS), CHUNK_ROWS)


def _kv_kernel(mem_ref, g_ref, w_ref, k_ref, v_ref):
    width = k_ref.shape[-1]
    mn = _rms(mem_ref[0], g_ref[...]).astype(BF16)
    kv = jnp.dot(mn, w_ref[...], preferred_element_type=F32)
    k_ref[0] = kv[:, :width].astype(BF16)
    v_ref[0] = kv[:, width:].astype(BF16)


def _kv_call(mem, g_mem, w_kv):
    b, m, d = mem.shape
    width = w_kv.shape[1] // 2
    return pl.pallas_call(
        _kv_kernel,
        grid=(b,),
        in_specs=[pl.BlockSpec((1, m, d), lambda i: (i, 0, 0)),
                  pl.BlockSpec((1, d), lambda i: (0, 0)),
                  pl.BlockSpec(w_kv.shape, lambda i: (0, 0))],
        out_specs=[pl.BlockSpec((1, m, width), lambda i: (i, 0, 0)),
                   pl.BlockSpec((1, m, width), lambda i: (i, 0, 0))],
        out_shape=[jax.ShapeDtypeStruct((b, m, width), BF16)] * 2,
        name="kv",
    )(mem, g_mem, w_kv)


def _ut_kernel(x_ref, g_ref, wt_ref, o_ref):
    h = _rms(x_ref[...], g_ref[...]).astype(BF16)
    ut = lax.dot_general(wt_ref[...], h, (((1,), (1,)), ((), ())), preferred_element_type=F32)
    ut = ut.reshape(ut.shape[0], ut.shape[1] // SSM_BLOCK, SSM_BLOCK)
    o_ref[...] = ut.reshape(o_ref.shape).astype(BF16)


def _ut_call(x2, g_mix, w_ssm_t):
    t, d = x2.shape
    c = w_ssm_t.shape[0]
    ts = 8 * SSM_BLOCK
    n_groups = c // SSM_GROUP
    return pl.pallas_call(
        _ut_kernel,
        grid=(t // ts,),
        in_specs=[pl.BlockSpec((ts, d), lambda i: (i, 0)),
                  pl.BlockSpec((1, d), lambda i: (0, 0)),
                  pl.BlockSpec((c, d), lambda i: (0, 0))],
        out_specs=pl.BlockSpec((n_groups, SSM_GROUP, ts // SSM_BLOCK, SSM_BLOCK),
                               lambda i: (0, 0, i, 0)),
        out_shape=jax.ShapeDtypeStruct((n_groups, SSM_GROUP, t // SSM_BLOCK, SSM_BLOCK), BF16),
        name="ut",
    )(x2, g_mix, w_ssm_t)


def _ssm_kernel(u_ref, kt_ref, w_ref, v_ref, a_ref, y_ref, acc_ref, z_ref, zs_ref, s_ref, slab_ref,
                *, n_batch):
    rows = u_ref.shape[2]
    n_blocks = rows // n_batch
    ri = lax.broadcasted_iota(jnp.int32, (SSM_BLOCK, SSM_BLOCK), 0)
    ci = lax.broadcasted_iota(jnp.int32, (SSM_BLOCK, SSM_BLOCK), 1)
    causal = ci >= ri

    n_pairs = SSM_GROUP // 2

    def build(cp, slot):
        for half in range(2):
            kt = kt_ref[0, 2 * cp + half]
            for c in range(SSM_GROUP):
                xb = jnp.broadcast_to(kt[c:c + 1, :], (SSM_BLOCK, SSM_BLOCK))
                toe = pltpu.roll(xb, 0, 1, stride=1, stride_axis=0)
                slab_ref[slot, half * SSM_BLOCK:(half + 1) * SSM_BLOCK, c * SSM_BLOCK:(c + 1) * SSM_BLOCK] = (
                    jnp.where(causal, toe, 0.0).astype(BF16))

    def apply(cp, slot, first):
        x2 = jnp.concatenate([u_ref[0, 2 * cp], u_ref[0, 2 * cp + 1]], axis=1)
        part = jnp.dot(x2, slab_ref[slot], preferred_element_type=F32)
        wrow = pl.multiple_of(cp * 2 * SSM_BLOCK, 2 * SSM_BLOCK)
        zpart = jnp.dot(x2, w_ref[0, pl.ds(wrow, 2 * SSM_BLOCK), :], preferred_element_type=F32)
        if first:
            acc_ref[...] = part
            z_ref[...] = zpart
        else:
            acc_ref[...] += part
            z_ref[...] += zpart

    build(0, 0)
    build(1, 1)
    apply(0, 0, True)

    def two(it, carry):
        cp = 2 * it + 1
        build(cp + 1, 0)
        apply(cp, 1, False)
        build(cp + 2, 1)
        apply(cp + 1, 0, False)
        return carry

    lax.fori_loop(0, (n_pairs - 2) // 2, two, 0)
    apply(n_pairs - 1, 1, False)

    a_full = a_ref[0, 0:1, :]
    a_swap = a_ref[0, 1:2, :]
    zs_ref[...] = pltpu.roll(z_ref[...], SSM_STATE, 1)
    st = jnp.zeros((n_batch, 2 * SSM_STATE), F32)
    sw = jnp.zeros((n_batch, 2 * SSM_STATE), F32)
    for blk in range(n_blocks):
        sl = pl.ds(blk, n_batch, stride=n_blocks)
        s_ref[sl, :] = st
        st, sw = (a_full * st + a_swap * sw + z_ref[sl, :],
                  a_full * sw - a_swap * st + zs_ref[sl, :])

    sb = s_ref[...].astype(BF16)
    y3 = jnp.stack([acc_ref[:, c * SSM_BLOCK:(c + 1) * SSM_BLOCK]
                    + jnp.dot(sb, v_ref[0, c], preferred_element_type=F32)
                    for c in range(SSM_GROUP)], axis=0)
    y_ref[...] = y3.reshape(y_ref.shape)


def _ssm_call(u4, kt, wz, vy, acoef, n_batch):
    g, c, rows, blk = u4.shape
    width = c * blk
    return pl.pallas_call(
        functools.partial(_ssm_kernel, n_batch=n_batch),
        grid=(g,),
        in_specs=[pl.BlockSpec((1, c, rows, blk), lambda i: (i, 0, 0, 0)),
                  pl.BlockSpec((1, c, c, blk), lambda i: (i, 0, 0, 0)),
                  pl.BlockSpec((1, width, 2 * SSM_STATE), lambda i: (i, 0, 0)),
                  pl.BlockSpec((1, c, 2 * SSM_STATE, blk), lambda i: (i, 0, 0, 0)),
                  pl.BlockSpec((1, 2, 2 * SSM_STATE), lambda i: (i, 0, 0))],
        out_specs=pl.BlockSpec((c, rows * blk), lambda i: (i, 0)),
        out_shape=jax.ShapeDtypeStruct((g * c, rows * blk), F32),
        scratch_shapes=[pltpu.VMEM((rows, width), F32),
                        pltpu.VMEM((rows, 2 * SSM_STATE), F32),
                        pltpu.VMEM((rows, 2 * SSM_STATE), F32),
                        pltpu.VMEM((rows, 2 * SSM_STATE), F32),
                        pltpu.VMEM((2, 2 * blk, width), BF16)],
        name="ssm",
    )(u4, kt, wz, vy, acoef)


def _ssm_tables(lam_re, lam_im, log_dt, b_re, b_im, c_re, c_im, d):
    hi = lax.Precision.HIGHEST
    g = lam_re.shape[0]
    dt = jnp.exp(log_dt)[:, None]
    er, ei = lam_re * dt, lam_im * dt
    cat = jnp.concatenate
    kk = jnp.arange(SSM_BLOCK + 1, dtype=F32)
    kdesc = jnp.arange(SSM_BLOCK - 1, -1, -1, dtype=F32)[None, :, None]
    ppk_m, ppk_a = jnp.exp(er[:, :, None] * kk), ei[:, :, None] * kk
    ppk_r, ppk_i = ppk_m * jnp.cos(ppk_a), ppk_m * jnp.sin(ppk_a)
    prev_m, prev_a = jnp.exp(kdesc * er[:, None, :]), kdesc * ei[:, None, :]
    prr, pir = prev_m * jnp.cos(prev_a), prev_m * jnp.sin(prev_a)
    nr, ni = ppk_r[:, :, 1] - 1.0, ppk_i[:, :, 1]
    den = lam_re * lam_re + lam_im * lam_im
    fr = (nr * lam_re + ni * lam_im) / den
    fi = (ni * lam_re - nr * lam_im) / den
    bbr = jnp.swapaxes(fr[:, :, None] * b_re - fi[:, :, None] * b_im, 1, 2)
    bbi = jnp.swapaxes(fr[:, :, None] * b_im + fi[:, :, None] * b_re, 1, 2)
    cb_r = c_re[:, None] * bbr[:, :, None, :] - c_im[:, None] * bbi[:, :, None, :]
    cb_i = c_re[:, None] * bbi[:, :, None, :] + c_im[:, None] * bbr[:, :, None, :]
    cb = cat([cb_r, -cb_i], axis=-1).reshape(g, SSM_GROUP * SSM_GROUP, 2 * SSM_STATE)
    pk = cat([ppk_r[:, :, :SSM_BLOCK], ppk_i[:, :, :SSM_BLOCK]], axis=1)
    kt = jnp.einsum("gmq,gqk->gmk", cb, pk, precision=hi).reshape(g, SSM_GROUP, SSM_GROUP, SSM_BLOCK)
    dmat = jnp.eye(SSM_GROUP, dtype=F32)[None] * d.reshape(g, 1, SSM_GROUP)
    kt = kt + dmat[..., None] * (jnp.arange(SSM_BLOCK) == 0).astype(F32)
    wz = (cat([prr, prr], -1)[:, None] * cat([bbr, bbi], -1)[:, :, None, :]
          + cat([pir, pir], -1)[:, None] * cat([-bbi, bbr], -1)[:, :, None, :])
    wz = wz.astype(BF16).reshape(g, SSM_GROUP * SSM_BLOCK, 2 * SSM_STATE)
    pt_r, pt_i = ppk_r[:, :, 1:], ppk_i[:, :, 1:]
    vy = (cat([c_re, -c_re], -1)[..., None] * cat([pt_r, pt_i], axis=1)[:, None]
          + cat([-c_im, -c_im], -1)[..., None] * cat([pt_i, pt_r], axis=1)[:, None]).astype(BF16)
    ar, ai = ppk_r[:, :, SSM_BLOCK], ppk_i[:, :, SSM_BLOCK]
    acoef = jnp.stack([cat([ar, ar], -1), cat([-ai, ai], -1)], axis=1)
    return kt, wz, vy, acoef


def _mix_kernel(x_ref, yt_ref, k_ref, v_ref, g_ref, wc_ref, wq_ref, wg_ref, dw_ref, dwb_ref,
                lng_ref, lnb_ref, wpw_ref, wglu_ref, wo_ref, wout_ref, o_ref, vext_ref, vsh_ref):
    ts = x_ref.shape[1]
    d = x_ref.shape[2]
    cw = dw_ref.shape[1]
    x = x_ref[0]
    h = _rms(x, g_ref[...]).astype(BF16)

    ci = jnp.dot(h, wc_ref[...], preferred_element_type=F32)
    v = ci[:, :cw] * _sigmoid(ci[:, cw:])
    q = jnp.dot(h, wq_ref[...], preferred_element_type=F32)

    @pl.when(pl.program_id(1) == 0)
    def _():
        vext_ref[0:CONV_HALO, :] = jnp.zeros((CONV_HALO, cw), F32)

    vext_ref[CONV_HALO:CONV_HALO + ts, :] = v
    acc = jnp.broadcast_to(dwb_ref[...], (ts, cw))
    first = CONV_HALO - (CONV_KERNEL - 1)
    for phase in range(8):
        offs = [first + k for k in range(CONV_KERNEL) if (first + k) % 8 == phase]
        if not offs:
            continue
        span = offs[-1] - offs[0] + ts
        vsh_ref[0:span, :] = vext_ref[offs[0]:offs[0] + span, :]
        for off in offs:
            acc = acc + dw_ref[off - first:off - first + 1, :] * vsh_ref[off - offs[0]:off - offs[0] + ts, :]
    vext_ref[0:CONV_HALO, :] = vext_ref[ts:ts + CONV_HALO, :]
    mu = jnp.mean(acc, axis=-1, keepdims=True)
    xc = acc - mu
    var = jnp.mean(xc * xc, axis=-1, keepdims=True)
    ln = xc * lax.rsqrt(var + EPS) * lng_ref[...] + lnb_ref[...]
    sw = ln * _sigmoid(ln)
    y_conv = jnp.dot(sw.astype(BF16), wpw_ref[...], preferred_element_type=F32)
    merged = _sigmoid(jnp.dot(h, wg_ref[:, 0:d], preferred_element_type=F32)) * y_conv

    gy = jax.nn.gelu(yt_ref[...]).astype(BF16)
    z = lax.dot_general(gy, wglu_ref[...], (((0,), (0,)), ((), ())), preferred_element_type=F32)
    y_ssm = z[:, :d] * _sigmoid(z[:, d:])
    merged = merged + _sigmoid(jnp.dot(h, wg_ref[:, d:2 * d], preferred_element_type=F32)) * y_ssm

    kk = k_ref[0]
    vv = v_ref[0]
    outs = []
    for hd in range(HEADS):
        sl = slice(hd * HEAD_DIM, (hd + 1) * HEAD_DIM)
        s = lax.dot_general(q[:, sl].astype(BF16), kk[:, sl], (((1,), (1,)), ((), ())),
                            preferred_element_type=F32) * (HEAD_DIM ** -0.5)
        p = jnp.exp(s - jnp.max(s, axis=-1, keepdims=True))
        den = jnp.sum(p, axis=-1, keepdims=True)
        o = jnp.dot(p.astype(BF16), vv[:, sl], preferred_element_type=F32) / den
        outs.append(o.astype(BF16))
    y_mem = jnp.dot(jnp.concatenate(outs, axis=1), wo_ref[...], preferred_element_type=F32)
    merged = merged + _sigmoid(jnp.dot(h, wg_ref[:, 2 * d:3 * d], preferred_element_type=F32)) * y_mem

    o_ref[0] = x + jnp.dot(merged.astype(BF16), wout_ref[...], preferred_element_type=F32)


def _mix_call(x, yt, kmem, vmem, g_mix, wc, wq, wg, dw, dwb, lng, lnb, wpw, wglu, wo, wout):
    b, s, d = x.shape
    ts = TOKEN_TILE
    nst = s // ts
    cw = dw.shape[1]
    m = kmem.shape[1]
    consts = [g_mix, wc, wq, wg, dw, dwb, lng, lnb, wpw, wglu, wo, wout]
    return pl.pallas_call(
        _mix_kernel,
        grid=(b, nst),
        in_specs=[pl.BlockSpec((1, ts, d), lambda i, j: (i, j, 0)),
                  pl.BlockSpec((yt.shape[0], ts), lambda i, j: (0, i * nst + j)),
                  pl.BlockSpec((1, m, kmem.shape[2]), lambda i, j: (i, 0, 0)),
                  pl.BlockSpec((1, m, vmem.shape[2]), lambda i, j: (i, 0, 0))]
                 + [_const_spec(c.shape) for c in consts],
        out_specs=pl.BlockSpec((1, ts, d), lambda i, j: (i, j, 0)),
        out_shape=jax.ShapeDtypeStruct((b, s, d), F32),
        scratch_shapes=[pltpu.VMEM((ts + CONV_HALO, cw), F32),
                        pltpu.VMEM((ts + CONV_HALO, cw), F32)],
        compiler_params=pltpu.CompilerParams(
            dimension_semantics=("arbitrary", "arbitrary"), vmem_limit_bytes=VMEM_LIMIT),
        name="mix",
    )(x, yt, kmem, vmem, *consts)


def _route_kernel(x_ref, g_ref, wr_ref, br_ref, xs_ref, info_ref, nch_ref):
    ts = x_ref.shape[0]
    cap = xs_ref.shape[0]
    h2 = _rms(x_ref[...], g_ref[...])
    hb = h2.astype(BF16)
    hl = (h2 - hb.astype(F32)).astype(BF16)
    hw = jnp.dot(hb, wr_ref[...], preferred_element_type=F32)
    logits = (hw[:, :LANES] + hw[:, LANES:]
              + jnp.dot(hl, wr_ref[:, :LANES], preferred_element_type=F32)) + br_ref[...]
    lt = logits.T
    le = lt[0:N_EXPERTS]
    lg = lt[N_EXPERTS:N_EXPERTS + 8]
    neg = jnp.float32(-1e30)
    big = jnp.float32(1e9)
    g_f = lax.broadcasted_iota(jnp.int32, (8, ts), 0).astype(F32)
    e_f = lax.broadcasted_iota(jnp.int32, (N_EXPERTS, ts), 0).astype(F32)

    gmask = g_f < MOE_GROUPS
    gmax = jnp.max(jnp.where(gmask, lg, neg), axis=0, keepdims=True)
    gidx = jnp.min(jnp.where(gmask & (lg == gmax), g_f, big), axis=0, keepdims=True)
    gsum = jnp.sum(jnp.where(gmask, jnp.exp(jnp.minimum(lg - gmax, 0.0)), 0.0), axis=0, keepdims=True)
    p_top = 1.0 / gsum
    emask = jnp.floor(e_f * (1.0 / EXPERTS_PER_GROUP)) == gidx
    m1 = jnp.max(jnp.where(emask, le, neg), axis=0, keepdims=True)
    i1 = jnp.min(jnp.where(emask & (le == m1), e_f, big), axis=0, keepdims=True)
    emask2 = emask & (e_f != i1)
    m2 = jnp.max(jnp.where(emask2, le, neg), axis=0, keepdims=True)
    i2 = jnp.min(jnp.where(emask2 & (le == m2), e_f, big), axis=0, keepdims=True)
    r = jnp.exp(m2 - m1)
    w1 = p_top / (1.0 + r)
    w2 = p_top * r / (1.0 + r)

    sel1 = e_f == i1
    sel2 = e_f == i2
    occ = jnp.where(sel1 | sel2, 1.0, 0.0)
    tr = lax.broadcasted_iota(jnp.int32, (ts, ts), 0)
    tc = lax.broadcasted_iota(jnp.int32, (ts, ts), 1)
    earlier = jnp.where(tr < tc, 1.0, 0.0).astype(BF16)
    rank = jnp.dot(occ.astype(BF16), earlier, preferred_element_type=F32)
    cnt = jnp.sum(occ, axis=1, keepdims=True)
    nch = jnp.floor((cnt + (CHUNK_ROWS - 1)) * (1.0 / CHUNK_ROWS))
    er = lax.broadcasted_iota(jnp.int32, (N_EXPERTS, N_EXPERTS), 0)
    ec = lax.broadcasted_iota(jnp.int32, (N_EXPERTS, N_EXPERTS), 1)
    lower = jnp.where(ec < er, 1.0, 0.0).astype(BF16)
    nch_b = jnp.broadcast_to(nch, (N_EXPERTS, LANES))
    start = jnp.dot(lower, nch_b.astype(BF16), preferred_element_type=F32)[:, 0:1] * CHUNK_ROWS
    slot = start + rank
    pos1 = jnp.sum(jnp.where(sel1, slot, 0.0), axis=0, keepdims=True)
    pos2 = jnp.sum(jnp.where(sel2, slot, 0.0), axis=0, keepdims=True)

    rowid = lax.broadcasted_iota(jnp.int32, (cap, ts), 0)
    p = jnp.where((rowid == pos1.astype(jnp.int32)) | (rowid == pos2.astype(jnp.int32)), 1.0, 0.0)
    xs_ref[...] = jnp.dot(p.astype(BF16), hb, preferred_element_type=F32).astype(BF16)
    sub = lax.broadcasted_iota(jnp.int32, (LANES, ts), 0)
    info_t = jnp.where(sub == 0, pos1, jnp.where(sub == 1, pos2,
                       jnp.where(sub == 2, w1, jnp.where(sub == 3, w2, 0.0))))
    info_ref[...] = info_t.T
    nch_ref[0] = jnp.concatenate([nch_b, jnp.zeros((LANES - N_EXPERTS, LANES), F32)], axis=0).T[0:8, :]


def _route_call(x1, g_ffn, wr, br, cap):
    t, d = x1.shape
    ts = TOKEN_TILE
    nt = t // ts
    return pl.pallas_call(
        _route_kernel,
        grid=(nt,),
        in_specs=[pl.BlockSpec((ts, d), lambda i: (i, 0)),
                  pl.BlockSpec((1, d), lambda i: (0, 0)),
                  pl.BlockSpec(wr.shape, lambda i: (0, 0)),
                  pl.BlockSpec((1, LANES), lambda i: (0, 0))],
        out_specs=[pl.BlockSpec((cap, d), lambda i: (i, 0)),
                   pl.BlockSpec((ts, LANES), lambda i: (i, 0)),
                   pl.BlockSpec((1, 8, LANES), lambda i: (i, 0, 0))],
        out_shape=[jax.ShapeDtypeStruct((nt * cap, d), BF16),
                   jax.ShapeDtypeStruct((t, LANES), F32),
                   jax.ShapeDtypeStruct((nt, 8, LANES), F32)],
        compiler_params=pltpu.CompilerParams(vmem_limit_bytes=VMEM_LIMIT),
        name="route",
    )(x1, g_ffn, wr, br)


def _expert_kernel(te_ref, nu_ref, src_ref, xs_ref, wg_ref, wu_ref, wd_ref, o_ref, xbuf_ref, sem_ref):
    i = pl.program_id(0)
    n_used = nu_ref[0]
    slot = i % 2

    def chunk_copy(tile, c, buf):
        return pltpu.make_async_copy(
            xs_ref.at[_chunk_rows(src_ref[tile * CHUNKS_PER_ETILE + c])],
            xbuf_ref.at[buf, pl.ds(c * CHUNK_ROWS, CHUNK_ROWS)], sem_ref.at[buf])

    def gather(tile, buf):
        for c in range(CHUNKS_PER_ETILE):
            chunk_copy(tile, c, buf).start(priority=c % 2)

    def drain(tile, buf):
        for c in range(CHUNKS_PER_ETILE):
            chunk_copy(tile, c, buf).wait()

    @pl.when(i == 0)
    def _():
        gather(0, 0)

    @pl.when(i < n_used)
    def _():
        nxt = jnp.minimum(i + 1, n_used - 1)
        gather(nxt, 1 - slot)
        drain(i, slot)
        x = xbuf_ref[slot]
        gate = jnp.dot(x, wg_ref[0].astype(BF16), preferred_element_type=F32)
        up = jnp.dot(x, wu_ref[0].astype(BF16), preferred_element_type=F32)
        act = (gate * _sigmoid(gate) * up).astype(BF16)
        o_ref[...] = jnp.dot(act, wd_ref[0].astype(BF16), preferred_element_type=F32).astype(BF16)

        @pl.when(i == n_used - 1)
        def _():
            drain(nxt, 1 - slot)

    @pl.when(i >= n_used)
    def _():
        o_ref[...] = jnp.zeros_like(o_ref)


def _expert_call(tile_expert, n_used, chunk_src, xs, wg, wu, wd):
    d = xs.shape[1]
    de = wg.shape[2]
    tm = EXPERT_TILE
    nt = tile_expert.shape[0]

    def w_map(i, te, nu, src):
        return (te[jnp.maximum(jnp.minimum(i, nu[0] - 1), 0)], 0, 0)

    return pl.pallas_call(
        _expert_kernel,
        grid_spec=pltpu.PrefetchScalarGridSpec(
            num_scalar_prefetch=3,
            grid=(nt,),
            in_specs=[pl.BlockSpec(memory_space=pl.ANY),
                      pl.BlockSpec((1, d, de), w_map),
                      pl.BlockSpec((1, d, de), w_map),
                      pl.BlockSpec((1, de, d), w_map)],
            out_specs=pl.BlockSpec((tm, d), lambda i, te, nu, src: (i, 0)),
            scratch_shapes=[pltpu.VMEM((2, tm, d), BF16),
                            pltpu.SemaphoreType.DMA((2,))]),
        out_shape=jax.ShapeDtypeStruct((nt * tm, d), BF16),
        compiler_params=pltpu.CompilerParams(dimension_semantics=("arbitrary",)),
        name="expert",
    )(tile_expert, n_used, chunk_src, xs, wg, wu, wd)


def _combine_kernel(dst_ref, x_ref, info_ref, g_ref, ys_ref, o_ref, ybuf_ref, sem_ref):
    i = pl.program_id(0)
    n_tiles = pl.num_programs(0)
    ts = x_ref.shape[0]
    cap = ybuf_ref.shape[1]
    cpt = cap // CHUNK_ROWS
    slot = i % 2

    def chunk_copy(tile, k, buf):
        return pltpu.make_async_copy(ys_ref.at[_chunk_rows(dst_ref[tile * cpt + k])],
                                     ybuf_ref.at[buf, pl.ds(k * CHUNK_ROWS, CHUNK_ROWS)], sem_ref.at[buf])

    def gather(tile, buf):
        for k in range(cpt):
            chunk_copy(tile, k, buf).start(priority=k % 2)

    def drain(tile, buf):
        for k in range(cpt):
            chunk_copy(tile, k, buf).wait()

    @pl.when(i == 0)
    def _():
        gather(0, 0)

    nxt = jnp.minimum(i + 1, n_tiles - 1)
    gather(nxt, 1 - slot)
    drain(i, slot)

    info = info_ref[...]
    pos1 = info[:, 0:1].astype(jnp.int32)
    pos2 = info[:, 1:2].astype(jnp.int32)
    w1 = info[:, 2:3]
    w2 = info[:, 3:4]
    rowid = lax.broadcasted_iota(jnp.int32, (ts, cap), 1)
    ys = ybuf_ref[slot]
    pw = jnp.where(rowid == pos1, w1, jnp.where(rowid == pos2, w2, 0.0)).astype(BF16)
    y = jnp.dot(pw, ys, preferred_element_type=F32)
    o_ref[...] = _rms(x_ref[...] + y, g_ref[...])

    @pl.when(i == n_tiles - 1)
    def _():
        drain(nxt, 1 - slot)


def _combine_call(chunk_dst, x1, info, g_final, ys_sorted, cap):
    t, d = x1.shape
    ts = TOKEN_TILE
    return pl.pallas_call(
        _combine_kernel,
        grid_spec=pltpu.PrefetchScalarGridSpec(
            num_scalar_prefetch=1,
            grid=(t // ts,),
            in_specs=[pl.BlockSpec((ts, d), lambda i, dst: (i, 0)),
                      pl.BlockSpec((ts, LANES), lambda i, dst: (i, 0)),
                      pl.BlockSpec((1, d), lambda i, dst: (0, 0)),
                      pl.BlockSpec(memory_space=pl.ANY)],
            out_specs=pl.BlockSpec((ts, d), lambda i, dst: (i, 0)),
            scratch_shapes=[pltpu.VMEM((2, cap, d), BF16),
                            pltpu.SemaphoreType.DMA((2,))]),
        out_shape=jax.ShapeDtypeStruct((t, d), F32),
        compiler_params=pltpu.CompilerParams(dimension_semantics=("arbitrary",),
                                             vmem_limit_bytes=VMEM_LIMIT),
        name="combine",
    )(chunk_dst, x1, info, g_final, ys_sorted)


def _run_tables(nch, cpt, n_etiles):
    n_tiles, n_exp = nch.shape
    per = CHUNKS_PER_ETILE
    tcum = jnp.cumsum(nch, axis=1)
    toff = tcum - nch
    ecum = jnp.cumsum(nch, axis=0)
    etot = ecum[-1]
    eseg = -(-etot // per) * per
    segcum = jnp.cumsum(eseg)
    ebase = segcum - eseg
    eoff = ebase[None, :] + ecum - nch
    n_used = (segcum[-1] // per).reshape(1)

    k = jnp.arange(cpt, dtype=jnp.int32)[None, :, None]
    in_run = (k >= toff[:, None, :]) & (k < tcum[:, None, :])
    chunk_dst = jnp.sum(jnp.where(in_run, (eoff - toff)[:, None, :] + k, 0), axis=-1)

    first = jnp.arange(n_etiles, dtype=jnp.int32)[:, None] * per
    owner = (first >= ebase[None, :]) & (first < segcum[None, :])
    tile_expert = jnp.sum(jnp.where(owner, jnp.arange(n_exp, dtype=jnp.int32)[None, :], 0), axis=-1)

    pick = lambda a: jnp.sum(jnp.where(owner[:, None, :], a[None, :, :], 0), axis=-1)
    run_lo, run_n = pick(eoff), pick(nch)
    shift = pick(jnp.arange(n_tiles, dtype=jnp.int32)[:, None] * cpt + toff - eoff)
    c = (first + jnp.arange(per, dtype=jnp.int32)[None, :])[:, :, None]
    hit = (c >= run_lo[:, None, :]) & (c < (run_lo + run_n)[:, None, :])
    chunk_src = jnp.sum(jnp.where(hit, shift[:, None, :] + c, 0), axis=-1)
    i32 = lambda a: a.reshape(-1).astype(jnp.int32)
    return i32(chunk_dst), i32(tile_expert), i32(n_used), i32(chunk_src)


def _layer(x, mem, g_mix, w_in, conv_dw, conv_dw_bias, conv_ln_g, conv_ln_b, w_conv_out,
           lam_re, lam_im, log_dt, b_re, b_im, c_re, c_im, ssm_d, w_ssm_glu, g_mem, w_mem_kv,
           w_mem_out, w_out, g_ffn, w_rg, b_rg, w_re, b_re_, w_eg, w_eu, w_ed, g_final):
    b, s, d = x.shape
    t = b * s
    cw = conv_dw.shape[1]
    sw = ssm_d.shape[0]
    qw = w_mem_out.shape[0]
    n_groups = sw // SSM_GROUP
    row = lambda a: a.reshape(1, -1)

    o0, o1, o2 = 2 * cw, 2 * cw + sw, 2 * cw + sw + qw
    wc = w_in[:, :o0].astype(BF16)
    w_ssm_t = w_in[:, o0:o1].T.astype(BF16)
    wq = w_in[:, o1:o2].astype(BF16)
    wg = w_in[:, o2:].astype(BF16)

    kmem, vmem = _kv_call(mem, row(g_mem), w_mem_kv.astype(BF16))

    ut = _ut_call(x.reshape(t, d), row(g_mix), w_ssm_t)
    kt, wz, vy, acoef = _ssm_tables(lam_re, lam_im, log_dt, b_re, b_im, c_re, c_im, ssm_d)
    yt = _ssm_call(ut, kt, wz, vy, acoef, b)

    dw = jnp.concatenate([conv_dw, jnp.zeros((CONV_HALO - CONV_KERNEL, cw), F32)], axis=0)
    x1 = _mix_call(x, yt, kmem, vmem, row(g_mix), wc, wq, wg, dw, row(conv_dw_bias),
                   row(conv_ln_g), row(conv_ln_b), w_conv_out.astype(BF16),
                   w_ssm_glu.astype(BF16), w_mem_out.astype(BF16), w_out.astype(BF16))
    x1 = x1.reshape(t, d)

    pad = LANES - N_EXPERTS - MOE_GROUPS
    wr = jnp.concatenate([w_re, w_rg, jnp.zeros((d, pad), F32)], axis=1)
    wr_hi = wr.astype(BF16)
    wr = jnp.concatenate([wr_hi, (wr - wr_hi.astype(F32)).astype(BF16)], axis=1)
    br = jnp.concatenate([b_re_, b_rg, jnp.zeros((pad,), F32)]).reshape(1, LANES)
    n_tiles = t // TOKEN_TILE
    max_tile_chunks = (2 * TOKEN_TILE + N_EXPERTS * (CHUNK_ROWS - 1)) // CHUNK_ROWS
    cpt = -(-max_tile_chunks // 8) * 8
    cap = cpt * CHUNK_ROWS
    xs_tiles, info, nch_f = _route_call(x1, row(g_ffn), wr, br, cap)

    max_chunks = n_tiles * max_tile_chunks + N_EXPERTS * (CHUNKS_PER_ETILE - 1)
    n_etiles = -(-max_chunks // CHUNKS_PER_ETILE)
    nch = nch_f[:, 0, :N_EXPERTS].astype(jnp.int32)
    chunk_dst, tile_expert, n_used, chunk_src = _run_tables(nch, cpt, n_etiles)

    ys_sorted = _expert_call(tile_expert, n_used, chunk_src, xs_tiles, w_eg, w_eu, w_ed)
    out = _combine_call(chunk_dst, x1, info, row(g_final), ys_sorted, cap)
    return out.reshape(b, s, d)


def kernel(x, mem, g_mix, w_in, conv_dw, conv_dw_bias, conv_ln_g, conv_ln_b, w_conv_out, ssm_lambda_re, ssm_lambda_im, ssm_log_dt, ssm_b_re, ssm_b_im, ssm_c_re, ssm_c_im, ssm_d, w_ssm_glu, g_mem, w_mem_kv, w_mem_out, w_out, g_ffn, w_router_group, b_router_group, w_router_expert, b_router_expert, w_exp_gate, w_exp_up, w_exp_down, g_final):
    assert g_mix.shape[0] == 1, "the problem fixes one layer"
    l = 0
    return _layer(
        x, mem, g_mix[l], w_in[l], conv_dw[l], conv_dw_bias[l], conv_ln_g[l], conv_ln_b[l],
        w_conv_out[l], ssm_lambda_re[l], ssm_lambda_im[l], ssm_log_dt[l], ssm_b_re[l],
        ssm_b_im[l], ssm_c_re[l], ssm_c_im[l], ssm_d[l], w_ssm_glu[l], g_mem[l], w_mem_kv[l],
        w_mem_out[l], w_out[l], g_ffn[l], w_router_group[l], b_router_group[l],
        w_router_expert[l], b_router_expert[l], w_exp_gate[l], w_exp_up[l], w_exp_down[l], g_final)
```

```python
import functools

import jax
import jax.numpy as jnp
from jax import lax
from jax.experimental import pallas as pl
from jax.experimental.pallas import tpu as pltpu

F32 = jnp.float32
BF16 = jnp.bfloat16
EPS = 1e-6

LANES = 128
CHUNK_ROWS = 16
SSM_GROUP = 16
SSM_STATE = 64
SSM_BLOCK = LANES
CONV_KERNEL = 31
CONV_HALO = 32
HEADS = 4
HEAD_DIM = 128
MOE_GROUPS = 4
EXPERTS_PER_GROUP = 8
N_EXPERTS = MOE_GROUPS * EXPERTS_PER_GROUP
TOKEN_TILE = 512
EXPERT_TILE = 1024
CHUNKS_PER_ETILE = EXPERT_TILE // CHUNK_ROWS
VMEM_LIMIT = 56 * 1024 * 1024


def _rms(x, g):
    return x * lax.rsqrt(jnp.mean(x * x, axis=-1, keepdims=True) + EPS) * g


def _sigmoid(x):
    return 0.5 * jnp.tanh(0.5 * x) + 0.5


def _const_spec(shape):
    zeros = (0,) * len(shape)
    return pl.BlockSpec(shape, lambda *_: zeros, pipeline_mode=pl.Buffered(1))


def _chunk_rows(chunk):
    return pl.ds(pl.multiple_of(chunk * CHUNK_ROWS, CHUNK_ROWS), CHUNK_ROWS)


def _kv_kernel(mem_ref, g_ref, w_ref, k_ref, v_ref):
    width = k_ref.shape[-1]
    mn = _rms(mem_ref[0], g_ref[...]).astype(BF16)
    kv = jnp.dot(mn, w_ref[...], preferred_element_type=F32)
    k_ref[0] = kv[:, :width].astype(BF16)
    v_ref[0] = kv[:, width:].astype(BF16)


def _kv_call(mem, g_mem, w_kv):
    b, m, d = mem.shape
    width = w_kv.shape[1] // 2
    return pl.pallas_call(
        _kv_kernel,
        grid=(b,),
        in_specs=[pl.BlockSpec((1, m, d), lambda i: (i, 0, 0)),
                  pl.BlockSpec((1, d), lambda i: (0, 0)),
                  pl.BlockSpec(w_kv.shape, lambda i: (0, 0))],
        out_specs=[pl.BlockSpec((1, m, width), lambda i: (i, 0, 0)),
                   pl.BlockSpec((1, m, width), lambda i: (i, 0, 0))],
        out_shape=[jax.ShapeDtypeStruct((b, m, width), BF16)] * 2,
        name="kv",
    )(mem, g_mem, w_kv)


def _ut_kernel(x_ref, g_ref, wt_ref, o_ref):
    h = _rms(x_ref[...], g_ref[...]).astype(BF16)
    ut = lax.dot_general(wt_ref[...], h, (((1,), (1,)), ((), ())), preferred_element_type=F32)
    ut = ut.reshape(ut.shape[0], ut.shape[1] // SSM_BLOCK, SSM_BLOCK)
    o_ref[...] = ut.reshape(o_ref.shape).astype(BF16)


def _ut_call(x2, g_mix, w_ssm_t):
    t, d = x2.shape
    c = w_ssm_t.shape[0]
    ts = 8 * SSM_BLOCK
    n_groups = c // SSM_GROUP
    return pl.pallas_call(
        _ut_kernel,
        grid=(t // ts,),
        in_specs=[pl.BlockSpec((ts, d), lambda i: (i, 0)),
                  pl.BlockSpec((1, d), lambda i: (0, 0)),
                  pl.BlockSpec((c, d), lambda i: (0, 0))],
        out_specs=pl.BlockSpec((n_groups, SSM_GROUP, ts // SSM_BLOCK, SSM_BLOCK),
                               lambda i: (0, 0, i, 0)),
        out_shape=jax.ShapeDtypeStruct((n_groups, SSM_GROUP, t // SSM_BLOCK, SSM_BLOCK), BF16),
        name="ut",
    )(x2, g_mix, w_ssm_t)


def _ssm_kernel(u_ref, kt_ref, w_ref, v_ref, a_ref, y_ref, acc_ref, z_ref, zs_ref, s_ref, slab_ref,
                *, n_batch):
    rows = u_ref.shape[2]
    n_blocks = rows // n_batch
    ri = lax.broadcasted_iota(jnp.int32, (SSM_BLOCK, SSM_BLOCK), 0)
    ci = lax.broadcasted_iota(jnp.int32, (SSM_BLOCK, SSM_BLOCK), 1)
    causal = ci >= ri

    n_pairs = SSM_GROUP // 2

    def build(cp, slot):
        for half in range(2):
            kt = kt_ref[0, 2 * cp + half]
            for c in range(SSM_GROUP):
                xb = jnp.broadcast_to(kt[c:c + 1, :], (SSM_BLOCK, SSM_BLOCK))
                toe = pltpu.roll(xb, 0, 1, stride=1, stride_axis=0)
                slab_ref[slot, half * SSM_BLOCK:(half + 1) * SSM_BLOCK, c * SSM_BLOCK:(c + 1) * SSM_BLOCK] = (
                    jnp.where(causal, toe, 0.0).astype(BF16))

    def apply(cp, slot, first):
        x2 = jnp.concatenate([u_ref[0, 2 * cp], u_ref[0, 2 * cp + 1]], axis=1)
        part = jnp.dot(x2, slab_ref[slot], preferred_element_type=F32)
        wrow = pl.multiple_of(cp * 2 * SSM_BLOCK, 2 * SSM_BLOCK)
        zpart = jnp.dot(x2, w_ref[0, pl.ds(wrow, 2 * SSM_BLOCK), :], preferred_element_type=F32)
        if first:
            acc_ref[...] = part
            z_ref[...] = zpart
        else:
            acc_ref[...] += part
            z_ref[...] += zpart

    build(0, 0)
    build(1, 1)
    apply(0, 0, True)

    def two(it, carry):
        cp = 2 * it + 1
        build(cp + 1, 0)
        apply(cp, 1, False)
        build(cp + 2, 1)
        apply(cp + 1, 0, False)
        return carry

    lax.fori_loop(0, (n_pairs - 2) // 2, two, 0)
    apply(n_pairs - 1, 1, False)

    a_full = a_ref[0, 0:1, :]
    a_swap = a_ref[0, 1:2, :]
    zs_ref[...] = pltpu.roll(z_ref[...], SSM_STATE, 1)
    st = jnp.zeros((n_batch, 2 * SSM_STATE), F32)
    sw = jnp.zeros((n_batch, 2 * SSM_STATE), F32)
    for blk in range(n_blocks):
        sl = pl.ds(blk, n_batch, stride=n_blocks)
        s_ref[sl, :] = st
        st, sw = (a_full * st + a_swap * sw + z_ref[sl, :],
                  a_full * sw - a_swap * st + zs_ref[sl, :])

    sb = s_ref[...].astype(BF16)
    y3 = jnp.stack([acc_ref[:, c * SSM_BLOCK:(c + 1) * SSM_BLOCK]
                    + jnp.dot(sb, v_ref[0, c], preferred_element_type=F32)
                    for c in range(SSM_GROUP)], axis=0)
    y_ref[...] = y3.reshape(y_ref.shape)


def _ssm_call(u4, kt, wz, vy, acoef, n_batch):
    g, c, rows, blk = u4.shape
    width = c * blk
    return pl.pallas_call(
        functools.partial(_ssm_kernel, n_batch=n_batch),
        grid=(g,),
        in_specs=[pl.BlockSpec((1, c, rows, blk), lambda i: (i, 0, 0, 0)),
                  pl.BlockSpec((1, c, c, blk), lambda i: (i, 0, 0, 0)),
                  pl.BlockSpec((1, width, 2 * SSM_STATE), lambda i: (i, 0, 0)),
                  pl.BlockSpec((1, c, 2 * SSM_STATE, blk), lambda i: (i, 0, 0, 0)),
                  pl.BlockSpec((1, 2, 2 * SSM_STATE), lambda i: (i, 0, 0))],
        out_specs=pl.BlockSpec((c, rows * blk), lambda i: (i, 0)),
        out_shape=jax.ShapeDtypeStruct((g * c, rows * blk), F32),
        scratch_shapes=[pltpu.VMEM((rows, width), F32),
                        pltpu.VMEM((rows, 2 * SSM_STATE), F32),
                        pltpu.VMEM((rows, 2 * SSM_STATE), F32),
                        pltpu.VMEM((rows, 2 * SSM_STATE), F32),
                        pltpu.VMEM((2, 2 * blk, width), BF16)],
        name="ssm",
    )(u4, kt, wz, vy, acoef)


def _ssm_tables(lam_re, lam_im, log_dt, b_re, b_im, c_re, c_im, d):
    hi = lax.Precision.HIGHEST
    g = lam_re.shape[0]
    dt = jnp.exp(log_dt)[:, None]
    er, ei = lam_re * dt, lam_im * dt
    cat = jnp.concatenate
    kk = jnp.arange(SSM_BLOCK + 1, dtype=F32)
    kdesc = jnp.arange(SSM_BLOCK - 1, -1, -1, dtype=F32)[None, :, None]
    ppk_m, ppk_a = jnp.exp(er[:, :, None] * kk), ei[:, :, None] * kk
    ppk_r, ppk_i = ppk_m * jnp.cos(ppk_a), ppk_m * jnp.sin(ppk_a)
    prev_m, prev_a = jnp.exp(kdesc * er[:, None, :]), kdesc * ei[:, None, :]
    prr, pir = prev_m * jnp.cos(prev_a), prev_m * jnp.sin(prev_a)
    nr, ni = ppk_r[:, :, 1] - 1.0, ppk_i[:, :, 1]
    den = lam_re * lam_re + lam_im * lam_im
    fr = (nr * lam_re + ni * lam_im) / den
    fi = (ni * lam_re - nr * lam_im) / den
    bbr = jnp.swapaxes(fr[:, :, None] * b_re - fi[:, :, None] * b_im, 1, 2)
    bbi = jnp.swapaxes(fr[:, :, None] * b_im + fi[:, :, None] * b_re, 1, 2)
    cb_r = c_re[:, None] * bbr[:, :, None, :] - c_im[:, None] * bbi[:, :, None, :]
    cb_i = c_re[:, None] * bbi[:, :, None, :] + c_im[:, None] * bbr[:, :, None, :]
    cb = cat([cb_r, -cb_i], axis=-1).reshape(g, SSM_GROUP * SSM_GROUP, 2 * SSM_STATE)
    pk = cat([ppk_r[:, :, :SSM_BLOCK], ppk_i[:, :, :SSM_BLOCK]], axis=1)
    kt = jnp.einsum("gmq,gqk->gmk", cb, pk, precision=hi).reshape(g, SSM_GROUP, SSM_GROUP, SSM_BLOCK)
    dmat = jnp.eye(SSM_GROUP, dtype=F32)[None] * d.reshape(g, 1, SSM_GROUP)
    kt = kt + dmat[..., None] * (jnp.arange(SSM_BLOCK) == 0).astype(F32)
    wz = (cat([prr, prr], -1)[:, None] * cat([bbr, bbi], -1)[:, :, None, :]
          + cat([pir, pir], -1)[:, None] * cat([-bbi, bbr], -1)[:, :, None, :])
    wz = wz.astype(BF16).reshape(g, SSM_GROUP * SSM_BLOCK, 2 * SSM_STATE)
    pt_r, pt_i = ppk_r[:, :, 1:], ppk_i[:, :, 1:]
    vy = (cat([c_re, -c_re], -1)[..., None] * cat([pt_r, pt_i], axis=1)[:, None]
          + cat([-c_im, -c_im], -1)[..., None] * cat([pt_i, pt_r], axis=1)[:, None]).astype(BF16)
    ar, ai = ppk_r[:, :, SSM_BLOCK], ppk_i[:, :, SSM_BLOCK]
    acoef = jnp.stack([cat([ar, ar], -1), cat([-ai, ai], -1)], axis=1)
    return kt, wz, vy, acoef


def _mix_kernel(x_ref, yt_ref, k_ref, v_ref, g_ref, wc_ref, wq_ref, wg_ref, dw_ref, dwb_ref,
                lng_ref, lnb_ref, wpw_ref, wglu_ref, wo_ref, wout_ref, o_ref, vext_ref, vsh_ref):
    ts = x_ref.shape[1]
    d = x_ref.shape[2]
    cw = dw_ref.shape[1]
    x = x_ref[0]
    h = _rms(x, g_ref[...]).astype(BF16)

    ci = jnp.dot(h, wc_ref[...], preferred_element_type=F32)
    v = ci[:, :cw] * _sigmoid(ci[:, cw:])
    q = jnp.dot(h, wq_ref[...], preferred_element_type=F32)

    @pl.when(pl.program_id(1) == 0)
    def _():
        vext_ref[0:CONV_HALO, :] = jnp.zeros((CONV_HALO, cw), F32)

    vext_ref[CONV_HALO:CONV_HALO + ts, :] = v
    acc = jnp.broadcast_to(dwb_ref[...], (ts, cw))
    first = CONV_HALO - (CONV_KERNEL - 1)
    for phase in range(8):
        offs = [first + k for k in range(CONV_KERNEL) if (first + k) % 8 == phase]
        if not offs:
            continue
        span = offs[-1] - offs[0] + ts
        vsh_ref[0:span, :] = vext_ref[offs[0]:offs[0] + span, :]
        for off in offs:
            acc = acc + dw_ref[off - first:off - first + 1, :] * vsh_ref[off - offs[0]:off - offs[0] + ts, :]
    vext_ref[0:CONV_HALO, :] = vext_ref[ts:ts + CONV_HALO, :]
    mu = jnp.mean(acc, axis=-1, keepdims=True)
    xc = acc - mu
    var = jnp.mean(xc * xc, axis=-1, keepdims=True)
    ln = xc * lax.rsqrt(var + EPS) * lng_ref[...] + lnb_ref[...]
    sw = ln * _sigmoid(ln)
    y_conv = jnp.dot(sw.astype(BF16), wpw_ref[...], preferred_element_type=F32)
    merged = _sigmoid(jnp.dot(h, wg_ref[:, 0:d], preferred_element_type=F32)) * y_conv

    gy = jax.nn.gelu(yt_ref[...]).astype(BF16)
    z = lax.dot_general(gy, wglu_ref[...], (((0,), (0,)), ((), ())), preferred_element_type=F32)
    y_ssm = z[:, :d] * _sigmoid(z[:, d:])
    merged = merged + _sigmoid(jnp.dot(h, wg_ref[:, d:2 * d], preferred_element_type=F32)) * y_ssm

    kk = k_ref[0]
    vv = v_ref[0]
    outs = []
    for hd in range(HEADS):
        sl = slice(hd * HEAD_DIM, (hd + 1) * HEAD_DIM)
        s = lax.dot_general(q[:, sl].astype(BF16), kk[:, sl], (((1,), (1,)), ((), ())),
                            preferred_element_type=F32) * (HEAD_DIM ** -0.5)
        p = jnp.exp(s - jnp.max(s, axis=-1, keepdims=True))
        den = jnp.sum(p, axis=-1, keepdims=True)
        o = jnp.dot(p.astype(BF16), vv[:, sl], preferred_element_type=F32) / den
        outs.append(o.astype(BF16))
    y_mem = jnp.dot(jnp.concatenate(outs, axis=1), wo_ref[...], preferred_element_type=F32)
    merged = merged + _sigmoid(jnp.dot(h, wg_ref[:, 2 * d:3 * d], preferred_element_type=F32)) * y_mem

    o_ref[0] = x + jnp.dot(merged.astype(BF16), wout_ref[...], preferred_element_type=F32)


def _mix_call(x, yt, kmem, vmem, g_mix, wc, wq, wg, dw, dwb, lng, lnb, wpw, wglu, wo, wout):
    b, s, d = x.shape
    ts = TOKEN_TILE
    nst = s // ts
    cw = dw.shape[1]
    m = kmem.shape[1]
    consts = [g_mix, wc, wq, wg, dw, dwb, lng, lnb, wpw, wglu, wo, wout]
    return pl.pallas_call(
        _mix_kernel,
        grid=(b, nst),
        in_specs=[pl.BlockSpec((1, ts, d), lambda i, j: (i, j, 0)),
                  pl.BlockSpec((yt.shape[0], ts), lambda i, j: (0, i * nst + j)),
                  pl.BlockSpec((1, m, kmem.shape[2]), lambda i, j: (i, 0, 0)),
                  pl.BlockSpec((1, m, vmem.shape[2]), lambda i, j: (i, 0, 0))]
                 + [_const_spec(c.shape) for c in consts],
        out_specs=pl.BlockSpec((1, ts, d), lambda i, j: (i, j, 0)),
        out_shape=jax.ShapeDtypeStruct((b, s, d), F32),
        scratch_shapes=[pltpu.VMEM((ts + CONV_HALO, cw), F32),
                        pltpu.VMEM((ts + CONV_HALO, cw), F32)],
        compiler_params=pltpu.CompilerParams(
            dimension_semantics=("arbitrary", "arbitrary"), vmem_limit_bytes=VMEM_LIMIT),
        name="mix",
    )(x, yt, kmem, vmem, *consts)


def _route_kernel(x_ref, g_ref, wr_ref, br_ref, xs_ref, info_ref, nch_ref):
    ts = x_ref.shape[0]
    cap = xs_ref.shape[0]
    h2 = _rms(x_ref[...], g_ref[...])
    hb = h2.astype(BF16)
    hl = (h2 - hb.astype(F32)).astype(BF16)
    hw = jnp.dot(hb, wr_ref[...], preferred_element_type=F32)
    logits = (hw[:, :LANES] + hw[:, LANES:]
              + jnp.dot(hl, wr_ref[:, :LANES], preferred_element_type=F32)) + br_ref[...]
    lt = logits.T
    le = lt[0:N_EXPERTS]
    lg = lt[N_EXPERTS:N_EXPERTS + 8]
    neg = jnp.float32(-1e30)
    big = jnp.float32(1e9)
    g_f = lax.broadcasted_iota(jnp.int32, (8, ts), 0).astype(F32)
    e_f = lax.broadcasted_iota(jnp.int32, (N_EXPERTS, ts), 0).astype(F32)

    gmask = g_f < MOE_GROUPS
    gmax = jnp.max(jnp.where(gmask, lg, neg), axis=0, keepdims=True)
    gidx = jnp.min(jnp.where(gmask & (lg == gmax), g_f, big), axis=0, keepdims=True)
    gsum = jnp.sum(jnp.where(gmask, jnp.exp(jnp.minimum(lg - gmax, 0.0)), 0.0), axis=0, keepdims=True)
    p_top = 1.0 / gsum
    emask = jnp.floor(e_f * (1.0 / EXPERTS_PER_GROUP)) == gidx
    m1 = jnp.max(jnp.where(emask, le, neg), axis=0, keepdims=True)
    i1 = jnp.min(jnp.where(emask & (le == m1), e_f, big), axis=0, keepdims=True)
    emask2 = emask & (e_f != i1)
    m2 = jnp.max(jnp.where(emask2, le, neg), axis=0, keepdims=True)
    i2 = jnp.min(jnp.where(emask2 & (le == m2), e_f, big), axis=0, keepdims=True)
    r = jnp.exp(m2 - m1)
    w1 = p_top / (1.0 + r)
    w2 = p_top * r / (1.0 + r)

    sel1 = e_f == i1
    sel2 = e_f == i2
    occ = jnp.where(sel1 | sel2, 1.0, 0.0)
    tr = lax.broadcasted_iota(jnp.int32, (ts, ts), 0)
    tc = lax.broadcasted_iota(jnp.int32, (ts, ts), 1)
    earlier = jnp.where(tr < tc, 1.0, 0.0).astype(BF16)
    rank = jnp.dot(occ.astype(BF16), earlier, preferred_element_type=F32)
    cnt = jnp.sum(occ, axis=1, keepdims=True)
    nch = jnp.floor((cnt + (CHUNK_ROWS - 1)) * (1.0 / CHUNK_ROWS))
    er = lax.broadcasted_iota(jnp.int32, (N_EXPERTS, N_EXPERTS), 0)
    ec = lax.broadcasted_iota(jnp.int32, (N_EXPERTS, N_EXPERTS), 1)
    lower = jnp.where(ec < er, 1.0, 0.0).astype(BF16)
    nch_b = jnp.broadcast_to(nch, (N_EXPERTS, LANES))
    start = jnp.dot(lower, nch_b.astype(BF16), preferred_element_type=F32)[:, 0:1] * CHUNK_ROWS
    slot = start + rank
    pos1 = jnp.sum(jnp.where(sel1, slot, 0.0), axis=0, keepdims=True)
    pos2 = jnp.sum(jnp.where(sel2, slot, 0.0), axis=0, keepdims=True)

    rowid = lax.broadcasted_iota(jnp.int32, (cap, ts), 0)
    p = jnp.where((rowid == pos1.astype(jnp.int32)) | (rowid == pos2.astype(jnp.int32)), 1.0, 0.0)
    xs_ref[...] = jnp.dot(p.astype(BF16), hb, preferred_element_type=F32).astype(BF16)
    sub = lax.broadcasted_iota(jnp.int32, (LANES, ts), 0)
    info_t = jnp.where(sub == 0, pos1, jnp.where(sub == 1, pos2,
                       jnp.where(sub == 2, w1, jnp.where(sub == 3, w2, 0.0))))
    info_ref[...] = info_t.T
    nch_ref[0] = jnp.concatenate([nch_b, jnp.zeros((LANES - N_EXPERTS, LANES), F32)], axis=0).T[0:8, :]


def _route_call(x1, g_ffn, wr, br, cap):
    t, d = x1.shape
    ts = TOKEN_TILE
    nt = t // ts
    return pl.pallas_call(
        _route_kernel,
        grid=(nt,),
        in_specs=[pl.BlockSpec((ts, d), lambda i: (i, 0)),
                  pl.BlockSpec((1, d), lambda i: (0, 0)),
                  pl.BlockSpec(wr.shape, lambda i: (0, 0)),
                  pl.BlockSpec((1, LANES), lambda i: (0, 0))],
        out_specs=[pl.BlockSpec((cap, d), lambda i: (i, 0)),
                   pl.BlockSpec((ts, LANES), lambda i: (i, 0)),
                   pl.BlockSpec((1, 8, LANES), lambda i: (i, 0, 0))],
        out_shape=[jax.ShapeDtypeStruct((nt * cap, d), BF16),
                   jax.ShapeDtypeStruct((t, LANES), F32),
                   jax.ShapeDtypeStruct((nt, 8, LANES), F32)],
        compiler_params=pltpu.CompilerParams(vmem_limit_bytes=VMEM_LIMIT),
        name="route",
    )(x1, g_ffn, wr, br)


def _expert_kernel(te_ref, nu_ref, src_ref, xs_ref, wg_ref, wu_ref, wd_ref, o_ref, xbuf_ref, sem_ref):
    i = pl.program_id(0)
    n_used = nu_ref[0]
    slot = i % 2

    def chunk_copy(tile, c, buf):
        return pltpu.make_async_copy(
            xs_ref.at[_chunk_rows(src_ref[tile * CHUNKS_PER_ETILE + c])],
            xbuf_ref.at[buf, pl.ds(c * CHUNK_ROWS, CHUNK_ROWS)], sem_ref.at[buf])

    def gather(tile, buf):
        for c in range(CHUNKS_PER_ETILE):
            chunk_copy(tile, c, buf).start(priority=c % 2)

    def drain(tile, buf):
        for c in range(CHUNKS_PER_ETILE):
            chunk_copy(tile, c, buf).wait()

    @pl.when(i == 0)
    def _():
        gather(0, 0)

    @pl.when(i < n_used)
    def _():
        nxt = jnp.minimum(i + 1, n_used - 1)
        gather(nxt, 1 - slot)
        drain(i, slot)
        x = xbuf_ref[slot]
        gate = jnp.dot(x, wg_ref[0].astype(BF16), preferred_element_type=F32)
        up = jnp.dot(x, wu_ref[0].astype(BF16), preferred_element_type=F32)
        act = (gate * _sigmoid(gate) * up).astype(BF16)
        o_ref[...] = jnp.dot(act, wd_ref[0].astype(BF16), preferred_element_type=F32).astype(BF16)

        @pl.when(i == n_used - 1)
        def _():
            drain(nxt, 1 - slot)

    @pl.when(i >= n_used)
    def _():
        o_ref[...] = jnp.zeros_like(o_ref)


def _expert_call(tile_expert, n_used, chunk_src, xs, wg, wu, wd):
    d = xs.shape[1]
    de = wg.shape[2]
    tm = EXPERT_TILE
    nt = tile_expert.shape[0]

    def w_map(i, te, nu, src):
        return (te[jnp.maximum(jnp.minimum(i, nu[0] - 1), 0)], 0, 0)

    return pl.pallas_call(
        _expert_kernel,
        grid_spec=pltpu.PrefetchScalarGridSpec(
            num_scalar_prefetch=3,
            grid=(nt,),
            in_specs=[pl.BlockSpec(memory_space=pl.ANY),
                      pl.BlockSpec((1, d, de), w_map),
                      pl.BlockSpec((1, d, de), w_map),
                      pl.BlockSpec((1, de, d), w_map)],
            out_specs=pl.BlockSpec((tm, d), lambda i, te, nu, src: (i, 0)),
            scratch_shapes=[pltpu.VMEM((2, tm, d), BF16),
                            pltpu.SemaphoreType.DMA((2,))]),
        out_shape=jax.ShapeDtypeStruct((nt * tm, d), BF16),
        compiler_params=pltpu.CompilerParams(dimension_semantics=("arbitrary",)),
        name="expert",
    )(tile_expert, n_used, chunk_src, xs, wg, wu, wd)


def _combine_kernel(dst_ref, x_ref, info_ref, g_ref, ys_ref, o_ref, ybuf_ref, sem_ref):
    i = pl.program_id(0)
    n_tiles = pl.num_programs(0)
    ts = x_ref.shape[0]
    cap = ybuf_ref.shape[1]
    cpt = cap // CHUNK_ROWS
    slot = i % 2

    def chunk_copy(tile, k, buf):
        return pltpu.make_async_copy(ys_ref.at[_chunk_rows(dst_ref[tile * cpt + k])],
                                     ybuf_ref.at[buf, pl.ds(k * CHUNK_ROWS, CHUNK_ROWS)], sem_ref.at[buf])

    def gather(tile, buf):
        for k in range(cpt):
            chunk_copy(tile, k, buf).start(priority=k % 2)

    def drain(tile, buf):
        for k in range(cpt):
            chunk_copy(tile, k, buf).wait()

    @pl.when(i == 0)
    def _():
        gather(0, 0)

    nxt = jnp.minimum(i + 1, n_tiles - 1)
    gather(nxt, 1 - slot)
    drain(i, slot)

    info = info_ref[...]
    pos1 = info[:, 0:1].astype(jnp.int32)
    pos2 = info[:, 1:2].astype(jnp.int32)
    w1 = info[:, 2:3]
    w2 = info[:, 3:4]
    rowid = lax.broadcasted_iota(jnp.int32, (ts, cap), 1)
    ys = ybuf_ref[slot]
    pw = jnp.where(rowid == pos1, w1, jnp.where(rowid == pos2, w2, 0.0)).astype(BF16)
    y = jnp.dot(pw, ys, preferred_element_type=F32)
    o_ref[...] = _rms(x_ref[...] + y, g_ref[...])

    @pl.when(i == n_tiles - 1)
    def _():
        drain(nxt, 1 - slot)


def _combine_call(chunk_dst, x1, info, g_final, ys_sorted, cap):
    t, d = x1.shape
    ts = TOKEN_TILE
    return pl.pallas_call(
        _combine_kernel,
        grid_spec=pltpu.PrefetchScalarGridSpec(
            num_scalar_prefetch=1,
            grid=(t // ts,),
            in_specs=[pl.BlockSpec((ts, d), lambda i, dst: (i, 0)),
                      pl.BlockSpec((ts, LANES), lambda i, dst: (i, 0)),
                      pl.BlockSpec((1, d), lambda i, dst: (0, 0)),
                      pl.BlockSpec(memory_space=pl.ANY)],
            out_specs=pl.BlockSpec((ts, d), lambda i, dst: (i, 0)),
            scratch_shapes=[pltpu.VMEM((2, cap, d), BF16),
                            pltpu.SemaphoreType.DMA((2,))]),
        out_shape=jax.ShapeDtypeStruct((t, d), F32),
        compiler_params=pltpu.CompilerParams(dimension_semantics=("arbitrary",),
                                             vmem_limit_bytes=VMEM_LIMIT),
        name="combine",
    )(chunk_dst, x1, info, g_final, ys_sorted)


def _run_tables(nch, cpt, n_etiles):
    n_tiles, n_exp = nch.shape
    per = CHUNKS_PER_ETILE
    tcum = jnp.cumsum(nch, axis=1)
    toff = tcum - nch
    ecum = jnp.cumsum(nch, axis=0)
    etot = ecum[-1]
    eseg = -(-etot // per) * per
    segcum = jnp.cumsum(eseg)
    ebase = segcum - eseg
    eoff = ebase[None, :] + ecum - nch
    n_used = (segcum[-1] // per).reshape(1)

    k = jnp.arange(cpt, dtype=jnp.int32)[None, :, None]
    in_run = (k >= toff[:, None, :]) & (k < tcum[:, None, :])
    chunk_dst = jnp.sum(jnp.where(in_run, (eoff - toff)[:, None, :] + k, 0), axis=-1)

    first = jnp.arange(n_etiles, dtype=jnp.int32)[:, None] * per
    owner = (first >= ebase[None, :]) & (first < segcum[None, :])
    tile_expert = jnp.sum(jnp.where(owner, jnp.arange(n_exp, dtype=jnp.int32)[None, :], 0), axis=-1)

    pick = lambda a: jnp.sum(jnp.where(owner[:, None, :], a[None, :, :], 0), axis=-1)
    run_lo, run_n = pick(eoff), pick(nch)
    shift = pick(jnp.arange(n_tiles, dtype=jnp.int32)[:, None] * cpt + toff - eoff)
    c = (first + jnp.arange(per, dtype=jnp.int32)[None, :])[:, :, None]
    hit = (c >= run_lo[:, None, :]) & (c < (run_lo + run_n)[:, None, :])
    chunk_src = jnp.sum(jnp.where(hit, shift[:, None, :] + c, 0), axis=-1)
    i32 = lambda a: a.reshape(-1).astype(jnp.int32)
    return i32(chunk_dst), i32(tile_expert), i32(n_used), i32(chunk_src)


def _layer(x, mem, g_mix, w_in, conv_dw, conv_dw_bias, conv_ln_g, conv_ln_b, w_conv_out,
           lam_re, lam_im, log_dt, b_re, b_im, c_re, c_im, ssm_d, w_ssm_glu, g_mem, w_mem_kv,
           w_mem_out, w_out, g_ffn, w_rg, b_rg, w_re, b_re_, w_eg, w_eu, w_ed, g_final):
    b, s, d = x.shape
    t = b * s
    cw = conv_dw.shape[1]
    sw = ssm_d.shape[0]
    qw = w_mem_out.shape[0]
    n_groups = sw // SSM_GROUP
    row = lambda a: a.reshape(1, -1)

    o0, o1, o2 = 2 * cw, 2 * cw + sw, 2 * cw + sw + qw
    wc = w_in[:, :o0].astype(BF16)
    w_ssm_t = w_in[:, o0:o1].T.astype(BF16)
    wq = w_in[:, o1:o2].astype(BF16)
    wg = w_in[:, o2:].astype(BF16)

    kmem, vmem = _kv_call(mem, row(g_mem), w_mem_kv.astype(BF16))

    ut = _ut_call(x.reshape(t, d), row(g_mix), w_ssm_t)
    kt, wz, vy, acoef = _ssm_tables(lam_re, lam_im, log_dt, b_re, b_im, c_re, c_im, ssm_d)
    yt = _ssm_call(ut, kt, wz, vy, acoef, b)

    dw = jnp.concatenate([conv_dw, jnp.zeros((CONV_HALO - CONV_KERNEL, cw), F32)], axis=0)
    x1 = _mix_call(x, yt, kmem, vmem, row(g_mix), wc, wq, wg, dw, row(conv_dw_bias),
                   row(conv_ln_g), row(conv_ln_b), w_conv_out.astype(BF16),
                   w_ssm_glu.astype(BF16), w_mem_out.astype(BF16), w_out.astype(BF16))
    x1 = x1.reshape(t, d)

    pad = LANES - N_EXPERTS - MOE_GROUPS
    wr = jnp.concatenate([w_re, w_rg, jnp.zeros((d, pad), F32)], axis=1)
    wr_hi = wr.astype(BF16)
    wr = jnp.concatenate([wr_hi, (wr - wr_hi.astype(F32)).astype(BF16)], axis=1)
    br = jnp.concatenate([b_re_, b_rg, jnp.zeros((pad,), F32)]).reshape(1, LANES)
    n_tiles = t // TOKEN_TILE
    max_tile_chunks = (2 * TOKEN_TILE + N_EXPERTS * (CHUNK_ROWS - 1)) // CHUNK_ROWS
    cpt = -(-max_tile_chunks // 8) * 8
    cap = cpt * CHUNK_ROWS
    xs_tiles, info, nch_f = _route_call(x1, row(g_ffn), wr, br, cap)

    max_chunks = n_tiles * max_tile_chunks + N_EXPERTS * (CHUNKS_PER_ETILE - 1)
    n_etiles = -(-max_chunks // CHUNKS_PER_ETILE)
    nch = nch_f[:, 0, :N_EXPERTS].astype(jnp.int32)
    chunk_dst, tile_expert, n_used, chunk_src = _run_tables(nch, cpt, n_etiles)

    ys_sorted = _expert_call(tile_expert, n_used, chunk_src, xs_tiles, w_eg, w_eu, w_ed)
    out = _combine_call(chunk_dst, x1, info, row(g_final), ys_sorted, cap)
    return out.reshape(b, s, d)


def kernel(x, mem, g_mix, w_in, conv_dw, conv_dw_bias, conv_ln_g, conv_ln_b, w_conv_out, ssm_lambda_re, ssm_lambda_im, ssm_log_dt, ssm_b_re, ssm_b_im, ssm_c_re, ssm_c_im, ssm_d, w_ssm_glu, g_mem, w_mem_kv, w_mem_out, w_out, g_ffn, w_router_group, b_router_group, w_router_expert, b_router_expert, w_exp_gate, w_exp_up, w_exp_down, g_final):
    assert g_mix.shape[0] == 1, "the problem fixes one layer"
    l = 0
    return _layer(
        x, mem, g_mix[l], w_in[l], conv_dw[l], conv_dw_bias[l], conv_ln_g[l], conv_ln_b[l],
        w_conv_out[l], ssm_lambda_re[l], ssm_lambda_im[l], ssm_log_dt[l], ssm_b_re[l],
        ssm_b_im[l], ssm_c_re[l], ssm_c_im[l], ssm_d[l], w_ssm_glu[l], g_mem[l], w_mem_kv[l],
        w_mem_out[l], w_out[l], g_ffn[l], w_router_group[l], b_router_group[l],
        w_router_expert[l], b_router_expert[l], w_exp_gate[l], w_exp_up[l], w_exp_down[l], g_final)
```

```python
import functools

import jax
import jax.numpy as jnp
from jax import lax
from jax.experimental import pallas as pl
from jax.experimental.pallas import tpu as pltpu

F32 = jnp.float32
BF16 = jnp.bfloat16
EPS = 1e-6

LANES = 128
CHUNK_ROWS = 16
SSM_GROUP = 16
SSM_STATE = 64
SSM_BLOCK = LANES
CONV_KERNEL = 31
CONV_HALO = 32
HEADS = 4
HEAD_DIM = 128
MOE_GROUPS = 4
EXPERTS_PER_GROUP = 8
N_EXPERTS = MOE_GROUPS * EXPERTS_PER_GROUP
TOKEN_TILE = 512
EXPERT_TILE = 512
CHUNKS_PER_ETILE = EXPERT_TILE // CHUNK_ROWS
VMEM_LIMIT = 56 * 1024 * 1024


def _rms(x, g):
    return x * lax.rsqrt(jnp.mean(x * x, axis=-1, keepdims=True) + EPS) * g


def _sigmoid(x):
    return 0.5 * jnp.tanh(0.5 * x) + 0.5


def _const_spec(shape):
    zeros = (0,) * len(shape)
    return pl.BlockSpec(shape, lambda *_: zeros, pipeline_mode=pl.Buffered(1))


def _chunk_rows(chunk):
    return pl.ds(pl.multiple_of(chunk * CHUNK_ROWS, CHUNK_ROWS), CHUNK_ROWS)


def _kv_kernel(mem_ref, g_ref, w_ref, k_ref, v_ref):
    width = k_ref.shape[-1]
    mn = _rms(mem_ref[0], g_ref[...]).astype(BF16)
    kv = jnp.dot(mn, w_ref[...], preferred_element_type=F32)
    k_ref[0] = kv[:, :width].astype(BF16)
    v_ref[0] = kv[:, width:].astype(BF16)


def _kv_call(mem, g_mem, w_kv):
    b, m, d = mem.shape
    width = w_kv.shape[1] // 2
    return pl.pallas_call(
        _kv_kernel,
        grid=(b,),
        in_specs=[pl.BlockSpec((1, m, d), lambda i: (i, 0, 0)),
                  pl.BlockSpec((1, d), lambda i: (0, 0)),
                  pl.BlockSpec(w_kv.shape, lambda i: (0, 0))],
        out_specs=[pl.BlockSpec((1, m, width), lambda i: (i, 0, 0)),
                   pl.BlockSpec((1, m, width), lambda i: (i, 0, 0))],
        out_shape=[jax.ShapeDtypeStruct((b, m, width), BF16)] * 2,
        name="kv",
    )(mem, g_mem, w_kv)


def _ut_kernel(x_ref, g_ref, wt_ref, o_ref):
    h = _rms(x_ref[...], g_ref[...]).astype(BF16)
    ut = lax.dot_general(wt_ref[...], h, (((1,), (1,)), ((), ())), preferred_element_type=F32)
    ut = ut.reshape(ut.shape[0], ut.shape[1] // SSM_BLOCK, SSM_BLOCK)
    o_ref[...] = ut.reshape(o_ref.shape).astype(BF16)


def _ut_call(x2, g_mix, w_ssm_t):
    t, d = x2.shape
    c = w_ssm_t.shape[0]
    ts = 8 * SSM_BLOCK
    n_groups = c // SSM_GROUP
    return pl.pallas_call(
        _ut_kernel,
        grid=(t // ts,),
        in_specs=[pl.BlockSpec((ts, d), lambda i: (i, 0)),
                  pl.BlockSpec((1, d), lambda i: (0, 0)),
                  pl.BlockSpec((c, d), lambda i: (0, 0))],
        out_specs=pl.BlockSpec((n_groups, SSM_GROUP, ts // SSM_BLOCK, SSM_BLOCK),
                               lambda i: (0, 0, i, 0)),
        out_shape=jax.ShapeDtypeStruct((n_groups, SSM_GROUP, t // SSM_BLOCK, SSM_BLOCK), BF16),
        name="ut",
    )(x2, g_mix, w_ssm_t)


def _ssm_kernel(u_ref, kt_ref, w_ref, v_ref, a_ref, y_ref, acc_ref, z_ref, zs_ref, s_ref, slab_ref,
                *, n_batch):
    rows = u_ref.shape[2]
    n_blocks = rows // n_batch
    ri = lax.broadcasted_iota(jnp.int32, (SSM_BLOCK, SSM_BLOCK), 0)
    ci = lax.broadcasted_iota(jnp.int32, (SSM_BLOCK, SSM_BLOCK), 1)
    causal = ci >= ri

    n_pairs = SSM_GROUP // 2

    def build(cp, slot):
        for half in range(2):
            kt = kt_ref[0, 2 * cp + half]
            for c in range(SSM_GROUP):
                xb = jnp.broadcast_to(kt[c:c + 1, :], (SSM_BLOCK, SSM_BLOCK))
                toe = pltpu.roll(xb, 0, 1, stride=1, stride_axis=0)
                slab_ref[slot, half * SSM_BLOCK:(half + 1) * SSM_BLOCK, c * SSM_BLOCK:(c + 1) * SSM_BLOCK] = (
                    jnp.where(causal, toe, 0.0).astype(BF16))

    def apply(cp, slot, first):
        x2 = jnp.concatenate([u_ref[0, 2 * cp], u_ref[0, 2 * cp + 1]], axis=1)
        part = jnp.dot(x2, slab_ref[slot], preferred_element_type=F32)
        wrow = pl.multiple_of(cp * 2 * SSM_BLOCK, 2 * SSM_BLOCK)
        zpart = jnp.dot(x2, w_ref[0, pl.ds(wrow, 2 * SSM_BLOCK), :], preferred_element_type=F32)
        if first:
            acc_ref[...] = part
            z_ref[...] = zpart
        else:
            acc_ref[...] += part
            z_ref[...] += zpart

    build(0, 0)
    build(1, 1)
    apply(0, 0, True)

    def two(it, carry):
        cp = 2 * it + 1
        build(cp + 1, 0)
        apply(cp, 1, False)
        build(cp + 2, 1)
        apply(cp + 1, 0, False)
        return carry

    lax.fori_loop(0, (n_pairs - 2) // 2, two, 0)
    apply(n_pairs - 1, 1, False)

    a_full = a_ref[0, 0:1, :]
    a_swap = a_ref[0, 1:2, :]
    zs_ref[...] = pltpu.roll(z_ref[...], SSM_STATE, 1)
    st = jnp.zeros((n_batch, 2 * SSM_STATE), F32)
    sw = jnp.zeros((n_batch, 2 * SSM_STATE), F32)
    for blk in range(n_blocks):
        sl = pl.ds(blk, n_batch, stride=n_blocks)
        s_ref[sl, :] = st
        st, sw = (a_full * st + a_swap * sw + z_ref[sl, :],
                  a_full * sw - a_swap * st + zs_ref[sl, :])

    sb = s_ref[...].astype(BF16)
    y3 = jnp.stack([acc_ref[:, c * SSM_BLOCK:(c + 1) * SSM_BLOCK]
                    + jnp.dot(sb, v_ref[0, c], preferred_element_type=F32)
                    for c in range(SSM_GROUP)], axis=0)
    y_ref[...] = y3.reshape(y_ref.shape)


def _ssm_call(u4, kt, wz, vy, acoef, n_batch):
    g, c, rows, blk = u4.shape
    width = c * blk
    return pl.pallas_call(
        functools.partial(_ssm_kernel, n_batch=n_batch),
        grid=(g,),
        in_specs=[pl.BlockSpec((1, c, rows, blk), lambda i: (i, 0, 0, 0)),
                  pl.BlockSpec((1, c, c, blk), lambda i: (i, 0, 0, 0)),
                  pl.BlockSpec((1, width, 2 * SSM_STATE), lambda i: (i, 0, 0)),
                  pl.BlockSpec((1, c, 2 * SSM_STATE, blk), lambda i: (i, 0, 0, 0)),
                  pl.BlockSpec((1, 2, 2 * SSM_STATE), lambda i: (i, 0, 0))],
        out_specs=pl.BlockSpec((c, rows * blk), lambda i: (i, 0)),
        out_shape=jax.ShapeDtypeStruct((g * c, rows * blk), F32),
        scratch_shapes=[pltpu.VMEM((rows, width), F32),
                        pltpu.VMEM((rows, 2 * SSM_STATE), F32),
                        pltpu.VMEM((rows, 2 * SSM_STATE), F32),
                        pltpu.VMEM((rows, 2 * SSM_STATE), F32),
                        pltpu.VMEM((2, 2 * blk, width), BF16)],
        name="ssm",
    )(u4, kt, wz, vy, acoef)


def _ssm_tables(lam_re, lam_im, log_dt, b_re, b_im, c_re, c_im, d):
    hi = lax.Precision.HIGHEST
    g = lam_re.shape[0]
    dt = jnp.exp(log_dt)[:, None]
    er, ei = lam_re * dt, lam_im * dt
    cat = jnp.concatenate
    kk = jnp.arange(SSM_BLOCK + 1, dtype=F32)
    kdesc = jnp.arange(SSM_BLOCK - 1, -1, -1, dtype=F32)[None, :, None]
    ppk_m, ppk_a = jnp.exp(er[:, :, None] * kk), ei[:, :, None] * kk
    ppk_r, ppk_i = ppk_m * jnp.cos(ppk_a), ppk_m * jnp.sin(ppk_a)
    prev_m, prev_a = jnp.exp(kdesc * er[:, None, :]), kdesc * ei[:, None, :]
    prr, pir = prev_m * jnp.cos(prev_a), prev_m * jnp.sin(prev_a)
    nr, ni = ppk_r[:, :, 1] - 1.0, ppk_i[:, :, 1]
    den = lam_re * lam_re + lam_im * lam_im
    fr = (nr * lam_re + ni * lam_im) / den
    fi = (ni * lam_re - nr * lam_im) / den
    bbr = jnp.swapaxes(fr[:, :, None] * b_re - fi[:, :, None] * b_im, 1, 2)
    bbi = jnp.swapaxes(fr[:, :, None] * b_im + fi[:, :, None] * b_re, 1, 2)
    cb_r = c_re[:, None] * bbr[:, :, None, :] - c_im[:, None] * bbi[:, :, None, :]
    cb_i = c_re[:, None] * bbi[:, :, None, :] + c_im[:, None] * bbr[:, :, None, :]
    cb = cat([cb_r, -cb_i], axis=-1).reshape(g, SSM_GROUP * SSM_GROUP, 2 * SSM_STATE)
    pk = cat([ppk_r[:, :, :SSM_BLOCK], ppk_i[:, :, :SSM_BLOCK]], axis=1)
    kt = jnp.einsum("gmq,gqk->gmk", cb, pk, precision=hi).reshape(g, SSM_GROUP, SSM_GROUP, SSM_BLOCK)
    dmat = jnp.eye(SSM_GROUP, dtype=F32)[None] * d.reshape(g, 1, SSM_GROUP)
    kt = kt + dmat[..., None] * (jnp.arange(SSM_BLOCK) == 0).astype(F32)
    wz = (cat([prr, prr], -1)[:, None] * cat([bbr, bbi], -1)[:, :, None, :]
          + cat([pir, pir], -1)[:, None] * cat([-bbi, bbr], -1)[:, :, None, :])
    wz = wz.astype(BF16).reshape(g, SSM_GROUP * SSM_BLOCK, 2 * SSM_STATE)
    pt_r, pt_i = ppk_r[:, :, 1:], ppk_i[:, :, 1:]
    vy = (cat([c_re, -c_re], -1)[..., None] * cat([pt_r, pt_i], axis=1)[:, None]
          + cat([-c_im, -c_im], -1)[..., None] * cat([pt_i, pt_r], axis=1)[:, None]).astype(BF16)
    ar, ai = ppk_r[:, :, SSM_BLOCK], ppk_i[:, :, SSM_BLOCK]
    acoef = jnp.stack([cat([ar, ar], -1), cat([-ai, ai], -1)], axis=1)
    return kt, wz, vy, acoef


def _mix_kernel(x_ref, yt_ref, k_ref, v_ref, g_ref, wc_ref, wq_ref, wg_ref, dw_ref, dwb_ref,
                lng_ref, lnb_ref, wpw_ref, wglu_ref, wo_ref, wout_ref, o_ref, vext_ref, vsh_ref):
    ts = x_ref.shape[1]
    d = x_ref.shape[2]
    cw = dw_ref.shape[1]
    x = x_ref[0]
    h = _rms(x, g_ref[...]).astype(BF16)

    ci = jnp.dot(h, wc_ref[...], preferred_element_type=F32)
    v = ci[:, :cw] * _sigmoid(ci[:, cw:])
    q = jnp.dot(h, wq_ref[...], preferred_element_type=F32)

    @pl.when(pl.program_id(1) == 0)
    def _():
        vext_ref[0:CONV_HALO, :] = jnp.zeros((CONV_HALO, cw), F32)

    vext_ref[CONV_HALO:CONV_HALO + ts, :] = v
    acc = jnp.broadcast_to(dwb_ref[...], (ts, cw))
    first = CONV_HALO - (CONV_KERNEL - 1)
    for phase in range(8):
        offs = [first + k for k in range(CONV_KERNEL) if (first + k) % 8 == phase]
        if not offs:
            continue
        span = offs[-1] - offs[0] + ts
        vsh_ref[0:span, :] = vext_ref[offs[0]:offs[0] + span, :]
        for off in offs:
            acc = acc + dw_ref[off - first:off - first + 1, :] * vsh_ref[off - offs[0]:off - offs[0] + ts, :]
    vext_ref[0:CONV_HALO, :] = vext_ref[ts:ts + CONV_HALO, :]
    mu = jnp.mean(acc, axis=-1, keepdims=True)
    xc = acc - mu
    var = jnp.mean(xc * xc, axis=-1, keepdims=True)
    ln = xc * lax.rsqrt(var + EPS) * lng_ref[...] + lnb_ref[...]
    sw = ln * _sigmoid(ln)
    y_conv = jnp.dot(sw.astype(BF16), wpw_ref[...], preferred_element_type=F32)
    merged = _sigmoid(jnp.dot(h, wg_ref[:, 0:d], preferred_element_type=F32)) * y_conv

    gy = jax.nn.gelu(yt_ref[...]).astype(BF16)
    z = lax.dot_general(gy, wglu_ref[...], (((0,), (0,)), ((), ())), preferred_element_type=F32)
    y_ssm = z[:, :d] * _sigmoid(z[:, d:])
    merged = merged + _sigmoid(jnp.dot(h, wg_ref[:, d:2 * d], preferred_element_type=F32)) * y_ssm

    kk = k_ref[0]
    vv = v_ref[0]
    outs = []
    for hd in range(HEADS):
        sl = slice(hd * HEAD_DIM, (hd + 1) * HEAD_DIM)
        s = lax.dot_general(q[:, sl].astype(BF16), kk[:, sl], (((1,), (1,)), ((), ())),
                            preferred_element_type=F32) * (HEAD_DIM ** -0.5)
        p = jnp.exp(s - jnp.max(s, axis=-1, keepdims=True))
        den = jnp.sum(p, axis=-1, keepdims=True)
        o = jnp.dot(p.astype(BF16), vv[:, sl], preferred_element_type=F32) / den
        outs.append(o.astype(BF16))
    y_mem = jnp.dot(jnp.concatenate(outs, axis=1), wo_ref[...], preferred_element_type=F32)
    merged = merged + _sigmoid(jnp.dot(h, wg_ref[:, 2 * d:3 * d], preferred_element_type=F32)) * y_mem

    o_ref[0] = x + jnp.dot(merged.astype(BF16), wout_ref[...], preferred_element_type=F32)


def _mix_call(x, yt, kmem, vmem, g_mix, wc, wq, wg, dw, dwb, lng, lnb, wpw, wglu, wo, wout):
    b, s, d = x.shape
    ts = TOKEN_TILE
    nst = s // ts
    cw = dw.shape[1]
    m = kmem.shape[1]
    consts = [g_mix, wc, wq, wg, dw, dwb, lng, lnb, wpw, wglu, wo, wout]
    return pl.pallas_call(
        _mix_kernel,
        grid=(b, nst),
        in_specs=[pl.BlockSpec((1, ts, d), lambda i, j: (i, j, 0)),
                  pl.BlockSpec((yt.shape[0], ts), lambda i, j: (0, i * nst + j)),
                  pl.BlockSpec((1, m, kmem.shape[2]), lambda i, j: (i, 0, 0)),
                  pl.BlockSpec((1, m, vmem.shape[2]), lambda i, j: (i, 0, 0))]
                 + [_const_spec(c.shape) for c in consts],
        out_specs=pl.BlockSpec((1, ts, d), lambda i, j: (i, j, 0)),
        out_shape=jax.ShapeDtypeStruct((b, s, d), F32),
        scratch_shapes=[pltpu.VMEM((ts + CONV_HALO, cw), F32),
                        pltpu.VMEM((ts + CONV_HALO, cw), F32)],
        compiler_params=pltpu.CompilerParams(
            dimension_semantics=("arbitrary", "arbitrary"), vmem_limit_bytes=VMEM_LIMIT),
        name="mix",
    )(x, yt, kmem, vmem, *consts)


def _route_kernel(x_ref, g_ref, wr_ref, br_ref, xs_ref, info_ref, nch_ref):
    ts = x_ref.shape[0]
    cap = xs_ref.shape[0]
    h2 = _rms(x_ref[...], g_ref[...])
    hb = h2.astype(BF16)
    hl = (h2 - hb.astype(F32)).astype(BF16)
    hw = jnp.dot(hb, wr_ref[...], preferred_element_type=F32)
    logits = (hw[:, :LANES] + hw[:, LANES:]
              + jnp.dot(hl, wr_ref[:, :LANES], preferred_element_type=F32)) + br_ref[...]
    lt = logits.T
    le = lt[0:N_EXPERTS]
    lg = lt[N_EXPERTS:N_EXPERTS + 8]
    neg = jnp.float32(-1e30)
    big = jnp.float32(1e9)
    g_f = lax.broadcasted_iota(jnp.int32, (8, ts), 0).astype(F32)
    e_f = lax.broadcasted_iota(jnp.int32, (N_EXPERTS, ts), 0).astype(F32)

    gmask = g_f < MOE_GROUPS
    gmax = jnp.max(jnp.where(gmask, lg, neg), axis=0, keepdims=True)
    gidx = jnp.min(jnp.where(gmask & (lg == gmax), g_f, big), axis=0, keepdims=True)
    gsum = jnp.sum(jnp.where(gmask, jnp.exp(jnp.minimum(lg - gmax, 0.0)), 0.0), axis=0, keepdims=True)
    p_top = 1.0 / gsum
    emask = jnp.floor(e_f * (1.0 / EXPERTS_PER_GROUP)) == gidx
    m1 = jnp.max(jnp.where(emask, le, neg), axis=0, keepdims=True)
    i1 = jnp.min(jnp.where(emask & (le == m1), e_f, big), axis=0, keepdims=True)
    emask2 = emask & (e_f != i1)
    m2 = jnp.max(jnp.where(emask2, le, neg), axis=0, keepdims=True)
    i2 = jnp.min(jnp.where(emask2 & (le == m2), e_f, big), axis=0, keepdims=True)
    r = jnp.exp(m2 - m1)
    w1 = p_top / (1.0 + r)
    w2 = p_top * r / (1.0 + r)

    sel1 = e_f == i1
    sel2 = e_f == i2
    occ = jnp.where(sel1 | sel2, 1.0, 0.0)
    tr = lax.broadcasted_iota(jnp.int32, (ts, ts), 0)
    tc = lax.broadcasted_iota(jnp.int32, (ts, ts), 1)
    earlier = jnp.where(tr < tc, 1.0, 0.0).astype(BF16)
    rank = jnp.dot(occ.astype(BF16), earlier, preferred_element_type=F32)
    cnt = jnp.sum(occ, axis=1, keepdims=True)
    nch = jnp.floor((cnt + (CHUNK_ROWS - 1)) * (1.0 / CHUNK_ROWS))
    er = lax.broadcasted_iota(jnp.int32, (N_EXPERTS, N_EXPERTS), 0)
    ec = lax.broadcasted_iota(jnp.int32, (N_EXPERTS, N_EXPERTS), 1)
    lower = jnp.where(ec < er, 1.0, 0.0).astype(BF16)
    nch_b = jnp.broadcast_to(nch, (N_EXPERTS, LANES))
    start = jnp.dot(lower, nch_b.astype(BF16), preferred_element_type=F32)[:, 0:1] * CHUNK_ROWS
    slot = start + rank
    pos1 = jnp.sum(jnp.where(sel1, slot, 0.0), axis=0, keepdims=True)
    pos2 = jnp.sum(jnp.where(sel2, slot, 0.0), axis=0, keepdims=True)

    rowid = lax.broadcasted_iota(jnp.int32, (cap, ts), 0)
    p = jnp.where((rowid == pos1.astype(jnp.int32)) | (rowid == pos2.astype(jnp.int32)), 1.0, 0.0)
    xs_ref[...] = jnp.dot(p.astype(BF16), hb, preferred_element_type=F32).astype(BF16)
    sub = lax.broadcasted_iota(jnp.int32, (LANES, ts), 0)
    info_t = jnp.where(sub == 0, pos1, jnp.where(sub == 1, pos2,
                       jnp.where(sub == 2, w1, jnp.where(sub == 3, w2, 0.0))))
    info_ref[...] = info_t.T
    nch_ref[0] = jnp.concatenate([nch_b, jnp.zeros((LANES - N_EXPERTS, LANES), F32)], axis=0).T[0:8, :]


def _route_call(x1, g_ffn, wr, br, cap):
    t, d = x1.shape
    ts = TOKEN_TILE
    nt = t // ts
    return pl.pallas_call(
        _route_kernel,
        grid=(nt,),
        in_specs=[pl.BlockSpec((ts, d), lambda i: (i, 0)),
                  pl.BlockSpec((1, d), lambda i: (0, 0)),
                  pl.BlockSpec(wr.shape, lambda i: (0, 0)),
                  pl.BlockSpec((1, LANES), lambda i: (0, 0))],
        out_specs=[pl.BlockSpec((cap, d), lambda i: (i, 0)),
                   pl.BlockSpec((ts, LANES), lambda i: (i, 0)),
                   pl.BlockSpec((1, 8, LANES), lambda i: (i, 0, 0))],
        out_shape=[jax.ShapeDtypeStruct((nt * cap, d), BF16),
                   jax.ShapeDtypeStruct((t, LANES), F32),
                   jax.ShapeDtypeStruct((nt, 8, LANES), F32)],
        compiler_params=pltpu.CompilerParams(vmem_limit_bytes=VMEM_LIMIT),
        name="route",
    )(x1, g_ffn, wr, br)


def _expert_kernel(te_ref, nu_ref, src_ref, xs_ref, wg_ref, wu_ref, wd_ref, o_ref, xbuf_ref, sem_ref):
    i = pl.program_id(0)
    n_used = nu_ref[0]
    slot = i % 2

    def chunk_copy(tile, c, buf):
        return pltpu.make_async_copy(
            xs_ref.at[_chunk_rows(src_ref[tile * CHUNKS_PER_ETILE + c])],
            xbuf_ref.at[buf, pl.ds(c * CHUNK_ROWS, CHUNK_ROWS)], sem_ref.at[buf])

    def gather(tile, buf):
        for c in range(CHUNKS_PER_ETILE):
            chunk_copy(tile, c, buf).start(priority=c % 2)

    def drain(tile, buf):
        del tile
        pltpu.make_async_copy(xs_ref.at[pl.ds(0, EXPERT_TILE)], xbuf_ref.at[buf], sem_ref.at[buf]).wait()

    @pl.when(i == 0)
    def _():
        gather(0, 0)

    @pl.when(i < n_used)
    def _():
        nxt = jnp.minimum(i + 1, n_used - 1)
        gather(nxt, 1 - slot)
        drain(i, slot)
        x = xbuf_ref[slot]
        gate = jnp.dot(x, wg_ref[0].astype(BF16), preferred_element_type=F32)
        up = jnp.dot(x, wu_ref[0].astype(BF16), preferred_element_type=F32)
        act = (gate * _sigmoid(gate) * up).astype(BF16)
        o_ref[...] = jnp.dot(act, wd_ref[0].astype(BF16), preferred_element_type=F32).astype(BF16)

        @pl.when(i == n_used - 1)
        def _():
            drain(nxt, 1 - slot)

    @pl.when(i >= n_used)
    def _():
        o_ref[...] = jnp.zeros_like(o_ref)


def _expert_call(tile_expert, n_used, chunk_src, xs, wg, wu, wd):
    d = xs.shape[1]
    de = wg.shape[2]
    tm = EXPERT_TILE
    nt = tile_expert.shape[0]

    def w_map(i, te, nu, src):
        return (te[jnp.maximum(jnp.minimum(i, nu[0] - 1), 0)], 0, 0)

    return pl.pallas_call(
        _expert_kernel,
        grid_spec=pltpu.PrefetchScalarGridSpec(
            num_scalar_prefetch=3,
            grid=(nt,),
            in_specs=[pl.BlockSpec(memory_space=pl.ANY),
                      pl.BlockSpec((1, d, de), w_map),
                      pl.BlockSpec((1, d, de), w_map),
                      pl.BlockSpec((1, de, d), w_map)],
            out_specs=pl.BlockSpec((tm, d), lambda i, te, nu, src: (i, 0)),
            scratch_shapes=[pltpu.VMEM((2, tm, d), BF16),
                            pltpu.SemaphoreType.DMA((2,))]),
        out_shape=jax.ShapeDtypeStruct((nt * tm, d), BF16),
        compiler_params=pltpu.CompilerParams(dimension_semantics=("arbitrary",)),
        name="expert",
    )(tile_expert, n_used, chunk_src, xs, wg, wu, wd)


def _combine_kernel(dst_ref, x_ref, info_ref, g_ref, ys_ref, o_ref, ybuf_ref, sem_ref):
    i = pl.program_id(0)
    n_tiles = pl.num_programs(0)
    ts = x_ref.shape[0]
    cap = ybuf_ref.shape[1]
    cpt = cap // CHUNK_ROWS
    slot = i % 2

    def chunk_copy(tile, k, buf):
        return pltpu.make_async_copy(ys_ref.at[_chunk_rows(dst_ref[tile * cpt + k])],
                                     ybuf_ref.at[buf, pl.ds(k * CHUNK_ROWS, CHUNK_ROWS)], sem_ref.at[buf])

    def gather(tile, buf):
        for k in range(cpt):
            chunk_copy(tile, k, buf).start(priority=k % 2)

    def drain(tile, buf):
        del tile
        pltpu.make_async_copy(ys_ref.at[pl.ds(0, cap)], ybuf_ref.at[buf], sem_ref.at[buf]).wait()

    @pl.when(i == 0)
    def _():
        gather(0, 0)

    nxt = jnp.minimum(i + 1, n_tiles - 1)
    gather(nxt, 1 - slot)
    drain(i, slot)

    info = info_ref[...]
    pos1 = info[:, 0:1].astype(jnp.int32)
    pos2 = info[:, 1:2].astype(jnp.int32)
    w1 = info[:, 2:3]
    w2 = info[:, 3:4]
    rowid = lax.broadcasted_iota(jnp.int32, (ts, cap), 1)
    ys = ybuf_ref[slot]
    pw = jnp.where(rowid == pos1, w1, jnp.where(rowid == pos2, w2, 0.0)).astype(BF16)
    y = jnp.dot(pw, ys, preferred_element_type=F32)
    o_ref[...] = _rms(x_ref[...] + y, g_ref[...])

    @pl.when(i == n_tiles - 1)
    def _():
        drain(nxt, 1 - slot)


def _combine_call(chunk_dst, x1, info, g_final, ys_sorted, cap):
    t, d = x1.shape
    ts = TOKEN_TILE
    return pl.pallas_call(
        _combine_kernel,
        grid_spec=pltpu.PrefetchScalarGridSpec(
            num_scalar_prefetch=1,
            grid=(t // ts,),
            in_specs=[pl.BlockSpec((ts, d), lambda i, dst: (i, 0)),
                      pl.BlockSpec((ts, LANES), lambda i, dst: (i, 0)),
                      pl.BlockSpec((1, d), lambda i, dst: (0, 0)),
                      pl.BlockSpec(memory_space=pl.ANY)],
            out_specs=pl.BlockSpec((ts, d), lambda i, dst: (i, 0)),
            scratch_shapes=[pltpu.VMEM((2, cap, d), BF16),
                            pltpu.SemaphoreType.DMA((2,))]),
        out_shape=jax.ShapeDtypeStruct((t, d), F32),
        compiler_params=pltpu.CompilerParams(dimension_semantics=("arbitrary",),
                                             vmem_limit_bytes=VMEM_LIMIT),
        name="combine",
    )(chunk_dst, x1, info, g_final, ys_sorted)


def _run_tables(nch, cpt, n_etiles):
    n_tiles, n_exp = nch.shape
    per = CHUNKS_PER_ETILE
    tcum = jnp.cumsum(nch, axis=1)
    toff = tcum - nch
    ecum = jnp.cumsum(nch, axis=0)
    etot = ecum[-1]
    eseg = -(-etot // per) * per
    segcum = jnp.cumsum(eseg)
    ebase = segcum - eseg
    eoff = ebase[None, :] + ecum - nch
    n_used = (segcum[-1] // per).reshape(1)

    k = jnp.arange(cpt, dtype=jnp.int32)[None, :, None]
    in_run = (k >= toff[:, None, :]) & (k < tcum[:, None, :])
    chunk_dst = jnp.sum(jnp.where(in_run, (eoff - toff)[:, None, :] + k, 0), axis=-1)

    first = jnp.arange(n_etiles, dtype=jnp.int32)[:, None] * per
    owner = (first >= ebase[None, :]) & (first < segcum[None, :])
    tile_expert = jnp.sum(jnp.where(owner, jnp.arange(n_exp, dtype=jnp.int32)[None, :], 0), axis=-1)

    pick = lambda a: jnp.sum(jnp.where(owner[:, None, :], a[None, :, :], 0), axis=-1)
    run_lo, run_n = pick(eoff), pick(nch)
    shift = pick(jnp.arange(n_tiles, dtype=jnp.int32)[:, None] * cpt + toff - eoff)
    c = (first + jnp.arange(per, dtype=jnp.int32)[None, :])[:, :, None]
    hit = (c >= run_lo[:, None, :]) & (c < (run_lo + run_n)[:, None, :])
    chunk_src = jnp.sum(jnp.where(hit, shift[:, None, :] + c, 0), axis=-1)
    i32 = lambda a: a.reshape(-1).astype(jnp.int32)
    return i32(chunk_dst), i32(tile_expert), i32(n_used), i32(chunk_src)


def _layer(x, mem, g_mix, w_in, conv_dw, conv_dw_bias, conv_ln_g, conv_ln_b, w_conv_out,
           lam_re, lam_im, log_dt, b_re, b_im, c_re, c_im, ssm_d, w_ssm_glu, g_mem, w_mem_kv,
           w_mem_out, w_out, g_ffn, w_rg, b_rg, w_re, b_re_, w_eg, w_eu, w_ed, g_final):
    b, s, d = x.shape
    t = b * s
    cw = conv_dw.shape[1]
    sw = ssm_d.shape[0]
    qw = w_mem_out.shape[0]
    n_groups = sw // SSM_GROUP
    row = lambda a: a.reshape(1, -1)

    o0, o1, o2 = 2 * cw, 2 * cw + sw, 2 * cw + sw + qw
    wc = w_in[:, :o0].astype(BF16)
    w_ssm_t = w_in[:, o0:o1].T.astype(BF16)
    wq = w_in[:, o1:o2].astype(BF16)
    wg = w_in[:, o2:].astype(BF16)

    kmem, vmem = _kv_call(mem, row(g_mem), w_mem_kv.astype(BF16))

    ut = _ut_call(x.reshape(t, d), row(g_mix), w_ssm_t)
    kt, wz, vy, acoef = _ssm_tables(lam_re, lam_im, log_dt, b_re, b_im, c_re, c_im, ssm_d)
    yt = _ssm_call(ut, kt, wz, vy, acoef, b)

    dw = jnp.concatenate([conv_dw, jnp.zeros((CONV_HALO - CONV_KERNEL, cw), F32)], axis=0)
    x1 = _mix_call(x, yt, kmem, vmem, row(g_mix), wc, wq, wg, dw, row(conv_dw_bias),
                   row(conv_ln_g), row(conv_ln_b), w_conv_out.astype(BF16),
                   w_ssm_glu.astype(BF16), w_mem_out.astype(BF16), w_out.astype(BF16))
    x1 = x1.reshape(t, d)

    pad = LANES - N_EXPERTS - MOE_GROUPS
    wr = jnp.concatenate([w_re, w_rg, jnp.zeros((d, pad), F32)], axis=1)
    wr_hi = wr.astype(BF16)
    wr = jnp.concatenate([wr_hi, (wr - wr_hi.astype(F32)).astype(BF16)], axis=1)
    br = jnp.concatenate([b_re_, b_rg, jnp.zeros((pad,), F32)]).reshape(1, LANES)
    n_tiles = t // TOKEN_TILE
    max_tile_chunks = (2 * TOKEN_TILE + N_EXPERTS * (CHUNK_ROWS - 1)) // CHUNK_ROWS
    cpt = -(-max_tile_chunks // 8) * 8
    cap = cpt * CHUNK_ROWS
    xs_tiles, info, nch_f = _route_call(x1, row(g_ffn), wr, br, cap)

    max_chunks = n_tiles * max_tile_chunks + N_EXPERTS * (CHUNKS_PER_ETILE - 1)
    n_etiles = -(-max_chunks // CHUNKS_PER_ETILE)
    nch = nch_f[:, 0, :N_EXPERTS].astype(jnp.int32)
    chunk_dst, tile_expert, n_used, chunk_src = _run_tables(nch, cpt, n_etiles)

    ys_sorted = _expert_call(tile_expert, n_used, chunk_src, xs_tiles, w_eg, w_eu, w_ed)
    out = _combine_call(chunk_dst, x1, info, row(g_final), ys_sorted, cap)
    return out.reshape(b, s, d)


def kernel(x, mem, g_mix, w_in, conv_dw, conv_dw_bias, conv_ln_g, conv_ln_b, w_conv_out, ssm_lambda_re, ssm_lambda_im, ssm_log_dt, ssm_b_re, ssm_b_im, ssm_c_re, ssm_c_im, ssm_d, w_ssm_glu, g_mem, w_mem_kv, w_mem_out, w_out, g_ffn, w_router_group, b_router_group, w_router_expert, b_router_expert, w_exp_gate, w_exp_up, w_exp_down, g_final):
    assert g_mix.shape[0] == 1, "the problem fixes one layer"
    l = 0
    return _layer(
        x, mem, g_mix[l], w_in[l], conv_dw[l], conv_dw_bias[l], conv_ln_g[l], conv_ln_b[l],
        w_conv_out[l], ssm_lambda_re[l], ssm_lambda_im[l], ssm_log_dt[l], ssm_b_re[l],
        ssm_b_im[l], ssm_c_re[l], ssm_c_im[l], ssm_d[l], w_ssm_glu[l], g_mem[l], w_mem_kv[l],
        w_mem_out[l], w_out[l], g_ffn[l], w_router_group[l], b_router_group[l],
        w_router_expert[l], b_router_expert[l], w_exp_gate[l], w_exp_up[l], w_exp_down[l], g_final)
```

```python
import functools

import jax
import jax.numpy as jnp
from jax import lax
from jax.experimental import pallas as pl
from jax.experimental.pallas import tpu as pltpu

F32 = jnp.float32
BF16 = jnp.bfloat16
EPS = 1e-6

LANES = 128
CHUNK_ROWS = 16
SSM_GROUP = 16
SSM_STATE = 64
SSM_BLOCK = LANES
CONV_KERNEL = 31
CONV_HALO = 32
HEADS = 4
HEAD_DIM = 128
MOE_GROUPS = 4
EXPERTS_PER_GROUP = 8
N_EXPERTS = MOE_GROUPS * EXPERTS_PER_GROUP
TOKEN_TILE = 512
EXPERT_TILE = 512
CHUNKS_PER_ETILE = EXPERT_TILE // CHUNK_ROWS
VMEM_LIMIT = 56 * 1024 * 1024


def _rms(x, g):
    return x * lax.rsqrt(jnp.mean(x * x, axis=-1, keepdims=True) + EPS) * g


def _sigmoid(x):
    return 0.5 * jnp.tanh(0.5 * x) + 0.5


def _const_spec(shape):
    zeros = (0,) * len(shape)
    return pl.BlockSpec(shape, lambda *_: zeros, pipeline_mode=pl.Buffered(1))


def _kv_kernel(mem_ref, g_ref, w_ref, k_ref, v_ref):
    width = k_ref.shape[-1]
    mn = _rms(mem_ref[0], g_ref[...]).astype(BF16)
    kv = jnp.dot(mn, w_ref[...], preferred_element_type=F32)
    k_ref[0] = kv[:, :width].astype(BF16)
    v_ref[0] = kv[:, width:].astype(BF16)


def _kv_call(mem, g_mem, w_kv):
    b, m, d = mem.shape
    width = w_kv.shape[1] // 2
    return pl.pallas_call(
        _kv_kernel,
        grid=(b,),
        in_specs=[pl.BlockSpec((1, m, d), lambda i: (i, 0, 0)),
                  pl.BlockSpec((1, d), lambda i: (0, 0)),
                  pl.BlockSpec(w_kv.shape, lambda i: (0, 0))],
        out_specs=[pl.BlockSpec((1, m, width), lambda i: (i, 0, 0)),
                   pl.BlockSpec((1, m, width), lambda i: (i, 0, 0))],
        out_shape=[jax.ShapeDtypeStruct((b, m, width), BF16)] * 2,
        name="kv",
    )(mem, g_mem, w_kv)


def _ut_kernel(x_ref, g_ref, wt_ref, o_ref):
    h = _rms(x_ref[...], g_ref[...]).astype(BF16)
    ut = lax.dot_general(wt_ref[...], h, (((1,), (1,)), ((), ())), preferred_element_type=F32)
    ut = ut.reshape(ut.shape[0], ut.shape[1] // SSM_BLOCK, SSM_BLOCK)
    o_ref[...] = ut.reshape(o_ref.shape).astype(BF16)


def _ut_call(x2, g_mix, w_ssm_t):
    t, d = x2.shape
    c = w_ssm_t.shape[0]
    ts = 8 * SSM_BLOCK
    n_groups = c // SSM_GROUP
    return pl.pallas_call(
        _ut_kernel,
        grid=(t // ts,),
        in_specs=[pl.BlockSpec((ts, d), lambda i: (i, 0)),
                  pl.BlockSpec((1, d), lambda i: (0, 0)),
                  pl.BlockSpec((c, d), lambda i: (0, 0))],
        out_specs=pl.BlockSpec((n_groups, SSM_GROUP, ts // SSM_BLOCK, SSM_BLOCK),
                               lambda i: (0, 0, i, 0)),
        out_shape=jax.ShapeDtypeStruct((n_groups, SSM_GROUP, t // SSM_BLOCK, SSM_BLOCK), BF16),
        name="ut",
    )(x2, g_mix, w_ssm_t)


def _ssm_kernel(u_ref, kt_ref, w_ref, v_ref, a_ref, y_ref, acc_ref, z_ref, zs_ref, s_ref, slab_ref,
                *, n_batch):
    rows = u_ref.shape[2]
    n_blocks = rows // n_batch
    ri = lax.broadcasted_iota(jnp.int32, (SSM_BLOCK, SSM_BLOCK), 0)
    ci = lax.broadcasted_iota(jnp.int32, (SSM_BLOCK, SSM_BLOCK), 1)
    causal = ci >= ri

    n_pairs = SSM_GROUP // 2

    def build(cp, slot):
        for half in range(2):
            kt = kt_ref[0, 2 * cp + half]
            for c in range(SSM_GROUP):
                xb = jnp.broadcast_to(kt[c:c + 1, :], (SSM_BLOCK, SSM_BLOCK))
                toe = pltpu.roll(xb, 0, 1, stride=1, stride_axis=0)
                slab_ref[slot, half * SSM_BLOCK:(half + 1) * SSM_BLOCK, c * SSM_BLOCK:(c + 1) * SSM_BLOCK] = (
                    jnp.where(causal, toe, 0.0).astype(BF16))

    def apply(cp, slot, first):
        x2 = jnp.concatenate([u_ref[0, 2 * cp], u_ref[0, 2 * cp + 1]], axis=1)
        part = jnp.dot(x2, slab_ref[slot], preferred_element_type=F32)
        wrow = pl.multiple_of(cp * 2 * SSM_BLOCK, 2 * SSM_BLOCK)
        zpart = jnp.dot(x2, w_ref[0, pl.ds(wrow, 2 * SSM_BLOCK), :], preferred_element_type=F32)
        if first:
            acc_ref[...] = part
            z_ref[...] = zpart
        else:
            acc_ref[...] += part
            z_ref[...] += zpart

    build(0, 0)
    build(1, 1)
    apply(0, 0, True)

    def two(it, carry):
        cp = 2 * it + 1
        build(cp + 1, 0)
        apply(cp, 1, False)
        build(cp + 2, 1)
        apply(cp + 1, 0, False)
        return carry

    lax.fori_loop(0, (n_pairs - 2) // 2, two, 0)
    apply(n_pairs - 1, 1, False)

    a_full = a_ref[0, 0:1, :]
    a_swap = a_ref[0, 1:2, :]
    zs_ref[...] = pltpu.roll(z_ref[...], SSM_STATE, 1)
    st = jnp.zeros((n_batch, 2 * SSM_STATE), F32)
    sw = jnp.zeros((n_batch, 2 * SSM_STATE), F32)
    for blk in range(n_blocks):
        sl = pl.ds(blk, n_batch, stride=n_blocks)
        s_ref[sl, :] = st
        st, sw = (a_full * st + a_swap * sw + z_ref[sl, :],
                  a_full * sw - a_swap * st + zs_ref[sl, :])

    sb = s_ref[...].astype(BF16)
    y3 = jnp.stack([acc_ref[:, c * SSM_BLOCK:(c + 1) * SSM_BLOCK]
                    + jnp.dot(sb, v_ref[0, c], preferred_element_type=F32)
                    for c in range(SSM_GROUP)], axis=0)
    y_ref[...] = y3.reshape(y_ref.shape)


def _ssm_call(u4, kt, wz, vy, acoef, n_batch):
    g, c, rows, blk = u4.shape
    width = c * blk
    return pl.pallas_call(
        functools.partial(_ssm_kernel, n_batch=n_batch),
        grid=(g,),
        in_specs=[pl.BlockSpec((1, c, rows, blk), lambda i: (i, 0, 0, 0)),
                  pl.BlockSpec((1, c, c, blk), lambda i: (i, 0, 0, 0)),
                  pl.BlockSpec((1, width, 2 * SSM_STATE), lambda i: (i, 0, 0)),
                  pl.BlockSpec((1, c, 2 * SSM_STATE, blk), lambda i: (i, 0, 0, 0)),
                  pl.BlockSpec((1, 2, 2 * SSM_STATE), lambda i: (i, 0, 0))],
        out_specs=pl.BlockSpec((c, rows * blk), lambda i: (i, 0)),
        out_shape=jax.ShapeDtypeStruct((g * c, rows * blk), F32),
        scratch_shapes=[pltpu.VMEM((rows, width), F32),
                        pltpu.VMEM((rows, 2 * SSM_STATE), F32),
                        pltpu.VMEM((rows, 2 * SSM_STATE), F32),
                        pltpu.VMEM((rows, 2 * SSM_STATE), F32),
                        pltpu.VMEM((2, 2 * blk, width), BF16)],
        name="ssm",
    )(u4, kt, wz, vy, acoef)


def _ssm_tables(lam_re, lam_im, log_dt, b_re, b_im, c_re, c_im, d):
    hi = lax.Precision.HIGHEST
    g = lam_re.shape[0]
    dt = jnp.exp(log_dt)[:, None]
    er, ei = lam_re * dt, lam_im * dt
    cat = jnp.concatenate
    kk = jnp.arange(SSM_BLOCK + 1, dtype=F32)
    kdesc = jnp.arange(SSM_BLOCK - 1, -1, -1, dtype=F32)[None, :, None]
    ppk_m, ppk_a = jnp.exp(er[:, :, None] * kk), ei[:, :, None] * kk
    ppk_r, ppk_i = ppk_m * jnp.cos(ppk_a), ppk_m * jnp.sin(ppk_a)
    prev_m, prev_a = jnp.exp(kdesc * er[:, None, :]), kdesc * ei[:, None, :]
    prr, pir = prev_m * jnp.cos(prev_a), prev_m * jnp.sin(prev_a)
    nr, ni = ppk_r[:, :, 1] - 1.0, ppk_i[:, :, 1]
    den = lam_re * lam_re + lam_im * lam_im
    fr = (nr * lam_re + ni * lam_im) / den
    fi = (ni * lam_re - nr * lam_im) / den
    bbr = jnp.swapaxes(fr[:, :, None] * b_re - fi[:, :, None] * b_im, 1, 2)
    bbi = jnp.swapaxes(fr[:, :, None] * b_im + fi[:, :, None] * b_re, 1, 2)
    cb_r = c_re[:, None] * bbr[:, :, None, :] - c_im[:, None] * bbi[:, :, None, :]
    cb_i = c_re[:, None] * bbi[:, :, None, :] + c_im[:, None] * bbr[:, :, None, :]
    cb = cat([cb_r, -cb_i], axis=-1).reshape(g, SSM_GROUP * SSM_GROUP, 2 * SSM_STATE)
    pk = cat([ppk_r[:, :, :SSM_BLOCK], ppk_i[:, :, :SSM_BLOCK]], axis=1)
    kt = jnp.einsum("gmq,gqk->gmk", cb, pk, precision=hi).reshape(g, SSM_GROUP, SSM_GROUP, SSM_BLOCK)
    dmat = jnp.eye(SSM_GROUP, dtype=F32)[None] * d.reshape(g, 1, SSM_GROUP)
    kt = kt + dmat[..., None] * (jnp.arange(SSM_BLOCK) == 0).astype(F32)
    wz = (cat([prr, prr], -1)[:, None] * cat([bbr, bbi], -1)[:, :, None, :]
          + cat([pir, pir], -1)[:, None] * cat([-bbi, bbr], -1)[:, :, None, :])
    wz = wz.astype(BF16).reshape(g, SSM_GROUP * SSM_BLOCK, 2 * SSM_STATE)
    pt_r, pt_i = ppk_r[:, :, 1:], ppk_i[:, :, 1:]
    vy = (cat([c_re, -c_re], -1)[..., None] * cat([pt_r, pt_i], axis=1)[:, None]
          + cat([-c_im, -c_im], -1)[..., None] * cat([pt_i, pt_r], axis=1)[:, None]).astype(BF16)
    ar, ai = ppk_r[:, :, SSM_BLOCK], ppk_i[:, :, SSM_BLOCK]
    acoef = jnp.stack([cat([ar, ar], -1), cat([-ai, ai], -1)], axis=1)
    return kt, wz, vy, acoef


def _mix_kernel(x_ref, yt_ref, k_ref, v_ref, g_ref, wc_ref, wq_ref, wg_ref, dw_ref, dwb_ref,
                lng_ref, lnb_ref, wpw_ref, wglu_ref, wo_ref, wout_ref, o_ref, vext_ref, vsh_ref):
    ts = x_ref.shape[1]
    d = x_ref.shape[2]
    cw = dw_ref.shape[1]
    x = x_ref[0]
    h = _rms(x, g_ref[...]).astype(BF16)

    ci = jnp.dot(h, wc_ref[...], preferred_element_type=F32)
    v = ci[:, :cw] * _sigmoid(ci[:, cw:])
    q = jnp.dot(h, wq_ref[...], preferred_element_type=F32)

    @pl.when(pl.program_id(1) == 0)
    def _():
        vext_ref[0:CONV_HALO, :] = jnp.zeros((CONV_HALO, cw), F32)

    vext_ref[CONV_HALO:CONV_HALO + ts, :] = v
    acc = jnp.broadcast_to(dwb_ref[...], (ts, cw))
    first = CONV_HALO - (CONV_KERNEL - 1)
    for phase in range(8):
        offs = [first + k for k in range(CONV_KERNEL) if (first + k) % 8 == phase]
        if not offs:
            continue
        span = offs[-1] - offs[0] + ts
        vsh_ref[0:span, :] = vext_ref[offs[0]:offs[0] + span, :]
        for off in offs:
            acc = acc + dw_ref[off - first:off - first + 1, :] * vsh_ref[off - offs[0]:off - offs[0] + ts, :]
    vext_ref[0:CONV_HALO, :] = vext_ref[ts:ts + CONV_HALO, :]
    mu = jnp.mean(acc, axis=-1, keepdims=True)
    xc = acc - mu
    var = jnp.mean(xc * xc, axis=-1, keepdims=True)
    ln = xc * lax.rsqrt(var + EPS) * lng_ref[...] + lnb_ref[...]
    sw = ln * _sigmoid(ln)
    y_conv = jnp.dot(sw.astype(BF16), wpw_ref[...], preferred_element_type=F32)
    merged = _sigmoid(jnp.dot(h, wg_ref[:, 0:d], preferred_element_type=F32)) * y_conv

    gy = jax.nn.gelu(yt_ref[...]).astype(BF16)
    z = lax.dot_general(gy, wglu_ref[...], (((0,), (0,)), ((), ())), preferred_element_type=F32)
    y_ssm = z[:, :d] * _sigmoid(z[:, d:])
    merged = merged + _sigmoid(jnp.dot(h, wg_ref[:, d:2 * d], preferred_element_type=F32)) * y_ssm

    kk = k_ref[0]
    vv = v_ref[0]
    outs = []
    for hd in range(HEADS):
        sl = slice(hd * HEAD_DIM, (hd + 1) * HEAD_DIM)
        s = lax.dot_general(q[:, sl].astype(BF16), kk[:, sl], (((1,), (1,)), ((), ())),
                            preferred_element_type=F32) * (HEAD_DIM ** -0.5)
        p = jnp.exp(s - jnp.max(s, axis=-1, keepdims=True))
        den = jnp.sum(p, axis=-1, keepdims=True)
        o = jnp.dot(p.astype(BF16), vv[:, sl], preferred_element_type=F32) / den
        outs.append(o.astype(BF16))
    y_mem = jnp.dot(jnp.concatenate(outs, axis=1), wo_ref[...], preferred_element_type=F32)
    merged = merged + _sigmoid(jnp.dot(h, wg_ref[:, 2 * d:3 * d], preferred_element_type=F32)) * y_mem

    o_ref[0] = x + jnp.dot(merged.astype(BF16), wout_ref[...], preferred_element_type=F32)


def _mix_call(x, yt, kmem, vmem, g_mix, wc, wq, wg, dw, dwb, lng, lnb, wpw, wglu, wo, wout):
    b, s, d = x.shape
    ts = TOKEN_TILE
    nst = s // ts
    cw = dw.shape[1]
    m = kmem.shape[1]
    consts = [g_mix, wc, wq, wg, dw, dwb, lng, lnb, wpw, wglu, wo, wout]
    return pl.pallas_call(
        _mix_kernel,
        grid=(b, nst),
        in_specs=[pl.BlockSpec((1, ts, d), lambda i, j: (i, j, 0)),
                  pl.BlockSpec((yt.shape[0], ts), lambda i, j: (0, i * nst + j)),
                  pl.BlockSpec((1, m, kmem.shape[2]), lambda i, j: (i, 0, 0)),
                  pl.BlockSpec((1, m, vmem.shape[2]), lambda i, j: (i, 0, 0))]
                 + [_const_spec(c.shape) for c in consts],
        out_specs=pl.BlockSpec((1, ts, d), lambda i, j: (i, j, 0)),
        out_shape=jax.ShapeDtypeStruct((b, s, d), F32),
        scratch_shapes=[pltpu.VMEM((ts + CONV_HALO, cw), F32),
                        pltpu.VMEM((ts + CONV_HALO, cw), F32)],
        compiler_params=pltpu.CompilerParams(
            dimension_semantics=("arbitrary", "arbitrary"), vmem_limit_bytes=VMEM_LIMIT),
        name="mix",
    )(x, yt, kmem, vmem, *consts)


def _route_kernel(x_ref, g_ref, wr_ref, br_ref, xs_ref, info_ref, nch_ref):
    ts = x_ref.shape[0]
    cap = xs_ref.shape[0] * CHUNK_ROWS
    h2 = _rms(x_ref[...], g_ref[...])
    hb = h2.astype(BF16)
    hl = (h2 - hb.astype(F32)).astype(BF16)
    hw = jnp.dot(hb, wr_ref[...], preferred_element_type=F32)
    logits = (hw[:, :LANES] + hw[:, LANES:]
              + jnp.dot(hl, wr_ref[:, :LANES], preferred_element_type=F32)) + br_ref[...]
    lt = logits.T
    le = lt[0:N_EXPERTS]
    lg = lt[N_EXPERTS:N_EXPERTS + 8]
    neg = jnp.float32(-1e30)
    big = jnp.float32(1e9)
    g_f = lax.broadcasted_iota(jnp.int32, (8, ts), 0).astype(F32)
    e_f = lax.broadcasted_iota(jnp.int32, (N_EXPERTS, ts), 0).astype(F32)

    gmask = g_f < MOE_GROUPS
    gmax = jnp.max(jnp.where(gmask, lg, neg), axis=0, keepdims=True)
    gidx = jnp.min(jnp.where(gmask & (lg == gmax), g_f, big), axis=0, keepdims=True)
    gsum = jnp.sum(jnp.where(gmask, jnp.exp(jnp.minimum(lg - gmax, 0.0)), 0.0), axis=0, keepdims=True)
    p_top = 1.0 / gsum
    emask = jnp.floor(e_f * (1.0 / EXPERTS_PER_GROUP)) == gidx
    m1 = jnp.max(jnp.where(emask, le, neg), axis=0, keepdims=True)
    i1 = jnp.min(jnp.where(emask & (le == m1), e_f, big), axis=0, keepdims=True)
    emask2 = emask & (e_f != i1)
    m2 = jnp.max(jnp.where(emask2, le, neg), axis=0, keepdims=True)
    i2 = jnp.min(jnp.where(emask2 & (le == m2), e_f, big), axis=0, keepdims=True)
    r = jnp.exp(m2 - m1)
    w1 = p_top / (1.0 + r)
    w2 = p_top * r / (1.0 + r)

    sel1 = e_f == i1
    sel2 = e_f == i2
    occ = jnp.where(sel1 | sel2, 1.0, 0.0)
    tr = lax.broadcasted_iota(jnp.int32, (ts, ts), 0)
    tc = lax.broadcasted_iota(jnp.int32, (ts, ts), 1)
    earlier = jnp.where(tr < tc, 1.0, 0.0).astype(BF16)
    rank = jnp.dot(occ.astype(BF16), earlier, preferred_element_type=F32)
    cnt = jnp.sum(occ, axis=1, keepdims=True)
    nch = jnp.floor((cnt + (CHUNK_ROWS - 1)) * (1.0 / CHUNK_ROWS))
    er = lax.broadcasted_iota(jnp.int32, (N_EXPERTS, N_EXPERTS), 0)
    ec = lax.broadcasted_iota(jnp.int32, (N_EXPERTS, N_EXPERTS), 1)
    lower = jnp.where(ec < er, 1.0, 0.0).astype(BF16)
    nch_b = jnp.broadcast_to(nch, (N_EXPERTS, LANES))
    start = jnp.dot(lower, nch_b.astype(BF16), preferred_element_type=F32)[:, 0:1] * CHUNK_ROWS
    slot = start + rank
    pos1 = jnp.sum(jnp.where(sel1, slot, 0.0), axis=0, keepdims=True)
    pos2 = jnp.sum(jnp.where(sel2, slot, 0.0), axis=0, keepdims=True)

    rowid = lax.broadcasted_iota(jnp.int32, (cap, ts), 0)
    p = jnp.where((rowid == pos1.astype(jnp.int32)) | (rowid == pos2.astype(jnp.int32)), 1.0, 0.0)
    xs = jnp.dot(p.astype(BF16), hb, preferred_element_type=F32).astype(BF16)
    xs_ref[...] = xs.reshape(xs_ref.shape)
    sub = lax.broadcasted_iota(jnp.int32, (LANES, ts), 0)
    info_t = jnp.where(sub == 0, pos1, jnp.where(sub == 1, pos2,
                       jnp.where(sub == 2, w1, jnp.where(sub == 3, w2, 0.0))))
    info_ref[...] = info_t.T
    nch_ref[0] = jnp.concatenate([nch_b, jnp.zeros((LANES - N_EXPERTS, LANES), F32)], axis=0).T[0:8, :]


def _route_call(x1, g_ffn, wr, br, cap):
    t, d = x1.shape
    ts = TOKEN_TILE
    nt = t // ts
    return pl.pallas_call(
        _route_kernel,
        grid=(nt,),
        in_specs=[pl.BlockSpec((ts, d), lambda i: (i, 0)),
                  pl.BlockSpec((1, d), lambda i: (0, 0)),
                  pl.BlockSpec(wr.shape, lambda i: (0, 0)),
                  pl.BlockSpec((1, LANES), lambda i: (0, 0))],
        out_specs=[pl.BlockSpec((cap // CHUNK_ROWS, CHUNK_ROWS, d), lambda i: (i, 0, 0)),
                   pl.BlockSpec((ts, LANES), lambda i: (i, 0)),
                   pl.BlockSpec((1, 8, LANES), lambda i: (i, 0, 0))],
        out_shape=[jax.ShapeDtypeStruct((nt * cap // CHUNK_ROWS, CHUNK_ROWS, d), BF16),
                   jax.ShapeDtypeStruct((t, LANES), F32),
                   jax.ShapeDtypeStruct((nt, 8, LANES), F32)],
        compiler_params=pltpu.CompilerParams(vmem_limit_bytes=VMEM_LIMIT),
        name="route",
    )(x1, g_ffn, wr, br)


def _expert_kernel(te_ref, nu_ref, src_ref, xs_ref, wg_ref, wu_ref, wd_ref, o_ref, xbuf_ref, sem_ref):
    i = pl.program_id(0)
    n_used = nu_ref[0]
    slot = i % 2

    def chunk_copy(tile, c, buf):
        return pltpu.make_async_copy(
            xs_ref.at[src_ref[tile * CHUNKS_PER_ETILE + c]], xbuf_ref.at[buf, c], sem_ref.at[buf])

    def gather(tile, buf):
        for c in range(CHUNKS_PER_ETILE):
            chunk_copy(tile, c, buf).start(priority=c % 2)

    def drain(tile, buf):
        del tile
        pltpu.make_async_copy(xs_ref.at[pl.ds(0, CHUNKS_PER_ETILE)], xbuf_ref.at[buf], sem_ref.at[buf]).wait()

    @pl.when(i == 0)
    def _():
        gather(0, 0)

    @pl.when(i < n_used)
    def _():
        nxt = jnp.minimum(i + 1, n_used - 1)
        gather(nxt, 1 - slot)
        drain(i, slot)
        x = xbuf_ref[slot].reshape(EXPERT_TILE, xbuf_ref.shape[-1])
        gate = jnp.dot(x, wg_ref[0].astype(BF16), preferred_element_type=F32)
        up = jnp.dot(x, wu_ref[0].astype(BF16), preferred_element_type=F32)
        act = (gate * _sigmoid(gate) * up).astype(BF16)
        y = jnp.dot(act, wd_ref[0].astype(BF16), preferred_element_type=F32).astype(BF16)
        o_ref[...] = y.reshape(o_ref.shape)

        @pl.when(i == n_used - 1)
        def _():
            drain(nxt, 1 - slot)

    @pl.when(i >= n_used)
    def _():
        o_ref[...] = jnp.zeros_like(o_ref)


def _expert_call(tile_expert, n_used, chunk_src, xs, wg, wu, wd):
    d = xs.shape[-1]
    de = wg.shape[2]
    tm = EXPERT_TILE
    nt = tile_expert.shape[0]

    def w_map(i, te, nu, src):
        return (te[jnp.maximum(jnp.minimum(i, nu[0] - 1), 0)], 0, 0)

    return pl.pallas_call(
        _expert_kernel,
        grid_spec=pltpu.PrefetchScalarGridSpec(
            num_scalar_prefetch=3,
            grid=(nt,),
            in_specs=[pl.BlockSpec(memory_space=pl.ANY),
                      pl.BlockSpec((1, d, de), w_map),
                      pl.BlockSpec((1, d, de), w_map),
                      pl.BlockSpec((1, de, d), w_map)],
            out_specs=pl.BlockSpec((CHUNKS_PER_ETILE, CHUNK_ROWS, d), lambda i, te, nu, src: (i, 0, 0)),
            scratch_shapes=[pltpu.VMEM((2, CHUNKS_PER_ETILE, CHUNK_ROWS, d), BF16),
                            pltpu.SemaphoreType.DMA((2,))]),
        out_shape=jax.ShapeDtypeStruct((nt * CHUNKS_PER_ETILE, CHUNK_ROWS, d), BF16),
        compiler_params=pltpu.CompilerParams(dimension_semantics=("arbitrary",)),
        name="expert",
    )(tile_expert, n_used, chunk_src, xs, wg, wu, wd)


def _combine_kernel(dst_ref, x_ref, info_ref, g_ref, ys_ref, o_ref, ybuf_ref, sem_ref):
    i = pl.program_id(0)
    n_tiles = pl.num_programs(0)
    ts = x_ref.shape[0]
    cpt = ybuf_ref.shape[1]
    cap = cpt * CHUNK_ROWS
    slot = i % 2

    def chunk_copy(tile, k, buf):
        return pltpu.make_async_copy(ys_ref.at[dst_ref[tile * cpt + k]], ybuf_ref.at[buf, k], sem_ref.at[buf])

    def gather(tile, buf):
        for k in range(cpt):
            chunk_copy(tile, k, buf).start(priority=k % 2)

    def drain(tile, buf):
        del tile
        pltpu.make_async_copy(ys_ref.at[pl.ds(0, cpt)], ybuf_ref.at[buf], sem_ref.at[buf]).wait()

    @pl.when(i == 0)
    def _():
        gather(0, 0)

    nxt = jnp.minimum(i + 1, n_tiles - 1)
    gather(nxt, 1 - slot)
    drain(i, slot)

    info = info_ref[...]
    pos1 = info[:, 0:1].astype(jnp.int32)
    pos2 = info[:, 1:2].astype(jnp.int32)
    w1 = info[:, 2:3]
    w2 = info[:, 3:4]
    rowid = lax.broadcasted_iota(jnp.int32, (ts, cap), 1)
    ys = ybuf_ref[slot].reshape(cap, ybuf_ref.shape[-1])
    pw = jnp.where(rowid == pos1, w1, jnp.where(rowid == pos2, w2, 0.0)).astype(BF16)
    y = jnp.dot(pw, ys, preferred_element_type=F32)
    o_ref[...] = _rms(x_ref[...] + y, g_ref[...])

    @pl.when(i == n_tiles - 1)
    def _():
        drain(nxt, 1 - slot)


def _combine_call(chunk_dst, x1, info, g_final, ys_sorted, cap):
    t, d = x1.shape
    ts = TOKEN_TILE
    return pl.pallas_call(
        _combine_kernel,
        grid_spec=pltpu.PrefetchScalarGridSpec(
            num_scalar_prefetch=1,
            grid=(t // ts,),
            in_specs=[pl.BlockSpec((ts, d), lambda i, dst: (i, 0)),
                      pl.BlockSpec((ts, LANES), lambda i, dst: (i, 0)),
                      pl.BlockSpec((1, d), lambda i, dst: (0, 0)),
                      pl.BlockSpec(memory_space=pl.ANY)],
            out_specs=pl.BlockSpec((ts, d), lambda i, dst: (i, 0)),
            scratch_shapes=[pltpu.VMEM((2, cap // CHUNK_ROWS, CHUNK_ROWS, d), BF16),
                            pltpu.SemaphoreType.DMA((2,))]),
        out_shape=jax.ShapeDtypeStruct((t, d), F32),
        compiler_params=pltpu.CompilerParams(dimension_semantics=("arbitrary",),
                                             vmem_limit_bytes=VMEM_LIMIT),
        name="combine",
    )(chunk_dst, x1, info, g_final, ys_sorted)


def _run_tables(nch, cpt, n_etiles):
    n_tiles, n_exp = nch.shape
    per = CHUNKS_PER_ETILE
    tcum = jnp.cumsum(nch, axis=1)
    toff = tcum - nch
    ecum = jnp.cumsum(nch, axis=0)
    etot = ecum[-1]
    eseg = -(-etot // per) * per
    segcum = jnp.cumsum(eseg)
    ebase = segcum - eseg
    eoff = ebase[None, :] + ecum - nch
    n_used = (segcum[-1] // per).reshape(1)

    k = jnp.arange(cpt, dtype=jnp.int32)[None, :, None]
    in_run = (k >= toff[:, None, :]) & (k < tcum[:, None, :])
    chunk_dst = jnp.sum(jnp.where(in_run, (eoff - toff)[:, None, :] + k, 0), axis=-1)

    first = jnp.arange(n_etiles, dtype=jnp.int32)[:, None] * per
    owner = (first >= ebase[None, :]) & (first < segcum[None, :])
    tile_expert = jnp.sum(jnp.where(owner, jnp.arange(n_exp, dtype=jnp.int32)[None, :], 0), axis=-1)

    pick = lambda a: jnp.sum(jnp.where(owner[:, None, :], a[None, :, :], 0), axis=-1)
    run_lo, run_n = pick(eoff), pick(nch)
    shift = pick(jnp.arange(n_tiles, dtype=jnp.int32)[:, None] * cpt + toff - eoff)
    c = (first + jnp.arange(per, dtype=jnp.int32)[None, :])[:, :, None]
    hit = (c >= run_lo[:, None, :]) & (c < (run_lo + run_n)[:, None, :])
    chunk_src = jnp.sum(jnp.where(hit, shift[:, None, :] + c, 0), axis=-1)
    i32 = lambda a: a.reshape(-1).astype(jnp.int32)
    return i32(chunk_dst), i32(tile_expert), i32(n_used), i32(chunk_src)


def _layer(x, mem, g_mix, w_in, conv_dw, conv_dw_bias, conv_ln_g, conv_ln_b, w_conv_out,
           lam_re, lam_im, log_dt, b_re, b_im, c_re, c_im, ssm_d, w_ssm_glu, g_mem, w_mem_kv,
           w_mem_out, w_out, g_ffn, w_rg, b_rg, w_re, b_re_, w_eg, w_eu, w_ed, g_final):
    b, s, d = x.shape
    t = b * s
    cw = conv_dw.shape[1]
    sw = ssm_d.shape[0]
    qw = w_mem_out.shape[0]
    n_groups = sw // SSM_GROUP
    row = lambda a: a.reshape(1, -1)

    o0, o1, o2 = 2 * cw, 2 * cw + sw, 2 * cw + sw + qw
    wc = w_in[:, :o0].astype(BF16)
    w_ssm_t = w_in[:, o0:o1].T.astype(BF16)
    wq = w_in[:, o1:o2].astype(BF16)
    wg = w_in[:, o2:].astype(BF16)

    kmem, vmem = _kv_call(mem, row(g_mem), w_mem_kv.astype(BF16))

    ut = _ut_call(x.reshape(t, d), row(g_mix), w_ssm_t)
    kt, wz, vy, acoef = _ssm_tables(lam_re, lam_im, log_dt, b_re, b_im, c_re, c_im, ssm_d)
    yt = _ssm_call(ut, kt, wz, vy, acoef, b)

    dw = jnp.concatenate([conv_dw, jnp.zeros((CONV_HALO - CONV_KERNEL, cw), F32)], axis=0)
    x1 = _mix_call(x, yt, kmem, vmem, row(g_mix), wc, wq, wg, dw, row(conv_dw_bias),
                   row(conv_ln_g), row(conv_ln_b), w_conv_out.astype(BF16),
                   w_ssm_glu.astype(BF16), w_mem_out.astype(BF16), w_out.astype(BF16))
    x1 = x1.reshape(t, d)

    pad = LANES - N_EXPERTS - MOE_GROUPS
    wr = jnp.concatenate([w_re, w_rg, jnp.zeros((d, pad), F32)], axis=1)
    wr_hi = wr.astype(BF16)
    wr = jnp.concatenate([wr_hi, (wr - wr_hi.astype(F32)).astype(BF16)], axis=1)
    br = jnp.concatenate([b_re_, b_rg, jnp.zeros((pad,), F32)]).reshape(1, LANES)
    n_tiles = t // TOKEN_TILE
    max_tile_chunks = (2 * TOKEN_TILE + N_EXPERTS * (CHUNK_ROWS - 1)) // CHUNK_ROWS
    cpt = -(-max_tile_chunks // 8) * 8
    cap = cpt * CHUNK_ROWS
    xs_tiles, info, nch_f = _route_call(x1, row(g_ffn), wr, br, cap)

    max_chunks = n_tiles * max_tile_chunks + N_EXPERTS * (CHUNKS_PER_ETILE - 1)
    n_etiles = -(-max_chunks // CHUNKS_PER_ETILE)
    nch = nch_f[:, 0, :N_EXPERTS].astype(jnp.int32)
    chunk_dst, tile_expert, n_used, chunk_src = _run_tables(nch, cpt, n_etiles)

    ys_sorted = _expert_call(tile_expert, n_used, chunk_src, xs_tiles, w_eg, w_eu, w_ed)
    out = _combine_call(chunk_dst, x1, info, row(g_final), ys_sorted, cap)
    return out.reshape(b, s, d)


def kernel(x, mem, g_mix, w_in, conv_dw, conv_dw_bias, conv_ln_g, conv_ln_b, w_conv_out, ssm_lambda_re, ssm_lambda_im, ssm_log_dt, ssm_b_re, ssm_b_im, ssm_c_re, ssm_c_im, ssm_d, w_ssm_glu, g_mem, w_mem_kv, w_mem_out, w_out, g_ffn, w_router_group, b_router_group, w_router_expert, b_router_expert, w_exp_gate, w_exp_up, w_exp_down, g_final):
    assert g_mix.shape[0] == 1, "the problem fixes one layer"
    l = 0
    return _layer(
        x, mem, g_mix[l], w_in[l], conv_dw[l], conv_dw_bias[l], conv_ln_g[l], conv_ln_b[l],
        w_conv_out[l], ssm_lambda_re[l], ssm_lambda_im[l], ssm_log_dt[l], ssm_b_re[l],
        ssm_b_im[l], ssm_c_re[l], ssm_c_im[l], ssm_d[l], w_ssm_glu[l], g_mem[l], w_mem_kv[l],
        w_mem_out[l], w_out[l], g_ffn[l], w_router_group[l], b_router_group[l],
        w_router_expert[l], b_router_expert[l], w_exp_gate[l], w_exp_up[l], w_exp_down[l], g_final)
```

```python
import functools

import jax
import jax.numpy as jnp
from jax import lax
from jax.experimental import pallas as pl
from jax.experimental.pallas import tpu as pltpu

F32 = jnp.float32
BF16 = jnp.bfloat16
EPS = 1e-6

LANES = 128
CHUNK_ROWS = 16
SSM_GROUP = 16
SSM_STATE = 64
SSM_BLOCK = LANES
CONV_KERNEL = 31
CONV_HALO = 32
HEADS = 4
HEAD_DIM = 128
MOE_GROUPS = 4
EXPERTS_PER_GROUP = 8
N_EXPERTS = MOE_GROUPS * EXPERTS_PER_GROUP
TOKEN_TILE = 512
EXPERT_TILE = 512
CHUNKS_PER_ETILE = EXPERT_TILE // CHUNK_ROWS
VMEM_LIMIT = 56 * 1024 * 1024
MIX_VMEM_LIMIT = 60 * 1024 * 1024
PROJ_BLOCK = 512
CONV_ROWS = 32


def _rms(x, g):
    return x * lax.rsqrt(jnp.mean(x * x, axis=-1, keepdims=True) + EPS) * g


def _sigmoid(x):
    return 0.5 * jnp.tanh(0.5 * x) + 0.5


def _const_spec(shape):
    zeros = (0,) * len(shape)
    return pl.BlockSpec(shape, lambda *_: zeros, pipeline_mode=pl.Buffered(1))


def _kv_kernel(mem_ref, g_ref, w_ref, k_ref, v_ref):
    width = k_ref.shape[-1]
    mn = _rms(mem_ref[0], g_ref[...]).astype(BF16)
    kv = jnp.dot(mn, w_ref[...], preferred_element_type=F32)
    k_ref[0] = kv[:, :width].astype(BF16)
    v_ref[0] = kv[:, width:].astype(BF16)


def _kv_call(mem, g_mem, w_kv):
    b, m, d = mem.shape
    width = w_kv.shape[1] // 2
    return pl.pallas_call(
        _kv_kernel,
        grid=(b,),
        in_specs=[pl.BlockSpec((1, m, d), lambda i: (i, 0, 0)),
                  pl.BlockSpec((1, d), lambda i: (0, 0)),
                  pl.BlockSpec(w_kv.shape, lambda i: (0, 0))],
        out_specs=[pl.BlockSpec((1, m, width), lambda i: (i, 0, 0)),
                   pl.BlockSpec((1, m, width), lambda i: (i, 0, 0))],
        out_shape=[jax.ShapeDtypeStruct((b, m, width), BF16)] * 2,
        name="kv",
    )(mem, g_mem, w_kv)


def _ut_kernel(x_ref, g_ref, wt_ref, o_ref):
    h = _rms(x_ref[...], g_ref[...]).astype(BF16)
    ut = lax.dot_general(wt_ref[...], h, (((1,), (1,)), ((), ())), preferred_element_type=F32)
    ut = ut.reshape(ut.shape[0], ut.shape[1] // SSM_BLOCK, SSM_BLOCK)
    o_ref[...] = ut.reshape(o_ref.shape).astype(BF16)


def _ut_call(x2, g_mix, w_ssm_t):
    t, d = x2.shape
    c = w_ssm_t.shape[0]
    ts = 8 * SSM_BLOCK
    n_groups = c // SSM_GROUP
    return pl.pallas_call(
        _ut_kernel,
        grid=(t // ts,),
        in_specs=[pl.BlockSpec((ts, d), lambda i: (i, 0)),
                  pl.BlockSpec((1, d), lambda i: (0, 0)),
                  pl.BlockSpec((c, d), lambda i: (0, 0))],
        out_specs=pl.BlockSpec((n_groups, SSM_GROUP, ts // SSM_BLOCK, SSM_BLOCK),
                               lambda i: (0, 0, i, 0)),
        out_shape=jax.ShapeDtypeStruct((n_groups, SSM_GROUP, t // SSM_BLOCK, SSM_BLOCK), BF16),
        name="ut",
    )(x2, g_mix, w_ssm_t)


def _ssm_kernel(u_ref, kt_ref, w_ref, v_ref, a_ref, y_ref, acc_ref, z_ref, zs_ref, s_ref, slab_ref,
                *, n_batch):
    rows = u_ref.shape[2]
    n_blocks = rows // n_batch
    ri = lax.broadcasted_iota(jnp.int32, (SSM_BLOCK, SSM_BLOCK), 0)
    ci = lax.broadcasted_iota(jnp.int32, (SSM_BLOCK, SSM_BLOCK), 1)
    causal = ci >= ri

    n_pairs = SSM_GROUP // 2

    def build(cp, slot):
        for half in range(2):
            kt = kt_ref[0, 2 * cp + half]
            for c in range(SSM_GROUP):
                xb = jnp.broadcast_to(kt[c:c + 1, :], (SSM_BLOCK, SSM_BLOCK))
                toe = pltpu.roll(xb, 0, 1, stride=1, stride_axis=0)
                slab_ref[slot, half * SSM_BLOCK:(half + 1) * SSM_BLOCK, c * SSM_BLOCK:(c + 1) * SSM_BLOCK] = (
                    jnp.where(causal, toe, 0.0).astype(BF16))

    def apply(cp, slot, first):
        x2 = jnp.concatenate([u_ref[0, 2 * cp], u_ref[0, 2 * cp + 1]], axis=1)
        part = jnp.dot(x2, slab_ref[slot], preferred_element_type=F32)
        wrow = pl.multiple_of(cp * 2 * SSM_BLOCK, 2 * SSM_BLOCK)
        zpart = jnp.dot(x2, w_ref[0, pl.ds(wrow, 2 * SSM_BLOCK), :], preferred_element_type=F32)
        if first:
            acc_ref[...] = part
            z_ref[...] = zpart
        else:
            acc_ref[...] += part
            z_ref[...] += zpart

    build(0, 0)
    build(1, 1)
    apply(0, 0, True)

    def two(it, carry):
        cp = 2 * it + 1
        build(cp + 1, 0)
        apply(cp, 1, False)
        build(cp + 2, 1)
        apply(cp + 1, 0, False)
        return carry

    lax.fori_loop(0, (n_pairs - 2) // 2, two, 0)
    apply(n_pairs - 1, 1, False)

    a_full = a_ref[0, 0:1, :]
    a_swap = a_ref[0, 1:2, :]
    zs_ref[...] = pltpu.roll(z_ref[...], SSM_STATE, 1)
    st = jnp.zeros((n_batch, 2 * SSM_STATE), F32)
    sw = jnp.zeros((n_batch, 2 * SSM_STATE), F32)
    for blk in range(n_blocks):
        sl = pl.ds(blk, n_batch, stride=n_blocks)
        s_ref[sl, :] = st
        st, sw = (a_full * st + a_swap * sw + z_ref[sl, :],
                  a_full * sw - a_swap * st + zs_ref[sl, :])

    sb = s_ref[...].astype(BF16)
    y3 = jnp.stack([acc_ref[:, c * SSM_BLOCK:(c + 1) * SSM_BLOCK]
                    + jnp.dot(sb, v_ref[0, c], preferred_element_type=F32)
                    for c in range(SSM_GROUP)], axis=0)
    y_ref[...] = y3.reshape(y_ref.shape)


def _ssm_call(u4, kt, wz, vy, acoef, n_batch):
    g, c, rows, blk = u4.shape
    width = c * blk
    return pl.pallas_call(
        functools.partial(_ssm_kernel, n_batch=n_batch),
        grid=(g,),
        in_specs=[pl.BlockSpec((1, c, rows, blk), lambda i: (i, 0, 0, 0)),
                  pl.BlockSpec((1, c, c, blk), lambda i: (i, 0, 0, 0)),
                  pl.BlockSpec((1, width, 2 * SSM_STATE), lambda i: (i, 0, 0)),
                  pl.BlockSpec((1, c, 2 * SSM_STATE, blk), lambda i: (i, 0, 0, 0)),
                  pl.BlockSpec((1, 2, 2 * SSM_STATE), lambda i: (i, 0, 0))],
        out_specs=pl.BlockSpec((c, rows * blk), lambda i: (i, 0)),
        out_shape=jax.ShapeDtypeStruct((g * c, rows * blk), F32),
        scratch_shapes=[pltpu.VMEM((rows, width), F32),
                        pltpu.VMEM((rows, 2 * SSM_STATE), F32),
                        pltpu.VMEM((rows, 2 * SSM_STATE), F32),
                        pltpu.VMEM((rows, 2 * SSM_STATE), F32),
                        pltpu.VMEM((2, 2 * blk, width), BF16)],
        name="ssm",
    )(u4, kt, wz, vy, acoef)


def _ssm_tables(lam_re, lam_im, log_dt, b_re, b_im, c_re, c_im, d):
    hi = lax.Precision.HIGHEST
    g = lam_re.shape[0]
    dt = jnp.exp(log_dt)[:, None]
    er, ei = lam_re * dt, lam_im * dt
    cat = jnp.concatenate
    kk = jnp.arange(SSM_BLOCK + 1, dtype=F32)
    kdesc = jnp.arange(SSM_BLOCK - 1, -1, -1, dtype=F32)[None, :, None]
    ppk_m, ppk_a = jnp.exp(er[:, :, None] * kk), ei[:, :, None] * kk
    ppk_r, ppk_i = ppk_m * jnp.cos(ppk_a), ppk_m * jnp.sin(ppk_a)
    prev_m, prev_a = jnp.exp(kdesc * er[:, None, :]), kdesc * ei[:, None, :]
    prr, pir = prev_m * jnp.cos(prev_a), prev_m * jnp.sin(prev_a)
    nr, ni = ppk_r[:, :, 1] - 1.0, ppk_i[:, :, 1]
    den = lam_re * lam_re + lam_im * lam_im
    fr = (nr * lam_re + ni * lam_im) / den
    fi = (ni * lam_re - nr * lam_im) / den
    bbr = jnp.swapaxes(fr[:, :, None] * b_re - fi[:, :, None] * b_im, 1, 2)
    bbi = jnp.swapaxes(fr[:, :, None] * b_im + fi[:, :, None] * b_re, 1, 2)
    cb_r = c_re[:, None] * bbr[:, :, None, :] - c_im[:, None] * bbi[:, :, None, :]
    cb_i = c_re[:, None] * bbi[:, :, None, :] + c_im[:, None] * bbr[:, :, None, :]
    cb = cat([cb_r, -cb_i], axis=-1).reshape(g, SSM_GROUP * SSM_GROUP, 2 * SSM_STATE)
    pk = cat([ppk_r[:, :, :SSM_BLOCK], ppk_i[:, :, :SSM_BLOCK]], axis=1)
    kt = jnp.einsum("gmq,gqk->gmk", cb, pk, precision=hi).reshape(g, SSM_GROUP, SSM_GROUP, SSM_BLOCK)
    dmat = jnp.eye(SSM_GROUP, dtype=F32)[None] * d.reshape(g, 1, SSM_GROUP)
    kt = kt + dmat[..., None] * (jnp.arange(SSM_BLOCK) == 0).astype(F32)
    wz = (cat([prr, prr], -1)[:, None] * cat([bbr, bbi], -1)[:, :, None, :]
          + cat([pir, pir], -1)[:, None] * cat([-bbi, bbr], -1)[:, :, None, :])
    wz = wz.astype(BF16).reshape(g, SSM_GROUP * SSM_BLOCK, 2 * SSM_STATE)
    pt_r, pt_i = ppk_r[:, :, 1:], ppk_i[:, :, 1:]
    vy = (cat([c_re, -c_re], -1)[..., None] * cat([pt_r, pt_i], axis=1)[:, None]
          + cat([-c_im, -c_im], -1)[..., None] * cat([pt_i, pt_r], axis=1)[:, None]).astype(BF16)
    ar, ai = ppk_r[:, :, SSM_BLOCK], ppk_i[:, :, SSM_BLOCK]
    acoef = jnp.stack([cat([ar, ar], -1), cat([-ai, ai], -1)], axis=1)
    return kt, wz, vy, acoef


def _mix_kernel(x_ref, yt_ref, k_ref, v_ref, g_ref, wc_ref, wl_ref, dw_ref, dwb_ref,
                lng_ref, lnb_ref, wpw_ref, wglu_ref, wo_ref, wout_ref, o_ref,
                vext_ref, vsh_ref, acc_ref, h_ref, proj_ref):
    ts = x_ref.shape[1]
    d = x_ref.shape[2]
    cw = dw_ref.shape[1]
    x = x_ref[0]
    h = _rms(x, g_ref[...]).astype(BF16)
    h_ref[...] = h

    ci = jnp.dot(h, wc_ref[...], preferred_element_type=F32)
    v = ci[:, :cw] * _sigmoid(ci[:, cw:])

    @pl.when(pl.program_id(1) == 0)
    def _():
        vext_ref[0:CONV_HALO, :] = jnp.zeros((CONV_HALO, cw), F32)

    vext_ref[CONV_HALO:CONV_HALO + ts, :] = v
    span = ts + CONV_HALO - 8
    for p in range(8):
        vsh_ref[p] = vext_ref[1 + p:1 + p + span, :]
    vext_ref[0:CONV_HALO, :] = vext_ref[ts:ts + CONV_HALO, :]
    acc_ref[...] = jnp.broadcast_to(dwb_ref[...], (ts, cw))

    def phase(p, carry):
        taps = [dw_ref[pl.ds(p + 8 * m, 1), :] for m in range(CONV_HALO // 8)]
        for r0 in range(0, ts, CONV_ROWS):
            acc = acc_ref[r0:r0 + CONV_ROWS, :]
            for m, w in enumerate(taps):
                acc = acc + w * vsh_ref[p, 8 * m + r0:8 * m + r0 + CONV_ROWS, :]
            acc_ref[r0:r0 + CONV_ROWS, :] = acc
        proj_ref[p] = jnp.dot(h_ref[...], wl_ref[p], preferred_element_type=F32).astype(BF16)
        return carry

    lax.fori_loop(0, 8, phase, 0)
    acc = acc_ref[...]
    gate = lambda b: jnp.concatenate([proj_ref[2 * b], proj_ref[2 * b + 1]], axis=1).astype(F32)
    q = proj_ref[6]
    mu = jnp.mean(acc, axis=-1, keepdims=True)
    xc = acc - mu
    var = jnp.mean(xc * xc, axis=-1, keepdims=True)
    ln = xc * lax.rsqrt(var + EPS) * lng_ref[...] + lnb_ref[...]
    sw = ln * _sigmoid(ln)
    y_conv = jnp.dot(sw.astype(BF16), wpw_ref[...], preferred_element_type=F32)
    merged = _sigmoid(gate(0)) * y_conv

    gy = jax.nn.gelu(yt_ref[...]).astype(BF16)
    z = lax.dot_general(gy, wglu_ref[...], (((0,), (0,)), ((), ())), preferred_element_type=F32)
    y_ssm = z[:, :d] * _sigmoid(z[:, d:])
    merged = merged + _sigmoid(gate(1)) * y_ssm

    kk = k_ref[0]
    vv = v_ref[0]
    outs = []
    for hd in range(HEADS):
        sl = slice(hd * HEAD_DIM, (hd + 1) * HEAD_DIM)
        s = lax.dot_general(q[:, sl], kk[:, sl], (((1,), (1,)), ((), ())),
                            preferred_element_type=F32) * (HEAD_DIM ** -0.5)
        p = jnp.exp(s - jnp.max(s, axis=-1, keepdims=True))
        den = jnp.sum(p, axis=-1, keepdims=True)
        o = jnp.dot(p.astype(BF16), vv[:, sl], preferred_element_type=F32) / den
        outs.append(o.astype(BF16))
    y_mem = jnp.dot(jnp.concatenate(outs, axis=1), wo_ref[...], preferred_element_type=F32)
    merged = merged + _sigmoid(gate(2)) * y_mem

    o_ref[0] = x + jnp.dot(merged.astype(BF16), wout_ref[...], preferred_element_type=F32)


def _mix_call(x, yt, kmem, vmem, g_mix, wc, wl, dw, dwb, lng, lnb, wpw, wglu, wo, wout):
    b, s, d = x.shape
    ts = TOKEN_TILE
    nst = s // ts
    cw = dw.shape[1]
    m = kmem.shape[1]
    consts = [g_mix, wc, wl, dw, dwb, lng, lnb, wpw, wglu, wo, wout]
    return pl.pallas_call(
        _mix_kernel,
        grid=(b, nst),
        in_specs=[pl.BlockSpec((1, ts, d), lambda i, j: (i, j, 0)),
                  pl.BlockSpec((yt.shape[0], ts), lambda i, j: (0, i * nst + j)),
                  pl.BlockSpec((1, m, kmem.shape[2]), lambda i, j: (i, 0, 0)),
                  pl.BlockSpec((1, m, vmem.shape[2]), lambda i, j: (i, 0, 0))]
                 + [_const_spec(c.shape) for c in consts],
        out_specs=pl.BlockSpec((1, ts, d), lambda i, j: (i, j, 0)),
        out_shape=jax.ShapeDtypeStruct((b, s, d), F32),
        scratch_shapes=[pltpu.VMEM((ts + CONV_HALO, cw), F32),
                        pltpu.VMEM((8, ts + CONV_HALO - 8, cw), F32),
                        pltpu.VMEM((ts, cw), F32),
                        pltpu.VMEM((ts, d), BF16),
                        pltpu.VMEM(wl.shape[:1] + (ts, wl.shape[2]), BF16)],
        compiler_params=pltpu.CompilerParams(
            dimension_semantics=("arbitrary", "arbitrary"), vmem_limit_bytes=MIX_VMEM_LIMIT),
        name="mix",
    )(x, yt, kmem, vmem, *consts)


def _route_kernel(x_ref, g_ref, wr_ref, br_ref, xs_ref, info_ref, nch_ref):
    ts = x_ref.shape[0]
    cap = xs_ref.shape[0] * CHUNK_ROWS
    h2 = _rms(x_ref[...], g_ref[...])
    hb = h2.astype(BF16)
    hl = (h2 - hb.astype(F32)).astype(BF16)
    hw = jnp.dot(hb, wr_ref[...], preferred_element_type=F32)
    logits = (hw[:, :LANES] + hw[:, LANES:]
              + jnp.dot(hl, wr_ref[:, :LANES], preferred_element_type=F32)) + br_ref[...]
    lt = logits.T
    le = lt[0:N_EXPERTS]
    lg = lt[N_EXPERTS:N_EXPERTS + 8]
    neg = jnp.float32(-1e30)
    big = jnp.float32(1e9)
    g_f = lax.broadcasted_iota(jnp.int32, (8, ts), 0).astype(F32)
    e_f = lax.broadcasted_iota(jnp.int32, (N_EXPERTS, ts), 0).astype(F32)

    gmask = g_f < MOE_GROUPS
    gmax = jnp.max(jnp.where(gmask, lg, neg), axis=0, keepdims=True)
    gidx = jnp.min(jnp.where(gmask & (lg == gmax), g_f, big), axis=0, keepdims=True)
    gsum = jnp.sum(jnp.where(gmask, jnp.exp(jnp.minimum(lg - gmax, 0.0)), 0.0), axis=0, keepdims=True)
    p_top = 1.0 / gsum
    emask = jnp.floor(e_f * (1.0 / EXPERTS_PER_GROUP)) == gidx
    m1 = jnp.max(jnp.where(emask, le, neg), axis=0, keepdims=True)
    i1 = jnp.min(jnp.where(emask & (le == m1), e_f, big), axis=0, keepdims=True)
    emask2 = emask & (e_f != i1)
    m2 = jnp.max(jnp.where(emask2, le, neg), axis=0, keepdims=True)
    i2 = jnp.min(jnp.where(emask2 & (le == m2), e_f, big), axis=0, keepdims=True)
    r = jnp.exp(m2 - m1)
    w1 = p_top / (1.0 + r)
    w2 = p_top * r / (1.0 + r)

    sel1 = e_f == i1
    sel2 = e_f == i2
    occ = jnp.where(sel1 | sel2, 1.0, 0.0)
    tr = lax.broadcasted_iota(jnp.int32, (ts, ts), 0)
    tc = lax.broadcasted_iota(jnp.int32, (ts, ts), 1)
    earlier = jnp.where(tr < tc, 1.0, 0.0).astype(BF16)
    rank = jnp.dot(occ.astype(BF16), earlier, preferred_element_type=F32)
    cnt = jnp.sum(occ, axis=1, keepdims=True)
    nch = jnp.floor((cnt + (CHUNK_ROWS - 1)) * (1.0 / CHUNK_ROWS))
    er = lax.broadcasted_iota(jnp.int32, (N_EXPERTS, N_EXPERTS), 0)
    ec = lax.broadcasted_iota(jnp.int32, (N_EXPERTS, N_EXPERTS), 1)
    lower = jnp.where(ec < er, 1.0, 0.0).astype(BF16)
    nch_b = jnp.broadcast_to(nch, (N_EXPERTS, LANES))
    start = jnp.dot(lower, nch_b.astype(BF16), preferred_element_type=F32)[:, 0:1] * CHUNK_ROWS
    slot = start + rank
    pos1 = jnp.sum(jnp.where(sel1, slot, 0.0), axis=0, keepdims=True)
    pos2 = jnp.sum(jnp.where(sel2, slot, 0.0), axis=0, keepdims=True)

    rowid = lax.broadcasted_iota(jnp.int32, (cap, ts), 0)
    p = jnp.where((rowid == pos1.astype(jnp.int32)) | (rowid == pos2.astype(jnp.int32)), 1.0, 0.0)
    xs = jnp.dot(p.astype(BF16), hb, preferred_element_type=F32).astype(BF16)
    xs_ref[...] = xs.reshape(xs_ref.shape)
    sub = lax.broadcasted_iota(jnp.int32, (LANES, ts), 0)
    info_t = jnp.where(sub == 0, pos1, jnp.where(sub == 1, pos2,
                       jnp.where(sub == 2, w1, jnp.where(sub == 3, w2, 0.0))))
    info_ref[...] = info_t.T
    nch_ref[0] = jnp.concatenate([nch_b, jnp.zeros((LANES - N_EXPERTS, LANES), F32)], axis=0).T[0:8, :]


def _route_call(x1, g_ffn, wr, br, cap):
    t, d = x1.shape
    ts = TOKEN_TILE
    nt = t // ts
    return pl.pallas_call(
        _route_kernel,
        grid=(nt,),
        in_specs=[pl.BlockSpec((ts, d), lambda i: (i, 0)),
                  pl.BlockSpec((1, d), lambda i: (0, 0)),
                  pl.BlockSpec(wr.shape, lambda i: (0, 0)),
                  pl.BlockSpec((1, LANES), lambda i: (0, 0))],
        out_specs=[pl.BlockSpec((cap // CHUNK_ROWS, CHUNK_ROWS, d), lambda i: (i, 0, 0)),
                   pl.BlockSpec((ts, LANES), lambda i: (i, 0)),
                   pl.BlockSpec((1, 8, LANES), lambda i: (i, 0, 0))],
        out_shape=[jax.ShapeDtypeStruct((nt * cap // CHUNK_ROWS, CHUNK_ROWS, d), BF16),
                   jax.ShapeDtypeStruct((t, LANES), F32),
                   jax.ShapeDtypeStruct((nt, 8, LANES), F32)],
        compiler_params=pltpu.CompilerParams(vmem_limit_bytes=VMEM_LIMIT),
        name="route",
    )(x1, g_ffn, wr, br)


def _expert_kernel(te_ref, nu_ref, src_ref, xs_ref, wg_ref, wu_ref, wd_ref, o_ref, xbuf_ref, sem_ref):
    i = pl.program_id(0)
    n_used = nu_ref[0]
    slot = i % 2

    def chunk_copy(tile, c, buf):
        return pltpu.make_async_copy(
            xs_ref.at[src_ref[tile * CHUNKS_PER_ETILE + c]], xbuf_ref.at[buf, c], sem_ref.at[buf])

    def gather(tile, buf):
        for c in range(CHUNKS_PER_ETILE):
            chunk_copy(tile, c, buf).start(priority=c % 2)

    def drain(tile, buf):
        del tile
        pltpu.make_async_copy(xs_ref.at[pl.ds(0, CHUNKS_PER_ETILE)], xbuf_ref.at[buf], sem_ref.at[buf]).wait()

    @pl.when(i == 0)
    def _():
        gather(0, 0)

    @pl.when(i < n_used)
    def _():
        nxt = jnp.minimum(i + 1, n_used - 1)
        gather(nxt, 1 - slot)
        drain(i, slot)
        x = xbuf_ref[slot].reshape(EXPERT_TILE, xbuf_ref.shape[-1])
        gate = jnp.dot(x, wg_ref[0].astype(BF16), preferred_element_type=F32)
        up = jnp.dot(x, wu_ref[0].astype(BF16), preferred_element_type=F32)
        act = (gate * _sigmoid(gate) * up).astype(BF16)
        y = jnp.dot(act, wd_ref[0].astype(BF16), preferred_element_type=F32).astype(BF16)
        o_ref[...] = y.reshape(o_ref.shape)

        @pl.when(i == n_used - 1)
        def _():
            drain(nxt, 1 - slot)

    @pl.when(i >= n_used)
    def _():
        o_ref[...] = jnp.zeros_like(o_ref)


def _expert_call(tile_expert, n_used, chunk_src, xs, wg, wu, wd):
    d = xs.shape[-1]
    de = wg.shape[2]
    tm = EXPERT_TILE
    nt = tile_expert.shape[0]

    def w_map(i, te, nu, src):
        return (te[jnp.maximum(jnp.minimum(i, nu[0] - 1), 0)], 0, 0)

    return pl.pallas_call(
        _expert_kernel,
        grid_spec=pltpu.PrefetchScalarGridSpec(
            num_scalar_prefetch=3,
            grid=(nt,),
            in_specs=[pl.BlockSpec(memory_space=pl.ANY),
                      pl.BlockSpec((1, d, de), w_map),
                      pl.BlockSpec((1, d, de), w_map),
                      pl.BlockSpec((1, de, d), w_map)],
            out_specs=pl.BlockSpec((CHUNKS_PER_ETILE, CHUNK_ROWS, d), lambda i, te, nu, src: (i, 0, 0)),
            scratch_shapes=[pltpu.VMEM((2, CHUNKS_PER_ETILE, CHUNK_ROWS, d), BF16),
                            pltpu.SemaphoreType.DMA((2,))]),
        out_shape=jax.ShapeDtypeStruct((nt * CHUNKS_PER_ETILE, CHUNK_ROWS, d), BF16),
        compiler_params=pltpu.CompilerParams(dimension_semantics=("arbitrary",)),
        name="expert",
    )(tile_expert, n_used, chunk_src, xs, wg, wu, wd)


def _combine_kernel(dst_ref, x_ref, info_ref, g_ref, ys_ref, o_ref, ybuf_ref, sem_ref):
    i = pl.program_id(0)
    n_tiles = pl.num_programs(0)
    ts = x_ref.shape[0]
    cpt = ybuf_ref.shape[1]
    cap = cpt * CHUNK_ROWS
    slot = i % 2

    def chunk_copy(tile, k, buf):
        return pltpu.make_async_copy(ys_ref.at[dst_ref[tile * cpt + k]], ybuf_ref.at[buf, k], sem_ref.at[buf])

    def gather(tile, buf):
        for k in range(cpt):
            chunk_copy(tile, k, buf).start(priority=k % 2)

    def drain(tile, buf):
        del tile
        pltpu.make_async_copy(ys_ref.at[pl.ds(0, cpt)], ybuf_ref.at[buf], sem_ref.at[buf]).wait()

    @pl.when(i == 0)
    def _():
        gather(0, 0)

    nxt = jnp.minimum(i + 1, n_tiles - 1)
    gather(nxt, 1 - slot)
    drain(i, slot)

    info = info_ref[...]
    pos1 = info[:, 0:1].astype(jnp.int32)
    pos2 = info[:, 1:2].astype(jnp.int32)
    w1 = info[:, 2:3]
    w2 = info[:, 3:4]
    rowid = lax.broadcasted_iota(jnp.int32, (ts, cap), 1)
    ys = ybuf_ref[slot].reshape(cap, ybuf_ref.shape[-1])
    pw = jnp.where(rowid == pos1, w1, jnp.where(rowid == pos2, w2, 0.0)).astype(BF16)
    y = jnp.dot(pw, ys, preferred_element_type=F32)
    o_ref[...] = _rms(x_ref[...] + y, g_ref[...])

    @pl.when(i == n_tiles - 1)
    def _():
        drain(nxt, 1 - slot)


def _combine_call(chunk_dst, x1, info, g_final, ys_sorted, cap):
    t, d = x1.shape
    ts = TOKEN_TILE
    return pl.pallas_call(
        _combine_kernel,
        grid_spec=pltpu.PrefetchScalarGridSpec(
            num_scalar_prefetch=1,
            grid=(t // ts,),
            in_specs=[pl.BlockSpec((ts, d), lambda i, dst: (i, 0)),
                      pl.BlockSpec((ts, LANES), lambda i, dst: (i, 0)),
                      pl.BlockSpec((1, d), lambda i, dst: (0, 0)),
                      pl.BlockSpec(memory_space=pl.ANY)],
            out_specs=pl.BlockSpec((ts, d), lambda i, dst: (i, 0)),
            scratch_shapes=[pltpu.VMEM((2, cap // CHUNK_ROWS, CHUNK_ROWS, d), BF16),
                            pltpu.SemaphoreType.DMA((2,))]),
        out_shape=jax.ShapeDtypeStruct((t, d), F32),
        compiler_params=pltpu.CompilerParams(dimension_semantics=("arbitrary",),
                                             vmem_limit_bytes=VMEM_LIMIT),
        name="combine",
    )(chunk_dst, x1, info, g_final, ys_sorted)


def _run_tables(nch, cpt, n_etiles):
    n_tiles, n_exp = nch.shape
    per = CHUNKS_PER_ETILE
    tcum = jnp.cumsum(nch, axis=1)
    toff = tcum - nch
    ecum = jnp.cumsum(nch, axis=0)
    etot = ecum[-1]
    eseg = -(-etot // per) * per
    segcum = jnp.cumsum(eseg)
    ebase = segcum - eseg
    eoff = ebase[None, :] + ecum - nch
    n_used = (segcum[-1] // per).reshape(1)

    k = jnp.arange(cpt, dtype=jnp.int32)[None, :, None]
    in_run = (k >= toff[:, None, :]) & (k < tcum[:, None, :])
    chunk_dst = jnp.sum(jnp.where(in_run, (eoff - toff)[:, None, :] + k, 0), axis=-1)

    first = jnp.arange(n_etiles, dtype=jnp.int32)[:, None] * per
    owner = (first >= ebase[None, :]) & (first < segcum[None, :])
    tile_expert = jnp.sum(jnp.where(owner, jnp.arange(n_exp, dtype=jnp.int32)[None, :], 0), axis=-1)

    pick = lambda a: jnp.sum(jnp.where(owner[:, None, :], a[None, :, :], 0), axis=-1)
    run_lo, run_n = pick(eoff), pick(nch)
    shift = pick(jnp.arange(n_tiles, dtype=jnp.int32)[:, None] * cpt + toff - eoff)
    c = (first + jnp.arange(per, dtype=jnp.int32)[None, :])[:, :, None]
    hit = (c >= run_lo[:, None, :]) & (c < (run_lo + run_n)[:, None, :])
    chunk_src = jnp.sum(jnp.where(hit, shift[:, None, :] + c, 0), axis=-1)
    i32 = lambda a: a.reshape(-1).astype(jnp.int32)
    return i32(chunk_dst), i32(tile_expert), i32(n_used), i32(chunk_src)


def _layer(x, mem, g_mix, w_in, conv_dw, conv_dw_bias, conv_ln_g, conv_ln_b, w_conv_out,
           lam_re, lam_im, log_dt, b_re, b_im, c_re, c_im, ssm_d, w_ssm_glu, g_mem, w_mem_kv,
           w_mem_out, w_out, g_ffn, w_rg, b_rg, w_re, b_re_, w_eg, w_eu, w_ed, g_final):
    b, s, d = x.shape
    t = b * s
    cw = conv_dw.shape[1]
    sw = ssm_d.shape[0]
    qw = w_mem_out.shape[0]
    n_groups = sw // SSM_GROUP
    row = lambda a: a.reshape(1, -1)

    o0, o1, o2 = 2 * cw, 2 * cw + sw, 2 * cw + sw + qw
    wc = w_in[:, :o0].astype(BF16)
    w_ssm_t = w_in[:, o0:o1].T.astype(BF16)
    wl = jnp.concatenate([w_in[:, o2:], w_in[:, o1:o2], jnp.zeros((d, 8 * PROJ_BLOCK - 3 * d - qw), F32)], axis=1)
    wl = wl.astype(BF16).reshape(d, 8, PROJ_BLOCK).transpose(1, 0, 2)

    kmem, vmem = _kv_call(mem, row(g_mem), w_mem_kv.astype(BF16))

    ut = _ut_call(x.reshape(t, d), row(g_mix), w_ssm_t)
    kt, wz, vy, acoef = _ssm_tables(lam_re, lam_im, log_dt, b_re, b_im, c_re, c_im, ssm_d)
    yt = _ssm_call(ut, kt, wz, vy, acoef, b)

    dw = jnp.concatenate([jnp.zeros((CONV_HALO - CONV_KERNEL, cw), F32), conv_dw], axis=0)
    x1 = _mix_call(x, yt, kmem, vmem, row(g_mix), wc, wl, dw, row(conv_dw_bias),
                   row(conv_ln_g), row(conv_ln_b), w_conv_out.astype(BF16),
                   w_ssm_glu.astype(BF16), w_mem_out.astype(BF16), w_out.astype(BF16))
    x1 = x1.reshape(t, d)

    pad = LANES - N_EXPERTS - MOE_GROUPS
    wr = jnp.concatenate([w_re, w_rg, jnp.zeros((d, pad), F32)], axis=1)
    wr_hi = wr.astype(BF16)
    wr = jnp.concatenate([wr_hi, (wr - wr_hi.astype(F32)).astype(BF16)], axis=1)
    br = jnp.concatenate([b_re_, b_rg, jnp.zeros((pad,), F32)]).reshape(1, LANES)
    n_tiles = t // TOKEN_TILE
    max_tile_chunks = (2 * TOKEN_TILE + N_EXPERTS * (CHUNK_ROWS - 1)) // CHUNK_ROWS
    cpt = -(-max_tile_chunks // 8) * 8
    cap = cpt * CHUNK_ROWS
    xs_tiles, info, nch_f = _route_call(x1, row(g_ffn), wr, br, cap)

    max_chunks = n_tiles * max_tile_chunks + N_EXPERTS * (CHUNKS_PER_ETILE - 1)
    n_etiles = -(-max_chunks // CHUNKS_PER_ETILE)
    nch = nch_f[:, 0, :N_EXPERTS].astype(jnp.int32)
    chunk_dst, tile_expert, n_used, chunk_src = _run_tables(nch, cpt, n_etiles)

    ys_sorted = _expert_call(tile_expert, n_used, chunk_src, xs_tiles, w_eg, w_eu, w_ed)
    out = _combine_call(chunk_dst, x1, info, row(g_final), ys_sorted, cap)
    return out.reshape(b, s, d)


def kernel(x, mem, g_mix, w_in, conv_dw, conv_dw_bias, conv_ln_g, conv_ln_b, w_conv_out, ssm_lambda_re, ssm_lambda_im, ssm_log_dt, ssm_b_re, ssm_b_im, ssm_c_re, ssm_c_im, ssm_d, w_ssm_glu, g_mem, w_mem_kv, w_mem_out, w_out, g_ffn, w_router_group, b_router_group, w_router_expert, b_router_expert, w_exp_gate, w_exp_up, w_exp_down, g_final):
    assert g_mix.shape[0] == 1, "the problem fixes one layer"
    l = 0
    return _layer(
        x, mem, g_mix[l], w_in[l], conv_dw[l], conv_dw_bias[l], conv_ln_g[l], conv_ln_b[l],
        w_conv_out[l], ssm_lambda_re[l], ssm_lambda_im[l], ssm_log_dt[l], ssm_b_re[l],
        ssm_b_im[l], ssm_c_re[l], ssm_c_im[l], ssm_d[l], w_ssm_glu[l], g_mem[l], w_mem_kv[l],
        w_mem_out[l], w_out[l], g_ffn[l], w_router_group[l], b_router_group[l],
        w_router_expert[l], b_router_expert[l], w_exp_gate[l], w_exp_up[l], w_exp_down[l], g_final)
```

```python
import functools

import jax
import jax.numpy as jnp
from jax import lax
from jax.experimental import pallas as pl
from jax.experimental.pallas import tpu as pltpu

F32 = jnp.float32
BF16 = jnp.bfloat16
EPS = 1e-6

LANES = 128
CHUNK_ROWS = 16
SSM_GROUP = 16
SSM_STATE = 64
SSM_BLOCK = LANES
CONV_KERNEL = 31
CONV_HALO = 32
HEADS = 4
HEAD_DIM = 128
MOE_GROUPS = 4
EXPERTS_PER_GROUP = 8
N_EXPERTS = MOE_GROUPS * EXPERTS_PER_GROUP
TOKEN_TILE = 512
MIX_TILE = 1024
EXPERT_TILE = 512
CHUNKS_PER_ETILE = EXPERT_TILE // CHUNK_ROWS
VMEM_LIMIT = 56 * 1024 * 1024


def _rms(x, g):
    return x * lax.rsqrt(jnp.mean(x * x, axis=-1, keepdims=True) + EPS) * g


def _sigmoid(x):
    return 0.5 * jnp.tanh(0.5 * x) + 0.5


def _const_spec(shape):
    zeros = (0,) * len(shape)
    return pl.BlockSpec(shape, lambda *_: zeros, pipeline_mode=pl.Buffered(1))


def _kv_kernel(mem_ref, g_ref, w_ref, k_ref, v_ref):
    width = k_ref.shape[-1]
    mn = _rms(mem_ref[0], g_ref[...]).astype(BF16)
    kv = jnp.dot(mn, w_ref[...], preferred_element_type=F32)
    k_ref[0] = kv[:, :width].astype(BF16)
    v_ref[0] = kv[:, width:].astype(BF16)


def _kv_call(mem, g_mem, w_kv):
    b, m, d = mem.shape
    width = w_kv.shape[1] // 2
    return pl.pallas_call(
        _kv_kernel,
        grid=(b,),
        in_specs=[pl.BlockSpec((1, m, d), lambda i: (i, 0, 0)),
                  pl.BlockSpec((1, d), lambda i: (0, 0)),
                  pl.BlockSpec(w_kv.shape, lambda i: (0, 0))],
        out_specs=[pl.BlockSpec((1, m, width), lambda i: (i, 0, 0)),
                   pl.BlockSpec((1, m, width), lambda i: (i, 0, 0))],
        out_shape=[jax.ShapeDtypeStruct((b, m, width), BF16)] * 2,
        name="kv",
    )(mem, g_mem, w_kv)


def _ut_kernel(x_ref, g_ref, wt_ref, o_ref):
    h = _rms(x_ref[...], g_ref[...]).astype(BF16)
    ut = lax.dot_general(wt_ref[...], h, (((1,), (1,)), ((), ())), preferred_element_type=F32)
    ut = ut.reshape(ut.shape[0], ut.shape[1] // SSM_BLOCK, SSM_BLOCK)
    o_ref[...] = ut.reshape(o_ref.shape).astype(BF16)


def _ut_call(x2, g_mix, w_ssm_t):
    t, d = x2.shape
    c = w_ssm_t.shape[0]
    ts = 8 * SSM_BLOCK
    n_groups = c // SSM_GROUP
    return pl.pallas_call(
        _ut_kernel,
        grid=(t // ts,),
        in_specs=[pl.BlockSpec((ts, d), lambda i: (i, 0)),
                  pl.BlockSpec((1, d), lambda i: (0, 0)),
                  pl.BlockSpec((c, d), lambda i: (0, 0))],
        out_specs=pl.BlockSpec((n_groups, SSM_GROUP, ts // SSM_BLOCK, SSM_BLOCK),
                               lambda i: (0, 0, i, 0)),
        out_shape=jax.ShapeDtypeStruct((n_groups, SSM_GROUP, t // SSM_BLOCK, SSM_BLOCK), BF16),
        name="ut",
    )(x2, g_mix, w_ssm_t)


def _ssm_kernel(u_ref, kt_ref, w_ref, v_ref, a_ref, y_ref, acc_ref, z_ref, zs_ref, s_ref, slab_ref,
                *, n_batch):
    rows = u_ref.shape[2]
    n_blocks = rows // n_batch
    ri = lax.broadcasted_iota(jnp.int32, (SSM_BLOCK, SSM_BLOCK), 0)
    ci = lax.broadcasted_iota(jnp.int32, (SSM_BLOCK, SSM_BLOCK), 1)
    causal = ci >= ri

    n_pairs = SSM_GROUP // 2

    def build(cp, slot):
        for half in range(2):
            kt = kt_ref[0, 2 * cp + half]
            for c in range(SSM_GROUP):
                xb = jnp.broadcast_to(kt[c:c + 1, :], (SSM_BLOCK, SSM_BLOCK))
                toe = pltpu.roll(xb, 0, 1, stride=1, stride_axis=0)
                slab_ref[slot, half * SSM_BLOCK:(half + 1) * SSM_BLOCK, c * SSM_BLOCK:(c + 1) * SSM_BLOCK] = (
                    jnp.where(causal, toe, 0.0).astype(BF16))

    def apply(cp, slot, first):
        x2 = jnp.concatenate([u_ref[0, 2 * cp], u_ref[0, 2 * cp + 1]], axis=1)
        part = jnp.dot(x2, slab_ref[slot], preferred_element_type=F32)
        wrow = pl.multiple_of(cp * 2 * SSM_BLOCK, 2 * SSM_BLOCK)
        zpart = jnp.dot(x2, w_ref[0, pl.ds(wrow, 2 * SSM_BLOCK), :], preferred_element_type=F32)
        if first:
            acc_ref[...] = part
            z_ref[...] = zpart
        else:
            acc_ref[...] += part
            z_ref[...] += zpart

    build(0, 0)
    build(1, 1)
    apply(0, 0, True)

    def two(it, carry):
        cp = 2 * it + 1
        build(cp + 1, 0)
        apply(cp, 1, False)
        build(cp + 2, 1)
        apply(cp + 1, 0, False)
        return carry

    lax.fori_loop(0, (n_pairs - 2) // 2, two, 0)
    apply(n_pairs - 1, 1, False)

    a_full = a_ref[0, 0:1, :]
    a_swap = a_ref[0, 1:2, :]
    zs_ref[...] = pltpu.roll(z_ref[...], SSM_STATE, 1)
    st = jnp.zeros((n_batch, 2 * SSM_STATE), F32)
    sw = jnp.zeros((n_batch, 2 * SSM_STATE), F32)
    for blk in range(n_blocks):
        sl = pl.ds(blk, n_batch, stride=n_blocks)
        s_ref[sl, :] = st
        st, sw = (a_full * st + a_swap * sw + z_ref[sl, :],
                  a_full * sw - a_swap * st + zs_ref[sl, :])

    sb = s_ref[...].astype(BF16)
    y3 = jnp.stack([acc_ref[:, c * SSM_BLOCK:(c + 1) * SSM_BLOCK]
                    + jnp.dot(sb, v_ref[0, c], preferred_element_type=F32)
                    for c in range(SSM_GROUP)], axis=0)
    y_ref[...] = y3.reshape(y_ref.shape)


def _ssm_call(u4, kt, wz, vy, acoef, n_batch):
    g, c, rows, blk = u4.shape
    width = c * blk
    return pl.pallas_call(
        functools.partial(_ssm_kernel, n_batch=n_batch),
        grid=(g,),
        in_specs=[pl.BlockSpec((1, c, rows, blk), lambda i: (i, 0, 0, 0)),
                  pl.BlockSpec((1, c, c, blk), lambda i: (i, 0, 0, 0)),
                  pl.BlockSpec((1, width, 2 * SSM_STATE), lambda i: (i, 0, 0)),
                  pl.BlockSpec((1, c, 2 * SSM_STATE, blk), lambda i: (i, 0, 0, 0)),
                  pl.BlockSpec((1, 2, 2 * SSM_STATE), lambda i: (i, 0, 0))],
        out_specs=pl.BlockSpec((c, rows * blk), lambda i: (i, 0)),
        out_shape=jax.ShapeDtypeStruct((g * c, rows * blk), F32),
        scratch_shapes=[pltpu.VMEM((rows, width), F32),
                        pltpu.VMEM((rows, 2 * SSM_STATE), F32),
                        pltpu.VMEM((rows, 2 * SSM_STATE), F32),
                        pltpu.VMEM((rows, 2 * SSM_STATE), F32),
                        pltpu.VMEM((2, 2 * blk, width), BF16)],
        name="ssm",
    )(u4, kt, wz, vy, acoef)


def _ssm_tables(lam_re, lam_im, log_dt, b_re, b_im, c_re, c_im, d):
    hi = lax.Precision.HIGHEST
    g = lam_re.shape[0]
    dt = jnp.exp(log_dt)[:, None]
    er, ei = lam_re * dt, lam_im * dt
    cat = jnp.concatenate
    kk = jnp.arange(SSM_BLOCK + 1, dtype=F32)
    kdesc = jnp.arange(SSM_BLOCK - 1, -1, -1, dtype=F32)[None, :, None]
    ppk_m, ppk_a = jnp.exp(er[:, :, None] * kk), ei[:, :, None] * kk
    ppk_r, ppk_i = ppk_m * jnp.cos(ppk_a), ppk_m * jnp.sin(ppk_a)
    prev_m, prev_a = jnp.exp(kdesc * er[:, None, :]), kdesc * ei[:, None, :]
    prr, pir = prev_m * jnp.cos(prev_a), prev_m * jnp.sin(prev_a)
    nr, ni = ppk_r[:, :, 1] - 1.0, ppk_i[:, :, 1]
    den = lam_re * lam_re + lam_im * lam_im
    fr = (nr * lam_re + ni * lam_im) / den
    fi = (ni * lam_re - nr * lam_im) / den
    bbr = jnp.swapaxes(fr[:, :, None] * b_re - fi[:, :, None] * b_im, 1, 2)
    bbi = jnp.swapaxes(fr[:, :, None] * b_im + fi[:, :, None] * b_re, 1, 2)
    cb_r = c_re[:, None] * bbr[:, :, None, :] - c_im[:, None] * bbi[:, :, None, :]
    cb_i = c_re[:, None] * bbi[:, :, None, :] + c_im[:, None] * bbr[:, :, None, :]
    cb = cat([cb_r, -cb_i], axis=-1).reshape(g, SSM_GROUP * SSM_GROUP, 2 * SSM_STATE)
    pk = cat([ppk_r[:, :, :SSM_BLOCK], ppk_i[:, :, :SSM_BLOCK]], axis=1)
    kt = jnp.einsum("gmq,gqk->gmk", cb, pk, precision=hi).reshape(g, SSM_GROUP, SSM_GROUP, SSM_BLOCK)
    dmat = jnp.eye(SSM_GROUP, dtype=F32)[None] * d.reshape(g, 1, SSM_GROUP)
    kt = kt + dmat[..., None] * (jnp.arange(SSM_BLOCK) == 0).astype(F32)
    wz = (cat([prr, prr], -1)[:, None] * cat([bbr, bbi], -1)[:, :, None, :]
          + cat([pir, pir], -1)[:, None] * cat([-bbi, bbr], -1)[:, :, None, :])
    wz = wz.astype(BF16).reshape(g, SSM_GROUP * SSM_BLOCK, 2 * SSM_STATE)
    pt_r, pt_i = ppk_r[:, :, 1:], ppk_i[:, :, 1:]
    vy = (cat([c_re, -c_re], -1)[..., None] * cat([pt_r, pt_i], axis=1)[:, None]
          + cat([-c_im, -c_im], -1)[..., None] * cat([pt_i, pt_r], axis=1)[:, None]).astype(BF16)
    ar, ai = ppk_r[:, :, SSM_BLOCK], ppk_i[:, :, SSM_BLOCK]
    acoef = jnp.stack([cat([ar, ar], -1), cat([-ai, ai], -1)], axis=1)
    return kt, wz, vy, acoef


def _mix_kernel(x_ref, yt_ref, k_ref, v_ref, g_ref, wc_ref, wq_ref, wg_ref, dw_ref, dwb_ref,
                lng_ref, lnb_ref, wpw_ref, wglu_ref, wo_ref, wout_ref, o_ref, vext_ref, vsh_ref):
    ts = x_ref.shape[1]
    d = x_ref.shape[2]
    cw = dw_ref.shape[1]
    x = x_ref[0]
    h = _rms(x, g_ref[...]).astype(BF16)

    ci = jnp.dot(h, wc_ref[...], preferred_element_type=F32)
    v = ci[:, :cw] * _sigmoid(ci[:, cw:])
    q = jnp.dot(h, wq_ref[...], preferred_element_type=F32)

    @pl.when(pl.program_id(1) == 0)
    def _():
        vext_ref[0:CONV_HALO, :] = jnp.zeros((CONV_HALO, cw), F32)

    vext_ref[CONV_HALO:CONV_HALO + ts, :] = v
    acc = jnp.broadcast_to(dwb_ref[...], (ts, cw))
    first = CONV_HALO - (CONV_KERNEL - 1)
    for phase in range(8):
        offs = [first + k for k in range(CONV_KERNEL) if (first + k) % 8 == phase]
        if not offs:
            continue
        span = offs[-1] - offs[0] + ts
        vsh_ref[0:span, :] = vext_ref[offs[0]:offs[0] + span, :]
        for off in offs:
            acc = acc + dw_ref[off - first:off - first + 1, :] * vsh_ref[off - offs[0]:off - offs[0] + ts, :]
    vext_ref[0:CONV_HALO, :] = vext_ref[ts:ts + CONV_HALO, :]
    mu = jnp.mean(acc, axis=-1, keepdims=True)
    xc = acc - mu
    var = jnp.mean(xc * xc, axis=-1, keepdims=True)
    ln = xc * lax.rsqrt(var + EPS) * lng_ref[...] + lnb_ref[...]
    sw = ln * _sigmoid(ln)
    y_conv = jnp.dot(sw.astype(BF16), wpw_ref[...], preferred_element_type=F32)
    merged = _sigmoid(jnp.dot(h, wg_ref[:, 0:d], preferred_element_type=F32)) * y_conv

    gy = jax.nn.gelu(yt_ref[...]).astype(BF16)
    z = lax.dot_general(gy, wglu_ref[...], (((0,), (0,)), ((), ())), preferred_element_type=F32)
    y_ssm = z[:, :d] * _sigmoid(z[:, d:])
    merged = merged + _sigmoid(jnp.dot(h, wg_ref[:, d:2 * d], preferred_element_type=F32)) * y_ssm

    kk = k_ref[0]
    vv = v_ref[0]
    outs = []
    for hd in range(HEADS):
        sl = slice(hd * HEAD_DIM, (hd + 1) * HEAD_DIM)
        s = lax.dot_general(q[:, sl].astype(BF16), kk[:, sl], (((1,), (1,)), ((), ())),
                            preferred_element_type=F32) * (HEAD_DIM ** -0.5)
        p = jnp.exp(s - jnp.max(s, axis=-1, keepdims=True))
        den = jnp.sum(p, axis=-1, keepdims=True)
        o = jnp.dot(p.astype(BF16), vv[:, sl], preferred_element_type=F32) / den
        outs.append(o.astype(BF16))
    y_mem = jnp.dot(jnp.concatenate(outs, axis=1), wo_ref[...], preferred_element_type=F32)
    merged = merged + _sigmoid(jnp.dot(h, wg_ref[:, 2 * d:3 * d], preferred_element_type=F32)) * y_mem

    o_ref[0] = x + jnp.dot(merged.astype(BF16), wout_ref[...], preferred_element_type=F32)


def _mix_call(x, yt, kmem, vmem, g_mix, wc, wq, wg, dw, dwb, lng, lnb, wpw, wglu, wo, wout):
    b, s, d = x.shape
    ts = MIX_TILE
    nst = s // ts
    cw = dw.shape[1]
    m = kmem.shape[1]
    consts = [g_mix, wc, wq, wg, dw, dwb, lng, lnb, wpw, wglu, wo, wout]
    return pl.pallas_call(
        _mix_kernel,
        grid=(b, nst),
        in_specs=[pl.BlockSpec((1, ts, d), lambda i, j: (i, j, 0)),
                  pl.BlockSpec((yt.shape[0], ts), lambda i, j: (0, i * nst + j)),
                  pl.BlockSpec((1, m, kmem.shape[2]), lambda i, j: (i, 0, 0)),
                  pl.BlockSpec((1, m, vmem.shape[2]), lambda i, j: (i, 0, 0))]
                 + [_const_spec(c.shape) for c in consts],
        out_specs=pl.BlockSpec((1, ts, d), lambda i, j: (i, j, 0)),
        out_shape=jax.ShapeDtypeStruct((b, s, d), F32),
        scratch_shapes=[pltpu.VMEM((ts + CONV_HALO, cw), F32),
                        pltpu.VMEM((ts + CONV_HALO, cw), F32)],
        compiler_params=pltpu.CompilerParams(
            dimension_semantics=("arbitrary", "arbitrary"), vmem_limit_bytes=VMEM_LIMIT),
        name="mix",
    )(x, yt, kmem, vmem, *consts)


def _route_kernel(x_ref, g_ref, wr_ref, br_ref, xs_ref, info_ref, nch_ref):
    ts = x_ref.shape[0]
    cap = xs_ref.shape[0] * CHUNK_ROWS
    h2 = _rms(x_ref[...], g_ref[...])
    hb = h2.astype(BF16)
    hl = (h2 - hb.astype(F32)).astype(BF16)
    hw = jnp.dot(hb, wr_ref[...], preferred_element_type=F32)
    logits = (hw[:, :LANES] + hw[:, LANES:]
              + jnp.dot(hl, wr_ref[:, :LANES], preferred_element_type=F32)) + br_ref[...]
    lt = logits.T
    le = lt[0:N_EXPERTS]
    lg = lt[N_EXPERTS:N_EXPERTS + 8]
    neg = jnp.float32(-1e30)
    big = jnp.float32(1e9)
    g_f = lax.broadcasted_iota(jnp.int32, (8, ts), 0).astype(F32)
    e_f = lax.broadcasted_iota(jnp.int32, (N_EXPERTS, ts), 0).astype(F32)

    gmask = g_f < MOE_GROUPS
    gmax = jnp.max(jnp.where(gmask, lg, neg), axis=0, keepdims=True)
    gidx = jnp.min(jnp.where(gmask & (lg == gmax), g_f, big), axis=0, keepdims=True)
    gsum = jnp.sum(jnp.where(gmask, jnp.exp(jnp.minimum(lg - gmax, 0.0)), 0.0), axis=0, keepdims=True)
    p_top = 1.0 / gsum
    emask = jnp.floor(e_f * (1.0 / EXPERTS_PER_GROUP)) == gidx
    m1 = jnp.max(jnp.where(emask, le, neg), axis=0, keepdims=True)
    i1 = jnp.min(jnp.where(emask & (le == m1), e_f, big), axis=0, keepdims=True)
    emask2 = emask & (e_f != i1)
    m2 = jnp.max(jnp.where(emask2, le, neg), axis=0, keepdims=True)
    i2 = jnp.min(jnp.where(emask2 & (le == m2), e_f, big), axis=0, keepdims=True)
    r = jnp.exp(m2 - m1)
    w1 = p_top / (1.0 + r)
    w2 = p_top * r / (1.0 + r)

    sel1 = e_f == i1
    sel2 = e_f == i2
    occ = jnp.where(sel1 | sel2, 1.0, 0.0)
    tr = lax.broadcasted_iota(jnp.int32, (ts, ts), 0)
    tc = lax.broadcasted_iota(jnp.int32, (ts, ts), 1)
    earlier = jnp.where(tr < tc, 1.0, 0.0).astype(BF16)
    rank = jnp.dot(occ.astype(BF16), earlier, preferred_element_type=F32)
    cnt = jnp.sum(occ, axis=1, keepdims=True)
    nch = jnp.floor((cnt + (CHUNK_ROWS - 1)) * (1.0 / CHUNK_ROWS))
    er = lax.broadcasted_iota(jnp.int32, (N_EXPERTS, N_EXPERTS), 0)
    ec = lax.broadcasted_iota(jnp.int32, (N_EXPERTS, N_EXPERTS), 1)
    lower = jnp.where(ec < er, 1.0, 0.0).astype(BF16)
    nch_b = jnp.broadcast_to(nch, (N_EXPERTS, LANES))
    start = jnp.dot(lower, nch_b.astype(BF16), preferred_element_type=F32)[:, 0:1] * CHUNK_ROWS
    slot = start + rank
    pos1 = jnp.sum(jnp.where(sel1, slot, 0.0), axis=0, keepdims=True)
    pos2 = jnp.sum(jnp.where(sel2, slot, 0.0), axis=0, keepdims=True)

    rowid = lax.broadcasted_iota(jnp.int32, (cap, ts), 0)
    p = jnp.where((rowid == pos1.astype(jnp.int32)) | (rowid == pos2.astype(jnp.int32)), 1.0, 0.0)
    xs = jnp.dot(p.astype(BF16), hb, preferred_element_type=F32).astype(BF16)
    xs_ref[...] = xs.reshape(xs_ref.shape)
    sub = lax.broadcasted_iota(jnp.int32, (LANES, ts), 0)
    info_t = jnp.where(sub == 0, pos1, jnp.where(sub == 1, pos2,
                       jnp.where(sub == 2, w1, jnp.where(sub == 3, w2, 0.0))))
    info_ref[...] = info_t.T
    nch_ref[0] = jnp.concatenate([nch_b, jnp.zeros((LANES - N_EXPERTS, LANES), F32)], axis=0).T[0:8, :]


def _route_call(x1, g_ffn, wr, br, cap):
    t, d = x1.shape
    ts = TOKEN_TILE
    nt = t // ts
    return pl.pallas_call(
        _route_kernel,
        grid=(nt,),
        in_specs=[pl.BlockSpec((ts, d), lambda i: (i, 0)),
                  pl.BlockSpec((1, d), lambda i: (0, 0)),
                  pl.BlockSpec(wr.shape, lambda i: (0, 0)),
                  pl.BlockSpec((1, LANES), lambda i: (0, 0))],
        out_specs=[pl.BlockSpec((cap // CHUNK_ROWS, CHUNK_ROWS, d), lambda i: (i, 0, 0)),
                   pl.BlockSpec((ts, LANES), lambda i: (i, 0)),
                   pl.BlockSpec((1, 8, LANES), lambda i: (i, 0, 0))],
        out_shape=[jax.ShapeDtypeStruct((nt * cap // CHUNK_ROWS, CHUNK_ROWS, d), BF16),
                   jax.ShapeDtypeStruct((t, LANES), F32),
                   jax.ShapeDtypeStruct((nt, 8, LANES), F32)],
        compiler_params=pltpu.CompilerParams(vmem_limit_bytes=VMEM_LIMIT),
        name="route",
    )(x1, g_ffn, wr, br)


def _expert_kernel(te_ref, nu_ref, src_ref, xs_ref, wg_ref, wu_ref, wd_ref, o_ref, xbuf_ref, sem_ref):
    i = pl.program_id(0)
    n_used = nu_ref[0]
    slot = i % 2

    def chunk_copy(tile, c, buf):
        return pltpu.make_async_copy(
            xs_ref.at[src_ref[tile * CHUNKS_PER_ETILE + c]], xbuf_ref.at[buf, c], sem_ref.at[buf])

    def gather(tile, buf):
        for c in range(CHUNKS_PER_ETILE):
            chunk_copy(tile, c, buf).start(priority=c % 2)

    def drain(tile, buf):
        del tile
        pltpu.make_async_copy(xs_ref.at[pl.ds(0, CHUNKS_PER_ETILE)], xbuf_ref.at[buf], sem_ref.at[buf]).wait()

    @pl.when(i == 0)
    def _():
        gather(0, 0)

    @pl.when(i < n_used)
    def _():
        nxt = jnp.minimum(i + 1, n_used - 1)
        gather(nxt, 1 - slot)
        drain(i, slot)
        x = xbuf_ref[slot].reshape(EXPERT_TILE, xbuf_ref.shape[-1])
        gate = jnp.dot(x, wg_ref[0].astype(BF16), preferred_element_type=F32)
        up = jnp.dot(x, wu_ref[0].astype(BF16), preferred_element_type=F32)
        act = (gate * _sigmoid(gate) * up).astype(BF16)
        y = jnp.dot(act, wd_ref[0].astype(BF16), preferred_element_type=F32).astype(BF16)
        o_ref[...] = y.reshape(o_ref.shape)

        @pl.when(i == n_used - 1)
        def _():
            drain(nxt, 1 - slot)

    @pl.when(i >= n_used)
    def _():
        o_ref[...] = jnp.zeros_like(o_ref)


def _expert_call(tile_expert, n_used, chunk_src, xs, wg, wu, wd):
    d = xs.shape[-1]
    de = wg.shape[2]
    tm = EXPERT_TILE
    nt = tile_expert.shape[0]

    def w_map(i, te, nu, src):
        return (te[jnp.maximum(jnp.minimum(i, nu[0] - 1), 0)], 0, 0)

    return pl.pallas_call(
        _expert_kernel,
        grid_spec=pltpu.PrefetchScalarGridSpec(
            num_scalar_prefetch=3,
            grid=(nt,),
            in_specs=[pl.BlockSpec(memory_space=pl.ANY),
                      pl.BlockSpec((1, d, de), w_map),
                      pl.BlockSpec((1, d, de), w_map),
                      pl.BlockSpec((1, de, d), w_map)],
            out_specs=pl.BlockSpec((CHUNKS_PER_ETILE, CHUNK_ROWS, d), lambda i, te, nu, src: (i, 0, 0)),
            scratch_shapes=[pltpu.VMEM((2, CHUNKS_PER_ETILE, CHUNK_ROWS, d), BF16),
                            pltpu.SemaphoreType.DMA((2,))]),
        out_shape=jax.ShapeDtypeStruct((nt * CHUNKS_PER_ETILE, CHUNK_ROWS, d), BF16),
        compiler_params=pltpu.CompilerParams(dimension_semantics=("arbitrary",)),
        name="expert",
    )(tile_expert, n_used, chunk_src, xs, wg, wu, wd)


def _combine_kernel(dst_ref, x_ref, info_ref, g_ref, ys_ref, o_ref, ybuf_ref, sem_ref):
    i = pl.program_id(0)
    n_tiles = pl.num_programs(0)
    ts = x_ref.shape[0]
    cpt = ybuf_ref.shape[1]
    cap = cpt * CHUNK_ROWS
    slot = i % 2

    def chunk_copy(tile, k, buf):
        return pltpu.make_async_copy(ys_ref.at[dst_ref[tile * cpt + k]], ybuf_ref.at[buf, k], sem_ref.at[buf])

    def gather(tile, buf):
        for k in range(cpt):
            chunk_copy(tile, k, buf).start(priority=k % 2)

    def drain(tile, buf):
        del tile
        pltpu.make_async_copy(ys_ref.at[pl.ds(0, cpt)], ybuf_ref.at[buf], sem_ref.at[buf]).wait()

    @pl.when(i == 0)
    def _():
        gather(0, 0)

    nxt = jnp.minimum(i + 1, n_tiles - 1)
    gather(nxt, 1 - slot)
    drain(i, slot)

    info = info_ref[...]
    pos1 = info[:, 0:1].astype(jnp.int32)
    pos2 = info[:, 1:2].astype(jnp.int32)
    w1 = info[:, 2:3]
    w2 = info[:, 3:4]
    rowid = lax.broadcasted_iota(jnp.int32, (ts, cap), 1)
    ys = ybuf_ref[slot].reshape(cap, ybuf_ref.shape[-1])
    pw = jnp.where(rowid == pos1, w1, jnp.where(rowid == pos2, w2, 0.0)).astype(BF16)
    y = jnp.dot(pw, ys, preferred_element_type=F32)
    o_ref[...] = _rms(x_ref[...] + y, g_ref[...])

    @pl.when(i == n_tiles - 1)
    def _():
        drain(nxt, 1 - slot)


def _combine_call(chunk_dst, x1, info, g_final, ys_sorted, cap):
    t, d = x1.shape
    ts = TOKEN_TILE
    return pl.pallas_call(
        _combine_kernel,
        grid_spec=pltpu.PrefetchScalarGridSpec(
            num_scalar_prefetch=1,
            grid=(t // ts,),
            in_specs=[pl.BlockSpec((ts, d), lambda i, dst: (i, 0)),
                      pl.BlockSpec((ts, LANES), lambda i, dst: (i, 0)),
                      pl.BlockSpec((1, d), lambda i, dst: (0, 0)),
                      pl.BlockSpec(memory_space=pl.ANY)],
            out_specs=pl.BlockSpec((ts, d), lambda i, dst: (i, 0)),
            scratch_shapes=[pltpu.VMEM((2, cap // CHUNK_ROWS, CHUNK_ROWS, d), BF16),
                            pltpu.SemaphoreType.DMA((2,))]),
        out_shape=jax.ShapeDtypeStruct((t, d), F32),
        compiler_params=pltpu.CompilerParams(dimension_semantics=("arbitrary",),
                                             vmem_limit_bytes=VMEM_LIMIT),
        name="combine",
    )(chunk_dst, x1, info, g_final, ys_sorted)


def _run_tables(nch, cpt, n_etiles):
    n_tiles, n_exp = nch.shape
    per = CHUNKS_PER_ETILE
    tcum = jnp.cumsum(nch, axis=1)
    toff = tcum - nch
    ecum = jnp.cumsum(nch, axis=0)
    etot = ecum[-1]
    eseg = -(-etot // per) * per
    segcum = jnp.cumsum(eseg)
    ebase = segcum - eseg
    eoff = ebase[None, :] + ecum - nch
    n_used = (segcum[-1] // per).reshape(1)

    k = jnp.arange(cpt, dtype=jnp.int32)[None, :, None]
    in_run = (k >= toff[:, None, :]) & (k < tcum[:, None, :])
    chunk_dst = jnp.sum(jnp.where(in_run, (eoff - toff)[:, None, :] + k, 0), axis=-1)

    first = jnp.arange(n_etiles, dtype=jnp.int32)[:, None] * per
    owner = (first >= ebase[None, :]) & (first < segcum[None, :])
    tile_expert = jnp.sum(jnp.where(owner, jnp.arange(n_exp, dtype=jnp.int32)[None, :], 0), axis=-1)

    pick = lambda a: jnp.sum(jnp.where(owner[:, None, :], a[None, :, :], 0), axis=-1)
    run_lo, run_n = pick(eoff), pick(nch)
    shift = pick(jnp.arange(n_tiles, dtype=jnp.int32)[:, None] * cpt + toff - eoff)
    c = (first + jnp.arange(per, dtype=jnp.int32)[None, :])[:, :, None]
    hit = (c >= run_lo[:, None, :]) & (c < (run_lo + run_n)[:, None, :])
    chunk_src = jnp.sum(jnp.where(hit, shift[:, None, :] + c, 0), axis=-1)
    i32 = lambda a: a.reshape(-1).astype(jnp.int32)
    return i32(chunk_dst), i32(tile_expert), i32(n_used), i32(chunk_src)


def _layer(x, mem, g_mix, w_in, conv_dw, conv_dw_bias, conv_ln_g, conv_ln_b, w_conv_out,
           lam_re, lam_im, log_dt, b_re, b_im, c_re, c_im, ssm_d, w_ssm_glu, g_mem, w_mem_kv,
           w_mem_out, w_out, g_ffn, w_rg, b_rg, w_re, b_re_, w_eg, w_eu, w_ed, g_final):
    b, s, d = x.shape
    t = b * s
    cw = conv_dw.shape[1]
    sw = ssm_d.shape[0]
    qw = w_mem_out.shape[0]
    n_groups = sw // SSM_GROUP
    row = lambda a: a.reshape(1, -1)

    o0, o1, o2 = 2 * cw, 2 * cw + sw, 2 * cw + sw + qw
    wc = w_in[:, :o0].astype(BF16)
    w_ssm_t = w_in[:, o0:o1].T.astype(BF16)
    wq = w_in[:, o1:o2].astype(BF16)
    wg = w_in[:, o2:].astype(BF16)

    kmem, vmem = _kv_call(mem, row(g_mem), w_mem_kv.astype(BF16))

    ut = _ut_call(x.reshape(t, d), row(g_mix), w_ssm_t)
    kt, wz, vy, acoef = _ssm_tables(lam_re, lam_im, log_dt, b_re, b_im, c_re, c_im, ssm_d)
    yt = _ssm_call(ut, kt, wz, vy, acoef, b)

    dw = jnp.concatenate([conv_dw, jnp.zeros((CONV_HALO - CONV_KERNEL, cw), F32)], axis=0)
    x1 = _mix_call(x, yt, kmem, vmem, row(g_mix), wc, wq, wg, dw, row(conv_dw_bias),
                   row(conv_ln_g), row(conv_ln_b), w_conv_out.astype(BF16),
                   w_ssm_glu.astype(BF16), w_mem_out.astype(BF16), w_out.astype(BF16))
    x1 = x1.reshape(t, d)

    pad = LANES - N_EXPERTS - MOE_GROUPS
    wr = jnp.concatenate([w_re, w_rg, jnp.zeros((d, pad), F32)], axis=1)
    wr_hi = wr.astype(BF16)
    wr = jnp.concatenate([wr_hi, (wr - wr_hi.astype(F32)).astype(BF16)], axis=1)
    br = jnp.concatenate([b_re_, b_rg, jnp.zeros((pad,), F32)]).reshape(1, LANES)
    n_tiles = t // TOKEN_TILE
    max_tile_chunks = (2 * TOKEN_TILE + N_EXPERTS * (CHUNK_ROWS - 1)) // CHUNK_ROWS
    cpt = -(-max_tile_chunks // 8) * 8
    cap = cpt * CHUNK_ROWS
    xs_tiles, info, nch_f = _route_call(x1, row(g_ffn), wr, br, cap)

    max_chunks = n_tiles * max_tile_chunks + N_EXPERTS * (CHUNKS_PER_ETILE - 1)
    n_etiles = -(-max_chunks // CHUNKS_PER_ETILE)
    nch = nch_f[:, 0, :N_EXPERTS].astype(jnp.int32)
    chunk_dst, tile_expert, n_used, chunk_src = _run_tables(nch, cpt, n_etiles)

    ys_sorted = _expert_call(tile_expert, n_used, chunk_src, xs_tiles, w_eg, w_eu, w_ed)
    out = _combine_call(chunk_dst, x1, info, row(g_final), ys_sorted, cap)
    return out.reshape(b, s, d)


def kernel(x, mem, g_mix, w_in, conv_dw, conv_dw_bias, conv_ln_g, conv_ln_b, w_conv_out, ssm_lambda_re, ssm_lambda_im, ssm_log_dt, ssm_b_re, ssm_b_im, ssm_c_re, ssm_c_im, ssm_d, w_ssm_glu, g_mem, w_mem_kv, w_mem_out, w_out, g_ffn, w_router_group, b_router_group, w_router_expert, b_router_expert, w_exp_gate, w_exp_up, w_exp_down, g_final):
    assert g_mix.shape[0] == 1, "the problem fixes one layer"
    l = 0
    return _layer(
        x, mem, g_mix[l], w_in[l], conv_dw[l], conv_dw_bias[l], conv_ln_g[l], conv_ln_b[l],
        w_conv_out[l], ssm_lambda_re[l], ssm_lambda_im[l], ssm_log_dt[l], ssm_b_re[l],
        ssm_b_im[l], ssm_c_re[l], ssm_c_im[l], ssm_d[l], w_ssm_glu[l], g_mem[l], w_mem_kv[l],
        w_mem_out[l], w_out[l], g_ffn[l], w_router_group[l], b_router_group[l],
        w_router_expert[l], b_router_expert[l], w_exp_gate[l], w_exp_up[l], w_exp_down[l], g_final)
```

```python
import functools

import jax
import jax.numpy as jnp
from jax import lax
from jax.experimental import pallas as pl
from jax.experimental.pallas import tpu as pltpu

F32 = jnp.float32
BF16 = jnp.bfloat16
EPS = 1e-6

LANES = 128
CHUNK_ROWS = 16
SSM_GROUP = 16
SSM_STATE = 64
SSM_BLOCK = LANES
CONV_KERNEL = 31
CONV_HALO = 32
HEADS = 4
HEAD_DIM = 128
MOE_GROUPS = 4
EXPERTS_PER_GROUP = 8
N_EXPERTS = MOE_GROUPS * EXPERTS_PER_GROUP
TOKEN_TILE = 512
MIX_TILE = 1024
EXPERT_TILE = 512
CHUNKS_PER_ETILE = EXPERT_TILE // CHUNK_ROWS
VMEM_LIMIT = 56 * 1024 * 1024


def _rms(x, g):
    return x * lax.rsqrt(jnp.mean(x * x, axis=-1, keepdims=True) + EPS) * g


def _sigmoid(x):
    return 0.5 * jnp.tanh(0.5 * x) + 0.5


def _const_spec(shape):
    zeros = (0,) * len(shape)
    return pl.BlockSpec(shape, lambda *_: zeros, pipeline_mode=pl.Buffered(1))


def _kv_kernel(mem_ref, g_ref, w_ref, k_ref, v_ref):
    width = k_ref.shape[-1]
    mn = _rms(mem_ref[0], g_ref[...]).astype(BF16)
    kv = jnp.dot(mn, w_ref[...], preferred_element_type=F32)
    k_ref[0] = kv[:, :width].astype(BF16)
    v_ref[0] = kv[:, width:].astype(BF16)


def _kv_call(mem, g_mem, w_kv):
    b, m, d = mem.shape
    width = w_kv.shape[1] // 2
    return pl.pallas_call(
        _kv_kernel,
        grid=(b,),
        in_specs=[pl.BlockSpec((1, m, d), lambda i: (i, 0, 0)),
                  pl.BlockSpec((1, d), lambda i: (0, 0)),
                  pl.BlockSpec(w_kv.shape, lambda i: (0, 0))],
        out_specs=[pl.BlockSpec((1, m, width), lambda i: (i, 0, 0)),
                   pl.BlockSpec((1, m, width), lambda i: (i, 0, 0))],
        out_shape=[jax.ShapeDtypeStruct((b, m, width), BF16)] * 2,
        name="kv",
    )(mem, g_mem, w_kv)


def _ut_kernel(x_ref, g_ref, wt_ref, o_ref):
    h = _rms(x_ref[...], g_ref[...]).astype(BF16)
    ut = lax.dot_general(wt_ref[...], h, (((1,), (1,)), ((), ())), preferred_element_type=F32)
    ut = ut.reshape(ut.shape[0], ut.shape[1] // SSM_BLOCK, SSM_BLOCK)
    o_ref[...] = ut.reshape(o_ref.shape).astype(BF16)


def _ut_call(x2, g_mix, w_ssm_t):
    t, d = x2.shape
    c = w_ssm_t.shape[0]
    ts = 16 * SSM_BLOCK
    n_groups = c // SSM_GROUP
    return pl.pallas_call(
        _ut_kernel,
        grid=(t // ts,),
        in_specs=[pl.BlockSpec((ts, d), lambda i: (i, 0)),
                  pl.BlockSpec((1, d), lambda i: (0, 0)),
                  pl.BlockSpec((c, d), lambda i: (0, 0))],
        out_specs=pl.BlockSpec((n_groups, SSM_GROUP, ts // SSM_BLOCK, SSM_BLOCK),
                               lambda i: (0, 0, i, 0)),
        out_shape=jax.ShapeDtypeStruct((n_groups, SSM_GROUP, t // SSM_BLOCK, SSM_BLOCK), BF16),
        name="ut",
    )(x2, g_mix, w_ssm_t)


def _ssm_kernel(u_ref, kt_ref, w_ref, v_ref, a_ref, y_ref, acc_ref, z_ref, zs_ref, s_ref, slab_ref,
                *, n_batch):
    rows = u_ref.shape[2]
    n_blocks = rows // n_batch
    ri = lax.broadcasted_iota(jnp.int32, (SSM_BLOCK, SSM_BLOCK), 0)
    ci = lax.broadcasted_iota(jnp.int32, (SSM_BLOCK, SSM_BLOCK), 1)
    causal = ci >= ri

    n_pairs = SSM_GROUP // 2

    def build(cp, slot):
        for half in range(2):
            kt = kt_ref[0, 2 * cp + half]
            for c in range(SSM_GROUP):
                xb = jnp.broadcast_to(kt[c:c + 1, :], (SSM_BLOCK, SSM_BLOCK))
                toe = pltpu.roll(xb, 0, 1, stride=1, stride_axis=0)
                slab_ref[slot, half * SSM_BLOCK:(half + 1) * SSM_BLOCK, c * SSM_BLOCK:(c + 1) * SSM_BLOCK] = (
                    jnp.where(causal, toe, 0.0).astype(BF16))

    def apply(cp, slot, first):
        x2 = jnp.concatenate([u_ref[0, 2 * cp], u_ref[0, 2 * cp + 1]], axis=1)
        part = jnp.dot(x2, slab_ref[slot], preferred_element_type=F32)
        wrow = pl.multiple_of(cp * 2 * SSM_BLOCK, 2 * SSM_BLOCK)
        zpart = jnp.dot(x2, w_ref[0, pl.ds(wrow, 2 * SSM_BLOCK), :], preferred_element_type=F32)
        if first:
            acc_ref[...] = part
            z_ref[...] = zpart
        else:
            acc_ref[...] += part
            z_ref[...] += zpart

    build(0, 0)
    build(1, 1)
    apply(0, 0, True)

    def two(it, carry):
        cp = 2 * it + 1
        build(cp + 1, 0)
        apply(cp, 1, False)
        build(cp + 2, 1)
        apply(cp + 1, 0, False)
        return carry

    lax.fori_loop(0, (n_pairs - 2) // 2, two, 0)
    apply(n_pairs - 1, 1, False)

    a_full = a_ref[0, 0:1, :]
    a_swap = a_ref[0, 1:2, :]
    zs_ref[...] = pltpu.roll(z_ref[...], SSM_STATE, 1)
    st = jnp.zeros((n_batch, 2 * SSM_STATE), F32)
    sw = jnp.zeros((n_batch, 2 * SSM_STATE), F32)
    for blk in range(n_blocks):
        sl = pl.ds(blk, n_batch, stride=n_blocks)
        s_ref[sl, :] = st
        st, sw = (a_full * st + a_swap * sw + z_ref[sl, :],
                  a_full * sw - a_swap * st + zs_ref[sl, :])

    sb = s_ref[...].astype(BF16)
    y3 = jnp.stack([acc_ref[:, c * SSM_BLOCK:(c + 1) * SSM_BLOCK]
                    + jnp.dot(sb, v_ref[0, c], preferred_element_type=F32)
                    for c in range(SSM_GROUP)], axis=0)
    y_ref[...] = y3.reshape(y_ref.shape)


def _ssm_call(u4, kt, wz, vy, acoef, n_batch):
    g, c, rows, blk = u4.shape
    width = c * blk
    return pl.pallas_call(
        functools.partial(_ssm_kernel, n_batch=n_batch),
        grid=(g,),
        in_specs=[pl.BlockSpec((1, c, rows, blk), lambda i: (i, 0, 0, 0)),
                  pl.BlockSpec((1, c, c, blk), lambda i: (i, 0, 0, 0)),
                  pl.BlockSpec((1, width, 2 * SSM_STATE), lambda i: (i, 0, 0)),
                  pl.BlockSpec((1, c, 2 * SSM_STATE, blk), lambda i: (i, 0, 0, 0)),
                  pl.BlockSpec((1, 2, 2 * SSM_STATE), lambda i: (i, 0, 0))],
        out_specs=pl.BlockSpec((c, rows * blk), lambda i: (i, 0)),
        out_shape=jax.ShapeDtypeStruct((g * c, rows * blk), F32),
        scratch_shapes=[pltpu.VMEM((rows, width), F32),
                        pltpu.VMEM((rows, 2 * SSM_STATE), F32),
                        pltpu.VMEM((rows, 2 * SSM_STATE), F32),
                        pltpu.VMEM((rows, 2 * SSM_STATE), F32),
                        pltpu.VMEM((2, 2 * blk, width), BF16)],
        name="ssm",
    )(u4, kt, wz, vy, acoef)


def _ssm_tables(lam_re, lam_im, log_dt, b_re, b_im, c_re, c_im, d):
    hi = lax.Precision.HIGHEST
    g = lam_re.shape[0]
    dt = jnp.exp(log_dt)[:, None]
    er, ei = lam_re * dt, lam_im * dt
    cat = jnp.concatenate
    kk = jnp.arange(SSM_BLOCK + 1, dtype=F32)
    kdesc = jnp.arange(SSM_BLOCK - 1, -1, -1, dtype=F32)[None, :, None]
    ppk_m, ppk_a = jnp.exp(er[:, :, None] * kk), ei[:, :, None] * kk
    ppk_r, ppk_i = ppk_m * jnp.cos(ppk_a), ppk_m * jnp.sin(ppk_a)
    prev_m, prev_a = jnp.exp(kdesc * er[:, None, :]), kdesc * ei[:, None, :]
    prr, pir = prev_m * jnp.cos(prev_a), prev_m * jnp.sin(prev_a)
    nr, ni = ppk_r[:, :, 1] - 1.0, ppk_i[:, :, 1]
    den = lam_re * lam_re + lam_im * lam_im
    fr = (nr * lam_re + ni * lam_im) / den
    fi = (ni * lam_re - nr * lam_im) / den
    bbr = jnp.swapaxes(fr[:, :, None] * b_re - fi[:, :, None] * b_im, 1, 2)
    bbi = jnp.swapaxes(fr[:, :, None] * b_im + fi[:, :, None] * b_re, 1, 2)
    cb_r = c_re[:, None] * bbr[:, :, None, :] - c_im[:, None] * bbi[:, :, None, :]
    cb_i = c_re[:, None] * bbi[:, :, None, :] + c_im[:, None] * bbr[:, :, None, :]
    cb = cat([cb_r, -cb_i], axis=-1).reshape(g, SSM_GROUP * SSM_GROUP, 2 * SSM_STATE)
    pk = cat([ppk_r[:, :, :SSM_BLOCK], ppk_i[:, :, :SSM_BLOCK]], axis=1)
    kt = jnp.einsum("gmq,gqk->gmk", cb, pk, precision=hi).reshape(g, SSM_GROUP, SSM_GROUP, SSM_BLOCK)
    dmat = jnp.eye(SSM_GROUP, dtype=F32)[None] * d.reshape(g, 1, SSM_GROUP)
    kt = kt + dmat[..., None] * (jnp.arange(SSM_BLOCK) == 0).astype(F32)
    wz = (cat([prr, prr], -1)[:, None] * cat([bbr, bbi], -1)[:, :, None, :]
          + cat([pir, pir], -1)[:, None] * cat([-bbi, bbr], -1)[:, :, None, :])
    wz = wz.astype(BF16).reshape(g, SSM_GROUP * SSM_BLOCK, 2 * SSM_STATE)
    pt_r, pt_i = ppk_r[:, :, 1:], ppk_i[:, :, 1:]
    vy = (cat([c_re, -c_re], -1)[..., None] * cat([pt_r, pt_i], axis=1)[:, None]
          + cat([-c_im, -c_im], -1)[..., None] * cat([pt_i, pt_r], axis=1)[:, None]).astype(BF16)
    ar, ai = ppk_r[:, :, SSM_BLOCK], ppk_i[:, :, SSM_BLOCK]
    acoef = jnp.stack([cat([ar, ar], -1), cat([-ai, ai], -1)], axis=1)
    return kt, wz, vy, acoef


def _mix_kernel(x_ref, yt_ref, k_ref, v_ref, g_ref, wc_ref, wq_ref, wg_ref, dw_ref, dwb_ref,
                lng_ref, lnb_ref, wpw_ref, wglu_ref, wo_ref, wout_ref, o_ref, vext_ref, vsh_ref):
    ts = x_ref.shape[1]
    d = x_ref.shape[2]
    cw = dw_ref.shape[1]
    x = x_ref[0]
    h = _rms(x, g_ref[...]).astype(BF16)

    ci = jnp.dot(h, wc_ref[...], preferred_element_type=F32)
    v = ci[:, :cw] * _sigmoid(ci[:, cw:])
    q = jnp.dot(h, wq_ref[...], preferred_element_type=F32)

    @pl.when(pl.program_id(1) == 0)
    def _():
        vext_ref[0:CONV_HALO, :] = jnp.zeros((CONV_HALO, cw), F32)

    vext_ref[CONV_HALO:CONV_HALO + ts, :] = v
    acc = jnp.broadcast_to(dwb_ref[...], (ts, cw))
    first = CONV_HALO - (CONV_KERNEL - 1)
    for phase in range(8):
        offs = [first + k for k in range(CONV_KERNEL) if (first + k) % 8 == phase]
        if not offs:
            continue
        span = offs[-1] - offs[0] + ts
        vsh_ref[0:span, :] = vext_ref[offs[0]:offs[0] + span, :]
        for off in offs:
            acc = acc + dw_ref[off - first:off - first + 1, :] * vsh_ref[off - offs[0]:off - offs[0] + ts, :]
    vext_ref[0:CONV_HALO, :] = vext_ref[ts:ts + CONV_HALO, :]
    mu = jnp.mean(acc, axis=-1, keepdims=True)
    xc = acc - mu
    var = jnp.mean(xc * xc, axis=-1, keepdims=True)
    ln = xc * lax.rsqrt(var + EPS) * lng_ref[...] + lnb_ref[...]
    sw = ln * _sigmoid(ln)
    y_conv = jnp.dot(sw.astype(BF16), wpw_ref[...], preferred_element_type=F32)
    merged = _sigmoid(jnp.dot(h, wg_ref[:, 0:d], preferred_element_type=F32)) * y_conv

    gy = jax.nn.gelu(yt_ref[...]).astype(BF16)
    z = lax.dot_general(gy, wglu_ref[...], (((0,), (0,)), ((), ())), preferred_element_type=F32)
    y_ssm = z[:, :d] * _sigmoid(z[:, d:])
    merged = merged + _sigmoid(jnp.dot(h, wg_ref[:, d:2 * d], preferred_element_type=F32)) * y_ssm

    kk = k_ref[0]
    vv = v_ref[0]
    outs = []
    for hd in range(HEADS):
        sl = slice(hd * HEAD_DIM, (hd + 1) * HEAD_DIM)
        s = lax.dot_general(q[:, sl].astype(BF16), kk[:, sl], (((1,), (1,)), ((), ())),
                            preferred_element_type=F32) * (HEAD_DIM ** -0.5)
        p = jnp.exp(s - jnp.max(s, axis=-1, keepdims=True))
        den = jnp.sum(p, axis=-1, keepdims=True)
        o = jnp.dot(p.astype(BF16), vv[:, sl], preferred_element_type=F32) / den
        outs.append(o.astype(BF16))
    y_mem = jnp.dot(jnp.concatenate(outs, axis=1), wo_ref[...], preferred_element_type=F32)
    merged = merged + _sigmoid(jnp.dot(h, wg_ref[:, 2 * d:3 * d], preferred_element_type=F32)) * y_mem

    o_ref[0] = x + jnp.dot(merged.astype(BF16), wout_ref[...], preferred_element_type=F32)


def _mix_call(x, yt, kmem, vmem, g_mix, wc, wq, wg, dw, dwb, lng, lnb, wpw, wglu, wo, wout):
    b, s, d = x.shape
    ts = MIX_TILE
    nst = s // ts
    cw = dw.shape[1]
    m = kmem.shape[1]
    consts = [g_mix, wc, wq, wg, dw, dwb, lng, lnb, wpw, wglu, wo, wout]
    return pl.pallas_call(
        _mix_kernel,
        grid=(b, nst),
        in_specs=[pl.BlockSpec((1, ts, d), lambda i, j: (i, j, 0)),
                  pl.BlockSpec((yt.shape[0], ts), lambda i, j: (0, i * nst + j)),
                  pl.BlockSpec((1, m, kmem.shape[2]), lambda i, j: (i, 0, 0)),
                  pl.BlockSpec((1, m, vmem.shape[2]), lambda i, j: (i, 0, 0))]
                 + [_const_spec(c.shape) for c in consts],
        out_specs=pl.BlockSpec((1, ts, d), lambda i, j: (i, j, 0)),
        out_shape=jax.ShapeDtypeStruct((b, s, d), F32),
        scratch_shapes=[pltpu.VMEM((ts + CONV_HALO, cw), F32),
                        pltpu.VMEM((ts + CONV_HALO, cw), F32)],
        compiler_params=pltpu.CompilerParams(
            dimension_semantics=("arbitrary", "arbitrary"), vmem_limit_bytes=VMEM_LIMIT),
        name="mix",
    )(x, yt, kmem, vmem, *consts)


def _route_kernel(x_ref, g_ref, wr_ref, br_ref, xs_ref, info_ref, nch_ref):
    ts = x_ref.shape[0]
    cap = xs_ref.shape[0] * CHUNK_ROWS
    h2 = _rms(x_ref[...], g_ref[...])
    hb = h2.astype(BF16)
    hl = (h2 - hb.astype(F32)).astype(BF16)
    hw = jnp.dot(hb, wr_ref[...], preferred_element_type=F32)
    logits = (hw[:, :LANES] + hw[:, LANES:]
              + jnp.dot(hl, wr_ref[:, :LANES], preferred_element_type=F32)) + br_ref[...]
    lt = logits.T
    le = lt[0:N_EXPERTS]
    lg = lt[N_EXPERTS:N_EXPERTS + 8]
    neg = jnp.float32(-1e30)
    big = jnp.float32(1e9)
    g_f = lax.broadcasted_iota(jnp.int32, (8, ts), 0).astype(F32)
    e_f = lax.broadcasted_iota(jnp.int32, (N_EXPERTS, ts), 0).astype(F32)

    gmask = g_f < MOE_GROUPS
    gmax = jnp.max(jnp.where(gmask, lg, neg), axis=0, keepdims=True)
    gidx = jnp.min(jnp.where(gmask & (lg == gmax), g_f, big), axis=0, keepdims=True)
    gsum = jnp.sum(jnp.where(gmask, jnp.exp(jnp.minimum(lg - gmax, 0.0)), 0.0), axis=0, keepdims=True)
    p_top = 1.0 / gsum
    emask = jnp.floor(e_f * (1.0 / EXPERTS_PER_GROUP)) == gidx
    m1 = jnp.max(jnp.where(emask, le, neg), axis=0, keepdims=True)
    i1 = jnp.min(jnp.where(emask & (le == m1), e_f, big), axis=0, keepdims=True)
    emask2 = emask & (e_f != i1)
    m2 = jnp.max(jnp.where(emask2, le, neg), axis=0, keepdims=True)
    i2 = jnp.min(jnp.where(emask2 & (le == m2), e_f, big), axis=0, keepdims=True)
    r = jnp.exp(m2 - m1)
    w1 = p_top / (1.0 + r)
    w2 = p_top * r / (1.0 + r)

    sel1 = e_f == i1
    sel2 = e_f == i2
    occ = jnp.where(sel1 | sel2, 1.0, 0.0)
    tr = lax.broadcasted_iota(jnp.int32, (ts, ts), 0)
    tc = lax.broadcasted_iota(jnp.int32, (ts, ts), 1)
    earlier = jnp.where(tr < tc, 1.0, 0.0).astype(BF16)
    rank = jnp.dot(occ.astype(BF16), earlier, preferred_element_type=F32)
    cnt = jnp.sum(occ, axis=1, keepdims=True)
    nch = jnp.floor((cnt + (CHUNK_ROWS - 1)) * (1.0 / CHUNK_ROWS))
    er = lax.broadcasted_iota(jnp.int32, (N_EXPERTS, N_EXPERTS), 0)
    ec = lax.broadcasted_iota(jnp.int32, (N_EXPERTS, N_EXPERTS), 1)
    lower = jnp.where(ec < er, 1.0, 0.0).astype(BF16)
    nch_b = jnp.broadcast_to(nch, (N_EXPERTS, LANES))
    start = jnp.dot(lower, nch_b.astype(BF16), preferred_element_type=F32)[:, 0:1] * CHUNK_ROWS
    slot = start + rank
    pos1 = jnp.sum(jnp.where(sel1, slot, 0.0), axis=0, keepdims=True)
    pos2 = jnp.sum(jnp.where(sel2, slot, 0.0), axis=0, keepdims=True)

    rowid = lax.broadcasted_iota(jnp.int32, (cap, ts), 0)
    p = jnp.where((rowid == pos1.astype(jnp.int32)) | (rowid == pos2.astype(jnp.int32)), 1.0, 0.0)
    xs = jnp.dot(p.astype(BF16), hb, preferred_element_type=F32).astype(BF16)
    xs_ref[...] = xs.reshape(xs_ref.shape)
    sub = lax.broadcasted_iota(jnp.int32, (LANES, ts), 0)
    info_t = jnp.where(sub == 0, pos1, jnp.where(sub == 1, pos2,
                       jnp.where(sub == 2, w1, jnp.where(sub == 3, w2, 0.0))))
    info_ref[...] = info_t.T
    nch_ref[0] = jnp.concatenate([nch_b, jnp.zeros((LANES - N_EXPERTS, LANES), F32)], axis=0).T[0:8, :]


def _route_call(x1, g_ffn, wr, br, cap):
    t, d = x1.shape
    ts = TOKEN_TILE
    nt = t // ts
    return pl.pallas_call(
        _route_kernel,
        grid=(nt,),
        in_specs=[pl.BlockSpec((ts, d), lambda i: (i, 0)),
                  pl.BlockSpec((1, d), lambda i: (0, 0)),
                  pl.BlockSpec(wr.shape, lambda i: (0, 0)),
                  pl.BlockSpec((1, LANES), lambda i: (0, 0))],
        out_specs=[pl.BlockSpec((cap // CHUNK_ROWS, CHUNK_ROWS, d), lambda i: (i, 0, 0)),
                   pl.BlockSpec((ts, LANES), lambda i: (i, 0)),
                   pl.BlockSpec((1, 8, LANES), lambda i: (i, 0, 0))],
        out_shape=[jax.ShapeDtypeStruct((nt * cap // CHUNK_ROWS, CHUNK_ROWS, d), BF16),
                   jax.ShapeDtypeStruct((t, LANES), F32),
                   jax.ShapeDtypeStruct((nt, 8, LANES), F32)],
        compiler_params=pltpu.CompilerParams(vmem_limit_bytes=VMEM_LIMIT),
        name="route",
    )(x1, g_ffn, wr, br)


def _expert_kernel(te_ref, nu_ref, src_ref, xs_ref, wg_ref, wu_ref, wd_ref, o_ref, xbuf_ref, sem_ref):
    i = pl.program_id(0)
    n_used = nu_ref[0]
    slot = i % 2

    def chunk_copy(tile, c, buf):
        return pltpu.make_async_copy(
            xs_ref.at[src_ref[tile * CHUNKS_PER_ETILE + c]], xbuf_ref.at[buf, c], sem_ref.at[buf])

    def gather(tile, buf):
        for c in range(CHUNKS_PER_ETILE):
            chunk_copy(tile, c, buf).start(priority=c % 2)

    def drain(tile, buf):
        del tile
        pltpu.make_async_copy(xs_ref.at[pl.ds(0, CHUNKS_PER_ETILE)], xbuf_ref.at[buf], sem_ref.at[buf]).wait()

    @pl.when(i == 0)
    def _():
        gather(0, 0)

    @pl.when(i < n_used)
    def _():
        nxt = jnp.minimum(i + 1, n_used - 1)
        gather(nxt, 1 - slot)
        drain(i, slot)
        x = xbuf_ref[slot].reshape(EXPERT_TILE, xbuf_ref.shape[-1])
        gate = jnp.dot(x, wg_ref[0].astype(BF16), preferred_element_type=F32)
        up = jnp.dot(x, wu_ref[0].astype(BF16), preferred_element_type=F32)
        act = (gate * _sigmoid(gate) * up).astype(BF16)
        y = jnp.dot(act, wd_ref[0].astype(BF16), preferred_element_type=F32).astype(BF16)
        o_ref[...] = y.reshape(o_ref.shape)

        @pl.when(i == n_used - 1)
        def _():
            drain(nxt, 1 - slot)

    @pl.when(i >= n_used)
    def _():
        o_ref[...] = jnp.zeros_like(o_ref)


def _expert_call(tile_expert, n_used, chunk_src, xs, wg, wu, wd):
    d = xs.shape[-1]
    de = wg.shape[2]
    tm = EXPERT_TILE
    nt = tile_expert.shape[0]

    def w_map(i, te, nu, src):
        return (te[jnp.maximum(jnp.minimum(i, nu[0] - 1), 0)], 0, 0)

    return pl.pallas_call(
        _expert_kernel,
        grid_spec=pltpu.PrefetchScalarGridSpec(
            num_scalar_prefetch=3,
            grid=(nt,),
            in_specs=[pl.BlockSpec(memory_space=pl.ANY),
                      pl.BlockSpec((1, d, de), w_map),
                      pl.BlockSpec((1, d, de), w_map),
                      pl.BlockSpec((1, de, d), w_map)],
            out_specs=pl.BlockSpec((CHUNKS_PER_ETILE, CHUNK_ROWS, d), lambda i, te, nu, src: (i, 0, 0)),
            scratch_shapes=[pltpu.VMEM((2, CHUNKS_PER_ETILE, CHUNK_ROWS, d), BF16),
                            pltpu.SemaphoreType.DMA((2,))]),
        out_shape=jax.ShapeDtypeStruct((nt * CHUNKS_PER_ETILE, CHUNK_ROWS, d), BF16),
        compiler_params=pltpu.CompilerParams(dimension_semantics=("arbitrary",)),
        name="expert",
    )(tile_expert, n_used, chunk_src, xs, wg, wu, wd)


def _combine_kernel(dst_ref, x_ref, info_ref, g_ref, ys_ref, o_ref, ybuf_ref, sem_ref):
    i = pl.program_id(0)
    n_tiles = pl.num_programs(0)
    ts = x_ref.shape[0]
    cpt = ybuf_ref.shape[1]
    cap = cpt * CHUNK_ROWS
    slot = i % 2

    def chunk_copy(tile, k, buf):
        return pltpu.make_async_copy(ys_ref.at[dst_ref[tile * cpt + k]], ybuf_ref.at[buf, k], sem_ref.at[buf])

    def gather(tile, buf):
        for k in range(cpt):
            chunk_copy(tile, k, buf).start(priority=k % 2)

    def drain(tile, buf):
        del tile
        pltpu.make_async_copy(ys_ref.at[pl.ds(0, cpt)], ybuf_ref.at[buf], sem_ref.at[buf]).wait()

    @pl.when(i == 0)
    def _():
        gather(0, 0)

    nxt = jnp.minimum(i + 1, n_tiles - 1)
    gather(nxt, 1 - slot)
    drain(i, slot)

    info = info_ref[...]
    pos1 = info[:, 0:1].astype(jnp.int32)
    pos2 = info[:, 1:2].astype(jnp.int32)
    w1 = info[:, 2:3]
    w2 = info[:, 3:4]
    rowid = lax.broadcasted_iota(jnp.int32, (ts, cap), 1)
    ys = ybuf_ref[slot].reshape(cap, ybuf_ref.shape[-1])
    pw = jnp.where(rowid == pos1, w1, jnp.where(rowid == pos2, w2, 0.0)).astype(BF16)
    y = jnp.dot(pw, ys, preferred_element_type=F32)
    o_ref[...] = _rms(x_ref[...] + y, g_ref[...])

    @pl.when(i == n_tiles - 1)
    def _():
        drain(nxt, 1 - slot)


def _combine_call(chunk_dst, x1, info, g_final, ys_sorted, cap):
    t, d = x1.shape
    ts = TOKEN_TILE
    return pl.pallas_call(
        _combine_kernel,
        grid_spec=pltpu.PrefetchScalarGridSpec(
            num_scalar_prefetch=1,
            grid=(t // ts,),
            in_specs=[pl.BlockSpec((ts, d), lambda i, dst: (i, 0)),
                      pl.BlockSpec((ts, LANES), lambda i, dst: (i, 0)),
                      pl.BlockSpec((1, d), lambda i, dst: (0, 0)),
                      pl.BlockSpec(memory_space=pl.ANY)],
            out_specs=pl.BlockSpec((ts, d), lambda i, dst: (i, 0)),
            scratch_shapes=[pltpu.VMEM((2, cap // CHUNK_ROWS, CHUNK_ROWS, d), BF16),
                            pltpu.SemaphoreType.DMA((2,))]),
        out_shape=jax.ShapeDtypeStruct((t, d), F32),
        compiler_params=pltpu.CompilerParams(dimension_semantics=("arbitrary",),
                                             vmem_limit_bytes=VMEM_LIMIT),
        name="combine",
    )(chunk_dst, x1, info, g_final, ys_sorted)


def _run_tables(nch, cpt, n_etiles):
    n_tiles, n_exp = nch.shape
    per = CHUNKS_PER_ETILE
    tcum = jnp.cumsum(nch, axis=1)
    toff = tcum - nch
    ecum = jnp.cumsum(nch, axis=0)
    etot = ecum[-1]
    eseg = -(-etot // per) * per
    segcum = jnp.cumsum(eseg)
    ebase = segcum - eseg
    eoff = ebase[None, :] + ecum - nch
    n_used = (segcum[-1] // per).reshape(1)

    k = jnp.arange(cpt, dtype=jnp.int32)[None, :, None]
    in_run = (k >= toff[:, None, :]) & (k < tcum[:, None, :])
    chunk_dst = jnp.sum(jnp.where(in_run, (eoff - toff)[:, None, :] + k, 0), axis=-1)

    first = jnp.arange(n_etiles, dtype=jnp.int32)[:, None] * per
    owner = (first >= ebase[None, :]) & (first < segcum[None, :])
    tile_expert = jnp.sum(jnp.where(owner, jnp.arange(n_exp, dtype=jnp.int32)[None, :], 0), axis=-1)

    pick = lambda a: jnp.sum(jnp.where(owner[:, None, :], a[None, :, :], 0), axis=-1)
    run_lo, run_n = pick(eoff), pick(nch)
    shift = pick(jnp.arange(n_tiles, dtype=jnp.int32)[:, None] * cpt + toff - eoff)
    c = (first + jnp.arange(per, dtype=jnp.int32)[None, :])[:, :, None]
    hit = (c >= run_lo[:, None, :]) & (c < (run_lo + run_n)[:, None, :])
    chunk_src = jnp.sum(jnp.where(hit, shift[:, None, :] + c, 0), axis=-1)
    i32 = lambda a: a.reshape(-1).astype(jnp.int32)
    return i32(chunk_dst), i32(tile_expert), i32(n_used), i32(chunk_src)


def _layer(x, mem, g_mix, w_in, conv_dw, conv_dw_bias, conv_ln_g, conv_ln_b, w_conv_out,
           lam_re, lam_im, log_dt, b_re, b_im, c_re, c_im, ssm_d, w_ssm_glu, g_mem, w_mem_kv,
           w_mem_out, w_out, g_ffn, w_rg, b_rg, w_re, b_re_, w_eg, w_eu, w_ed, g_final):
    b, s, d = x.shape
    t = b * s
    cw = conv_dw.shape[1]
    sw = ssm_d.shape[0]
    qw = w_mem_out.shape[0]
    n_groups = sw // SSM_GROUP
    row = lambda a: a.reshape(1, -1)

    o0, o1, o2 = 2 * cw, 2 * cw + sw, 2 * cw + sw + qw
    wc = w_in[:, :o0].astype(BF16)
    w_ssm_t = w_in[:, o0:o1].T.astype(BF16)
    wq = w_in[:, o1:o2].astype(BF16)
    wg = w_in[:, o2:].astype(BF16)

    kmem, vmem = _kv_call(mem, row(g_mem), w_mem_kv.astype(BF16))

    ut = _ut_call(x.reshape(t, d), row(g_mix), w_ssm_t)
    kt, wz, vy, acoef = _ssm_tables(lam_re, lam_im, log_dt, b_re, b_im, c_re, c_im, ssm_d)
    yt = _ssm_call(ut, kt, wz, vy, acoef, b)

    dw = jnp.concatenate([conv_dw, jnp.zeros((CONV_HALO - CONV_KERNEL, cw), F32)], axis=0)
    x1 = _mix_call(x, yt, kmem, vmem, row(g_mix), wc, wq, wg, dw, row(conv_dw_bias),
                   row(conv_ln_g), row(conv_ln_b), w_conv_out.astype(BF16),
                   w_ssm_glu.astype(BF16), w_mem_out.astype(BF16), w_out.astype(BF16))
    x1 = x1.reshape(t, d)

    pad = LANES - N_EXPERTS - MOE_GROUPS
    wr = jnp.concatenate([w_re, w_rg, jnp.zeros((d, pad), F32)], axis=1)
    wr_hi = wr.astype(BF16)
    wr = jnp.concatenate([wr_hi, (wr - wr_hi.astype(F32)).astype(BF16)], axis=1)
    br = jnp.concatenate([b_re_, b_rg, jnp.zeros((pad,), F32)]).reshape(1, LANES)
    n_tiles = t // TOKEN_TILE
    max_tile_chunks = (2 * TOKEN_TILE + N_EXPERTS * (CHUNK_ROWS - 1)) // CHUNK_ROWS
    cpt = -(-max_tile_chunks // 8) * 8
    cap = cpt * CHUNK_ROWS
    xs_tiles, info, nch_f = _route_call(x1, row(g_ffn), wr, br, cap)

    max_chunks = n_tiles * max_tile_chunks + N_EXPERTS * (CHUNKS_PER_ETILE - 1)
    n_etiles = -(-max_chunks // CHUNKS_PER_ETILE)
    nch = nch_f[:, 0, :N_EXPERTS].astype(jnp.int32)
    chunk_dst, tile_expert, n_used, chunk_src = _run_tables(nch, cpt, n_etiles)

    ys_sorted = _expert_call(tile_expert, n_used, chunk_src, xs_tiles, w_eg, w_eu, w_ed)
    out = _combine_call(chunk_dst, x1, info, row(g_final), ys_sorted, cap)
    return out.reshape(b, s, d)


def kernel(x, mem, g_mix, w_in, conv_dw, conv_dw_bias, conv_ln_g, conv_ln_b, w_conv_out, ssm_lambda_re, ssm_lambda_im, ssm_log_dt, ssm_b_re, ssm_b_im, ssm_c_re, ssm_c_im, ssm_d, w_ssm_glu, g_mem, w_mem_kv, w_mem_out, w_out, g_ffn, w_router_group, b_router_group, w_router_expert, b_router_expert, w_exp_gate, w_exp_up, w_exp_down, g_final):
    assert g_mix.shape[0] == 1, "the problem fixes one layer"
    l = 0
    return _layer(
        x, mem, g_mix[l], w_in[l], conv_dw[l], conv_dw_bias[l], conv_ln_g[l], conv_ln_b[l],
        w_conv_out[l], ssm_lambda_re[l], ssm_lambda_im[l], ssm_log_dt[l], ssm_b_re[l],
        ssm_b_im[l], ssm_c_re[l], ssm_c_im[l], ssm_d[l], w_ssm_glu[l], g_mem[l], w_mem_kv[l],
        w_mem_out[l], w_out[l], g_ffn[l], w_router_group[l], b_router_group[l],
        w_router_expert[l], b_router_expert[l], w_exp_gate[l], w_exp_up[l], w_exp_down[l], g_final)
```

```python
import functools

import jax
import jax.numpy as jnp
from jax import lax
from jax.experimental import pallas as pl
from jax.experimental.pallas import tpu as pltpu

F32 = jnp.float32
BF16 = jnp.bfloat16
EPS = 1e-6

LANES = 128
CHUNK_ROWS = 16
SSM_GROUP = 16
SSM_STATE = 64
SSM_BLOCK = LANES
CONV_KERNEL = 31
CONV_HALO = 32
HEADS = 4
HEAD_DIM = 128
MOE_GROUPS = 4
EXPERTS_PER_GROUP = 8
N_EXPERTS = MOE_GROUPS * EXPERTS_PER_GROUP
TOKEN_TILE = 512
MIX_TILE = 1024
COMBINE_TILES = 2
EXPERT_TILE = 512
CHUNKS_PER_ETILE = EXPERT_TILE // CHUNK_ROWS
VMEM_LIMIT = 56 * 1024 * 1024


def _rms(x, g):
    return x * lax.rsqrt(jnp.mean(x * x, axis=-1, keepdims=True) + EPS) * g


def _sigmoid(x):
    return 0.5 * jnp.tanh(0.5 * x) + 0.5


def _const_spec(shape):
    zeros = (0,) * len(shape)
    return pl.BlockSpec(shape, lambda *_: zeros, pipeline_mode=pl.Buffered(1))


def _kv_kernel(mem_ref, g_ref, w_ref, k_ref, v_ref):
    width = k_ref.shape[-1]
    mn = _rms(mem_ref[0], g_ref[...]).astype(BF16)
    kv = jnp.dot(mn, w_ref[...], preferred_element_type=F32)
    k_ref[0] = kv[:, :width].astype(BF16)
    v_ref[0] = kv[:, width:].astype(BF16)


def _kv_call(mem, g_mem, w_kv):
    b, m, d = mem.shape
    width = w_kv.shape[1] // 2
    return pl.pallas_call(
        _kv_kernel,
        grid=(b,),
        in_specs=[pl.BlockSpec((1, m, d), lambda i: (i, 0, 0)),
                  pl.BlockSpec((1, d), lambda i: (0, 0)),
                  pl.BlockSpec(w_kv.shape, lambda i: (0, 0))],
        out_specs=[pl.BlockSpec((1, m, width), lambda i: (i, 0, 0)),
                   pl.BlockSpec((1, m, width), lambda i: (i, 0, 0))],
        out_shape=[jax.ShapeDtypeStruct((b, m, width), BF16)] * 2,
        name="kv",
    )(mem, g_mem, w_kv)


def _ut_kernel(x_ref, g_ref, wt_ref, o_ref):
    h = _rms(x_ref[...], g_ref[...]).astype(BF16)
    ut = lax.dot_general(wt_ref[...], h, (((1,), (1,)), ((), ())), preferred_element_type=F32)
    ut = ut.reshape(ut.shape[0], ut.shape[1] // SSM_BLOCK, SSM_BLOCK)
    o_ref[...] = ut.reshape(o_ref.shape).astype(BF16)


def _ut_call(x2, g_mix, w_ssm_t):
    t, d = x2.shape
    c = w_ssm_t.shape[0]
    ts = 16 * SSM_BLOCK
    n_groups = c // SSM_GROUP
    return pl.pallas_call(
        _ut_kernel,
        grid=(t // ts,),
        in_specs=[pl.BlockSpec((ts, d), lambda i: (i, 0)),
                  pl.BlockSpec((1, d), lambda i: (0, 0)),
                  pl.BlockSpec((c, d), lambda i: (0, 0))],
        out_specs=pl.BlockSpec((n_groups, SSM_GROUP, ts // SSM_BLOCK, SSM_BLOCK),
                               lambda i: (0, 0, i, 0)),
        out_shape=jax.ShapeDtypeStruct((n_groups, SSM_GROUP, t // SSM_BLOCK, SSM_BLOCK), BF16),
        name="ut",
    )(x2, g_mix, w_ssm_t)


def _ssm_kernel(u_ref, kt_ref, w_ref, v_ref, a_ref, y_ref, acc_ref, z_ref, zs_ref, s_ref, slab_ref,
                *, n_batch):
    rows = u_ref.shape[2]
    n_blocks = rows // n_batch
    ri = lax.broadcasted_iota(jnp.int32, (SSM_BLOCK, SSM_BLOCK), 0)
    ci = lax.broadcasted_iota(jnp.int32, (SSM_BLOCK, SSM_BLOCK), 1)
    causal = ci >= ri

    n_pairs = SSM_GROUP // 2

    def build(cp, slot):
        for half in range(2):
            kt = kt_ref[0, 2 * cp + half]
            for c in range(SSM_GROUP):
                xb = jnp.broadcast_to(kt[c:c + 1, :], (SSM_BLOCK, SSM_BLOCK))
                toe = pltpu.roll(xb, 0, 1, stride=1, stride_axis=0)
                slab_ref[slot, half * SSM_BLOCK:(half + 1) * SSM_BLOCK, c * SSM_BLOCK:(c + 1) * SSM_BLOCK] = (
                    jnp.where(causal, toe, 0.0).astype(BF16))

    def apply(cp, slot, first):
        x2 = jnp.concatenate([u_ref[0, 2 * cp], u_ref[0, 2 * cp + 1]], axis=1)
        part = jnp.dot(x2, slab_ref[slot], preferred_element_type=F32)
        wrow = pl.multiple_of(cp * 2 * SSM_BLOCK, 2 * SSM_BLOCK)
        zpart = jnp.dot(x2, w_ref[0, pl.ds(wrow, 2 * SSM_BLOCK), :], preferred_element_type=F32)
        if first:
            acc_ref[...] = part
            z_ref[...] = zpart
        else:
            acc_ref[...] += part
            z_ref[...] += zpart

    build(0, 0)
    build(1, 1)
    apply(0, 0, True)

    def two(it, carry):
        cp = 2 * it + 1
        build(cp + 1, 0)
        apply(cp, 1, False)
        build(cp + 2, 1)
        apply(cp + 1, 0, False)
        return carry

    lax.fori_loop(0, (n_pairs - 2) // 2, two, 0)
    apply(n_pairs - 1, 1, False)

    a_full = a_ref[0, 0:1, :]
    a_swap = a_ref[0, 1:2, :]
    zs_ref[...] = pltpu.roll(z_ref[...], SSM_STATE, 1)
    st = jnp.zeros((n_batch, 2 * SSM_STATE), F32)
    sw = jnp.zeros((n_batch, 2 * SSM_STATE), F32)
    for blk in range(n_blocks):
        sl = pl.ds(blk, n_batch, stride=n_blocks)
        s_ref[sl, :] = st
        st, sw = (a_full * st + a_swap * sw + z_ref[sl, :],
                  a_full * sw - a_swap * st + zs_ref[sl, :])

    sb = s_ref[...].astype(BF16)
    y3 = jnp.stack([acc_ref[:, c * SSM_BLOCK:(c + 1) * SSM_BLOCK]
                    + jnp.dot(sb, v_ref[0, c], preferred_element_type=F32)
                    for c in range(SSM_GROUP)], axis=0)
    y_ref[...] = y3.reshape(y_ref.shape)


def _ssm_call(u4, kt, wz, vy, acoef, n_batch):
    g, c, rows, blk = u4.shape
    width = c * blk
    return pl.pallas_call(
        functools.partial(_ssm_kernel, n_batch=n_batch),
        grid=(g,),
        in_specs=[pl.BlockSpec((1, c, rows, blk), lambda i: (i, 0, 0, 0)),
                  pl.BlockSpec((1, c, c, blk), lambda i: (i, 0, 0, 0)),
                  pl.BlockSpec((1, width, 2 * SSM_STATE), lambda i: (i, 0, 0)),
                  pl.BlockSpec((1, c, 2 * SSM_STATE, blk), lambda i: (i, 0, 0, 0)),
                  pl.BlockSpec((1, 2, 2 * SSM_STATE), lambda i: (i, 0, 0))],
        out_specs=pl.BlockSpec((c, rows * blk), lambda i: (i, 0)),
        out_shape=jax.ShapeDtypeStruct((g * c, rows * blk), F32),
        scratch_shapes=[pltpu.VMEM((rows, width), F32),
                        pltpu.VMEM((rows, 2 * SSM_STATE), F32),
                        pltpu.VMEM((rows, 2 * SSM_STATE), F32),
                        pltpu.VMEM((rows, 2 * SSM_STATE), F32),
                        pltpu.VMEM((2, 2 * blk, width), BF16)],
        name="ssm",
    )(u4, kt, wz, vy, acoef)


def _ssm_tables(lam_re, lam_im, log_dt, b_re, b_im, c_re, c_im, d):
    hi = lax.Precision.HIGHEST
    g = lam_re.shape[0]
    dt = jnp.exp(log_dt)[:, None]
    er, ei = lam_re * dt, lam_im * dt
    cat = jnp.concatenate
    kk = jnp.arange(SSM_BLOCK + 1, dtype=F32)
    kdesc = jnp.arange(SSM_BLOCK - 1, -1, -1, dtype=F32)[None, :, None]
    ppk_m, ppk_a = jnp.exp(er[:, :, None] * kk), ei[:, :, None] * kk
    ppk_r, ppk_i = ppk_m * jnp.cos(ppk_a), ppk_m * jnp.sin(ppk_a)
    prev_m, prev_a = jnp.exp(kdesc * er[:, None, :]), kdesc * ei[:, None, :]
    prr, pir = prev_m * jnp.cos(prev_a), prev_m * jnp.sin(prev_a)
    nr, ni = ppk_r[:, :, 1] - 1.0, ppk_i[:, :, 1]
    den = lam_re * lam_re + lam_im * lam_im
    fr = (nr * lam_re + ni * lam_im) / den
    fi = (ni * lam_re - nr * lam_im) / den
    bbr = jnp.swapaxes(fr[:, :, None] * b_re - fi[:, :, None] * b_im, 1, 2)
    bbi = jnp.swapaxes(fr[:, :, None] * b_im + fi[:, :, None] * b_re, 1, 2)
    cb_r = c_re[:, None] * bbr[:, :, None, :] - c_im[:, None] * bbi[:, :, None, :]
    cb_i = c_re[:, None] * bbi[:, :, None, :] + c_im[:, None] * bbr[:, :, None, :]
    cb = cat([cb_r, -cb_i], axis=-1).reshape(g, SSM_GROUP * SSM_GROUP, 2 * SSM_STATE)
    pk = cat([ppk_r[:, :, :SSM_BLOCK], ppk_i[:, :, :SSM_BLOCK]], axis=1)
    kt = jnp.einsum("gmq,gqk->gmk", cb, pk, precision=hi).reshape(g, SSM_GROUP, SSM_GROUP, SSM_BLOCK)
    dmat = jnp.eye(SSM_GROUP, dtype=F32)[None] * d.reshape(g, 1, SSM_GROUP)
    kt = kt + dmat[..., None] * (jnp.arange(SSM_BLOCK) == 0).astype(F32)
    wz = (cat([prr, prr], -1)[:, None] * cat([bbr, bbi], -1)[:, :, None, :]
          + cat([pir, pir], -1)[:, None] * cat([-bbi, bbr], -1)[:, :, None, :])
    wz = wz.astype(BF16).reshape(g, SSM_GROUP * SSM_BLOCK, 2 * SSM_STATE)
    pt_r, pt_i = ppk_r[:, :, 1:], ppk_i[:, :, 1:]
    vy = (cat([c_re, -c_re], -1)[..., None] * cat([pt_r, pt_i], axis=1)[:, None]
          + cat([-c_im, -c_im], -1)[..., None] * cat([pt_i, pt_r], axis=1)[:, None]).astype(BF16)
    ar, ai = ppk_r[:, :, SSM_BLOCK], ppk_i[:, :, SSM_BLOCK]
    acoef = jnp.stack([cat([ar, ar], -1), cat([-ai, ai], -1)], axis=1)
    return kt, wz, vy, acoef


def _mix_kernel(x_ref, yt_ref, k_ref, v_ref, g_ref, wc_ref, wq_ref, wg_ref, dw_ref, dwb_ref,
                lng_ref, lnb_ref, wpw_ref, wglu_ref, wo_ref, wout_ref, o_ref, vext_ref, vsh_ref):
    ts = x_ref.shape[1]
    d = x_ref.shape[2]
    cw = dw_ref.shape[1]
    x = x_ref[0]
    h = _rms(x, g_ref[...]).astype(BF16)

    ci = jnp.dot(h, wc_ref[...], preferred_element_type=F32)
    v = ci[:, :cw] * _sigmoid(ci[:, cw:])
    q = jnp.dot(h, wq_ref[...], preferred_element_type=F32)

    @pl.when(pl.program_id(1) == 0)
    def _():
        vext_ref[0:CONV_HALO, :] = jnp.zeros((CONV_HALO, cw), F32)

    vext_ref[CONV_HALO:CONV_HALO + ts, :] = v
    acc = jnp.broadcast_to(dwb_ref[...], (ts, cw))
    first = CONV_HALO - (CONV_KERNEL - 1)
    for phase in range(8):
        offs = [first + k for k in range(CONV_KERNEL) if (first + k) % 8 == phase]
        if not offs:
            continue
        span = offs[-1] - offs[0] + ts
        vsh_ref[0:span, :] = vext_ref[offs[0]:offs[0] + span, :]
        for off in offs:
            acc = acc + dw_ref[off - first:off - first + 1, :] * vsh_ref[off - offs[0]:off - offs[0] + ts, :]
    vext_ref[0:CONV_HALO, :] = vext_ref[ts:ts + CONV_HALO, :]
    mu = jnp.mean(acc, axis=-1, keepdims=True)
    xc = acc - mu
    var = jnp.mean(xc * xc, axis=-1, keepdims=True)
    ln = xc * lax.rsqrt(var + EPS) * lng_ref[...] + lnb_ref[...]
    sw = ln * _sigmoid(ln)
    y_conv = jnp.dot(sw.astype(BF16), wpw_ref[...], preferred_element_type=F32)
    merged = _sigmoid(jnp.dot(h, wg_ref[:, 0:d], preferred_element_type=F32)) * y_conv

    gy = jax.nn.gelu(yt_ref[...]).astype(BF16)
    z = lax.dot_general(gy, wglu_ref[...], (((0,), (0,)), ((), ())), preferred_element_type=F32)
    y_ssm = z[:, :d] * _sigmoid(z[:, d:])
    merged = merged + _sigmoid(jnp.dot(h, wg_ref[:, d:2 * d], preferred_element_type=F32)) * y_ssm

    kk = k_ref[0]
    vv = v_ref[0]
    outs = []
    for hd in range(HEADS):
        sl = slice(hd * HEAD_DIM, (hd + 1) * HEAD_DIM)
        s = lax.dot_general(q[:, sl].astype(BF16), kk[:, sl], (((1,), (1,)), ((), ())),
                            preferred_element_type=F32) * (HEAD_DIM ** -0.5)
        p = jnp.exp(s - jnp.max(s, axis=-1, keepdims=True))
        den = jnp.sum(p, axis=-1, keepdims=True)
        o = jnp.dot(p.astype(BF16), vv[:, sl], preferred_element_type=F32) / den
        outs.append(o.astype(BF16))
    y_mem = jnp.dot(jnp.concatenate(outs, axis=1), wo_ref[...], preferred_element_type=F32)
    merged = merged + _sigmoid(jnp.dot(h, wg_ref[:, 2 * d:3 * d], preferred_element_type=F32)) * y_mem

    o_ref[0] = x + jnp.dot(merged.astype(BF16), wout_ref[...], preferred_element_type=F32)


def _mix_call(x, yt, kmem, vmem, g_mix, wc, wq, wg, dw, dwb, lng, lnb, wpw, wglu, wo, wout):
    b, s, d = x.shape
    ts = MIX_TILE
    nst = s // ts
    cw = dw.shape[1]
    m = kmem.shape[1]
    consts = [g_mix, wc, wq, wg, dw, dwb, lng, lnb, wpw, wglu, wo, wout]
    return pl.pallas_call(
        _mix_kernel,
        grid=(b, nst),
        in_specs=[pl.BlockSpec((1, ts, d), lambda i, j: (i, j, 0)),
                  pl.BlockSpec((yt.shape[0], ts), lambda i, j: (0, i * nst + j)),
                  pl.BlockSpec((1, m, kmem.shape[2]), lambda i, j: (i, 0, 0)),
                  pl.BlockSpec((1, m, vmem.shape[2]), lambda i, j: (i, 0, 0))]
                 + [_const_spec(c.shape) for c in consts],
        out_specs=pl.BlockSpec((1, ts, d), lambda i, j: (i, j, 0)),
        out_shape=jax.ShapeDtypeStruct((b, s, d), F32),
        scratch_shapes=[pltpu.VMEM((ts + CONV_HALO, cw), F32),
                        pltpu.VMEM((ts + CONV_HALO, cw), F32)],
        compiler_params=pltpu.CompilerParams(
            dimension_semantics=("arbitrary", "arbitrary"), vmem_limit_bytes=VMEM_LIMIT),
        name="mix",
    )(x, yt, kmem, vmem, *consts)


def _route_kernel(x_ref, g_ref, wr_ref, br_ref, xs_ref, info_ref, nch_ref):
    ts = x_ref.shape[0]
    cap = xs_ref.shape[0] * CHUNK_ROWS
    h2 = _rms(x_ref[...], g_ref[...])
    hb = h2.astype(BF16)
    hl = (h2 - hb.astype(F32)).astype(BF16)
    hw = jnp.dot(hb, wr_ref[...], preferred_element_type=F32)
    logits = (hw[:, :LANES] + hw[:, LANES:]
              + jnp.dot(hl, wr_ref[:, :LANES], preferred_element_type=F32)) + br_ref[...]
    lt = logits.T
    le = lt[0:N_EXPERTS]
    lg = lt[N_EXPERTS:N_EXPERTS + 8]
    neg = jnp.float32(-1e30)
    big = jnp.float32(1e9)
    g_f = lax.broadcasted_iota(jnp.int32, (8, ts), 0).astype(F32)
    e_f = lax.broadcasted_iota(jnp.int32, (N_EXPERTS, ts), 0).astype(F32)

    gmask = g_f < MOE_GROUPS
    gmax = jnp.max(jnp.where(gmask, lg, neg), axis=0, keepdims=True)
    gidx = jnp.min(jnp.where(gmask & (lg == gmax), g_f, big), axis=0, keepdims=True)
    gsum = jnp.sum(jnp.where(gmask, jnp.exp(jnp.minimum(lg - gmax, 0.0)), 0.0), axis=0, keepdims=True)
    p_top = 1.0 / gsum
    emask = jnp.floor(e_f * (1.0 / EXPERTS_PER_GROUP)) == gidx
    m1 = jnp.max(jnp.where(emask, le, neg), axis=0, keepdims=True)
    i1 = jnp.min(jnp.where(emask & (le == m1), e_f, big), axis=0, keepdims=True)
    emask2 = emask & (e_f != i1)
    m2 = jnp.max(jnp.where(emask2, le, neg), axis=0, keepdims=True)
    i2 = jnp.min(jnp.where(emask2 & (le == m2), e_f, big), axis=0, keepdims=True)
    r = jnp.exp(m2 - m1)
    w1 = p_top / (1.0 + r)
    w2 = p_top * r / (1.0 + r)

    sel1 = e_f == i1
    sel2 = e_f == i2
    occ = jnp.where(sel1 | sel2, 1.0, 0.0)
    tr = lax.broadcasted_iota(jnp.int32, (ts, ts), 0)
    tc = lax.broadcasted_iota(jnp.int32, (ts, ts), 1)
    earlier = jnp.where(tr < tc, 1.0, 0.0).astype(BF16)
    rank = jnp.dot(occ.astype(BF16), earlier, preferred_element_type=F32)
    cnt = jnp.sum(occ, axis=1, keepdims=True)
    nch = jnp.floor((cnt + (CHUNK_ROWS - 1)) * (1.0 / CHUNK_ROWS))
    er = lax.broadcasted_iota(jnp.int32, (N_EXPERTS, N_EXPERTS), 0)
    ec = lax.broadcasted_iota(jnp.int32, (N_EXPERTS, N_EXPERTS), 1)
    lower = jnp.where(ec < er, 1.0, 0.0).astype(BF16)
    nch_b = jnp.broadcast_to(nch, (N_EXPERTS, LANES))
    start = jnp.dot(lower, nch_b.astype(BF16), preferred_element_type=F32)[:, 0:1] * CHUNK_ROWS
    slot = start + rank
    pos1 = jnp.sum(jnp.where(sel1, slot, 0.0), axis=0, keepdims=True)
    pos2 = jnp.sum(jnp.where(sel2, slot, 0.0), axis=0, keepdims=True)

    rowid = lax.broadcasted_iota(jnp.int32, (cap, ts), 0)
    p = jnp.where((rowid == pos1.astype(jnp.int32)) | (rowid == pos2.astype(jnp.int32)), 1.0, 0.0)
    xs = jnp.dot(p.astype(BF16), hb, preferred_element_type=F32).astype(BF16)
    xs_ref[...] = xs.reshape(xs_ref.shape)
    sub = lax.broadcasted_iota(jnp.int32, (LANES, ts), 0)
    info_t = jnp.where(sub == 0, pos1, jnp.where(sub == 1, pos2,
                       jnp.where(sub == 2, w1, jnp.where(sub == 3, w2, 0.0))))
    info_ref[...] = info_t.T
    nch_ref[0] = jnp.concatenate([nch_b, jnp.zeros((LANES - N_EXPERTS, LANES), F32)], axis=0).T[0:8, :]


def _route_call(x1, g_ffn, wr, br, cap):
    t, d = x1.shape
    ts = TOKEN_TILE
    nt = t // ts
    return pl.pallas_call(
        _route_kernel,
        grid=(nt,),
        in_specs=[pl.BlockSpec((ts, d), lambda i: (i, 0)),
                  pl.BlockSpec((1, d), lambda i: (0, 0)),
                  pl.BlockSpec(wr.shape, lambda i: (0, 0)),
                  pl.BlockSpec((1, LANES), lambda i: (0, 0))],
        out_specs=[pl.BlockSpec((cap // CHUNK_ROWS, CHUNK_ROWS, d), lambda i: (i, 0, 0)),
                   pl.BlockSpec((ts, LANES), lambda i: (i, 0)),
                   pl.BlockSpec((1, 8, LANES), lambda i: (i, 0, 0))],
        out_shape=[jax.ShapeDtypeStruct((nt * cap // CHUNK_ROWS, CHUNK_ROWS, d), BF16),
                   jax.ShapeDtypeStruct((t, LANES), F32),
                   jax.ShapeDtypeStruct((nt, 8, LANES), F32)],
        compiler_params=pltpu.CompilerParams(vmem_limit_bytes=VMEM_LIMIT),
        name="route",
    )(x1, g_ffn, wr, br)


def _expert_kernel(te_ref, nu_ref, src_ref, xs_ref, wg_ref, wu_ref, wd_ref, o_ref, xbuf_ref, sem_ref):
    i = pl.program_id(0)
    n_used = nu_ref[0]
    slot = i % 2

    def chunk_copy(tile, c, buf):
        return pltpu.make_async_copy(
            xs_ref.at[src_ref[tile * CHUNKS_PER_ETILE + c]], xbuf_ref.at[buf, c], sem_ref.at[buf])

    def gather(tile, buf):
        for c in range(CHUNKS_PER_ETILE):
            chunk_copy(tile, c, buf).start(priority=c % 2)

    def drain(tile, buf):
        del tile
        pltpu.make_async_copy(xs_ref.at[pl.ds(0, CHUNKS_PER_ETILE)], xbuf_ref.at[buf], sem_ref.at[buf]).wait()

    @pl.when(i == 0)
    def _():
        gather(0, 0)

    @pl.when(i < n_used)
    def _():
        nxt = jnp.minimum(i + 1, n_used - 1)
        gather(nxt, 1 - slot)
        drain(i, slot)
        x = xbuf_ref[slot].reshape(EXPERT_TILE, xbuf_ref.shape[-1])
        gate = jnp.dot(x, wg_ref[0].astype(BF16), preferred_element_type=F32)
        up = jnp.dot(x, wu_ref[0].astype(BF16), preferred_element_type=F32)
        act = (gate * _sigmoid(gate) * up).astype(BF16)
        y = jnp.dot(act, wd_ref[0].astype(BF16), preferred_element_type=F32).astype(BF16)
        o_ref[...] = y.reshape(o_ref.shape)

        @pl.when(i == n_used - 1)
        def _():
            drain(nxt, 1 - slot)

    @pl.when(i >= n_used)
    def _():
        o_ref[...] = jnp.zeros_like(o_ref)


def _expert_call(tile_expert, n_used, chunk_src, xs, wg, wu, wd):
    d = xs.shape[-1]
    de = wg.shape[2]
    tm = EXPERT_TILE
    nt = tile_expert.shape[0]

    def w_map(i, te, nu, src):
        return (te[jnp.maximum(jnp.minimum(i, nu[0] - 1), 0)], 0, 0)

    return pl.pallas_call(
        _expert_kernel,
        grid_spec=pltpu.PrefetchScalarGridSpec(
            num_scalar_prefetch=3,
            grid=(nt,),
            in_specs=[pl.BlockSpec(memory_space=pl.ANY),
                      pl.BlockSpec((1, d, de), w_map),
                      pl.BlockSpec((1, d, de), w_map),
                      pl.BlockSpec((1, de, d), w_map)],
            out_specs=pl.BlockSpec((CHUNKS_PER_ETILE, CHUNK_ROWS, d), lambda i, te, nu, src: (i, 0, 0)),
            scratch_shapes=[pltpu.VMEM((2, CHUNKS_PER_ETILE, CHUNK_ROWS, d), BF16),
                            pltpu.SemaphoreType.DMA((2,))]),
        out_shape=jax.ShapeDtypeStruct((nt * CHUNKS_PER_ETILE, CHUNK_ROWS, d), BF16),
        compiler_params=pltpu.CompilerParams(dimension_semantics=("arbitrary",)),
        name="expert",
    )(tile_expert, n_used, chunk_src, xs, wg, wu, wd)


def _combine_kernel(dst_ref, x_ref, info_ref, g_ref, ys_ref, o_ref, ybuf_ref, sem_ref):
    i = pl.program_id(0)
    n_tiles = pl.num_programs(0)
    ts = x_ref.shape[0] // COMBINE_TILES
    n_chunks = ybuf_ref.shape[1]
    cap = n_chunks // COMBINE_TILES * CHUNK_ROWS
    slot = i % 2

    def chunk_copy(tile, k, buf):
        return pltpu.make_async_copy(ys_ref.at[dst_ref[tile * n_chunks + k]], ybuf_ref.at[buf, k], sem_ref.at[buf])

    def gather(tile, buf):
        for k in range(n_chunks):
            chunk_copy(tile, k, buf).start(priority=k % 2)

    def drain(tile, buf):
        del tile
        pltpu.make_async_copy(ys_ref.at[pl.ds(0, n_chunks)], ybuf_ref.at[buf], sem_ref.at[buf]).wait()

    @pl.when(i == 0)
    def _():
        gather(0, 0)

    nxt = jnp.minimum(i + 1, n_tiles - 1)
    gather(nxt, 1 - slot)
    drain(i, slot)

    rowid = lax.broadcasted_iota(jnp.int32, (ts, cap), 1)
    for sub in range(COMBINE_TILES):
        rows = slice(sub * ts, (sub + 1) * ts)
        info = info_ref[rows, :]
        pos1 = info[:, 0:1].astype(jnp.int32)
        pos2 = info[:, 1:2].astype(jnp.int32)
        w1 = info[:, 2:3]
        w2 = info[:, 3:4]
        ys = ybuf_ref[slot, sub * (n_chunks // COMBINE_TILES):(sub + 1) * (n_chunks // COMBINE_TILES)]
        ys = ys.reshape(cap, ybuf_ref.shape[-1])
        pw = jnp.where(rowid == pos1, w1, jnp.where(rowid == pos2, w2, 0.0)).astype(BF16)
        y = jnp.dot(pw, ys, preferred_element_type=F32)
        o_ref[rows, :] = _rms(x_ref[rows, :] + y, g_ref[...])

    @pl.when(i == n_tiles - 1)
    def _():
        drain(nxt, 1 - slot)


def _combine_call(chunk_dst, x1, info, g_final, ys_sorted, cap):
    t, d = x1.shape
    ts = TOKEN_TILE * COMBINE_TILES
    return pl.pallas_call(
        _combine_kernel,
        grid_spec=pltpu.PrefetchScalarGridSpec(
            num_scalar_prefetch=1,
            grid=(t // ts,),
            in_specs=[pl.BlockSpec((ts, d), lambda i, dst: (i, 0)),
                      pl.BlockSpec((ts, LANES), lambda i, dst: (i, 0)),
                      pl.BlockSpec((1, d), lambda i, dst: (0, 0)),
                      pl.BlockSpec(memory_space=pl.ANY)],
            out_specs=pl.BlockSpec((ts, d), lambda i, dst: (i, 0)),
            scratch_shapes=[pltpu.VMEM((2, COMBINE_TILES * cap // CHUNK_ROWS, CHUNK_ROWS, d), BF16),
                            pltpu.SemaphoreType.DMA((2,))]),
        out_shape=jax.ShapeDtypeStruct((t, d), F32),
        compiler_params=pltpu.CompilerParams(dimension_semantics=("arbitrary",),
                                             vmem_limit_bytes=VMEM_LIMIT),
        name="combine",
    )(chunk_dst, x1, info, g_final, ys_sorted)


def _run_tables(nch, cpt, n_etiles):
    n_tiles, n_exp = nch.shape
    per = CHUNKS_PER_ETILE
    tcum = jnp.cumsum(nch, axis=1)
    toff = tcum - nch
    ecum = jnp.cumsum(nch, axis=0)
    etot = ecum[-1]
    eseg = -(-etot // per) * per
    segcum = jnp.cumsum(eseg)
    ebase = segcum - eseg
    eoff = ebase[None, :] + ecum - nch
    n_used = (segcum[-1] // per).reshape(1)

    k = jnp.arange(cpt, dtype=jnp.int32)[None, :, None]
    in_run = (k >= toff[:, None, :]) & (k < tcum[:, None, :])
    chunk_dst = jnp.sum(jnp.where(in_run, (eoff - toff)[:, None, :] + k, 0), axis=-1)

    first = jnp.arange(n_etiles, dtype=jnp.int32)[:, None] * per
    owner = (first >= ebase[None, :]) & (first < segcum[None, :])
    tile_expert = jnp.sum(jnp.where(owner, jnp.arange(n_exp, dtype=jnp.int32)[None, :], 0), axis=-1)

    pick = lambda a: jnp.sum(jnp.where(owner[:, None, :], a[None, :, :], 0), axis=-1)
    run_lo, run_n = pick(eoff), pick(nch)
    shift = pick(jnp.arange(n_tiles, dtype=jnp.int32)[:, None] * cpt + toff - eoff)
    c = (first + jnp.arange(per, dtype=jnp.int32)[None, :])[:, :, None]
    hit = (c >= run_lo[:, None, :]) & (c < (run_lo + run_n)[:, None, :])
    chunk_src = jnp.sum(jnp.where(hit, shift[:, None, :] + c, 0), axis=-1)
    i32 = lambda a: a.reshape(-1).astype(jnp.int32)
    return i32(chunk_dst), i32(tile_expert), i32(n_used), i32(chunk_src)


def _layer(x, mem, g_mix, w_in, conv_dw, conv_dw_bias, conv_ln_g, conv_ln_b, w_conv_out,
           lam_re, lam_im, log_dt, b_re, b_im, c_re, c_im, ssm_d, w_ssm_glu, g_mem, w_mem_kv,
           w_mem_out, w_out, g_ffn, w_rg, b_rg, w_re, b_re_, w_eg, w_eu, w_ed, g_final):
    b, s, d = x.shape
    t = b * s
    cw = conv_dw.shape[1]
    sw = ssm_d.shape[0]
    qw = w_mem_out.shape[0]
    n_groups = sw // SSM_GROUP
    row = lambda a: a.reshape(1, -1)

    o0, o1, o2 = 2 * cw, 2 * cw + sw, 2 * cw + sw + qw
    wc = w_in[:, :o0].astype(BF16)
    w_ssm_t = w_in[:, o0:o1].T.astype(BF16)
    wq = w_in[:, o1:o2].astype(BF16)
    wg = w_in[:, o2:].astype(BF16)

    kmem, vmem = _kv_call(mem, row(g_mem), w_mem_kv.astype(BF16))

    ut = _ut_call(x.reshape(t, d), row(g_mix), w_ssm_t)
    kt, wz, vy, acoef = _ssm_tables(lam_re, lam_im, log_dt, b_re, b_im, c_re, c_im, ssm_d)
    yt = _ssm_call(ut, kt, wz, vy, acoef, b)

    dw = jnp.concatenate([conv_dw, jnp.zeros((CONV_HALO - CONV_KERNEL, cw), F32)], axis=0)
    x1 = _mix_call(x, yt, kmem, vmem, row(g_mix), wc, wq, wg, dw, row(conv_dw_bias),
                   row(conv_ln_g), row(conv_ln_b), w_conv_out.astype(BF16),
                   w_ssm_glu.astype(BF16), w_mem_out.astype(BF16), w_out.astype(BF16))
    x1 = x1.reshape(t, d)

    pad = LANES - N_EXPERTS - MOE_GROUPS
    wr = jnp.concatenate([w_re, w_rg, jnp.zeros((d, pad), F32)], axis=1)
    wr_hi = wr.astype(BF16)
    wr = jnp.concatenate([wr_hi, (wr - wr_hi.astype(F32)).astype(BF16)], axis=1)
    br = jnp.concatenate([b_re_, b_rg, jnp.zeros((pad,), F32)]).reshape(1, LANES)
    n_tiles = t // TOKEN_TILE
    max_tile_chunks = (2 * TOKEN_TILE + N_EXPERTS * (CHUNK_ROWS - 1)) // CHUNK_ROWS
    cpt = -(-max_tile_chunks // 8) * 8
    cap = cpt * CHUNK_ROWS
    xs_tiles, info, nch_f = _route_call(x1, row(g_ffn), wr, br, cap)

    max_chunks = n_tiles * max_tile_chunks + N_EXPERTS * (CHUNKS_PER_ETILE - 1)
    n_etiles = -(-max_chunks // CHUNKS_PER_ETILE)
    nch = nch_f[:, 0, :N_EXPERTS].astype(jnp.int32)
    chunk_dst, tile_expert, n_used, chunk_src = _run_tables(nch, cpt, n_etiles)

    ys_sorted = _expert_call(tile_expert, n_used, chunk_src, xs_tiles, w_eg, w_eu, w_ed)
    out = _combine_call(chunk_dst, x1, info, row(g_final), ys_sorted, cap)
    return out.reshape(b, s, d)


def kernel(x, mem, g_mix, w_in, conv_dw, conv_dw_bias, conv_ln_g, conv_ln_b, w_conv_out, ssm_lambda_re, ssm_lambda_im, ssm_log_dt, ssm_b_re, ssm_b_im, ssm_c_re, ssm_c_im, ssm_d, w_ssm_glu, g_mem, w_mem_kv, w_mem_out, w_out, g_ffn, w_router_group, b_router_group, w_router_expert, b_router_expert, w_exp_gate, w_exp_up, w_exp_down, g_final):
    assert g_mix.shape[0] == 1, "the problem fixes one layer"
    l = 0
    return _layer(
        x, mem, g_mix[l], w_in[l], conv_dw[l], conv_dw_bias[l], conv_ln_g[l], conv_ln_b[l],
        w_conv_out[l], ssm_lambda_re[l], ssm_lambda_im[l], ssm_log_dt[l], ssm_b_re[l],
        ssm_b_im[l], ssm_c_re[l], ssm_c_im[l], ssm_d[l], w_ssm_glu[l], g_mem[l], w_mem_kv[l],
        w_mem_out[l], w_out[l], g_ffn[l], w_router_group[l], b_router_group[l],
        w_router_expert[l], b_router_expert[l], w_exp_gate[l], w_exp_up[l], w_exp_down[l], g_final)
```

```python
import functools

import jax
import jax.numpy as jnp
from jax import lax
from jax.experimental import pallas as pl
from jax.experimental.pallas import tpu as pltpu

F32 = jnp.float32
BF16 = jnp.bfloat16
EPS = 1e-6

LANES = 128
CHUNK_ROWS = 16
SSM_GROUP = 16
SSM_STATE = 64
SSM_BLOCK = LANES
CONV_KERNEL = 31
CONV_HALO = 32
HEADS = 4
HEAD_DIM = 128
MOE_GROUPS = 4
EXPERTS_PER_GROUP = 8
N_EXPERTS = MOE_GROUPS * EXPERTS_PER_GROUP
TOKEN_TILE = 512
MIX_TILE = 1024
COMBINE_TILES = 2
ROUTE_TILES = 2
EXPERT_TILE = 512
CHUNKS_PER_ETILE = EXPERT_TILE // CHUNK_ROWS
VMEM_LIMIT = 56 * 1024 * 1024


def _rms(x, g):
    return x * lax.rsqrt(jnp.mean(x * x, axis=-1, keepdims=True) + EPS) * g


def _sigmoid(x):
    return 0.5 * jnp.tanh(0.5 * x) + 0.5


def _const_spec(shape):
    zeros = (0,) * len(shape)
    return pl.BlockSpec(shape, lambda *_: zeros, pipeline_mode=pl.Buffered(1))


def _kv_kernel(mem_ref, g_ref, w_ref, k_ref, v_ref):
    width = k_ref.shape[-1]
    mn = _rms(mem_ref[0], g_ref[...]).astype(BF16)
    kv = jnp.dot(mn, w_ref[...], preferred_element_type=F32)
    k_ref[0] = kv[:, :width].astype(BF16)
    v_ref[0] = kv[:, width:].astype(BF16)


def _kv_call(mem, g_mem, w_kv):
    b, m, d = mem.shape
    width = w_kv.shape[1] // 2
    return pl.pallas_call(
        _kv_kernel,
        grid=(b,),
        in_specs=[pl.BlockSpec((1, m, d), lambda i: (i, 0, 0)),
                  pl.BlockSpec((1, d), lambda i: (0, 0)),
                  pl.BlockSpec(w_kv.shape, lambda i: (0, 0))],
        out_specs=[pl.BlockSpec((1, m, width), lambda i: (i, 0, 0)),
                   pl.BlockSpec((1, m, width), lambda i: (i, 0, 0))],
        out_shape=[jax.ShapeDtypeStruct((b, m, width), BF16)] * 2,
        name="kv",
    )(mem, g_mem, w_kv)


def _ut_kernel(x_ref, g_ref, wt_ref, o_ref):
    h = _rms(x_ref[...], g_ref[...]).astype(BF16)
    ut = lax.dot_general(wt_ref[...], h, (((1,), (1,)), ((), ())), preferred_element_type=F32)
    ut = ut.reshape(ut.shape[0], ut.shape[1] // SSM_BLOCK, SSM_BLOCK)
    o_ref[...] = ut.reshape(o_ref.shape).astype(BF16)


def _ut_call(x2, g_mix, w_ssm_t):
    t, d = x2.shape
    c = w_ssm_t.shape[0]
    ts = 16 * SSM_BLOCK
    n_groups = c // SSM_GROUP
    return pl.pallas_call(
        _ut_kernel,
        grid=(t // ts,),
        in_specs=[pl.BlockSpec((ts, d), lambda i: (i, 0)),
                  pl.BlockSpec((1, d), lambda i: (0, 0)),
                  pl.BlockSpec((c, d), lambda i: (0, 0))],
        out_specs=pl.BlockSpec((n_groups, SSM_GROUP, ts // SSM_BLOCK, SSM_BLOCK),
                               lambda i: (0, 0, i, 0)),
        out_shape=jax.ShapeDtypeStruct((n_groups, SSM_GROUP, t // SSM_BLOCK, SSM_BLOCK), BF16),
        name="ut",
    )(x2, g_mix, w_ssm_t)


def _ssm_kernel(u_ref, kt_ref, w_ref, v_ref, a_ref, y_ref, acc_ref, z_ref, zs_ref, s_ref, slab_ref,
                *, n_batch):
    rows = u_ref.shape[2]
    n_blocks = rows // n_batch
    ri = lax.broadcasted_iota(jnp.int32, (SSM_BLOCK, SSM_BLOCK), 0)
    ci = lax.broadcasted_iota(jnp.int32, (SSM_BLOCK, SSM_BLOCK), 1)
    causal = ci >= ri

    n_pairs = SSM_GROUP // 2

    def build(cp, slot):
        for half in range(2):
            kt = kt_ref[0, 2 * cp + half]
            for c in range(SSM_GROUP):
                xb = jnp.broadcast_to(kt[c:c + 1, :], (SSM_BLOCK, SSM_BLOCK))
                toe = pltpu.roll(xb, 0, 1, stride=1, stride_axis=0)
                slab_ref[slot, half * SSM_BLOCK:(half + 1) * SSM_BLOCK, c * SSM_BLOCK:(c + 1) * SSM_BLOCK] = (
                    jnp.where(causal, toe, 0.0).astype(BF16))

    def apply(cp, slot, first):
        x2 = jnp.concatenate([u_ref[0, 2 * cp], u_ref[0, 2 * cp + 1]], axis=1)
        part = jnp.dot(x2, slab_ref[slot], preferred_element_type=F32)
        wrow = pl.multiple_of(cp * 2 * SSM_BLOCK, 2 * SSM_BLOCK)
        zpart = jnp.dot(x2, w_ref[0, pl.ds(wrow, 2 * SSM_BLOCK), :], preferred_element_type=F32)
        if first:
            acc_ref[...] = part
            z_ref[...] = zpart
        else:
            acc_ref[...] += part
            z_ref[...] += zpart

    build(0, 0)
    build(1, 1)
    apply(0, 0, True)

    def two(it, carry):
        cp = 2 * it + 1
        build(cp + 1, 0)
        apply(cp, 1, False)
        build(cp + 2, 1)
        apply(cp + 1, 0, False)
        return carry

    lax.fori_loop(0, (n_pairs - 2) // 2, two, 0)
    apply(n_pairs - 1, 1, False)

    a_full = a_ref[0, 0:1, :]
    a_swap = a_ref[0, 1:2, :]
    zs_ref[...] = pltpu.roll(z_ref[...], SSM_STATE, 1)
    st = jnp.zeros((n_batch, 2 * SSM_STATE), F32)
    sw = jnp.zeros((n_batch, 2 * SSM_STATE), F32)
    for blk in range(n_blocks):
        sl = pl.ds(blk, n_batch, stride=n_blocks)
        s_ref[sl, :] = st
        st, sw = (a_full * st + a_swap * sw + z_ref[sl, :],
                  a_full * sw - a_swap * st + zs_ref[sl, :])

    sb = s_ref[...].astype(BF16)
    y3 = jnp.stack([acc_ref[:, c * SSM_BLOCK:(c + 1) * SSM_BLOCK]
                    + jnp.dot(sb, v_ref[0, c], preferred_element_type=F32)
                    for c in range(SSM_GROUP)], axis=0)
    y_ref[...] = y3.reshape(y_ref.shape)


def _ssm_call(u4, kt, wz, vy, acoef, n_batch):
    g, c, rows, blk = u4.shape
    width = c * blk
    return pl.pallas_call(
        functools.partial(_ssm_kernel, n_batch=n_batch),
        grid=(g,),
        in_specs=[pl.BlockSpec((1, c, rows, blk), lambda i: (i, 0, 0, 0)),
                  pl.BlockSpec((1, c, c, blk), lambda i: (i, 0, 0, 0)),
                  pl.BlockSpec((1, width, 2 * SSM_STATE), lambda i: (i, 0, 0)),
                  pl.BlockSpec((1, c, 2 * SSM_STATE, blk), lambda i: (i, 0, 0, 0)),
                  pl.BlockSpec((1, 2, 2 * SSM_STATE), lambda i: (i, 0, 0))],
        out_specs=pl.BlockSpec((c, rows * blk), lambda i: (i, 0)),
        out_shape=jax.ShapeDtypeStruct((g * c, rows * blk), F32),
        scratch_shapes=[pltpu.VMEM((rows, width), F32),
                        pltpu.VMEM((rows, 2 * SSM_STATE), F32),
                        pltpu.VMEM((rows, 2 * SSM_STATE), F32),
                        pltpu.VMEM((rows, 2 * SSM_STATE), F32),
                        pltpu.VMEM((2, 2 * blk, width), BF16)],
        name="ssm",
    )(u4, kt, wz, vy, acoef)


def _ssm_tables(lam_re, lam_im, log_dt, b_re, b_im, c_re, c_im, d):
    hi = lax.Precision.HIGHEST
    g = lam_re.shape[0]
    dt = jnp.exp(log_dt)[:, None]
    er, ei = lam_re * dt, lam_im * dt
    cat = jnp.concatenate
    kk = jnp.arange(SSM_BLOCK + 1, dtype=F32)
    kdesc = jnp.arange(SSM_BLOCK - 1, -1, -1, dtype=F32)[None, :, None]
    ppk_m, ppk_a = jnp.exp(er[:, :, None] * kk), ei[:, :, None] * kk
    ppk_r, ppk_i = ppk_m * jnp.cos(ppk_a), ppk_m * jnp.sin(ppk_a)
    prev_m, prev_a = jnp.exp(kdesc * er[:, None, :]), kdesc * ei[:, None, :]
    prr, pir = prev_m * jnp.cos(prev_a), prev_m * jnp.sin(prev_a)
    nr, ni = ppk_r[:, :, 1] - 1.0, ppk_i[:, :, 1]
    den = lam_re * lam_re + lam_im * lam_im
    fr = (nr * lam_re + ni * lam_im) / den
    fi = (ni * lam_re - nr * lam_im) / den
    bbr = jnp.swapaxes(fr[:, :, None] * b_re - fi[:, :, None] * b_im, 1, 2)
    bbi = jnp.swapaxes(fr[:, :, None] * b_im + fi[:, :, None] * b_re, 1, 2)
    cb_r = c_re[:, None] * bbr[:, :, None, :] - c_im[:, None] * bbi[:, :, None, :]
    cb_i = c_re[:, None] * bbi[:, :, None, :] + c_im[:, None] * bbr[:, :, None, :]
    cb = cat([cb_r, -cb_i], axis=-1).reshape(g, SSM_GROUP * SSM_GROUP, 2 * SSM_STATE)
    pk = cat([ppk_r[:, :, :SSM_BLOCK], ppk_i[:, :, :SSM_BLOCK]], axis=1)
    kt = jnp.einsum("gmq,gqk->gmk", cb, pk, precision=hi).reshape(g, SSM_GROUP, SSM_GROUP, SSM_BLOCK)
    dmat = jnp.eye(SSM_GROUP, dtype=F32)[None] * d.reshape(g, 1, SSM_GROUP)
    kt = kt + dmat[..., None] * (jnp.arange(SSM_BLOCK) == 0).astype(F32)
    wz = (cat([prr, prr], -1)[:, None] * cat([bbr, bbi], -1)[:, :, None, :]
          + cat([pir, pir], -1)[:, None] * cat([-bbi, bbr], -1)[:, :, None, :])
    wz = wz.astype(BF16).reshape(g, SSM_GROUP * SSM_BLOCK, 2 * SSM_STATE)
    pt_r, pt_i = ppk_r[:, :, 1:], ppk_i[:, :, 1:]
    vy = (cat([c_re, -c_re], -1)[..., None] * cat([pt_r, pt_i], axis=1)[:, None]
          + cat([-c_im, -c_im], -1)[..., None] * cat([pt_i, pt_r], axis=1)[:, None]).astype(BF16)
    ar, ai = ppk_r[:, :, SSM_BLOCK], ppk_i[:, :, SSM_BLOCK]
    acoef = jnp.stack([cat([ar, ar], -1), cat([-ai, ai], -1)], axis=1)
    return kt, wz, vy, acoef


def _mix_kernel(x_ref, yt_ref, k_ref, v_ref, g_ref, wc_ref, wq_ref, wg_ref, dw_ref, dwb_ref,
                lng_ref, lnb_ref, wpw_ref, wglu_ref, wo_ref, wout_ref, o_ref, vext_ref, vsh_ref):
    ts = x_ref.shape[1]
    d = x_ref.shape[2]
    cw = dw_ref.shape[1]
    x = x_ref[0]
    h = _rms(x, g_ref[...]).astype(BF16)

    ci = jnp.dot(h, wc_ref[...], preferred_element_type=F32)
    v = ci[:, :cw] * _sigmoid(ci[:, cw:])
    q = jnp.dot(h, wq_ref[...], preferred_element_type=F32)

    @pl.when(pl.program_id(1) == 0)
    def _():
        vext_ref[0:CONV_HALO, :] = jnp.zeros((CONV_HALO, cw), F32)

    vext_ref[CONV_HALO:CONV_HALO + ts, :] = v
    acc = jnp.broadcast_to(dwb_ref[...], (ts, cw))
    first = CONV_HALO - (CONV_KERNEL - 1)
    for phase in range(8):
        offs = [first + k for k in range(CONV_KERNEL) if (first + k) % 8 == phase]
        if not offs:
            continue
        span = offs[-1] - offs[0] + ts
        vsh_ref[0:span, :] = vext_ref[offs[0]:offs[0] + span, :]
        for off in offs:
            acc = acc + dw_ref[off - first:off - first + 1, :] * vsh_ref[off - offs[0]:off - offs[0] + ts, :]
    vext_ref[0:CONV_HALO, :] = vext_ref[ts:ts + CONV_HALO, :]
    mu = jnp.mean(acc, axis=-1, keepdims=True)
    xc = acc - mu
    var = jnp.mean(xc * xc, axis=-1, keepdims=True)
    ln = xc * lax.rsqrt(var + EPS) * lng_ref[...] + lnb_ref[...]
    sw = ln * _sigmoid(ln)
    y_conv = jnp.dot(sw.astype(BF16), wpw_ref[...], preferred_element_type=F32)
    merged = _sigmoid(jnp.dot(h, wg_ref[:, 0:d], preferred_element_type=F32)) * y_conv

    gy = jax.nn.gelu(yt_ref[...]).astype(BF16)
    z = lax.dot_general(gy, wglu_ref[...], (((0,), (0,)), ((), ())), preferred_element_type=F32)
    y_ssm = z[:, :d] * _sigmoid(z[:, d:])
    merged = merged + _sigmoid(jnp.dot(h, wg_ref[:, d:2 * d], preferred_element_type=F32)) * y_ssm

    kk = k_ref[0]
    vv = v_ref[0]
    outs = []
    for hd in range(HEADS):
        sl = slice(hd * HEAD_DIM, (hd + 1) * HEAD_DIM)
        s = lax.dot_general(q[:, sl].astype(BF16), kk[:, sl], (((1,), (1,)), ((), ())),
                            preferred_element_type=F32) * (HEAD_DIM ** -0.5)
        p = jnp.exp(s - jnp.max(s, axis=-1, keepdims=True))
        den = jnp.sum(p, axis=-1, keepdims=True)
        o = jnp.dot(p.astype(BF16), vv[:, sl], preferred_element_type=F32) / den
        outs.append(o.astype(BF16))
    y_mem = jnp.dot(jnp.concatenate(outs, axis=1), wo_ref[...], preferred_element_type=F32)
    merged = merged + _sigmoid(jnp.dot(h, wg_ref[:, 2 * d:3 * d], preferred_element_type=F32)) * y_mem

    o_ref[0] = x + jnp.dot(merged.astype(BF16), wout_ref[...], preferred_element_type=F32)


def _mix_call(x, yt, kmem, vmem, g_mix, wc, wq, wg, dw, dwb, lng, lnb, wpw, wglu, wo, wout):
    b, s, d = x.shape
    ts = MIX_TILE
    nst = s // ts
    cw = dw.shape[1]
    m = kmem.shape[1]
    consts = [g_mix, wc, wq, wg, dw, dwb, lng, lnb, wpw, wglu, wo, wout]
    return pl.pallas_call(
        _mix_kernel,
        grid=(b, nst),
        in_specs=[pl.BlockSpec((1, ts, d), lambda i, j: (i, j, 0)),
                  pl.BlockSpec((yt.shape[0], ts), lambda i, j: (0, i * nst + j)),
                  pl.BlockSpec((1, m, kmem.shape[2]), lambda i, j: (i, 0, 0)),
                  pl.BlockSpec((1, m, vmem.shape[2]), lambda i, j: (i, 0, 0))]
                 + [_const_spec(c.shape) for c in consts],
        out_specs=pl.BlockSpec((1, ts, d), lambda i, j: (i, j, 0)),
        out_shape=jax.ShapeDtypeStruct((b, s, d), F32),
        scratch_shapes=[pltpu.VMEM((ts + CONV_HALO, cw), F32),
                        pltpu.VMEM((ts + CONV_HALO, cw), F32)],
        compiler_params=pltpu.CompilerParams(
            dimension_semantics=("arbitrary", "arbitrary"), vmem_limit_bytes=VMEM_LIMIT),
        name="mix",
    )(x, yt, kmem, vmem, *consts)


def _route_kernel(x_ref, g_ref, wr_ref, br_ref, xs_ref, info_ref, nch_ref):
    ts = x_ref.shape[0] // ROUTE_TILES
    cpt = xs_ref.shape[0] // ROUTE_TILES
    for sub in range(ROUTE_TILES):
        rows = slice(sub * ts, (sub + 1) * ts)
        xs, info, nch = _route_tile(x_ref[rows, :], g_ref[...], wr_ref, br_ref[...], cpt * CHUNK_ROWS)
        xs_ref[sub * cpt:(sub + 1) * cpt] = xs.reshape(cpt, CHUNK_ROWS, xs.shape[-1])
        info_ref[rows, :] = info
        nch_ref[sub] = nch


def _route_tile(x, g, wr_ref, br, cap):
    ts = x.shape[0]
    h2 = _rms(x, g)
    hb = h2.astype(BF16)
    hl = (h2 - hb.astype(F32)).astype(BF16)
    hw = jnp.dot(hb, wr_ref[...], preferred_element_type=F32)
    logits = (hw[:, :LANES] + hw[:, LANES:]
              + jnp.dot(hl, wr_ref[:, :LANES], preferred_element_type=F32)) + br
    lt = logits.T
    le = lt[0:N_EXPERTS]
    lg = lt[N_EXPERTS:N_EXPERTS + 8]
    neg = jnp.float32(-1e30)
    big = jnp.float32(1e9)
    g_f = lax.broadcasted_iota(jnp.int32, (8, ts), 0).astype(F32)
    e_f = lax.broadcasted_iota(jnp.int32, (N_EXPERTS, ts), 0).astype(F32)

    gmask = g_f < MOE_GROUPS
    gmax = jnp.max(jnp.where(gmask, lg, neg), axis=0, keepdims=True)
    gidx = jnp.min(jnp.where(gmask & (lg == gmax), g_f, big), axis=0, keepdims=True)
    gsum = jnp.sum(jnp.where(gmask, jnp.exp(jnp.minimum(lg - gmax, 0.0)), 0.0), axis=0, keepdims=True)
    p_top = 1.0 / gsum
    emask = jnp.floor(e_f * (1.0 / EXPERTS_PER_GROUP)) == gidx
    m1 = jnp.max(jnp.where(emask, le, neg), axis=0, keepdims=True)
    i1 = jnp.min(jnp.where(emask & (le == m1), e_f, big), axis=0, keepdims=True)
    emask2 = emask & (e_f != i1)
    m2 = jnp.max(jnp.where(emask2, le, neg), axis=0, keepdims=True)
    i2 = jnp.min(jnp.where(emask2 & (le == m2), e_f, big), axis=0, keepdims=True)
    r = jnp.exp(m2 - m1)
    w1 = p_top / (1.0 + r)
    w2 = p_top * r / (1.0 + r)

    sel1 = e_f == i1
    sel2 = e_f == i2
    occ = jnp.where(sel1 | sel2, 1.0, 0.0)
    tr = lax.broadcasted_iota(jnp.int32, (ts, ts), 0)
    tc = lax.broadcasted_iota(jnp.int32, (ts, ts), 1)
    earlier = jnp.where(tr < tc, 1.0, 0.0).astype(BF16)
    rank = jnp.dot(occ.astype(BF16), earlier, preferred_element_type=F32)
    cnt = jnp.sum(occ, axis=1, keepdims=True)
    nch = jnp.floor((cnt + (CHUNK_ROWS - 1)) * (1.0 / CHUNK_ROWS))
    er = lax.broadcasted_iota(jnp.int32, (N_EXPERTS, N_EXPERTS), 0)
    ec = lax.broadcasted_iota(jnp.int32, (N_EXPERTS, N_EXPERTS), 1)
    lower = jnp.where(ec < er, 1.0, 0.0).astype(BF16)
    nch_b = jnp.broadcast_to(nch, (N_EXPERTS, LANES))
    start = jnp.dot(lower, nch_b.astype(BF16), preferred_element_type=F32)[:, 0:1] * CHUNK_ROWS
    slot = start + rank
    pos1 = jnp.sum(jnp.where(sel1, slot, 0.0), axis=0, keepdims=True)
    pos2 = jnp.sum(jnp.where(sel2, slot, 0.0), axis=0, keepdims=True)

    rowid = lax.broadcasted_iota(jnp.int32, (cap, ts), 0)
    p = jnp.where((rowid == pos1.astype(jnp.int32)) | (rowid == pos2.astype(jnp.int32)), 1.0, 0.0)
    xs = jnp.dot(p.astype(BF16), hb, preferred_element_type=F32).astype(BF16)
    sub = lax.broadcasted_iota(jnp.int32, (LANES, ts), 0)
    info_t = jnp.where(sub == 0, pos1, jnp.where(sub == 1, pos2,
                       jnp.where(sub == 2, w1, jnp.where(sub == 3, w2, 0.0))))
    nch_rows = jnp.concatenate([nch_b, jnp.zeros((LANES - N_EXPERTS, LANES), F32)], axis=0).T[0:8, :]
    return xs, info_t.T, nch_rows


def _route_call(x1, g_ffn, wr, br, cap):
    t, d = x1.shape
    nt = t // TOKEN_TILE
    ts = TOKEN_TILE * ROUTE_TILES
    return pl.pallas_call(
        _route_kernel,
        grid=(t // ts,),
        in_specs=[pl.BlockSpec((ts, d), lambda i: (i, 0)),
                  pl.BlockSpec((1, d), lambda i: (0, 0)),
                  pl.BlockSpec(wr.shape, lambda i: (0, 0)),
                  pl.BlockSpec((1, LANES), lambda i: (0, 0))],
        out_specs=[pl.BlockSpec((ROUTE_TILES * cap // CHUNK_ROWS, CHUNK_ROWS, d), lambda i: (i, 0, 0)),
                   pl.BlockSpec((ts, LANES), lambda i: (i, 0)),
                   pl.BlockSpec((ROUTE_TILES, 8, LANES), lambda i: (i, 0, 0))],
        out_shape=[jax.ShapeDtypeStruct((nt * cap // CHUNK_ROWS, CHUNK_ROWS, d), BF16),
                   jax.ShapeDtypeStruct((t, LANES), F32),
                   jax.ShapeDtypeStruct((nt, 8, LANES), F32)],
        compiler_params=pltpu.CompilerParams(vmem_limit_bytes=VMEM_LIMIT),
        name="route",
    )(x1, g_ffn, wr, br)


def _expert_kernel(te_ref, nu_ref, src_ref, xs_ref, wg_ref, wu_ref, wd_ref, o_ref, xbuf_ref, sem_ref):
    i = pl.program_id(0)
    n_used = nu_ref[0]
    slot = i % 2

    def chunk_copy(tile, c, buf):
        return pltpu.make_async_copy(
            xs_ref.at[src_ref[tile * CHUNKS_PER_ETILE + c]], xbuf_ref.at[buf, c], sem_ref.at[buf])

    def gather(tile, buf):
        for c in range(CHUNKS_PER_ETILE):
            chunk_copy(tile, c, buf).start(priority=c % 2)

    def drain(tile, buf):
        del tile
        pltpu.make_async_copy(xs_ref.at[pl.ds(0, CHUNKS_PER_ETILE)], xbuf_ref.at[buf], sem_ref.at[buf]).wait()

    @pl.when(i == 0)
    def _():
        gather(0, 0)

    @pl.when(i < n_used)
    def _():
        nxt = jnp.minimum(i + 1, n_used - 1)
        gather(nxt, 1 - slot)
        drain(i, slot)
        x = xbuf_ref[slot].reshape(EXPERT_TILE, xbuf_ref.shape[-1])
        gate = jnp.dot(x, wg_ref[0].astype(BF16), preferred_element_type=F32)
        up = jnp.dot(x, wu_ref[0].astype(BF16), preferred_element_type=F32)
        act = (gate * _sigmoid(gate) * up).astype(BF16)
        y = jnp.dot(act, wd_ref[0].astype(BF16), preferred_element_type=F32).astype(BF16)
        o_ref[...] = y.reshape(o_ref.shape)

        @pl.when(i == n_used - 1)
        def _():
            drain(nxt, 1 - slot)

    @pl.when(i >= n_used)
    def _():
        o_ref[...] = jnp.zeros_like(o_ref)


def _expert_call(tile_expert, n_used, chunk_src, xs, wg, wu, wd):
    d = xs.shape[-1]
    de = wg.shape[2]
    tm = EXPERT_TILE
    nt = tile_expert.shape[0]

    def w_map(i, te, nu, src):
        return (te[jnp.maximum(jnp.minimum(i, nu[0] - 1), 0)], 0, 0)

    return pl.pallas_call(
        _expert_kernel,
        grid_spec=pltpu.PrefetchScalarGridSpec(
            num_scalar_prefetch=3,
            grid=(nt,),
            in_specs=[pl.BlockSpec(memory_space=pl.ANY),
                      pl.BlockSpec((1, d, de), w_map),
                      pl.BlockSpec((1, d, de), w_map),
                      pl.BlockSpec((1, de, d), w_map)],
            out_specs=pl.BlockSpec((CHUNKS_PER_ETILE, CHUNK_ROWS, d), lambda i, te, nu, src: (i, 0, 0)),
            scratch_shapes=[pltpu.VMEM((2, CHUNKS_PER_ETILE, CHUNK_ROWS, d), BF16),
                            pltpu.SemaphoreType.DMA((2,))]),
        out_shape=jax.ShapeDtypeStruct((nt * CHUNKS_PER_ETILE, CHUNK_ROWS, d), BF16),
        compiler_params=pltpu.CompilerParams(dimension_semantics=("arbitrary",)),
        name="expert",
    )(tile_expert, n_used, chunk_src, xs, wg, wu, wd)


def _combine_kernel(dst_ref, x_ref, info_ref, g_ref, ys_ref, o_ref, ybuf_ref, sem_ref):
    i = pl.program_id(0)
    n_tiles = pl.num_programs(0)
    ts = x_ref.shape[0] // COMBINE_TILES
    n_chunks = ybuf_ref.shape[1]
    cap = n_chunks // COMBINE_TILES * CHUNK_ROWS
    slot = i % 2

    def chunk_copy(tile, k, buf):
        return pltpu.make_async_copy(ys_ref.at[dst_ref[tile * n_chunks + k]], ybuf_ref.at[buf, k], sem_ref.at[buf])

    def gather(tile, buf):
        for k in range(n_chunks):
            chunk_copy(tile, k, buf).start(priority=k % 2)

    def drain(tile, buf):
        del tile
        pltpu.make_async_copy(ys_ref.at[pl.ds(0, n_chunks)], ybuf_ref.at[buf], sem_ref.at[buf]).wait()

    @pl.when(i == 0)
    def _():
        gather(0, 0)

    nxt = jnp.minimum(i + 1, n_tiles - 1)
    gather(nxt, 1 - slot)
    drain(i, slot)

    rowid = lax.broadcasted_iota(jnp.int32, (ts, cap), 1)
    for sub in range(COMBINE_TILES):
        rows = slice(sub * ts, (sub + 1) * ts)
        info = info_ref[rows, :]
        pos1 = info[:, 0:1].astype(jnp.int32)
        pos2 = info[:, 1:2].astype(jnp.int32)
        w1 = info[:, 2:3]
        w2 = info[:, 3:4]
        ys = ybuf_ref[slot, sub * (n_chunks // COMBINE_TILES):(sub + 1) * (n_chunks // COMBINE_TILES)]
        ys = ys.reshape(cap, ybuf_ref.shape[-1])
        pw = jnp.where(rowid == pos1, w1, jnp.where(rowid == pos2, w2, 0.0)).astype(BF16)
        y = jnp.dot(pw, ys, preferred_element_type=F32)
        o_ref[rows, :] = _rms(x_ref[rows, :] + y, g_ref[...])

    @pl.when(i == n_tiles - 1)
    def _():
        drain(nxt, 1 - slot)


def _combine_call(chunk_dst, x1, info, g_final, ys_sorted, cap):
    t, d = x1.shape
    ts = TOKEN_TILE * COMBINE_TILES
    return pl.pallas_call(
        _combine_kernel,
        grid_spec=pltpu.PrefetchScalarGridSpec(
            num_scalar_prefetch=1,
            grid=(t // ts,),
            in_specs=[pl.BlockSpec((ts, d), lambda i, dst: (i, 0)),
                      pl.BlockSpec((ts, LANES), lambda i, dst: (i, 0)),
                      pl.BlockSpec((1, d), lambda i, dst: (0, 0)),
                      pl.BlockSpec(memory_space=pl.ANY)],
            out_specs=pl.BlockSpec((ts, d), lambda i, dst: (i, 0)),
            scratch_shapes=[pltpu.VMEM((2, COMBINE_TILES * cap // CHUNK_ROWS, CHUNK_ROWS, d), BF16),
                            pltpu.SemaphoreType.DMA((2,))]),
        out_shape=jax.ShapeDtypeStruct((t, d), F32),
        compiler_params=pltpu.CompilerParams(dimension_semantics=("arbitrary",),
                                             vmem_limit_bytes=VMEM_LIMIT),
        name="combine",
    )(chunk_dst, x1, info, g_final, ys_sorted)


def _run_tables(nch, cpt, n_etiles):
    n_tiles, n_exp = nch.shape
    per = CHUNKS_PER_ETILE
    tcum = jnp.cumsum(nch, axis=1)
    toff = tcum - nch
    ecum = jnp.cumsum(nch, axis=0)
    etot = ecum[-1]
    eseg = -(-etot // per) * per
    segcum = jnp.cumsum(eseg)
    ebase = segcum - eseg
    eoff = ebase[None, :] + ecum - nch
    n_used = (segcum[-1] // per).reshape(1)

    k = jnp.arange(cpt, dtype=jnp.int32)[None, :, None]
    in_run = (k >= toff[:, None, :]) & (k < tcum[:, None, :])
    chunk_dst = jnp.sum(jnp.where(in_run, (eoff - toff)[:, None, :] + k, 0), axis=-1)

    first = jnp.arange(n_etiles, dtype=jnp.int32)[:, None] * per
    owner = (first >= ebase[None, :]) & (first < segcum[None, :])
    tile_expert = jnp.sum(jnp.where(owner, jnp.arange(n_exp, dtype=jnp.int32)[None, :], 0), axis=-1)

    pick = lambda a: jnp.sum(jnp.where(owner[:, None, :], a[None, :, :], 0), axis=-1)
    run_lo, run_n = pick(eoff), pick(nch)
    shift = pick(jnp.arange(n_tiles, dtype=jnp.int32)[:, None] * cpt + toff - eoff)
    c = (first + jnp.arange(per, dtype=jnp.int32)[None, :])[:, :, None]
    hit = (c >= run_lo[:, None, :]) & (c < (run_lo + run_n)[:, None, :])
    chunk_src = jnp.sum(jnp.where(hit, shift[:, None, :] + c, 0), axis=-1)
    i32 = lambda a: a.reshape(-1).astype(jnp.int32)
    return i32(chunk_dst), i32(tile_expert), i32(n_used), i32(chunk_src)


def _layer(x, mem, g_mix, w_in, conv_dw, conv_dw_bias, conv_ln_g, conv_ln_b, w_conv_out,
           lam_re, lam_im, log_dt, b_re, b_im, c_re, c_im, ssm_d, w_ssm_glu, g_mem, w_mem_kv,
           w_mem_out, w_out, g_ffn, w_rg, b_rg, w_re, b_re_, w_eg, w_eu, w_ed, g_final):
    b, s, d = x.shape
    t = b * s
    cw = conv_dw.shape[1]
    sw = ssm_d.shape[0]
    qw = w_mem_out.shape[0]
    n_groups = sw // SSM_GROUP
    row = lambda a: a.reshape(1, -1)

    o0, o1, o2 = 2 * cw, 2 * cw + sw, 2 * cw + sw + qw
    wc = w_in[:, :o0].astype(BF16)
    w_ssm_t = w_in[:, o0:o1].T.astype(BF16)
    wq = w_in[:, o1:o2].astype(BF16)
    wg = w_in[:, o2:].astype(BF16)

    kmem, vmem = _kv_call(mem, row(g_mem), w_mem_kv.astype(BF16))

    ut = _ut_call(x.reshape(t, d), row(g_mix), w_ssm_t)
    kt, wz, vy, acoef = _ssm_tables(lam_re, lam_im, log_dt, b_re, b_im, c_re, c_im, ssm_d)
    yt = _ssm_call(ut, kt, wz, vy, acoef, b)

    dw = jnp.concatenate([conv_dw, jnp.zeros((CONV_HALO - CONV_KERNEL, cw), F32)], axis=0)
    x1 = _mix_call(x, yt, kmem, vmem, row(g_mix), wc, wq, wg, dw, row(conv_dw_bias),
                   row(conv_ln_g), row(conv_ln_b), w_conv_out.astype(BF16),
                   w_ssm_glu.astype(BF16), w_mem_out.astype(BF16), w_out.astype(BF16))
    x1 = x1.reshape(t, d)

    pad = LANES - N_EXPERTS - MOE_GROUPS
    wr = jnp.concatenate([w_re, w_rg, jnp.zeros((d, pad), F32)], axis=1)
    wr_hi = wr.astype(BF16)
    wr = jnp.concatenate([wr_hi, (wr - wr_hi.astype(F32)).astype(BF16)], axis=1)
    br = jnp.concatenate([b_re_, b_rg, jnp.zeros((pad,), F32)]).reshape(1, LANES)
    n_tiles = t // TOKEN_TILE
    max_tile_chunks = (2 * TOKEN_TILE + N_EXPERTS * (CHUNK_ROWS - 1)) // CHUNK_ROWS
    cpt = -(-max_tile_chunks // 8) * 8
    cap = cpt * CHUNK_ROWS
    xs_tiles, info, nch_f = _route_call(x1, row(g_ffn), wr, br, cap)

    max_chunks = n_tiles * max_tile_chunks + N_EXPERTS * (CHUNKS_PER_ETILE - 1)
    n_etiles = -(-max_chunks // CHUNKS_PER_ETILE)
    nch = nch_f[:, 0, :N_EXPERTS].astype(jnp.int32)
    chunk_dst, tile_expert, n_used, chunk_src = _run_tables(nch, cpt, n_etiles)

    ys_sorted = _expert_call(tile_expert, n_used, chunk_src, xs_tiles, w_eg, w_eu, w_ed)
    out = _combine_call(chunk_dst, x1, info, row(g_final), ys_sorted, cap)
    return out.reshape(b, s, d)


def kernel(x, mem, g_mix, w_in, conv_dw, conv_dw_bias, conv_ln_g, conv_ln_b, w_conv_out, ssm_lambda_re, ssm_lambda_im, ssm_log_dt, ssm_b_re, ssm_b_im, ssm_c_re, ssm_c_im, ssm_d, w_ssm_glu, g_mem, w_mem_kv, w_mem_out, w_out, g_ffn, w_router_group, b_router_group, w_router_expert, b_router_expert, w_exp_gate, w_exp_up, w_exp_down, g_final):
    assert g_mix.shape[0] == 1, "the problem fixes one layer"
    l = 0
    return _layer(
        x, mem, g_mix[l], w_in[l], conv_dw[l], conv_dw_bias[l], conv_ln_g[l], conv_ln_b[l],
        w_conv_out[l], ssm_lambda_re[l], ssm_lambda_im[l], ssm_log_dt[l], ssm_b_re[l],
        ssm_b_im[l], ssm_c_re[l], ssm_c_im[l], ssm_d[l], w_ssm_glu[l], g_mem[l], w_mem_kv[l],
        w_mem_out[l], w_out[l], g_ffn[l], w_router_group[l], b_router_group[l],
        w_router_expert[l], b_router_expert[l], w_exp_gate[l], w_exp_up[l], w_exp_down[l], g_final)
```

```python
import functools

import jax
import jax.numpy as jnp
from jax import lax
from jax.experimental import pallas as pl
from jax.experimental.pallas import tpu as pltpu

F32 = jnp.float32
BF16 = jnp.bfloat16
EPS = 1e-6

LANES = 128
SUBLANES = 8
V7X_VMEM_BYTES = 64 * 1024 * 1024
CHUNK_ROWS = 2 * SUBLANES
SSM_GROUP = 16
SSM_STATE = 64
SSM_BLOCK = LANES
CONV_KERNEL = 31
CONV_HALO = 32
HEADS = 4
HEAD_DIM = 128
MOE_GROUPS = 4
EXPERTS_PER_GROUP = 8
N_EXPERTS = MOE_GROUPS * EXPERTS_PER_GROUP
TOKEN_TILE = 512
MIX_TILE = 1024
COMBINE_TILES = 2
ROUTE_TILES = 2
EXPERT_TILE = 512
CHUNKS_PER_ETILE = EXPERT_TILE // CHUNK_ROWS
UT_TILE = 2048
VMEM_LIMIT = V7X_VMEM_BYTES * 7 // 8


def _rms(x, g):
    return x * lax.rsqrt(jnp.mean(x * x, axis=-1, keepdims=True) + EPS) * g


def _sigmoid(x):
    return 0.5 * jnp.tanh(0.5 * x) + 0.5


def _const_spec(shape):
    zeros = (0,) * len(shape)
    return pl.BlockSpec(shape, lambda *_: zeros, pipeline_mode=pl.Buffered(1))


def _kv_kernel(mem_ref, g_ref, w_ref, k_ref, v_ref):
    width = k_ref.shape[-1]
    mn = _rms(mem_ref[0], g_ref[...]).astype(BF16)
    kv = jnp.dot(mn, w_ref[...], preferred_element_type=F32)
    k_ref[0] = kv[:, :width].astype(BF16)
    v_ref[0] = kv[:, width:].astype(BF16)


def _kv_call(mem, g_mem, w_kv):
    b, m, d = mem.shape
    width = w_kv.shape[1] // 2
    return pl.pallas_call(
        _kv_kernel,
        grid=(b,),
        in_specs=[pl.BlockSpec((1, m, d), lambda i: (i, 0, 0)),
                  pl.BlockSpec((1, d), lambda i: (0, 0)),
                  pl.BlockSpec(w_kv.shape, lambda i: (0, 0))],
        out_specs=[pl.BlockSpec((1, m, width), lambda i: (i, 0, 0)),
                   pl.BlockSpec((1, m, width), lambda i: (i, 0, 0))],
        out_shape=[jax.ShapeDtypeStruct((b, m, width), BF16)] * 2,
        name="kv",
    )(mem, g_mem, w_kv)


def _ut_kernel(x_ref, g_ref, wt_ref, o_ref):
    h = _rms(x_ref[...], g_ref[...]).astype(BF16)
    ut = lax.dot_general(wt_ref[...], h, (((1,), (1,)), ((), ())), preferred_element_type=F32)
    ut = ut.reshape(ut.shape[0], ut.shape[1] // SSM_BLOCK, SSM_BLOCK)
    o_ref[...] = ut.reshape(o_ref.shape).astype(BF16)


def _ut_call(x2, g_mix, w_ssm_t):
    t, d = x2.shape
    c = w_ssm_t.shape[0]
    ts = UT_TILE
    n_groups = c // SSM_GROUP
    return pl.pallas_call(
        _ut_kernel,
        grid=(t // ts,),
        in_specs=[pl.BlockSpec((ts, d), lambda i: (i, 0)),
                  pl.BlockSpec((1, d), lambda i: (0, 0)),
                  pl.BlockSpec((c, d), lambda i: (0, 0))],
        out_specs=pl.BlockSpec((n_groups, SSM_GROUP, ts // SSM_BLOCK, SSM_BLOCK),
                               lambda i: (0, 0, i, 0)),
        out_shape=jax.ShapeDtypeStruct((n_groups, SSM_GROUP, t // SSM_BLOCK, SSM_BLOCK), BF16),
        name="ut",
    )(x2, g_mix, w_ssm_t)


def _ssm_kernel(u_ref, kt_ref, w_ref, v_ref, a_ref, y_ref, acc_ref, z_ref, zs_ref, s_ref, slab_ref,
                *, n_batch):
    rows = u_ref.shape[2]
    n_blocks = rows // n_batch
    ri = lax.broadcasted_iota(jnp.int32, (SSM_BLOCK, SSM_BLOCK), 0)
    ci = lax.broadcasted_iota(jnp.int32, (SSM_BLOCK, SSM_BLOCK), 1)
    causal = ci >= ri

    n_pairs = SSM_GROUP // 2

    def build(cp, slot):
        for half in range(2):
            kt = kt_ref[0, 2 * cp + half]
            for c in range(SSM_GROUP):
                xb = jnp.broadcast_to(kt[c:c + 1, :], (SSM_BLOCK, SSM_BLOCK))
                toe = pltpu.roll(xb, 0, 1, stride=1, stride_axis=0)
                slab_ref[slot, half * SSM_BLOCK:(half + 1) * SSM_BLOCK, c * SSM_BLOCK:(c + 1) * SSM_BLOCK] = (
                    jnp.where(causal, toe, 0.0).astype(BF16))

    def apply(cp, slot, first):
        x2 = jnp.concatenate([u_ref[0, 2 * cp], u_ref[0, 2 * cp + 1]], axis=1)
        part = jnp.dot(x2, slab_ref[slot], preferred_element_type=F32)
        wrow = pl.multiple_of(cp * 2 * SSM_BLOCK, 2 * SSM_BLOCK)
        zpart = jnp.dot(x2, w_ref[0, pl.ds(wrow, 2 * SSM_BLOCK), :], preferred_element_type=F32)
        if first:
            acc_ref[...] = part
            z_ref[...] = zpart
        else:
            acc_ref[...] += part
            z_ref[...] += zpart

    build(0, 0)
    build(1, 1)
    apply(0, 0, True)

    def two(it, carry):
        cp = 2 * it + 1
        build(cp + 1, 0)
        apply(cp, 1, False)
        build(cp + 2, 1)
        apply(cp + 1, 0, False)
        return carry

    lax.fori_loop(0, (n_pairs - 2) // 2, two, 0)
    apply(n_pairs - 1, 1, False)

    a_full = a_ref[0, 0:1, :]
    a_swap = a_ref[0, 1:2, :]
    zs_ref[...] = pltpu.roll(z_ref[...], SSM_STATE, 1)
    st = jnp.zeros((n_batch, 2 * SSM_STATE), F32)
    sw = jnp.zeros((n_batch, 2 * SSM_STATE), F32)
    for blk in range(n_blocks):
        sl = pl.ds(blk, n_batch, stride=n_blocks)
        s_ref[sl, :] = st
        st, sw = (a_full * st + a_swap * sw + z_ref[sl, :],
                  a_full * sw - a_swap * st + zs_ref[sl, :])

    sb = s_ref[...].astype(BF16)
    y3 = jnp.stack([acc_ref[:, c * SSM_BLOCK:(c + 1) * SSM_BLOCK]
                    + jnp.dot(sb, v_ref[0, c], preferred_element_type=F32)
                    for c in range(SSM_GROUP)], axis=0)
    y_ref[...] = y3.reshape(y_ref.shape)


def _ssm_call(u4, kt, wz, vy, acoef, n_batch):
    g, c, rows, blk = u4.shape
    width = c * blk
    return pl.pallas_call(
        functools.partial(_ssm_kernel, n_batch=n_batch),
        grid=(g,),
        in_specs=[pl.BlockSpec((1, c, rows, blk), lambda i: (i, 0, 0, 0)),
                  pl.BlockSpec((1, c, c, blk), lambda i: (i, 0, 0, 0)),
                  pl.BlockSpec((1, width, 2 * SSM_STATE), lambda i: (i, 0, 0)),
                  pl.BlockSpec((1, c, 2 * SSM_STATE, blk), lambda i: (i, 0, 0, 0)),
                  pl.BlockSpec((1, 2, 2 * SSM_STATE), lambda i: (i, 0, 0))],
        out_specs=pl.BlockSpec((c, rows * blk), lambda i: (i, 0)),
        out_shape=jax.ShapeDtypeStruct((g * c, rows * blk), F32),
        scratch_shapes=[pltpu.VMEM((rows, width), F32),
                        pltpu.VMEM((rows, 2 * SSM_STATE), F32),
                        pltpu.VMEM((rows, 2 * SSM_STATE), F32),
                        pltpu.VMEM((rows, 2 * SSM_STATE), F32),
                        pltpu.VMEM((2, 2 * blk, width), BF16)],
        name="ssm",
    )(u4, kt, wz, vy, acoef)


def _ssm_tables(lam_re, lam_im, log_dt, b_re, b_im, c_re, c_im, d):
    hi = lax.Precision.HIGHEST
    g = lam_re.shape[0]
    dt = jnp.exp(log_dt)[:, None]
    er, ei = lam_re * dt, lam_im * dt
    cat = jnp.concatenate
    kk = jnp.arange(SSM_BLOCK, dtype=F32)
    kdesc = jnp.arange(SSM_BLOCK - 1, -1, -1, dtype=F32)[None, :, None]

    def powers_pk(k):
        mag, ang = jnp.exp(er[:, :, None] * k), ei[:, :, None] * k
        return mag * jnp.cos(ang), mag * jnp.sin(ang)

    ppk_r, ppk_i = powers_pk(kk)
    pt_r, pt_i = powers_pk(kk + 1.0)
    prev_m, prev_a = jnp.exp(kdesc * er[:, None, :]), kdesc * ei[:, None, :]
    prr, pir = prev_m * jnp.cos(prev_a), prev_m * jnp.sin(prev_a)
    nr, ni = pt_r[:, :, 0] - 1.0, pt_i[:, :, 0]
    den = lam_re * lam_re + lam_im * lam_im
    fr = (nr * lam_re + ni * lam_im) / den
    fi = (ni * lam_re - nr * lam_im) / den
    bbr = jnp.swapaxes(fr[:, :, None] * b_re - fi[:, :, None] * b_im, 1, 2)
    bbi = jnp.swapaxes(fr[:, :, None] * b_im + fi[:, :, None] * b_re, 1, 2)
    cb_r = c_re[:, None] * bbr[:, :, None, :] - c_im[:, None] * bbi[:, :, None, :]
    cb_i = c_re[:, None] * bbi[:, :, None, :] + c_im[:, None] * bbr[:, :, None, :]
    cb = cat([cb_r, -cb_i], axis=-1).reshape(g, SSM_GROUP * SSM_GROUP, 2 * SSM_STATE)
    pk = cat([ppk_r, ppk_i], axis=1)
    kt = jnp.einsum("gmq,gqk->gmk", cb, pk, precision=hi).reshape(g, SSM_GROUP, SSM_GROUP, SSM_BLOCK)
    dmat = jnp.eye(SSM_GROUP, dtype=F32)[None] * d.reshape(g, 1, SSM_GROUP)
    kt = kt + dmat[..., None] * (jnp.arange(SSM_BLOCK) == 0).astype(F32)
    wz = (cat([prr, prr], -1)[:, None] * cat([bbr, bbi], -1)[:, :, None, :]
          + cat([pir, pir], -1)[:, None] * cat([-bbi, bbr], -1)[:, :, None, :])
    wz = wz.astype(BF16).reshape(g, SSM_GROUP * SSM_BLOCK, 2 * SSM_STATE)
    vy = (cat([c_re, -c_re], -1)[..., None] * cat([pt_r, pt_i], axis=1)[:, None]
          + cat([-c_im, -c_im], -1)[..., None] * cat([pt_i, pt_r], axis=1)[:, None]).astype(BF16)
    ar, ai = pt_r[:, :, SSM_BLOCK - 1], pt_i[:, :, SSM_BLOCK - 1]
    acoef = jnp.stack([cat([ar, ar], -1), cat([-ai, ai], -1)], axis=1)
    return kt, wz, vy, acoef


def _mix_kernel(x_ref, yt_ref, k_ref, v_ref, g_ref, wc_ref, wq_ref, wg_ref, dw_ref, dwb_ref,
                lng_ref, lnb_ref, wpw_ref, wglu_ref, wo_ref, wout_ref, o_ref, vext_ref, vsh_ref):
    ts = x_ref.shape[1]
    d = x_ref.shape[2]
    cw = dw_ref.shape[1]
    x = x_ref[0]
    h = _rms(x, g_ref[...]).astype(BF16)

    ci = jnp.dot(h, wc_ref[...], preferred_element_type=F32)
    v = ci[:, :cw] * _sigmoid(ci[:, cw:])
    q = jnp.dot(h, wq_ref[...], preferred_element_type=F32)

    @pl.when(pl.program_id(1) == 0)
    def _():
        vext_ref[0:CONV_HALO, :] = jnp.zeros((CONV_HALO, cw), F32)

    vext_ref[CONV_HALO:CONV_HALO + ts, :] = v
    acc = jnp.broadcast_to(dwb_ref[...], (ts, cw))
    first = CONV_HALO - (CONV_KERNEL - 1)
    for phase in range(SUBLANES):
        offs = [first + k for k in range(CONV_KERNEL) if (first + k) % SUBLANES == phase]
        if not offs:
            continue
        span = offs[-1] - offs[0] + ts
        vsh_ref[0:span, :] = vext_ref[offs[0]:offs[0] + span, :]
        for off in offs:
            acc = acc + dw_ref[off - first:off - first + 1, :] * vsh_ref[off - offs[0]:off - offs[0] + ts, :]
    vext_ref[0:CONV_HALO, :] = vext_ref[ts:ts + CONV_HALO, :]
    mu = jnp.mean(acc, axis=-1, keepdims=True)
    xc = acc - mu
    var = jnp.mean(xc * xc, axis=-1, keepdims=True)
    ln = xc * lax.rsqrt(var + EPS) * lng_ref[...] + lnb_ref[...]
    sw = ln * _sigmoid(ln)
    y_conv = jnp.dot(sw.astype(BF16), wpw_ref[...], preferred_element_type=F32)
    merged = _sigmoid(jnp.dot(h, wg_ref[:, 0:d], preferred_element_type=F32)) * y_conv

    gy = jax.nn.gelu(yt_ref[...]).astype(BF16)
    z = lax.dot_general(gy, wglu_ref[...], (((0,), (0,)), ((), ())), preferred_element_type=F32)
    y_ssm = z[:, :d] * _sigmoid(z[:, d:])
    merged = merged + _sigmoid(jnp.dot(h, wg_ref[:, d:2 * d], preferred_element_type=F32)) * y_ssm

    kk = k_ref[0]
    vv = v_ref[0]
    outs = []
    for hd in range(HEADS):
        sl = slice(hd * HEAD_DIM, (hd + 1) * HEAD_DIM)
        s = lax.dot_general(q[:, sl].astype(BF16), kk[:, sl], (((1,), (1,)), ((), ())),
                            preferred_element_type=F32) * (HEAD_DIM ** -0.5)
        p = jnp.exp(s - jnp.max(s, axis=-1, keepdims=True))
        den = jnp.sum(p, axis=-1, keepdims=True)
        o = jnp.dot(p.astype(BF16), vv[:, sl], preferred_element_type=F32) / den
        outs.append(o.astype(BF16))
    y_mem = jnp.dot(jnp.concatenate(outs, axis=1), wo_ref[...], preferred_element_type=F32)
    merged = merged + _sigmoid(jnp.dot(h, wg_ref[:, 2 * d:3 * d], preferred_element_type=F32)) * y_mem

    o_ref[0] = x + jnp.dot(merged.astype(BF16), wout_ref[...], preferred_element_type=F32)


def _mix_call(x, yt, kmem, vmem, g_mix, wc, wq, wg, dw, dwb, lng, lnb, wpw, wglu, wo, wout):
    b, s, d = x.shape
    ts = MIX_TILE
    nst = s // ts
    cw = dw.shape[1]
    m = kmem.shape[1]
    consts = [g_mix, wc, wq, wg, dw, dwb, lng, lnb, wpw, wglu, wo, wout]
    return pl.pallas_call(
        _mix_kernel,
        grid=(b, nst),
        in_specs=[pl.BlockSpec((1, ts, d), lambda i, j: (i, j, 0)),
                  pl.BlockSpec((yt.shape[0], ts), lambda i, j: (0, i * nst + j)),
                  pl.BlockSpec((1, m, kmem.shape[2]), lambda i, j: (i, 0, 0)),
                  pl.BlockSpec((1, m, vmem.shape[2]), lambda i, j: (i, 0, 0))]
                 + [_const_spec(c.shape) for c in consts],
        out_specs=pl.BlockSpec((1, ts, d), lambda i, j: (i, j, 0)),
        out_shape=jax.ShapeDtypeStruct((b, s, d), F32),
        scratch_shapes=[pltpu.VMEM((ts + CONV_HALO, cw), F32),
                        pltpu.VMEM((ts + CONV_HALO, cw), F32)],
        compiler_params=pltpu.CompilerParams(
            dimension_semantics=("arbitrary", "arbitrary"), vmem_limit_bytes=VMEM_LIMIT),
        name="mix",
    )(x, yt, kmem, vmem, *consts)


def _route_kernel(x_ref, g_ref, wr_ref, br_ref, xs_ref, info_ref, nch_ref):
    ts = x_ref.shape[0] // ROUTE_TILES
    cpt = xs_ref.shape[0] // ROUTE_TILES
    for sub in range(ROUTE_TILES):
        rows = slice(sub * ts, (sub + 1) * ts)
        xs, info, nch = _route_tile(x_ref[rows, :], g_ref[...], wr_ref, br_ref[...], cpt * CHUNK_ROWS)
        xs_ref[sub * cpt:(sub + 1) * cpt] = xs.reshape(cpt, CHUNK_ROWS, xs.shape[-1])
        info_ref[rows, :] = info
        nch_ref[sub] = nch


def _route_tile(x, g, wr_ref, br, cap):
    ts = x.shape[0]
    h2 = _rms(x, g)
    hb = h2.astype(BF16)
    hl = (h2 - hb.astype(F32)).astype(BF16)
    hw = jnp.dot(hb, wr_ref[...], preferred_element_type=F32)
    logits = (hw[:, :LANES] + hw[:, LANES:]
              + jnp.dot(hl, wr_ref[:, :LANES], preferred_element_type=F32)) + br
    lt = logits.T
    le = lt[0:N_EXPERTS]
    lg = lt[N_EXPERTS:N_EXPERTS + SUBLANES]
    neg = jnp.float32(-1e30)
    big = jnp.float32(1e9)
    g_f = lax.broadcasted_iota(jnp.int32, (SUBLANES, ts), 0).astype(F32)
    e_f = lax.broadcasted_iota(jnp.int32, (N_EXPERTS, ts), 0).astype(F32)

    gmask = g_f < MOE_GROUPS
    gmax = jnp.max(jnp.where(gmask, lg, neg), axis=0, keepdims=True)
    gidx = jnp.min(jnp.where(gmask & (lg == gmax), g_f, big), axis=0, keepdims=True)
    gsum = jnp.sum(jnp.where(gmask, jnp.exp(jnp.minimum(lg - gmax, 0.0)), 0.0), axis=0, keepdims=True)
    p_top = 1.0 / gsum
    emask = jnp.floor(e_f * (1.0 / EXPERTS_PER_GROUP)) == gidx
    m1 = jnp.max(jnp.where(emask, le, neg), axis=0, keepdims=True)
    i1 = jnp.min(jnp.where(emask & (le == m1), e_f, big), axis=0, keepdims=True)
    emask2 = emask & (e_f != i1)
    m2 = jnp.max(jnp.where(emask2, le, neg), axis=0, keepdims=True)
    i2 = jnp.min(jnp.where(emask2 & (le == m2), e_f, big), axis=0, keepdims=True)
    r = jnp.exp(m2 - m1)
    w1 = p_top / (1.0 + r)
    w2 = p_top * r / (1.0 + r)

    sel1 = e_f == i1
    sel2 = e_f == i2
    occ = jnp.where(sel1 | sel2, 1.0, 0.0)
    tr = lax.broadcasted_iota(jnp.int32, (ts, ts), 0)
    tc = lax.broadcasted_iota(jnp.int32, (ts, ts), 1)
    earlier = jnp.where(tr < tc, 1.0, 0.0).astype(BF16)
    rank = jnp.dot(occ.astype(BF16), earlier, preferred_element_type=F32)
    cnt = jnp.sum(occ, axis=1, keepdims=True)
    nch = jnp.floor((cnt + (CHUNK_ROWS - 1)) * (1.0 / CHUNK_ROWS))
    er = lax.broadcasted_iota(jnp.int32, (N_EXPERTS, N_EXPERTS), 0)
    ec = lax.broadcasted_iota(jnp.int32, (N_EXPERTS, N_EXPERTS), 1)
    lower = jnp.where(ec < er, 1.0, 0.0).astype(BF16)
    nch_b = jnp.broadcast_to(nch, (N_EXPERTS, LANES))
    start = jnp.dot(lower, nch_b.astype(BF16), preferred_element_type=F32)[:, 0:1] * CHUNK_ROWS
    slot = start + rank
    pos1 = jnp.sum(jnp.where(sel1, slot, 0.0), axis=0, keepdims=True)
    pos2 = jnp.sum(jnp.where(sel2, slot, 0.0), axis=0, keepdims=True)

    rowid = lax.broadcasted_iota(jnp.int32, (cap, ts), 0)
    p = jnp.where((rowid == pos1.astype(jnp.int32)) | (rowid == pos2.astype(jnp.int32)), 1.0, 0.0)
    xs = jnp.dot(p.astype(BF16), hb, preferred_element_type=F32).astype(BF16)
    sub = lax.broadcasted_iota(jnp.int32, (LANES, ts), 0)
    info_t = jnp.where(sub == 0, pos1, jnp.where(sub == 1, pos2,
                       jnp.where(sub == 2, w1, jnp.where(sub == 3, w2, 0.0))))
    nch_rows = jnp.concatenate([nch_b, jnp.zeros((LANES - N_EXPERTS, LANES), F32)], axis=0).T[0:SUBLANES, :]
    return xs, info_t.T, nch_rows


def _route_call(x1, g_ffn, wr, br, cap):
    t, d = x1.shape
    nt = t // TOKEN_TILE
    ts = TOKEN_TILE * ROUTE_TILES
    return pl.pallas_call(
        _route_kernel,
        grid=(t // ts,),
        in_specs=[pl.BlockSpec((ts, d), lambda i: (i, 0)),
                  pl.BlockSpec((1, d), lambda i: (0, 0)),
                  pl.BlockSpec(wr.shape, lambda i: (0, 0)),
                  pl.BlockSpec((1, LANES), lambda i: (0, 0))],
        out_specs=[pl.BlockSpec((ROUTE_TILES * cap // CHUNK_ROWS, CHUNK_ROWS, d), lambda i: (i, 0, 0)),
                   pl.BlockSpec((ts, LANES), lambda i: (i, 0)),
                   pl.BlockSpec((ROUTE_TILES, SUBLANES, LANES), lambda i: (i, 0, 0))],
        out_shape=[jax.ShapeDtypeStruct((nt * cap // CHUNK_ROWS, CHUNK_ROWS, d), BF16),
                   jax.ShapeDtypeStruct((t, LANES), F32),
                   jax.ShapeDtypeStruct((nt, SUBLANES, LANES), F32)],
        compiler_params=pltpu.CompilerParams(vmem_limit_bytes=VMEM_LIMIT),
        name="route",
    )(x1, g_ffn, wr, br)


def _expert_kernel(te_ref, nu_ref, src_ref, xs_ref, wg_ref, wu_ref, wd_ref, o_ref, xbuf_ref, sem_ref):
    i = pl.program_id(0)
    n_used = nu_ref[0]
    slot = i % 2

    def chunk_copy(tile, c, buf):
        return pltpu.make_async_copy(
            xs_ref.at[src_ref[tile * CHUNKS_PER_ETILE + c]], xbuf_ref.at[buf, c], sem_ref.at[buf])

    def gather(tile, buf):
        for c in range(CHUNKS_PER_ETILE):
            chunk_copy(tile, c, buf).start(priority=c % 2)

    def drain(buf):
        pltpu.make_async_copy(xs_ref.at[pl.ds(0, CHUNKS_PER_ETILE)], xbuf_ref.at[buf], sem_ref.at[buf]).wait()

    @pl.when(i == 0)
    def _():
        gather(0, 0)

    @pl.when(i < n_used)
    def _():
        nxt = jnp.minimum(i + 1, n_used - 1)
        gather(nxt, 1 - slot)
        drain(slot)
        x = xbuf_ref[slot].reshape(EXPERT_TILE, xbuf_ref.shape[-1])
        gate = jnp.dot(x, wg_ref[0].astype(BF16), preferred_element_type=F32)
        up = jnp.dot(x, wu_ref[0].astype(BF16), preferred_element_type=F32)
        act = (gate * _sigmoid(gate) * up).astype(BF16)
        y = jnp.dot(act, wd_ref[0].astype(BF16), preferred_element_type=F32).astype(BF16)
        o_ref[...] = y.reshape(o_ref.shape)

        @pl.when(i == n_used - 1)
        def _():
            drain(1 - slot)

    @pl.when(i >= n_used)
    def _():
        o_ref[...] = jnp.zeros_like(o_ref)


def _expert_call(tile_expert, n_used, chunk_src, xs, wg, wu, wd):
    d = xs.shape[-1]
    de = wg.shape[2]
    tm = EXPERT_TILE
    nt = tile_expert.shape[0]

    def w_map(i, te, nu, src):
        return (te[jnp.maximum(jnp.minimum(i, nu[0] - 1), 0)], 0, 0)

    return pl.pallas_call(
        _expert_kernel,
        grid_spec=pltpu.PrefetchScalarGridSpec(
            num_scalar_prefetch=3,
            grid=(nt,),
            in_specs=[pl.BlockSpec(memory_space=pl.ANY),
                      pl.BlockSpec((1, d, de), w_map),
                      pl.BlockSpec((1, d, de), w_map),
                      pl.BlockSpec((1, de, d), w_map)],
            out_specs=pl.BlockSpec((CHUNKS_PER_ETILE, CHUNK_ROWS, d), lambda i, te, nu, src: (i, 0, 0)),
            scratch_shapes=[pltpu.VMEM((2, CHUNKS_PER_ETILE, CHUNK_ROWS, d), BF16),
                            pltpu.SemaphoreType.DMA((2,))]),
        out_shape=jax.ShapeDtypeStruct((nt * CHUNKS_PER_ETILE, CHUNK_ROWS, d), BF16),
        compiler_params=pltpu.CompilerParams(dimension_semantics=("arbitrary",)),
        name="expert",
    )(tile_expert, n_used, chunk_src, xs, wg, wu, wd)


def _combine_kernel(dst_ref, x_ref, info_ref, g_ref, ys_ref, o_ref, ybuf_ref, sem_ref):
    i = pl.program_id(0)
    n_tiles = pl.num_programs(0)
    ts = x_ref.shape[0] // COMBINE_TILES
    n_chunks = ybuf_ref.shape[1]
    cap = n_chunks // COMBINE_TILES * CHUNK_ROWS
    slot = i % 2

    def chunk_copy(tile, k, buf):
        return pltpu.make_async_copy(ys_ref.at[dst_ref[tile * n_chunks + k]], ybuf_ref.at[buf, k], sem_ref.at[buf])

    def gather(tile, buf):
        for k in range(n_chunks):
            chunk_copy(tile, k, buf).start(priority=k % 2)

    def drain(buf):
        pltpu.make_async_copy(ys_ref.at[pl.ds(0, n_chunks)], ybuf_ref.at[buf], sem_ref.at[buf]).wait()

    @pl.when(i == 0)
    def _():
        gather(0, 0)

    nxt = jnp.minimum(i + 1, n_tiles - 1)
    gather(nxt, 1 - slot)
    drain(slot)

    rowid = lax.broadcasted_iota(jnp.int32, (ts, cap), 1)
    for sub in range(COMBINE_TILES):
        rows = slice(sub * ts, (sub + 1) * ts)
        info = info_ref[rows, :]
        pos1 = info[:, 0:1].astype(jnp.int32)
        pos2 = info[:, 1:2].astype(jnp.int32)
        w1 = info[:, 2:3]
        w2 = info[:, 3:4]
        ys = ybuf_ref[slot, sub * (n_chunks // COMBINE_TILES):(sub + 1) * (n_chunks // COMBINE_TILES)]
        ys = ys.reshape(cap, ybuf_ref.shape[-1])
        pw = jnp.where(rowid == pos1, w1, jnp.where(rowid == pos2, w2, 0.0)).astype(BF16)
        y = jnp.dot(pw, ys, preferred_element_type=F32)
        o_ref[rows, :] = _rms(x_ref[rows, :] + y, g_ref[...])

    @pl.when(i == n_tiles - 1)
    def _():
        drain(1 - slot)


def _combine_call(chunk_dst, x1, info, g_final, ys_sorted, cap):
    t, d = x1.shape
    ts = TOKEN_TILE * COMBINE_TILES
    return pl.pallas_call(
        _combine_kernel,
        grid_spec=pltpu.PrefetchScalarGridSpec(
            num_scalar_prefetch=1,
            grid=(t // ts,),
            in_specs=[pl.BlockSpec((ts, d), lambda i, dst: (i, 0)),
                      pl.BlockSpec((ts, LANES), lambda i, dst: (i, 0)),
                      pl.BlockSpec((1, d), lambda i, dst: (0, 0)),
                      pl.BlockSpec(memory_space=pl.ANY)],
            out_specs=pl.BlockSpec((ts, d), lambda i, dst: (i, 0)),
            scratch_shapes=[pltpu.VMEM((2, COMBINE_TILES * cap // CHUNK_ROWS, CHUNK_ROWS, d), BF16),
                            pltpu.SemaphoreType.DMA((2,))]),
        out_shape=jax.ShapeDtypeStruct((t, d), F32),
        compiler_params=pltpu.CompilerParams(dimension_semantics=("arbitrary",),
                                             vmem_limit_bytes=VMEM_LIMIT),
        name="combine",
    )(chunk_dst, x1, info, g_final, ys_sorted)


def _run_tables(nch, cpt, n_etiles):
    n_tiles, n_exp = nch.shape
    per = CHUNKS_PER_ETILE
    tcum = jnp.cumsum(nch, axis=1)
    toff = tcum - nch
    ecum = jnp.cumsum(nch, axis=0)
    etot = ecum[-1]
    eseg = -(-etot // per) * per
    segcum = jnp.cumsum(eseg)
    ebase = segcum - eseg
    eoff = ebase[None, :] + ecum - nch
    n_used = (segcum[-1] // per).reshape(1)

    k = jnp.arange(cpt, dtype=jnp.int32)[None, :, None]
    in_run = (k >= toff[:, None, :]) & (k < tcum[:, None, :])
    chunk_dst = jnp.sum(jnp.where(in_run, (eoff - toff)[:, None, :] + k, 0), axis=-1)

    first = jnp.arange(n_etiles, dtype=jnp.int32)[:, None] * per
    owner = (first >= ebase[None, :]) & (first < segcum[None, :])
    tile_expert = jnp.sum(jnp.where(owner, jnp.arange(n_exp, dtype=jnp.int32)[None, :], 0), axis=-1)

    pick = lambda a: jnp.sum(jnp.where(owner[:, None, :], a[None, :, :], 0), axis=-1)
    run_lo, run_n = pick(eoff), pick(nch)
    shift = pick(jnp.arange(n_tiles, dtype=jnp.int32)[:, None] * cpt + toff - eoff)
    c = (first + jnp.arange(per, dtype=jnp.int32)[None, :])[:, :, None]
    hit = (c >= run_lo[:, None, :]) & (c < (run_lo + run_n)[:, None, :])
    chunk_src = jnp.sum(jnp.where(hit, shift[:, None, :] + c, 0), axis=-1)
    i32 = lambda a: a.reshape(-1).astype(jnp.int32)
    return i32(chunk_dst), i32(tile_expert), i32(n_used), i32(chunk_src)


def _layer(x, mem, g_mix, w_in, conv_dw, conv_dw_bias, conv_ln_g, conv_ln_b, w_conv_out,
           lam_re, lam_im, log_dt, b_re, b_im, c_re, c_im, ssm_d, w_ssm_glu, g_mem, w_mem_kv,
           w_mem_out, w_out, g_ffn, w_rg, b_rg, w_re, b_re_, w_eg, w_eu, w_ed, g_final):
    b, s, d = x.shape
    t = b * s
    cw = conv_dw.shape[1]
    sw = ssm_d.shape[0]
    qw = w_mem_out.shape[0]
    row = lambda a: a.reshape(1, -1)

    o0, o1, o2 = 2 * cw, 2 * cw + sw, 2 * cw + sw + qw
    wc = w_in[:, :o0].astype(BF16)
    w_ssm_t = w_in[:, o0:o1].T.astype(BF16)
    wq = w_in[:, o1:o2].astype(BF16)
    wg = w_in[:, o2:].astype(BF16)

    kmem, vmem = _kv_call(mem, row(g_mem), w_mem_kv.astype(BF16))

    ut = _ut_call(x.reshape(t, d), row(g_mix), w_ssm_t)
    kt, wz, vy, acoef = _ssm_tables(lam_re, lam_im, log_dt, b_re, b_im, c_re, c_im, ssm_d)
    yt = _ssm_call(ut, kt, wz, vy, acoef, b)

    dw = jnp.concatenate([conv_dw, jnp.zeros((CONV_HALO - CONV_KERNEL, cw), F32)], axis=0)
    x1 = _mix_call(x, yt, kmem, vmem, row(g_mix), wc, wq, wg, dw, row(conv_dw_bias),
                   row(conv_ln_g), row(conv_ln_b), w_conv_out.astype(BF16),
                   w_ssm_glu.astype(BF16), w_mem_out.astype(BF16), w_out.astype(BF16))
    x1 = x1.reshape(t, d)

    pad = LANES - N_EXPERTS - MOE_GROUPS
    wr = jnp.concatenate([w_re, w_rg, jnp.zeros((d, pad), F32)], axis=1)
    wr_hi = wr.astype(BF16)
    wr = jnp.concatenate([wr_hi, (wr - wr_hi.astype(F32)).astype(BF16)], axis=1)
    br = jnp.concatenate([b_re_, b_rg, jnp.zeros((pad,), F32)]).reshape(1, LANES)
    n_tiles = t // TOKEN_TILE
    max_tile_chunks = (2 * TOKEN_TILE + N_EXPERTS * (CHUNK_ROWS - 1)) // CHUNK_ROWS
    cpt = -(-max_tile_chunks // SUBLANES) * SUBLANES
    cap = cpt * CHUNK_ROWS
    xs_tiles, info, nch_f = _route_call(x1, row(g_ffn), wr, br, cap)

    max_chunks = n_tiles * max_tile_chunks + N_EXPERTS * (CHUNKS_PER_ETILE - 1)
    n_etiles = -(-max_chunks // CHUNKS_PER_ETILE)
    nch = nch_f[:, 0, :N_EXPERTS].astype(jnp.int32)
    chunk_dst, tile_expert, n_used, chunk_src = _run_tables(nch, cpt, n_etiles)

    ys_sorted = _expert_call(tile_expert, n_used, chunk_src, xs_tiles, w_eg, w_eu, w_ed)
    out = _combine_call(chunk_dst, x1, info, row(g_final), ys_sorted, cap)
    return out.reshape(b, s, d)


def kernel(x, mem, g_mix, w_in, conv_dw, conv_dw_bias, conv_ln_g, conv_ln_b, w_conv_out, ssm_lambda_re, ssm_lambda_im, ssm_log_dt, ssm_b_re, ssm_b_im, ssm_c_re, ssm_c_im, ssm_d, w_ssm_glu, g_mem, w_mem_kv, w_mem_out, w_out, g_ffn, w_router_group, b_router_group, w_router_expert, b_router_expert, w_exp_gate, w_exp_up, w_exp_down, g_final):
    assert g_mix.shape[0] == 1, "the problem fixes one layer"
    l = 0
    return _layer(
        x, mem, g_mix[l], w_in[l], conv_dw[l], conv_dw_bias[l], conv_ln_g[l], conv_ln_b[l],
        w_conv_out[l], ssm_lambda_re[l], ssm_lambda_im[l], ssm_log_dt[l], ssm_b_re[l],
        ssm_b_im[l], ssm_c_re[l], ssm_c_im[l], ssm_d[l], w_ssm_glu[l], g_mem[l], w_mem_kv[l],
        w_mem_out[l], w_out[l], g_ffn[l], w_router_group[l], b_router_group[l],
        w_router_expert[l], b_router_expert[l], w_exp_gate[l], w_exp_up[l], w_exp_down[l], g_final)
```

```python
import functools

import jax
import jax.numpy as jnp
from jax import lax
from jax.experimental import pallas as pl
from jax.experimental.pallas import tpu as pltpu

F32 = jnp.float32
BF16 = jnp.bfloat16
EPS = 1e-6

LANES = 128
SUBLANES = 8
V7X_VMEM_BYTES = 64 * 1024 * 1024
CHUNK_ROWS = 2 * SUBLANES
SSM_GROUP = 16
SSM_STATE = 64
SSM_BLOCK = LANES
CONV_KERNEL = 31
CONV_HALO = 32
HEADS = 4
HEAD_DIM = 128
MOE_GROUPS = 4
EXPERTS_PER_GROUP = 8
N_EXPERTS = MOE_GROUPS * EXPERTS_PER_GROUP
TOKEN_TILE = 512
MIX_TILE = 1024
COMBINE_TILES = 2
ROUTE_TILES = 2
EXPERT_TILE = 512
CHUNKS_PER_ETILE = EXPERT_TILE // CHUNK_ROWS
UT_TILE = 2048
VMEM_LIMIT = V7X_VMEM_BYTES * 7 // 8


def _rms(x, g):
    return x * lax.rsqrt(jnp.mean(x * x, axis=-1, keepdims=True) + EPS) * g


def _sigmoid(x):
    return 0.5 * jnp.tanh(0.5 * x) + 0.5


def _const_spec(shape):
    zeros = (0,) * len(shape)
    return pl.BlockSpec(shape, lambda *_: zeros, pipeline_mode=pl.Buffered(1))


def _kv_kernel(mem_ref, g_ref, w_ref, k_ref, v_ref):
    width = k_ref.shape[-1]
    mn = _rms(mem_ref[0], g_ref[...]).astype(BF16)
    kv = jnp.dot(mn, w_ref[...], preferred_element_type=F32)
    k_ref[0] = kv[:, :width].astype(BF16)
    v_ref[0] = kv[:, width:].astype(BF16)


def _kv_call(mem, g_mem, w_kv):
    b, m, d = mem.shape
    width = w_kv.shape[1] // 2
    return pl.pallas_call(
        _kv_kernel,
        grid=(b,),
        in_specs=[pl.BlockSpec((1, m, d), lambda i: (i, 0, 0)),
                  pl.BlockSpec((1, d), lambda i: (0, 0)),
                  pl.BlockSpec(w_kv.shape, lambda i: (0, 0))],
        out_specs=[pl.BlockSpec((1, m, width), lambda i: (i, 0, 0)),
                   pl.BlockSpec((1, m, width), lambda i: (i, 0, 0))],
        out_shape=[jax.ShapeDtypeStruct((b, m, width), BF16)] * 2,
        name="kv",
    )(mem, g_mem, w_kv)


def _ut_kernel(x_ref, g_ref, wt_ref, o_ref):
    h = _rms(x_ref[...], g_ref[...]).astype(BF16)
    ut = lax.dot_general(wt_ref[...], h, (((1,), (1,)), ((), ())), preferred_element_type=F32)
    ut = ut.reshape(ut.shape[0], ut.shape[1] // SSM_BLOCK, SSM_BLOCK)
    o_ref[...] = ut.reshape(o_ref.shape).astype(BF16)


def _ut_call(x2, g_mix, w_ssm_t):
    t, d = x2.shape
    c = w_ssm_t.shape[0]
    ts = UT_TILE
    n_groups = c // SSM_GROUP
    return pl.pallas_call(
        _ut_kernel,
        grid=(t // ts,),
        in_specs=[pl.BlockSpec((ts, d), lambda i: (i, 0)),
                  pl.BlockSpec((1, d), lambda i: (0, 0)),
                  pl.BlockSpec((c, d), lambda i: (0, 0))],
        out_specs=pl.BlockSpec((n_groups, SSM_GROUP, ts // SSM_BLOCK, SSM_BLOCK),
                               lambda i: (0, 0, i, 0)),
        out_shape=jax.ShapeDtypeStruct((n_groups, SSM_GROUP, t // SSM_BLOCK, SSM_BLOCK), BF16),
        name="ut",
    )(x2, g_mix, w_ssm_t)


def _ssm_kernel(u_ref, kt_ref, w_ref, v_ref, a_ref, y_ref, acc_ref, z_ref, zs_ref, s_ref, slab_ref,
                *, n_batch):
    rows = u_ref.shape[2]
    n_blocks = rows // n_batch
    ri = lax.broadcasted_iota(jnp.int32, (SSM_BLOCK, SSM_BLOCK), 0)
    ci = lax.broadcasted_iota(jnp.int32, (SSM_BLOCK, SSM_BLOCK), 1)
    causal = ci >= ri

    n_pairs = SSM_GROUP // 2

    def build(cp, slot):
        for half in range(2):
            kt = kt_ref[0, 2 * cp + half]
            for c in range(SSM_GROUP):
                xb = jnp.broadcast_to(kt[c:c + 1, :], (SSM_BLOCK, SSM_BLOCK))
                toe = pltpu.roll(xb, 0, 1, stride=1, stride_axis=0)
                slab_ref[slot, half * SSM_BLOCK:(half + 1) * SSM_BLOCK, c * SSM_BLOCK:(c + 1) * SSM_BLOCK] = (
                    jnp.where(causal, toe, 0.0).astype(BF16))

    def apply(cp, slot, first):
        x2 = jnp.concatenate([u_ref[0, 2 * cp], u_ref[0, 2 * cp + 1]], axis=1)
        part = jnp.dot(x2, slab_ref[slot], preferred_element_type=F32)
        wrow = pl.multiple_of(cp * 2 * SSM_BLOCK, 2 * SSM_BLOCK)
        zpart = jnp.dot(x2, w_ref[0, pl.ds(wrow, 2 * SSM_BLOCK), :], preferred_element_type=F32)
        if first:
            acc_ref[...] = part
            z_ref[...] = zpart
        else:
            acc_ref[...] += part
            z_ref[...] += zpart

    build(0, 0)
    build(1, 1)
    apply(0, 0, True)

    def two(it, carry):
        cp = 2 * it + 1
        build(cp + 1, 0)
        apply(cp, 1, False)
        build(cp + 2, 1)
        apply(cp + 1, 0, False)
        return carry

    lax.fori_loop(0, (n_pairs - 2) // 2, two, 0)
    apply(n_pairs - 1, 1, False)

    a_full = a_ref[0, 0:1, :]
    a_swap = a_ref[0, 1:2, :]
    zs_ref[...] = pltpu.roll(z_ref[...], SSM_STATE, 1)
    st = jnp.zeros((n_batch, 2 * SSM_STATE), F32)
    sw = jnp.zeros((n_batch, 2 * SSM_STATE), F32)
    for blk in range(n_blocks):
        sl = pl.ds(blk, n_batch, stride=n_blocks)
        s_ref[sl, :] = st
        st, sw = (a_full * st + a_swap * sw + z_ref[sl, :],
                  a_full * sw - a_swap * st + zs_ref[sl, :])

    sb = s_ref[...].astype(BF16)
    y3 = jnp.stack([acc_ref[:, c * SSM_BLOCK:(c + 1) * SSM_BLOCK]
                    + jnp.dot(sb, v_ref[0, c], preferred_element_type=F32)
                    for c in range(SSM_GROUP)], axis=0)
    y_ref[...] = y3.reshape(y_ref.shape)


def _ssm_call(u4, kt, wz, vy, acoef, n_batch):
    g, c, rows, blk = u4.shape
    width = c * blk
    return pl.pallas_call(
        functools.partial(_ssm_kernel, n_batch=n_batch),
        grid=(g,),
        in_specs=[pl.BlockSpec((1, c, rows, blk), lambda i: (i, 0, 0, 0)),
                  pl.BlockSpec((1, c, c, blk), lambda i: (i, 0, 0, 0)),
                  pl.BlockSpec((1, width, 2 * SSM_STATE), lambda i: (i, 0, 0)),
                  pl.BlockSpec((1, c, 2 * SSM_STATE, blk), lambda i: (i, 0, 0, 0)),
                  pl.BlockSpec((1, 2, 2 * SSM_STATE), lambda i: (i, 0, 0))],
        out_specs=pl.BlockSpec((c, rows * blk), lambda i: (i, 0)),
        out_shape=jax.ShapeDtypeStruct((g * c, rows * blk), F32),
        scratch_shapes=[pltpu.VMEM((rows, width), F32),
                        pltpu.VMEM((rows, 2 * SSM_STATE), F32),
                        pltpu.VMEM((rows, 2 * SSM_STATE), F32),
                        pltpu.VMEM((rows, 2 * SSM_STATE), F32),
                        pltpu.VMEM((2, 2 * blk, width), BF16)],
        name="ssm",
    )(u4, kt, wz, vy, acoef)


def _ssm_tables(lam_re, lam_im, log_dt, b_re, b_im, c_re, c_im, d):
    hi = lax.Precision.HIGHEST
    g = lam_re.shape[0]
    dt = jnp.exp(log_dt)[:, None]
    er, ei = lam_re * dt, lam_im * dt
    cat = jnp.concatenate
    kk = jnp.arange(SSM_BLOCK, dtype=F32)
    kdesc = jnp.arange(SSM_BLOCK - 1, -1, -1, dtype=F32)[None, :, None]

    def powers_pk(k):
        mag, ang = jnp.exp(er[:, :, None] * k), ei[:, :, None] * k
        return mag * jnp.cos(ang), mag * jnp.sin(ang)

    ppk_r, ppk_i = powers_pk(kk)
    pt_r, pt_i = powers_pk(kk + 1.0)
    prev_m, prev_a = jnp.exp(kdesc * er[:, None, :]), kdesc * ei[:, None, :]
    prr, pir = prev_m * jnp.cos(prev_a), prev_m * jnp.sin(prev_a)
    nr, ni = pt_r[:, :, 0] - 1.0, pt_i[:, :, 0]
    den = lam_re * lam_re + lam_im * lam_im
    fr = (nr * lam_re + ni * lam_im) / den
    fi = (ni * lam_re - nr * lam_im) / den
    bbr = jnp.swapaxes(fr[:, :, None] * b_re - fi[:, :, None] * b_im, 1, 2)
    bbi = jnp.swapaxes(fr[:, :, None] * b_im + fi[:, :, None] * b_re, 1, 2)
    cb_r = c_re[:, None] * bbr[:, :, None, :] - c_im[:, None] * bbi[:, :, None, :]
    cb_i = c_re[:, None] * bbi[:, :, None, :] + c_im[:, None] * bbr[:, :, None, :]
    cb = cat([cb_r, -cb_i], axis=-1).reshape(g, SSM_GROUP * SSM_GROUP, 2 * SSM_STATE)
    pk = cat([ppk_r, ppk_i], axis=1)
    kt = jnp.einsum("gmq,gqk->gmk", cb, pk, precision=hi).reshape(g, SSM_GROUP, SSM_GROUP, SSM_BLOCK)
    dmat = jnp.eye(SSM_GROUP, dtype=F32)[None] * d.reshape(g, 1, SSM_GROUP)
    kt = kt + dmat[..., None] * (jnp.arange(SSM_BLOCK) == 0).astype(F32)
    wz = (cat([prr, prr], -1)[:, None] * cat([bbr, bbi], -1)[:, :, None, :]
          + cat([pir, pir], -1)[:, None] * cat([-bbi, bbr], -1)[:, :, None, :])
    wz = wz.astype(BF16).reshape(g, SSM_GROUP * SSM_BLOCK, 2 * SSM_STATE)
    vy = (cat([c_re, -c_re], -1)[..., None] * cat([pt_r, pt_i], axis=1)[:, None]
          + cat([-c_im, -c_im], -1)[..., None] * cat([pt_i, pt_r], axis=1)[:, None]).astype(BF16)
    ar, ai = pt_r[:, :, SSM_BLOCK - 1], pt_i[:, :, SSM_BLOCK - 1]
    acoef = jnp.stack([cat([ar, ar], -1), cat([-ai, ai], -1)], axis=1)
    return kt, wz, vy, acoef


def _mix_kernel(x_ref, yt_ref, k_ref, v_ref, g_ref, wc_ref, wq_ref, wg_ref, dw_ref, dwb_ref,
                lng_ref, lnb_ref, wpw_ref, wglu_ref, wo_ref, wout_ref, o_ref, vext_ref, vsh_ref):
    ts = x_ref.shape[1]
    d = x_ref.shape[2]
    cw = dw_ref.shape[1]
    x = x_ref[0]
    h = _rms(x, g_ref[...]).astype(BF16)

    ci = jnp.dot(h, wc_ref[...], preferred_element_type=F32)
    v = ci[:, :cw] * _sigmoid(ci[:, cw:])
    q = jnp.dot(h, wq_ref[...], preferred_element_type=F32)

    @pl.when(pl.program_id(1) == 0)
    def _():
        vext_ref[0:CONV_HALO, :] = jnp.zeros((CONV_HALO, cw), F32)

    vext_ref[CONV_HALO:CONV_HALO + ts, :] = v
    acc = jnp.broadcast_to(dwb_ref[...], (ts, cw))
    first = CONV_HALO - (CONV_KERNEL - 1)
    for phase in range(SUBLANES):
        offs = [first + k for k in range(CONV_KERNEL) if (first + k) % SUBLANES == phase]
        if not offs:
            continue
        span = offs[-1] - offs[0] + ts
        vsh_ref[0:span, :] = vext_ref[offs[0]:offs[0] + span, :]
        for off in offs:
            acc = acc + dw_ref[off - first:off - first + 1, :] * vsh_ref[off - offs[0]:off - offs[0] + ts, :]
    vext_ref[0:CONV_HALO, :] = vext_ref[ts:ts + CONV_HALO, :]
    mu = jnp.mean(acc, axis=-1, keepdims=True)
    xc = acc - mu
    var = jnp.mean(xc * xc, axis=-1, keepdims=True)
    ln = xc * lax.rsqrt(var + EPS) * lng_ref[...] + lnb_ref[...]
    sw = ln * _sigmoid(ln)
    y_conv = jnp.dot(sw.astype(BF16), wpw_ref[...], preferred_element_type=F32)
    merged = _sigmoid(jnp.dot(h, wg_ref[:, 0:d], preferred_element_type=F32)) * y_conv

    gy = jax.nn.gelu(yt_ref[...]).astype(BF16)
    z = lax.dot_general(gy, wglu_ref[...], (((0,), (0,)), ((), ())), preferred_element_type=F32)
    y_ssm = z[:, :d] * _sigmoid(z[:, d:])
    merged = merged + _sigmoid(jnp.dot(h, wg_ref[:, d:2 * d], preferred_element_type=F32)) * y_ssm

    kk = k_ref[0]
    vv = v_ref[0]
    outs = []
    for hd in range(HEADS):
        sl = slice(hd * HEAD_DIM, (hd + 1) * HEAD_DIM)
        s = lax.dot_general(q[:, sl].astype(BF16), kk[:, sl], (((1,), (1,)), ((), ())),
                            preferred_element_type=F32) * (HEAD_DIM ** -0.5)
        p = jnp.exp(s - jnp.max(s, axis=-1, keepdims=True))
        den = jnp.sum(p, axis=-1, keepdims=True)
        o = jnp.dot(p.astype(BF16), vv[:, sl], preferred_element_type=F32) / den
        outs.append(o.astype(BF16))
    y_mem = jnp.dot(jnp.concatenate(outs, axis=1), wo_ref[...], preferred_element_type=F32)
    merged = merged + _sigmoid(jnp.dot(h, wg_ref[:, 2 * d:3 * d], preferred_element_type=F32)) * y_mem

    o_ref[0] = x + jnp.dot(merged.astype(BF16), wout_ref[...], preferred_element_type=F32)


def _mix_call(x, yt, kmem, vmem, g_mix, wc, wq, wg, dw, dwb, lng, lnb, wpw, wglu, wo, wout):
    b, s, d = x.shape
    ts = MIX_TILE
    nst = s // ts
    cw = dw.shape[1]
    m = kmem.shape[1]
    consts = [g_mix, wc, wq, wg, dw, dwb, lng, lnb, wpw, wglu, wo, wout]
    return pl.pallas_call(
        _mix_kernel,
        grid=(b, nst),
        in_specs=[pl.BlockSpec((1, ts, d), lambda i, j: (i, j, 0)),
                  pl.BlockSpec((yt.shape[0], ts), lambda i, j: (0, i * nst + j)),
                  pl.BlockSpec((1, m, kmem.shape[2]), lambda i, j: (i, 0, 0)),
                  pl.BlockSpec((1, m, vmem.shape[2]), lambda i, j: (i, 0, 0))]
                 + [_const_spec(c.shape) for c in consts],
        out_specs=pl.BlockSpec((1, ts, d), lambda i, j: (i, j, 0)),
        out_shape=jax.ShapeDtypeStruct((b, s, d), F32),
        scratch_shapes=[pltpu.VMEM((ts + CONV_HALO, cw), F32),
                        pltpu.VMEM((ts + CONV_HALO, cw), F32)],
        compiler_params=pltpu.CompilerParams(
            dimension_semantics=("arbitrary", "arbitrary"), vmem_limit_bytes=VMEM_LIMIT),
        name="mix",
    )(x, yt, kmem, vmem, *consts)


def _route_kernel(x_ref, g_ref, wr_ref, br_ref, xs_ref, info_ref, nch_ref):
    ts = x_ref.shape[0] // ROUTE_TILES
    cpt = xs_ref.shape[0] // ROUTE_TILES
    for sub in range(ROUTE_TILES):
        rows = slice(sub * ts, (sub + 1) * ts)
        xs, info, nch = _route_tile(x_ref[rows, :], g_ref[...], wr_ref, br_ref[...], cpt * CHUNK_ROWS)
        xs_ref[sub * cpt:(sub + 1) * cpt] = xs.reshape(cpt, CHUNK_ROWS, xs.shape[-1])
        info_ref[rows, :] = info
        nch_ref[sub] = nch


def _route_tile(x, g, wr_ref, br, cap):
    ts = x.shape[0]
    h2 = _rms(x, g)
    hb = h2.astype(BF16)
    hl = (h2 - hb.astype(F32)).astype(BF16)
    hw = jnp.dot(hb, wr_ref[...], preferred_element_type=F32)
    logits = (hw[:, :LANES] + hw[:, LANES:]
              + jnp.dot(hl, wr_ref[:, :LANES], preferred_element_type=F32)) + br
    lt = logits.T
    le = lt[0:N_EXPERTS]
    lg = lt[N_EXPERTS:N_EXPERTS + SUBLANES]
    neg = jnp.float32(-1e30)
    big = jnp.float32(1e9)
    g_f = lax.broadcasted_iota(jnp.int32, (SUBLANES, ts), 0).astype(F32)
    e_f = lax.broadcasted_iota(jnp.int32, (N_EXPERTS, ts), 0).astype(F32)

    gmask = g_f < MOE_GROUPS
    gmax = jnp.max(jnp.where(gmask, lg, neg), axis=0, keepdims=True)
    gidx = jnp.min(jnp.where(gmask & (lg == gmax), g_f, big), axis=0, keepdims=True)
    gsum = jnp.sum(jnp.where(gmask, jnp.exp(jnp.minimum(lg - gmax, 0.0)), 0.0), axis=0, keepdims=True)
    p_top = 1.0 / gsum
    emask = jnp.floor(e_f * (1.0 / EXPERTS_PER_GROUP)) == gidx
    m1 = jnp.max(jnp.where(emask, le, neg), axis=0, keepdims=True)
    i1 = jnp.min(jnp.where(emask & (le == m1), e_f, big), axis=0, keepdims=True)
    emask2 = emask & (e_f != i1)
    m2 = jnp.max(jnp.where(emask2, le, neg), axis=0, keepdims=True)
    i2 = jnp.min(jnp.where(emask2 & (le == m2), e_f, big), axis=0, keepdims=True)
    r = jnp.exp(m2 - m1)
    w1 = p_top / (1.0 + r)
    w2 = p_top * r / (1.0 + r)

    sel1 = e_f == i1
    sel2 = e_f == i2
    occ = jnp.where(sel1 | sel2, 1.0, 0.0)
    tr = lax.broadcasted_iota(jnp.int32, (ts, ts), 0)
    tc = lax.broadcasted_iota(jnp.int32, (ts, ts), 1)
    earlier = jnp.where(tr < tc, 1.0, 0.0).astype(BF16)
    rank = jnp.dot(occ.astype(BF16), earlier, preferred_element_type=F32)
    cnt = jnp.sum(occ, axis=1, keepdims=True)
    nch = jnp.floor((cnt + (CHUNK_ROWS - 1)) * (1.0 / CHUNK_ROWS))
    er = lax.broadcasted_iota(jnp.int32, (N_EXPERTS, N_EXPERTS), 0)
    ec = lax.broadcasted_iota(jnp.int32, (N_EXPERTS, N_EXPERTS), 1)
    lower = jnp.where(ec < er, 1.0, 0.0).astype(BF16)
    nch_b = jnp.broadcast_to(nch, (N_EXPERTS, LANES))
    start = jnp.dot(lower, nch_b.astype(BF16), preferred_element_type=F32)[:, 0:1] * CHUNK_ROWS
    slot = start + rank
    pos1 = jnp.sum(jnp.where(sel1, slot, 0.0), axis=0, keepdims=True)
    pos2 = jnp.sum(jnp.where(sel2, slot, 0.0), axis=0, keepdims=True)

    rowid = lax.broadcasted_iota(jnp.int32, (cap, ts), 0)
    p = jnp.where((rowid == pos1.astype(jnp.int32)) | (rowid == pos2.astype(jnp.int32)), 1.0, 0.0)
    xs = jnp.dot(p.astype(BF16), hb, preferred_element_type=F32).astype(BF16)
    sub = lax.broadcasted_iota(jnp.int32, (LANES, ts), 0)
    info_t = jnp.where(sub == 0, pos1, jnp.where(sub == 1, pos2,
                       jnp.where(sub == 2, w1, jnp.where(sub == 3, w2, 0.0))))
    nch_rows = jnp.concatenate([nch_b, jnp.zeros((LANES - N_EXPERTS, LANES), F32)], axis=0).T[0:SUBLANES, :]
    return xs, info_t.T, nch_rows


def _route_call(x1, g_ffn, wr, br, cap):
    t, d = x1.shape
    nt = t // TOKEN_TILE
    ts = TOKEN_TILE * ROUTE_TILES
    return pl.pallas_call(
        _route_kernel,
        grid=(t // ts,),
        in_specs=[pl.BlockSpec((ts, d), lambda i: (i, 0)),
                  pl.BlockSpec((1, d), lambda i: (0, 0)),
                  pl.BlockSpec(wr.shape, lambda i: (0, 0)),
                  pl.BlockSpec((1, LANES), lambda i: (0, 0))],
        out_specs=[pl.BlockSpec((ROUTE_TILES * cap // CHUNK_ROWS, CHUNK_ROWS, d), lambda i: (i, 0, 0)),
                   pl.BlockSpec((ts, LANES), lambda i: (i, 0)),
                   pl.BlockSpec((ROUTE_TILES, SUBLANES, LANES), lambda i: (i, 0, 0))],
        out_shape=[jax.ShapeDtypeStruct((nt * cap // CHUNK_ROWS, CHUNK_ROWS, d), BF16),
                   jax.ShapeDtypeStruct((t, LANES), F32),
                   jax.ShapeDtypeStruct((nt, SUBLANES, LANES), F32)],
        compiler_params=pltpu.CompilerParams(vmem_limit_bytes=VMEM_LIMIT),
        name="route",
    )(x1, g_ffn, wr, br)


def _expert_kernel(te_ref, nu_ref, src_ref, xs_ref, wg_ref, wu_ref, wd_ref, o_ref, xbuf_ref, sem_ref):
    i = pl.program_id(0)
    n_used = nu_ref[0]
    slot = i % 2

    def chunk_copy(tile, c, buf):
        return pltpu.make_async_copy(
            xs_ref.at[src_ref[tile * CHUNKS_PER_ETILE + c]], xbuf_ref.at[buf, c], sem_ref.at[buf])

    def gather(tile, buf):
        for c in range(CHUNKS_PER_ETILE):
            chunk_copy(tile, c, buf).start(priority=c % 2)

    def drain(buf):
        pltpu.make_async_copy(xs_ref.at[pl.ds(0, CHUNKS_PER_ETILE)], xbuf_ref.at[buf], sem_ref.at[buf]).wait()

    @pl.when(i == 0)
    def _():
        gather(0, 0)

    @pl.when(i < n_used)
    def _():
        nxt = jnp.minimum(i + 1, n_used - 1)
        gather(nxt, 1 - slot)
        drain(slot)
        x = xbuf_ref[slot].reshape(EXPERT_TILE, xbuf_ref.shape[-1])
        gate = jnp.dot(x, wg_ref[0].astype(BF16), preferred_element_type=F32)
        up = jnp.dot(x, wu_ref[0].astype(BF16), preferred_element_type=F32)
        act = (gate * _sigmoid(gate) * up).astype(BF16)
        y = jnp.dot(act, wd_ref[0].astype(BF16), preferred_element_type=F32).astype(BF16)
        o_ref[...] = y.reshape(o_ref.shape)

        @pl.when(i == n_used - 1)
        def _():
            drain(1 - slot)

    @pl.when(i >= n_used)
    def _():
        o_ref[...] = jnp.zeros_like(o_ref)


def _expert_call(tile_expert, n_used, chunk_src, xs, wg, wu, wd):
    d = xs.shape[-1]
    de = wg.shape[2]
    tm = EXPERT_TILE
    nt = tile_expert.shape[0]

    def w_map(i, te, nu, src):
        return (te[jnp.maximum(jnp.minimum(i, nu[0] - 1), 0)], 0, 0)

    return pl.pallas_call(
        _expert_kernel,
        grid_spec=pltpu.PrefetchScalarGridSpec(
            num_scalar_prefetch=3,
            grid=(nt,),
            in_specs=[pl.BlockSpec(memory_space=pl.ANY),
                      pl.BlockSpec((1, d, de), w_map),
                      pl.BlockSpec((1, d, de), w_map),
                      pl.BlockSpec((1, de, d), w_map)],
            out_specs=pl.BlockSpec((CHUNKS_PER_ETILE, CHUNK_ROWS, d), lambda i, te, nu, src: (i, 0, 0)),
            scratch_shapes=[pltpu.VMEM((2, CHUNKS_PER_ETILE, CHUNK_ROWS, d), BF16),
                            pltpu.SemaphoreType.DMA((2,))]),
        out_shape=jax.ShapeDtypeStruct((nt * CHUNKS_PER_ETILE, CHUNK_ROWS, d), BF16),
        compiler_params=pltpu.CompilerParams(dimension_semantics=("arbitrary",)),
        name="expert",
    )(tile_expert, n_used, chunk_src, xs, wg, wu, wd)


def _combine_kernel(dst_ref, x_ref, info_ref, g_ref, ys_ref, o_ref, ybuf_ref, sem_ref):
    i = pl.program_id(0)
    n_tiles = pl.num_programs(0)
    ts = x_ref.shape[0] // COMBINE_TILES
    n_chunks = ybuf_ref.shape[1]
    cap = n_chunks // COMBINE_TILES * CHUNK_ROWS
    slot = i % 2

    def chunk_copy(tile, k, buf):
        return pltpu.make_async_copy(ys_ref.at[dst_ref[tile * n_chunks + k]], ybuf_ref.at[buf, k], sem_ref.at[buf])

    def gather(tile, buf):
        for k in range(n_chunks):
            chunk_copy(tile, k, buf).start(priority=k % 2)

    def drain(buf):
        pltpu.make_async_copy(ys_ref.at[pl.ds(0, n_chunks)], ybuf_ref.at[buf], sem_ref.at[buf]).wait()

    @pl.when(i == 0)
    def _():
        gather(0, 0)

    nxt = jnp.minimum(i + 1, n_tiles - 1)
    gather(nxt, 1 - slot)
    drain(slot)

    rowid = lax.broadcasted_iota(jnp.int32, (ts, cap), 1)
    for sub in range(COMBINE_TILES):
        rows = slice(sub * ts, (sub + 1) * ts)
        info = info_ref[rows, :]
        pos1 = info[:, 0:1].astype(jnp.int32)
        pos2 = info[:, 1:2].astype(jnp.int32)
        w1 = info[:, 2:3]
        w2 = info[:, 3:4]
        ys = ybuf_ref[slot, sub * (n_chunks // COMBINE_TILES):(sub + 1) * (n_chunks // COMBINE_TILES)]
        ys = ys.reshape(cap, ybuf_ref.shape[-1])
        pw = jnp.where(rowid == pos1, w1, jnp.where(rowid == pos2, w2, 0.0)).astype(BF16)
        y = jnp.dot(pw, ys, preferred_element_type=F32)
        o_ref[rows, :] = _rms(x_ref[rows, :] + y, g_ref[...])

    @pl.when(i == n_tiles - 1)
    def _():
        drain(1 - slot)


def _combine_call(chunk_dst, x1, info, g_final, ys_sorted, cap):
    t, d = x1.shape
    ts = TOKEN_TILE * COMBINE_TILES
    return pl.pallas_call(
        _combine_kernel,
        grid_spec=pltpu.PrefetchScalarGridSpec(
            num_scalar_prefetch=1,
            grid=(t // ts,),
            in_specs=[pl.BlockSpec((ts, d), lambda i, dst: (i, 0)),
                      pl.BlockSpec((ts, LANES), lambda i, dst: (i, 0)),
                      pl.BlockSpec((1, d), lambda i, dst: (0, 0)),
                      pl.BlockSpec(memory_space=pl.ANY)],
            out_specs=pl.BlockSpec((ts, d), lambda i, dst: (i, 0)),
            scratch_shapes=[pltpu.VMEM((2, COMBINE_TILES * cap // CHUNK_ROWS, CHUNK_ROWS, d), BF16),
                            pltpu.SemaphoreType.DMA((2,))]),
        out_shape=jax.ShapeDtypeStruct((t, d), F32),
        input_output_aliases={1: 0},
        compiler_params=pltpu.CompilerParams(dimension_semantics=("arbitrary",),
                                             vmem_limit_bytes=VMEM_LIMIT),
        name="combine",
    )(chunk_dst, x1, info, g_final, ys_sorted)


def _run_tables(nch, cpt, n_etiles):
    n_tiles, n_exp = nch.shape
    per = CHUNKS_PER_ETILE
    tcum = jnp.cumsum(nch, axis=1)
    toff = tcum - nch
    ecum = jnp.cumsum(nch, axis=0)
    etot = ecum[-1]
    eseg = -(-etot // per) * per
    segcum = jnp.cumsum(eseg)
    ebase = segcum - eseg
    eoff = ebase[None, :] + ecum - nch
    n_used = (segcum[-1] // per).reshape(1)

    k = jnp.arange(cpt, dtype=jnp.int32)[None, :, None]
    in_run = (k >= toff[:, None, :]) & (k < tcum[:, None, :])
    chunk_dst = jnp.sum(jnp.where(in_run, (eoff - toff)[:, None, :] + k, 0), axis=-1)

    first = jnp.arange(n_etiles, dtype=jnp.int32)[:, None] * per
    owner = (first >= ebase[None, :]) & (first < segcum[None, :])
    tile_expert = jnp.sum(jnp.where(owner, jnp.arange(n_exp, dtype=jnp.int32)[None, :], 0), axis=-1)

    pick = lambda a: jnp.sum(jnp.where(owner[:, None, :], a[None, :, :], 0), axis=-1)
    run_lo, run_n = pick(eoff), pick(nch)
    shift = pick(jnp.arange(n_tiles, dtype=jnp.int32)[:, None] * cpt + toff - eoff)
    c = (first + jnp.arange(per, dtype=jnp.int32)[None, :])[:, :, None]
    hit = (c >= run_lo[:, None, :]) & (c < (run_lo + run_n)[:, None, :])
    chunk_src = jnp.sum(jnp.where(hit, shift[:, None, :] + c, 0), axis=-1)
    i32 = lambda a: a.reshape(-1).astype(jnp.int32)
    return i32(chunk_dst), i32(tile_expert), i32(n_used), i32(chunk_src)


def _layer(x, mem, g_mix, w_in, conv_dw, conv_dw_bias, conv_ln_g, conv_ln_b, w_conv_out,
           lam_re, lam_im, log_dt, b_re, b_im, c_re, c_im, ssm_d, w_ssm_glu, g_mem, w_mem_kv,
           w_mem_out, w_out, g_ffn, w_rg, b_rg, w_re, b_re_, w_eg, w_eu, w_ed, g_final):
    b, s, d = x.shape
    t = b * s
    cw = conv_dw.shape[1]
    sw = ssm_d.shape[0]
    qw = w_mem_out.shape[0]
    row = lambda a: a.reshape(1, -1)

    o0, o1, o2 = 2 * cw, 2 * cw + sw, 2 * cw + sw + qw
    wc = w_in[:, :o0].astype(BF16)
    w_ssm_t = w_in[:, o0:o1].T.astype(BF16)
    wq = w_in[:, o1:o2].astype(BF16)
    wg = w_in[:, o2:].astype(BF16)

    kmem, vmem = _kv_call(mem, row(g_mem), w_mem_kv.astype(BF16))

    ut = _ut_call(x.reshape(t, d), row(g_mix), w_ssm_t)
    kt, wz, vy, acoef = _ssm_tables(lam_re, lam_im, log_dt, b_re, b_im, c_re, c_im, ssm_d)
    yt = _ssm_call(ut, kt, wz, vy, acoef, b)

    dw = jnp.concatenate([conv_dw, jnp.zeros((CONV_HALO - CONV_KERNEL, cw), F32)], axis=0)
    x1 = _mix_call(x, yt, kmem, vmem, row(g_mix), wc, wq, wg, dw, row(conv_dw_bias),
                   row(conv_ln_g), row(conv_ln_b), w_conv_out.astype(BF16),
                   w_ssm_glu.astype(BF16), w_mem_out.astype(BF16), w_out.astype(BF16))
    x1 = x1.reshape(t, d)

    pad = LANES - N_EXPERTS - MOE_GROUPS
    wr = jnp.concatenate([w_re, w_rg, jnp.zeros((d, pad), F32)], axis=1)
    wr_hi = wr.astype(BF16)
    wr = jnp.concatenate([wr_hi, (wr - wr_hi.astype(F32)).astype(BF16)], axis=1)
    br = jnp.concatenate([b_re_, b_rg, jnp.zeros((pad,), F32)]).reshape(1, LANES)
    n_tiles = t // TOKEN_TILE
    max_tile_chunks = (2 * TOKEN_TILE + N_EXPERTS * (CHUNK_ROWS - 1)) // CHUNK_ROWS
    cpt = -(-max_tile_chunks // SUBLANES) * SUBLANES
    cap = cpt * CHUNK_ROWS
    xs_tiles, info, nch_f = _route_call(x1, row(g_ffn), wr, br, cap)

    max_chunks = n_tiles * max_tile_chunks + N_EXPERTS * (CHUNKS_PER_ETILE - 1)
    n_etiles = -(-max_chunks // CHUNKS_PER_ETILE)
    nch = nch_f[:, 0, :N_EXPERTS].astype(jnp.int32)
    chunk_dst, tile_expert, n_used, chunk_src = _run_tables(nch, cpt, n_etiles)

    ys_sorted = _expert_call(tile_expert, n_used, chunk_src, xs_tiles, w_eg, w_eu, w_ed)
    out = _combine_call(chunk_dst, x1, info, row(g_final), ys_sorted, cap)
    return out.reshape(b, s, d)


def kernel(x, mem, g_mix, w_in, conv_dw, conv_dw_bias, conv_ln_g, conv_ln_b, w_conv_out, ssm_lambda_re, ssm_lambda_im, ssm_log_dt, ssm_b_re, ssm_b_im, ssm_c_re, ssm_c_im, ssm_d, w_ssm_glu, g_mem, w_mem_kv, w_mem_out, w_out, g_ffn, w_router_group, b_router_group, w_router_expert, b_router_expert, w_exp_gate, w_exp_up, w_exp_down, g_final):
    assert g_mix.shape[0] == 1, "the problem fixes one layer"
    l = 0
    return _layer(
        x, mem, g_mix[l], w_in[l], conv_dw[l], conv_dw_bias[l], conv_ln_g[l], conv_ln_b[l],
        w_conv_out[l], ssm_lambda_re[l], ssm_lambda_im[l], ssm_log_dt[l], ssm_b_re[l],
        ssm_b_im[l], ssm_c_re[l], ssm_c_im[l], ssm_d[l], w_ssm_glu[l], g_mem[l], w_mem_kv[l],
        w_mem_out[l], w_out[l], g_ffn[l], w_router_group[l], b_router_group[l],
        w_router_expert[l], b_router_expert[l], w_exp_gate[l], w_exp_up[l], w_exp_down[l], g_final)
```

```python
import functools

import jax
import jax.numpy as jnp
from jax import lax
from jax.experimental import pallas as pl
from jax.experimental.pallas import tpu as pltpu

F32 = jnp.float32
BF16 = jnp.bfloat16
EPS = 1e-6

LANES = 128
SUBLANES = 8
V7X_VMEM_BYTES = 64 * 1024 * 1024
CHUNK_ROWS = 2 * SUBLANES
SSM_GROUP = 16
SSM_STATE = 64
SSM_BLOCK = LANES
CONV_KERNEL = 31
CONV_HALO = 32
HEADS = 4
HEAD_DIM = 128
MOE_GROUPS = 4
EXPERTS_PER_GROUP = 8
N_EXPERTS = MOE_GROUPS * EXPERTS_PER_GROUP
TOKEN_TILE = 512
MIX_TILE = 1024
COMBINE_TILES = 2
ROUTE_TILES = 2
EXPERT_TILE = 512
CHUNKS_PER_ETILE = EXPERT_TILE // CHUNK_ROWS
UT_TILE = 2048
VMEM_LIMIT = V7X_VMEM_BYTES * 7 // 8


def _rms(x, g):
    return x * lax.rsqrt(jnp.mean(x * x, axis=-1, keepdims=True) + EPS) * g


def _sigmoid(x):
    return 0.5 * jnp.tanh(0.5 * x) + 0.5


def _const_spec(shape):
    zeros = (0,) * len(shape)
    return pl.BlockSpec(shape, lambda *_: zeros, pipeline_mode=pl.Buffered(1))


def _kv_kernel(mem_ref, g_ref, w_ref, k_ref, v_ref):
    width = k_ref.shape[-1]
    mn = _rms(mem_ref[0], g_ref[...]).astype(BF16)
    kv = jnp.dot(mn, w_ref[...], preferred_element_type=F32)
    k_ref[0] = kv[:, :width].astype(BF16)
    v_ref[0] = kv[:, width:].astype(BF16)


def _kv_call(mem, g_mem, w_kv):
    b, m, d = mem.shape
    width = w_kv.shape[1] // 2
    return pl.pallas_call(
        _kv_kernel,
        grid=(b,),
        in_specs=[pl.BlockSpec((1, m, d), lambda i: (i, 0, 0)),
                  pl.BlockSpec((1, d), lambda i: (0, 0)),
                  pl.BlockSpec(w_kv.shape, lambda i: (0, 0))],
        out_specs=[pl.BlockSpec((1, m, width), lambda i: (i, 0, 0)),
                   pl.BlockSpec((1, m, width), lambda i: (i, 0, 0))],
        out_shape=[jax.ShapeDtypeStruct((b, m, width), BF16)] * 2,
        name="kv",
    )(mem, g_mem, w_kv)


def _ut_kernel(x_ref, g_ref, wt_ref, o_ref):
    h = _rms(x_ref[...], g_ref[...]).astype(BF16)
    ut = lax.dot_general(wt_ref[...], h, (((1,), (1,)), ((), ())), preferred_element_type=F32)
    ut = ut.reshape(ut.shape[0], ut.shape[1] // SSM_BLOCK, SSM_BLOCK)
    o_ref[...] = ut.reshape(o_ref.shape).astype(BF16)


def _ut_call(x2, g_mix, w_ssm_t):
    t, d = x2.shape
    c = w_ssm_t.shape[0]
    ts = UT_TILE
    n_groups = c // SSM_GROUP
    return pl.pallas_call(
        _ut_kernel,
        grid=(t // ts,),
        in_specs=[pl.BlockSpec((ts, d), lambda i: (i, 0)),
                  pl.BlockSpec((1, d), lambda i: (0, 0)),
                  pl.BlockSpec((c, d), lambda i: (0, 0))],
        out_specs=pl.BlockSpec((n_groups, SSM_GROUP, ts // SSM_BLOCK, SSM_BLOCK),
                               lambda i: (0, 0, i, 0)),
        out_shape=jax.ShapeDtypeStruct((n_groups, SSM_GROUP, t // SSM_BLOCK, SSM_BLOCK), BF16),
        name="ut",
    )(x2, g_mix, w_ssm_t)


def _ssm_kernel(u_ref, kt_ref, w_ref, v_ref, a_ref, y_ref, acc_ref, z_ref, zs_ref, s_ref, slab_ref,
                *, n_batch):
    rows = u_ref.shape[2]
    n_blocks = rows // n_batch
    ri = lax.broadcasted_iota(jnp.int32, (SSM_BLOCK, SSM_BLOCK), 0)
    ci = lax.broadcasted_iota(jnp.int32, (SSM_BLOCK, SSM_BLOCK), 1)
    causal = ci >= ri

    n_pairs = SSM_GROUP // 2

    def build(cp, slot):
        for half in range(2):
            kt = kt_ref[0, 2 * cp + half]
            for c in range(SSM_GROUP):
                xb = jnp.broadcast_to(kt[c:c + 1, :], (SSM_BLOCK, SSM_BLOCK))
                toe = pltpu.roll(xb, 0, 1, stride=1, stride_axis=0)
                slab_ref[slot, half * SSM_BLOCK:(half + 1) * SSM_BLOCK, c * SSM_BLOCK:(c + 1) * SSM_BLOCK] = (
                    jnp.where(causal, toe, 0.0).astype(BF16))

    def apply(cp, slot, first):
        x2 = jnp.concatenate([u_ref[0, 2 * cp], u_ref[0, 2 * cp + 1]], axis=1)
        part = jnp.dot(x2, slab_ref[slot], preferred_element_type=F32)
        wrow = pl.multiple_of(cp * 2 * SSM_BLOCK, 2 * SSM_BLOCK)
        zpart = jnp.dot(x2, w_ref[0, pl.ds(wrow, 2 * SSM_BLOCK), :], preferred_element_type=F32)
        if first:
            acc_ref[...] = part
            z_ref[...] = zpart
        else:
            acc_ref[...] += part
            z_ref[...] += zpart

    build(0, 0)
    build(1, 1)
    apply(0, 0, True)

    def two(it, carry):
        cp = 2 * it + 1
        build(cp + 1, 0)
        apply(cp, 1, False)
        build(cp + 2, 1)
        apply(cp + 1, 0, False)
        return carry

    lax.fori_loop(0, (n_pairs - 2) // 2, two, 0)
    apply(n_pairs - 1, 1, False)

    a_full = a_ref[0, 0:1, :]
    a_swap = a_ref[0, 1:2, :]
    zs_ref[...] = pltpu.roll(z_ref[...], SSM_STATE, 1)
    st = jnp.zeros((n_batch, 2 * SSM_STATE), F32)
    sw = jnp.zeros((n_batch, 2 * SSM_STATE), F32)
    for blk in range(n_blocks):
        sl = pl.ds(blk, n_batch, stride=n_blocks)
        s_ref[sl, :] = st
        st, sw = (a_full * st + a_swap * sw + z_ref[sl, :],
                  a_full * sw - a_swap * st + zs_ref[sl, :])

    sb = s_ref[...].astype(BF16)
    y3 = jnp.stack([acc_ref[:, c * SSM_BLOCK:(c + 1) * SSM_BLOCK]
                    + jnp.dot(sb, v_ref[0, c], preferred_element_type=F32)
                    for c in range(SSM_GROUP)], axis=0)
    y_ref[...] = y3.reshape(y_ref.shape)


def _ssm_call(u4, kt, wz, vy, acoef, n_batch):
    g, c, rows, blk = u4.shape
    width = c * blk
    return pl.pallas_call(
        functools.partial(_ssm_kernel, n_batch=n_batch),
        grid=(g,),
        in_specs=[pl.BlockSpec((1, c, rows, blk), lambda i: (i, 0, 0, 0)),
                  pl.BlockSpec((1, c, c, blk), lambda i: (i, 0, 0, 0)),
                  pl.BlockSpec((1, width, 2 * SSM_STATE), lambda i: (i, 0, 0)),
                  pl.BlockSpec((1, c, 2 * SSM_STATE, blk), lambda i: (i, 0, 0, 0)),
                  pl.BlockSpec((1, 2, 2 * SSM_STATE), lambda i: (i, 0, 0))],
        out_specs=pl.BlockSpec((c, rows * blk), lambda i: (i, 0)),
        out_shape=jax.ShapeDtypeStruct((g * c, rows * blk), F32),
        scratch_shapes=[pltpu.VMEM((rows, width), F32),
                        pltpu.VMEM((rows, 2 * SSM_STATE), F32),
                        pltpu.VMEM((rows, 2 * SSM_STATE), F32),
                        pltpu.VMEM((rows, 2 * SSM_STATE), F32),
                        pltpu.VMEM((2, 2 * blk, width), BF16)],
        name="ssm",
    )(u4, kt, wz, vy, acoef)


def _ssm_tables(lam_re, lam_im, log_dt, b_re, b_im, c_re, c_im, d):
    hi = lax.Precision.HIGHEST
    g = lam_re.shape[0]
    dt = jnp.exp(log_dt)[:, None]
    er, ei = lam_re * dt, lam_im * dt
    cat = jnp.concatenate
    kk = jnp.arange(SSM_BLOCK + 1, dtype=F32)
    kdesc = jnp.arange(SSM_BLOCK - 1, -1, -1, dtype=F32)[None, :, None]
    ppk_m, ppk_a = jnp.exp(er[:, :, None] * kk), ei[:, :, None] * kk
    ppk_r, ppk_i = ppk_m * jnp.cos(ppk_a), ppk_m * jnp.sin(ppk_a)
    prev_m, prev_a = jnp.exp(kdesc * er[:, None, :]), kdesc * ei[:, None, :]
    prr, pir = prev_m * jnp.cos(prev_a), prev_m * jnp.sin(prev_a)
    nr, ni = ppk_r[:, :, 1] - 1.0, ppk_i[:, :, 1]
    den = lam_re * lam_re + lam_im * lam_im
    fr = (nr * lam_re + ni * lam_im) / den
    fi = (ni * lam_re - nr * lam_im) / den
    bbr = jnp.swapaxes(fr[:, :, None] * b_re - fi[:, :, None] * b_im, 1, 2)
    bbi = jnp.swapaxes(fr[:, :, None] * b_im + fi[:, :, None] * b_re, 1, 2)
    cb_r = c_re[:, None] * bbr[:, :, None, :] - c_im[:, None] * bbi[:, :, None, :]
    cb_i = c_re[:, None] * bbi[:, :, None, :] + c_im[:, None] * bbr[:, :, None, :]
    cb = cat([cb_r, -cb_i], axis=-1).reshape(g, SSM_GROUP * SSM_GROUP, 2 * SSM_STATE)
    pk = cat([ppk_r[:, :, :SSM_BLOCK], ppk_i[:, :, :SSM_BLOCK]], axis=1)
    kt = jnp.einsum("gmq,gqk->gmk", cb, pk, precision=hi).reshape(g, SSM_GROUP, SSM_GROUP, SSM_BLOCK)
    dmat = jnp.eye(SSM_GROUP, dtype=F32)[None] * d.reshape(g, 1, SSM_GROUP)
    kt = kt + dmat[..., None] * (jnp.arange(SSM_BLOCK) == 0).astype(F32)
    wz = (cat([prr, prr], -1)[:, None] * cat([bbr, bbi], -1)[:, :, None, :]
          + cat([pir, pir], -1)[:, None] * cat([-bbi, bbr], -1)[:, :, None, :])
    wz = wz.astype(BF16).reshape(g, SSM_GROUP * SSM_BLOCK, 2 * SSM_STATE)
    pt_r, pt_i = ppk_r[:, :, 1:], ppk_i[:, :, 1:]
    vy = (cat([c_re, -c_re], -1)[..., None] * cat([pt_r, pt_i], axis=1)[:, None]
          + cat([-c_im, -c_im], -1)[..., None] * cat([pt_i, pt_r], axis=1)[:, None]).astype(BF16)
    ar, ai = ppk_r[:, :, SSM_BLOCK], ppk_i[:, :, SSM_BLOCK]
    acoef = jnp.stack([cat([ar, ar], -1), cat([-ai, ai], -1)], axis=1)
    return kt, wz, vy, acoef


def _mix_kernel(x_ref, yt_ref, k_ref, v_ref, g_ref, wc_ref, wq_ref, wg_ref, dw_ref, dwb_ref,
                lng_ref, lnb_ref, wpw_ref, wglu_ref, wo_ref, wout_ref, o_ref, vext_ref, vsh_ref):
    ts = x_ref.shape[1]
    d = x_ref.shape[2]
    cw = dw_ref.shape[1]
    x = x_ref[0]
    h = _rms(x, g_ref[...]).astype(BF16)

    ci = jnp.dot(h, wc_ref[...], preferred_element_type=F32)
    v = ci[:, :cw] * _sigmoid(ci[:, cw:])
    q = jnp.dot(h, wq_ref[...], preferred_element_type=F32)

    @pl.when(pl.program_id(1) == 0)
    def _():
        vext_ref[0:CONV_HALO, :] = jnp.zeros((CONV_HALO, cw), F32)

    vext_ref[CONV_HALO:CONV_HALO + ts, :] = v
    acc = jnp.broadcast_to(dwb_ref[...], (ts, cw))
    first = CONV_HALO - (CONV_KERNEL - 1)
    for phase in range(SUBLANES):
        offs = [first + k for k in range(CONV_KERNEL) if (first + k) % SUBLANES == phase]
        if not offs:
            continue
        span = offs[-1] - offs[0] + ts
        vsh_ref[0:span, :] = vext_ref[offs[0]:offs[0] + span, :]
        for off in offs:
            acc = acc + dw_ref[off - first:off - first + 1, :] * vsh_ref[off - offs[0]:off - offs[0] + ts, :]
    vext_ref[0:CONV_HALO, :] = vext_ref[ts:ts + CONV_HALO, :]
    mu = jnp.mean(acc, axis=-1, keepdims=True)
    xc = acc - mu
    var = jnp.mean(xc * xc, axis=-1, keepdims=True)
    ln = xc * lax.rsqrt(var + EPS) * lng_ref[...] + lnb_ref[...]
    sw = ln * _sigmoid(ln)
    y_conv = jnp.dot(sw.astype(BF16), wpw_ref[...], preferred_element_type=F32)
    merged = _sigmoid(jnp.dot(h, wg_ref[:, 0:d], preferred_element_type=F32)) * y_conv

    gy = jax.nn.gelu(yt_ref[...]).astype(BF16)
    z = lax.dot_general(gy, wglu_ref[...], (((0,), (0,)), ((), ())), preferred_element_type=F32)
    y_ssm = z[:, :d] * _sigmoid(z[:, d:])
    merged = merged + _sigmoid(jnp.dot(h, wg_ref[:, d:2 * d], preferred_element_type=F32)) * y_ssm

    kk = k_ref[0]
    vv = v_ref[0]
    outs = []
    for hd in range(HEADS):
        sl = slice(hd * HEAD_DIM, (hd + 1) * HEAD_DIM)
        s = lax.dot_general(q[:, sl].astype(BF16), kk[:, sl], (((1,), (1,)), ((), ())),
                            preferred_element_type=F32) * (HEAD_DIM ** -0.5)
        p = jnp.exp(s - jnp.max(s, axis=-1, keepdims=True))
        den = jnp.sum(p, axis=-1, keepdims=True)
        o = jnp.dot(p.astype(BF16), vv[:, sl], preferred_element_type=F32) / den
        outs.append(o.astype(BF16))
    y_mem = jnp.dot(jnp.concatenate(outs, axis=1), wo_ref[...], preferred_element_type=F32)
    merged = merged + _sigmoid(jnp.dot(h, wg_ref[:, 2 * d:3 * d], preferred_element_type=F32)) * y_mem

    o_ref[0] = x + jnp.dot(merged.astype(BF16), wout_ref[...], preferred_element_type=F32)


def _mix_call(x, yt, kmem, vmem, g_mix, wc, wq, wg, dw, dwb, lng, lnb, wpw, wglu, wo, wout):
    b, s, d = x.shape
    ts = MIX_TILE
    nst = s // ts
    cw = dw.shape[1]
    m = kmem.shape[1]
    consts = [g_mix, wc, wq, wg, dw, dwb, lng, lnb, wpw, wglu, wo, wout]
    return pl.pallas_call(
        _mix_kernel,
        grid=(b, nst),
        in_specs=[pl.BlockSpec((1, ts, d), lambda i, j: (i, j, 0)),
                  pl.BlockSpec((yt.shape[0], ts), lambda i, j: (0, i * nst + j)),
                  pl.BlockSpec((1, m, kmem.shape[2]), lambda i, j: (i, 0, 0)),
                  pl.BlockSpec((1, m, vmem.shape[2]), lambda i, j: (i, 0, 0))]
                 + [_const_spec(c.shape) for c in consts],
        out_specs=pl.BlockSpec((1, ts, d), lambda i, j: (i, j, 0)),
        out_shape=jax.ShapeDtypeStruct((b, s, d), F32),
        scratch_shapes=[pltpu.VMEM((ts + CONV_HALO, cw), F32),
                        pltpu.VMEM((ts + CONV_HALO, cw), F32)],
        compiler_params=pltpu.CompilerParams(
            dimension_semantics=("arbitrary", "arbitrary"), vmem_limit_bytes=VMEM_LIMIT),
        name="mix",
    )(x, yt, kmem, vmem, *consts)


def _route_kernel(x_ref, g_ref, wr_ref, br_ref, xs_ref, info_ref, nch_ref):
    ts = x_ref.shape[0] // ROUTE_TILES
    cpt = xs_ref.shape[0] // ROUTE_TILES
    for sub in range(ROUTE_TILES):
        rows = slice(sub * ts, (sub + 1) * ts)
        xs, info, nch = _route_tile(x_ref[rows, :], g_ref[...], wr_ref, br_ref[...], cpt * CHUNK_ROWS)
        xs_ref[sub * cpt:(sub + 1) * cpt] = xs.reshape(cpt, CHUNK_ROWS, xs.shape[-1])
        info_ref[rows, :] = info
        nch_ref[sub] = nch


def _route_tile(x, g, wr_ref, br, cap):
    ts = x.shape[0]
    h2 = _rms(x, g)
    hb = h2.astype(BF16)
    hl = (h2 - hb.astype(F32)).astype(BF16)
    hw = jnp.dot(hb, wr_ref[...], preferred_element_type=F32)
    logits = (hw[:, :LANES] + hw[:, LANES:]
              + jnp.dot(hl, wr_ref[:, :LANES], preferred_element_type=F32)) + br
    lt = logits.T
    le = lt[0:N_EXPERTS]
    lg = lt[N_EXPERTS:N_EXPERTS + SUBLANES]
    neg = jnp.float32(-1e30)
    big = jnp.float32(1e9)
    g_f = lax.broadcasted_iota(jnp.int32, (SUBLANES, ts), 0).astype(F32)
    e_f = lax.broadcasted_iota(jnp.int32, (N_EXPERTS, ts), 0).astype(F32)

    gmask = g_f < MOE_GROUPS
    gmax = jnp.max(jnp.where(gmask, lg, neg), axis=0, keepdims=True)
    gidx = jnp.min(jnp.where(gmask & (lg == gmax), g_f, big), axis=0, keepdims=True)
    gsum = jnp.sum(jnp.where(gmask, jnp.exp(jnp.minimum(lg - gmax, 0.0)), 0.0), axis=0, keepdims=True)
    p_top = 1.0 / gsum
    emask = jnp.floor(e_f * (1.0 / EXPERTS_PER_GROUP)) == gidx
    m1 = jnp.max(jnp.where(emask, le, neg), axis=0, keepdims=True)
    i1 = jnp.min(jnp.where(emask & (le == m1), e_f, big), axis=0, keepdims=True)
    emask2 = emask & (e_f != i1)
    m2 = jnp.max(jnp.where(emask2, le, neg), axis=0, keepdims=True)
    i2 = jnp.min(jnp.where(emask2 & (le == m2), e_f, big), axis=0, keepdims=True)
    r = jnp.exp(m2 - m1)
    w1 = p_top / (1.0 + r)
    w2 = p_top * r / (1.0 + r)

    sel1 = e_f == i1
    sel2 = e_f == i2
    occ = jnp.where(sel1 | sel2, 1.0, 0.0)
    tr = lax.broadcasted_iota(jnp.int32, (ts, ts), 0)
    tc = lax.broadcasted_iota(jnp.int32, (ts, ts), 1)
    earlier = jnp.where(tr < tc, 1.0, 0.0).astype(BF16)
    rank = jnp.dot(occ.astype(BF16), earlier, preferred_element_type=F32)
    cnt = jnp.sum(occ, axis=1, keepdims=True)
    nch = jnp.floor((cnt + (CHUNK_ROWS - 1)) * (1.0 / CHUNK_ROWS))
    er = lax.broadcasted_iota(jnp.int32, (N_EXPERTS, N_EXPERTS), 0)
    ec = lax.broadcasted_iota(jnp.int32, (N_EXPERTS, N_EXPERTS), 1)
    lower = jnp.where(ec < er, 1.0, 0.0).astype(BF16)
    nch_b = jnp.broadcast_to(nch, (N_EXPERTS, LANES))
    start = jnp.dot(lower, nch_b.astype(BF16), preferred_element_type=F32)[:, 0:1] * CHUNK_ROWS
    slot = start + rank
    pos1 = jnp.sum(jnp.where(sel1, slot, 0.0), axis=0, keepdims=True)
    pos2 = jnp.sum(jnp.where(sel2, slot, 0.0), axis=0, keepdims=True)

    rowid = lax.broadcasted_iota(jnp.int32, (cap, ts), 0)
    p = jnp.where((rowid == pos1.astype(jnp.int32)) | (rowid == pos2.astype(jnp.int32)), 1.0, 0.0)
    xs = jnp.dot(p.astype(BF16), hb, preferred_element_type=F32).astype(BF16)
    sub = lax.broadcasted_iota(jnp.int32, (LANES, ts), 0)
    info_t = jnp.where(sub == 0, pos1, jnp.where(sub == 1, pos2,
                       jnp.where(sub == 2, w1, jnp.where(sub == 3, w2, 0.0))))
    nch_rows = jnp.concatenate([nch_b, jnp.zeros((LANES - N_EXPERTS, LANES), F32)], axis=0).T[0:SUBLANES, :]
    return xs, info_t.T, nch_rows


def _route_call(x1, g_ffn, wr, br, cap):
    t, d = x1.shape
    nt = t // TOKEN_TILE
    ts = TOKEN_TILE * ROUTE_TILES
    return pl.pallas_call(
        _route_kernel,
        grid=(t // ts,),
        in_specs=[pl.BlockSpec((ts, d), lambda i: (i, 0)),
                  pl.BlockSpec((1, d), lambda i: (0, 0)),
                  pl.BlockSpec(wr.shape, lambda i: (0, 0)),
                  pl.BlockSpec((1, LANES), lambda i: (0, 0))],
        out_specs=[pl.BlockSpec((ROUTE_TILES * cap // CHUNK_ROWS, CHUNK_ROWS, d), lambda i: (i, 0, 0)),
                   pl.BlockSpec((ts, LANES), lambda i: (i, 0)),
                   pl.BlockSpec((ROUTE_TILES, SUBLANES, LANES), lambda i: (i, 0, 0))],
        out_shape=[jax.ShapeDtypeStruct((nt * cap // CHUNK_ROWS, CHUNK_ROWS, d), BF16),
                   jax.ShapeDtypeStruct((t, LANES), F32),
                   jax.ShapeDtypeStruct((nt, SUBLANES, LANES), F32)],
        compiler_params=pltpu.CompilerParams(vmem_limit_bytes=VMEM_LIMIT),
        name="route",
    )(x1, g_ffn, wr, br)


def _expert_kernel(te_ref, nu_ref, src_ref, first_ref, wslot_ref, nxt_ref, xs_ref, wg_ref, wu_ref, wd_ref,
                   o_ref, xbuf_ref, wgb_ref, wub_ref, wdb_ref, sem_ref, wsem_ref):
    i = pl.program_id(0)
    n_used = nu_ref[0]
    slot = i % 2

    def weight_copies(e, ws):
        return (pltpu.make_async_copy(wg_ref.at[e], wgb_ref.at[ws], wsem_ref.at[ws, 0]),
                pltpu.make_async_copy(wu_ref.at[e], wub_ref.at[ws], wsem_ref.at[ws, 1]),
                pltpu.make_async_copy(wd_ref.at[e], wdb_ref.at[ws], wsem_ref.at[ws, 2]))

    def chunk_copy(tile, c, buf):
        return pltpu.make_async_copy(
            xs_ref.at[src_ref[tile * CHUNKS_PER_ETILE + c]], xbuf_ref.at[buf, c], sem_ref.at[buf])

    def gather(tile, buf):
        for c in range(CHUNKS_PER_ETILE):
            chunk_copy(tile, c, buf).start(priority=c % 2)

    def drain(buf):
        pltpu.make_async_copy(xs_ref.at[pl.ds(0, CHUNKS_PER_ETILE)], xbuf_ref.at[buf], sem_ref.at[buf]).wait()

    @pl.when(i == 0)
    def _():
        gather(0, 0)
        for cp in weight_copies(te_ref[0], 0):
            cp.start()

    @pl.when(i < n_used)
    def _():
        ws = wslot_ref[i]

        @pl.when(first_ref[i] == 1)
        def _():
            for cp in weight_copies(te_ref[i], ws):
                cp.wait()

            @pl.when(nxt_ref[i] >= 0)
            def _():
                for cp in weight_copies(nxt_ref[i], 1 - ws):
                    cp.start()

        nxt = jnp.minimum(i + 1, n_used - 1)
        gather(nxt, 1 - slot)
        drain(slot)
        x = xbuf_ref[slot].reshape(EXPERT_TILE, xbuf_ref.shape[-1])
        gate = jnp.dot(x, wgb_ref[ws].astype(BF16), preferred_element_type=F32)
        up = jnp.dot(x, wub_ref[ws].astype(BF16), preferred_element_type=F32)
        act = (gate * _sigmoid(gate) * up).astype(BF16)
        y = jnp.dot(act, wdb_ref[ws].astype(BF16), preferred_element_type=F32).astype(BF16)
        o_ref[...] = y.reshape(o_ref.shape)

        @pl.when(i == n_used - 1)
        def _():
            drain(1 - slot)

    @pl.when(i >= n_used)
    def _():
        o_ref[...] = jnp.zeros_like(o_ref)


def _expert_call(tile_expert, n_used, chunk_src, first, wslot, nxt, xs, wg, wu, wd):
    d = xs.shape[-1]
    de = wg.shape[2]
    nt = tile_expert.shape[0]
    return pl.pallas_call(
        _expert_kernel,
        grid_spec=pltpu.PrefetchScalarGridSpec(
            num_scalar_prefetch=6,
            grid=(nt,),
            in_specs=[pl.BlockSpec(memory_space=pl.ANY)] * 4,
            out_specs=pl.BlockSpec((CHUNKS_PER_ETILE, CHUNK_ROWS, d), lambda i, *_: (i, 0, 0)),
            scratch_shapes=[pltpu.VMEM((2, CHUNKS_PER_ETILE, CHUNK_ROWS, d), BF16),
                            pltpu.VMEM((2, d, de), F32),
                            pltpu.VMEM((2, d, de), F32),
                            pltpu.VMEM((2, de, d), F32),
                            pltpu.SemaphoreType.DMA((2,)),
                            pltpu.SemaphoreType.DMA((2, 3))]),
        out_shape=jax.ShapeDtypeStruct((nt * CHUNKS_PER_ETILE, CHUNK_ROWS, d), BF16),
        compiler_params=pltpu.CompilerParams(dimension_semantics=("arbitrary",)),
        name="expert",
    )(tile_expert, n_used, chunk_src, first, wslot, nxt, xs, wg, wu, wd)


def _combine_kernel(dst_ref, x_ref, info_ref, g_ref, ys_ref, o_ref, ybuf_ref, sem_ref):
    i = pl.program_id(0)
    n_tiles = pl.num_programs(0)
    ts = x_ref.shape[0] // COMBINE_TILES
    n_chunks = ybuf_ref.shape[1]
    cap = n_chunks // COMBINE_TILES * CHUNK_ROWS
    slot = i % 2

    def chunk_copy(tile, k, buf):
        return pltpu.make_async_copy(ys_ref.at[dst_ref[tile * n_chunks + k]], ybuf_ref.at[buf, k], sem_ref.at[buf])

    def gather(tile, buf):
        for k in range(n_chunks):
            chunk_copy(tile, k, buf).start(priority=k % 2)

    def drain(buf):
        pltpu.make_async_copy(ys_ref.at[pl.ds(0, n_chunks)], ybuf_ref.at[buf], sem_ref.at[buf]).wait()

    @pl.when(i == 0)
    def _():
        gather(0, 0)

    nxt = jnp.minimum(i + 1, n_tiles - 1)
    gather(nxt, 1 - slot)
    drain(slot)

    rowid = lax.broadcasted_iota(jnp.int32, (ts, cap), 1)
    for sub in range(COMBINE_TILES):
        rows = slice(sub * ts, (sub + 1) * ts)
        info = info_ref[rows, :]
        pos1 = info[:, 0:1].astype(jnp.int32)
        pos2 = info[:, 1:2].astype(jnp.int32)
        w1 = info[:, 2:3]
        w2 = info[:, 3:4]
        ys = ybuf_ref[slot, sub * (n_chunks // COMBINE_TILES):(sub + 1) * (n_chunks // COMBINE_TILES)]
        ys = ys.reshape(cap, ybuf_ref.shape[-1])
        pw = jnp.where(rowid == pos1, w1, jnp.where(rowid == pos2, w2, 0.0)).astype(BF16)
        y = jnp.dot(pw, ys, preferred_element_type=F32)
        o_ref[rows, :] = _rms(x_ref[rows, :] + y, g_ref[...])

    @pl.when(i == n_tiles - 1)
    def _():
        drain(1 - slot)


def _combine_call(chunk_dst, x1, info, g_final, ys_sorted, cap):
    t, d = x1.shape
    ts = TOKEN_TILE * COMBINE_TILES
    return pl.pallas_call(
        _combine_kernel,
        grid_spec=pltpu.PrefetchScalarGridSpec(
            num_scalar_prefetch=1,
            grid=(t // ts,),
            in_specs=[pl.BlockSpec((ts, d), lambda i, dst: (i, 0)),
                      pl.BlockSpec((ts, LANES), lambda i, dst: (i, 0)),
                      pl.BlockSpec((1, d), lambda i, dst: (0, 0)),
                      pl.BlockSpec(memory_space=pl.ANY)],
            out_specs=pl.BlockSpec((ts, d), lambda i, dst: (i, 0)),
            scratch_shapes=[pltpu.VMEM((2, COMBINE_TILES * cap // CHUNK_ROWS, CHUNK_ROWS, d), BF16),
                            pltpu.SemaphoreType.DMA((2,))]),
        out_shape=jax.ShapeDtypeStruct((t, d), F32),
        compiler_params=pltpu.CompilerParams(dimension_semantics=("arbitrary",),
                                             vmem_limit_bytes=VMEM_LIMIT),
        name="combine",
    )(chunk_dst, x1, info, g_final, ys_sorted)


def _run_tables(nch, cpt, n_etiles):
    n_tiles, n_exp = nch.shape
    per = CHUNKS_PER_ETILE
    tcum = jnp.cumsum(nch, axis=1)
    toff = tcum - nch
    ecum = jnp.cumsum(nch, axis=0)
    etot = ecum[-1]
    eseg = -(-etot // per) * per
    segcum = jnp.cumsum(eseg)
    ebase = segcum - eseg
    eoff = ebase[None, :] + ecum - nch
    n_used = (segcum[-1] // per).reshape(1)

    k = jnp.arange(cpt, dtype=jnp.int32)[None, :, None]
    in_run = (k >= toff[:, None, :]) & (k < tcum[:, None, :])
    chunk_dst = jnp.sum(jnp.where(in_run, (eoff - toff)[:, None, :] + k, 0), axis=-1)

    first = jnp.arange(n_etiles, dtype=jnp.int32)[:, None] * per
    owner = (first >= ebase[None, :]) & (first < segcum[None, :])
    tile_expert = jnp.sum(jnp.where(owner, jnp.arange(n_exp, dtype=jnp.int32)[None, :], 0), axis=-1)

    pick = lambda a: jnp.sum(jnp.where(owner[:, None, :], a[None, :, :], 0), axis=-1)
    run_lo, run_n = pick(eoff), pick(nch)
    shift = pick(jnp.arange(n_tiles, dtype=jnp.int32)[:, None] * cpt + toff - eoff)
    c = (first + jnp.arange(per, dtype=jnp.int32)[None, :])[:, :, None]
    hit = (c >= run_lo[:, None, :]) & (c < (run_lo + run_n)[:, None, :])
    chunk_src = jnp.sum(jnp.where(hit, shift[:, None, :] + c, 0), axis=-1)

    eidx = jnp.arange(n_exp, dtype=jnp.int32)
    starts_here = owner & (first == ebase[None, :])
    is_first = jnp.any(starts_here, axis=-1).astype(jnp.int32)
    wslot = (jnp.cumsum(is_first) - 1) % 2
    later_used = (eidx[None, :] > eidx[:, None]) & (eseg[None, :] > 0)
    next_e = jnp.min(jnp.where(later_used, eidx[None, :], n_exp), axis=-1)
    next_e = jnp.where(next_e == n_exp, -1, next_e)
    nxt = jnp.sum(jnp.where(owner, next_e[None, :], 0), axis=-1)
    i32 = lambda a: a.reshape(-1).astype(jnp.int32)
    return (i32(chunk_dst), i32(tile_expert), i32(n_used), i32(chunk_src),
            i32(is_first), i32(wslot), i32(nxt))


def _layer(x, mem, g_mix, w_in, conv_dw, conv_dw_bias, conv_ln_g, conv_ln_b, w_conv_out,
           lam_re, lam_im, log_dt, b_re, b_im, c_re, c_im, ssm_d, w_ssm_glu, g_mem, w_mem_kv,
           w_mem_out, w_out, g_ffn, w_rg, b_rg, w_re, b_re_, w_eg, w_eu, w_ed, g_final):
    b, s, d = x.shape
    t = b * s
    cw = conv_dw.shape[1]
    sw = ssm_d.shape[0]
    qw = w_mem_out.shape[0]
    row = lambda a: a.reshape(1, -1)

    o0, o1, o2 = 2 * cw, 2 * cw + sw, 2 * cw + sw + qw
    wc = w_in[:, :o0].astype(BF16)
    w_ssm_t = w_in[:, o0:o1].T.astype(BF16)
    wq = w_in[:, o1:o2].astype(BF16)
    wg = w_in[:, o2:].astype(BF16)

    kmem, vmem = _kv_call(mem, row(g_mem), w_mem_kv.astype(BF16))

    ut = _ut_call(x.reshape(t, d), row(g_mix), w_ssm_t)
    kt, wz, vy, acoef = _ssm_tables(lam_re, lam_im, log_dt, b_re, b_im, c_re, c_im, ssm_d)
    yt = _ssm_call(ut, kt, wz, vy, acoef, b)

    dw = jnp.concatenate([conv_dw, jnp.zeros((CONV_HALO - CONV_KERNEL, cw), F32)], axis=0)
    x1 = _mix_call(x, yt, kmem, vmem, row(g_mix), wc, wq, wg, dw, row(conv_dw_bias),
                   row(conv_ln_g), row(conv_ln_b), w_conv_out.astype(BF16),
                   w_ssm_glu.astype(BF16), w_mem_out.astype(BF16), w_out.astype(BF16))
    x1 = x1.reshape(t, d)

    pad = LANES - N_EXPERTS - MOE_GROUPS
    wr = jnp.concatenate([w_re, w_rg, jnp.zeros((d, pad), F32)], axis=1)
    wr_hi = wr.astype(BF16)
    wr = jnp.concatenate([wr_hi, (wr - wr_hi.astype(F32)).astype(BF16)], axis=1)
    br = jnp.concatenate([b_re_, b_rg, jnp.zeros((pad,), F32)]).reshape(1, LANES)
    n_tiles = t // TOKEN_TILE
    max_tile_chunks = (2 * TOKEN_TILE + N_EXPERTS * (CHUNK_ROWS - 1)) // CHUNK_ROWS
    cpt = -(-max_tile_chunks // SUBLANES) * SUBLANES
    cap = cpt * CHUNK_ROWS
    xs_tiles, info, nch_f = _route_call(x1, row(g_ffn), wr, br, cap)

    max_chunks = n_tiles * max_tile_chunks + N_EXPERTS * (CHUNKS_PER_ETILE - 1)
    n_etiles = -(-max_chunks // CHUNKS_PER_ETILE)
    nch = nch_f[:, 0, :N_EXPERTS].astype(jnp.int32)
    chunk_dst, tile_expert, n_used, chunk_src, is_first, wslot, nxt = _run_tables(nch, cpt, n_etiles)

    ys_sorted = _expert_call(tile_expert, n_used, chunk_src, is_first, wslot, nxt, xs_tiles, w_eg, w_eu, w_ed)
    out = _combine_call(chunk_dst, x1, info, row(g_final), ys_sorted, cap)
    return out.reshape(b, s, d)


def kernel(x, mem, g_mix, w_in, conv_dw, conv_dw_bias, conv_ln_g, conv_ln_b, w_conv_out, ssm_lambda_re, ssm_lambda_im, ssm_log_dt, ssm_b_re, ssm_b_im, ssm_c_re, ssm_c_im, ssm_d, w_ssm_glu, g_mem, w_mem_kv, w_mem_out, w_out, g_ffn, w_router_group, b_router_group, w_router_expert, b_router_expert, w_exp_gate, w_exp_up, w_exp_down, g_final):
    assert g_mix.shape[0] == 1, "the problem fixes one layer"
    l = 0
    return _layer(
        x, mem, g_mix[l], w_in[l], conv_dw[l], conv_dw_bias[l], conv_ln_g[l], conv_ln_b[l],
        w_conv_out[l], ssm_lambda_re[l], ssm_lambda_im[l], ssm_log_dt[l], ssm_b_re[l],
        ssm_b_im[l], ssm_c_re[l], ssm_c_im[l], ssm_d[l], w_ssm_glu[l], g_mem[l], w_mem_kv[l],
        w_mem_out[l], w_out[l], g_ffn[l], w_router_group[l], b_router_group[l],
        w_router_expert[l], b_router_expert[l], w_exp_gate[l], w_exp_up[l], w_exp_down[l], g_final)
```

```python
import functools

import jax
import jax.numpy as jnp
from jax import lax
from jax.experimental import pallas as pl
from jax.experimental.pallas import tpu as pltpu

F32 = jnp.float32
BF16 = jnp.bfloat16
EPS = 1e-6

LANES = 128
SUBLANES = 8
V7X_VMEM_BYTES = 64 * 1024 * 1024
CHUNK_ROWS = 2 * SUBLANES
SSM_GROUP = 16
SSM_STATE = 64
SSM_BLOCK = LANES
CONV_KERNEL = 31
CONV_HALO = 32
HEADS = 4
HEAD_DIM = 128
MOE_GROUPS = 4
EXPERTS_PER_GROUP = 8
N_EXPERTS = MOE_GROUPS * EXPERTS_PER_GROUP
TOKEN_TILE = 512
MIX_TILE = 1024
COMBINE_TILES = 2
ROUTE_TILES = 2
SSM_GROUPS_PER_STEP = 2
EXPERT_TILE = 512
CHUNKS_PER_ETILE = EXPERT_TILE // CHUNK_ROWS
UT_TILE = 2048
VMEM_LIMIT = V7X_VMEM_BYTES * 7 // 8


def _rms(x, g):
    return x * lax.rsqrt(jnp.mean(x * x, axis=-1, keepdims=True) + EPS) * g


def _sigmoid(x):
    return 0.5 * jnp.tanh(0.5 * x) + 0.5


def _const_spec(shape):
    zeros = (0,) * len(shape)
    return pl.BlockSpec(shape, lambda *_: zeros, pipeline_mode=pl.Buffered(1))


def _kv_kernel(mem_ref, g_ref, w_ref, k_ref, v_ref):
    width = k_ref.shape[-1]
    mn = _rms(mem_ref[0], g_ref[...]).astype(BF16)
    kv = jnp.dot(mn, w_ref[...], preferred_element_type=F32)
    k_ref[0] = kv[:, :width].astype(BF16)
    v_ref[0] = kv[:, width:].astype(BF16)


def _kv_call(mem, g_mem, w_kv):
    b, m, d = mem.shape
    width = w_kv.shape[1] // 2
    return pl.pallas_call(
        _kv_kernel,
        grid=(b,),
        in_specs=[pl.BlockSpec((1, m, d), lambda i: (i, 0, 0)),
                  pl.BlockSpec((1, d), lambda i: (0, 0)),
                  pl.BlockSpec(w_kv.shape, lambda i: (0, 0))],
        out_specs=[pl.BlockSpec((1, m, width), lambda i: (i, 0, 0)),
                   pl.BlockSpec((1, m, width), lambda i: (i, 0, 0))],
        out_shape=[jax.ShapeDtypeStruct((b, m, width), BF16)] * 2,
        name="kv",
    )(mem, g_mem, w_kv)


def _ut_kernel(x_ref, g_ref, wt_ref, o_ref):
    h = _rms(x_ref[...], g_ref[...]).astype(BF16)
    ut = lax.dot_general(wt_ref[...], h, (((1,), (1,)), ((), ())), preferred_element_type=F32)
    ut = ut.reshape(ut.shape[0], ut.shape[1] // SSM_BLOCK, SSM_BLOCK)
    o_ref[...] = ut.reshape(o_ref.shape).astype(BF16)


def _ut_call(x2, g_mix, w_ssm_t):
    t, d = x2.shape
    c = w_ssm_t.shape[0]
    ts = UT_TILE
    n_groups = c // SSM_GROUP
    return pl.pallas_call(
        _ut_kernel,
        grid=(t // ts,),
        in_specs=[pl.BlockSpec((ts, d), lambda i: (i, 0)),
                  pl.BlockSpec((1, d), lambda i: (0, 0)),
                  pl.BlockSpec((c, d), lambda i: (0, 0))],
        out_specs=pl.BlockSpec((n_groups, SSM_GROUP, ts // SSM_BLOCK, SSM_BLOCK),
                               lambda i: (0, 0, i, 0)),
        out_shape=jax.ShapeDtypeStruct((n_groups, SSM_GROUP, t // SSM_BLOCK, SSM_BLOCK), BF16),
        name="ut",
    )(x2, g_mix, w_ssm_t)


def _ssm_kernel(*refs, n_batch):
    for gi in range(SSM_GROUPS_PER_STEP):
        _ssm_group(gi, *refs, n_batch=n_batch)


def _ssm_group(gi, u_ref, kt_ref, w_ref, v_ref, a_ref, y_ref, acc_ref, z_ref, zs_ref, s_ref, slab_ref,
               *, n_batch):
    rows = u_ref.shape[2]
    n_blocks = rows // n_batch
    ri = lax.broadcasted_iota(jnp.int32, (SSM_BLOCK, SSM_BLOCK), 0)
    ci = lax.broadcasted_iota(jnp.int32, (SSM_BLOCK, SSM_BLOCK), 1)
    causal = ci >= ri

    n_pairs = SSM_GROUP // 2

    def build(cp, slot):
        for half in range(2):
            kt = kt_ref[gi,2 * cp + half]
            for c in range(SSM_GROUP):
                xb = jnp.broadcast_to(kt[c:c + 1, :], (SSM_BLOCK, SSM_BLOCK))
                toe = pltpu.roll(xb, 0, 1, stride=1, stride_axis=0)
                slab_ref[slot, half * SSM_BLOCK:(half + 1) * SSM_BLOCK, c * SSM_BLOCK:(c + 1) * SSM_BLOCK] = (
                    jnp.where(causal, toe, 0.0).astype(BF16))

    def apply(cp, slot, first):
        x2 = jnp.concatenate([u_ref[gi,2 * cp], u_ref[gi,2 * cp + 1]], axis=1)
        part = jnp.dot(x2, slab_ref[slot], preferred_element_type=F32)
        wrow = pl.multiple_of(cp * 2 * SSM_BLOCK, 2 * SSM_BLOCK)
        zpart = jnp.dot(x2, w_ref[gi,pl.ds(wrow, 2 * SSM_BLOCK), :], preferred_element_type=F32)
        if first:
            acc_ref[...] = part
            z_ref[...] = zpart
        else:
            acc_ref[...] += part
            z_ref[...] += zpart

    build(0, 0)
    build(1, 1)
    apply(0, 0, True)

    def two(it, carry):
        cp = 2 * it + 1
        build(cp + 1, 0)
        apply(cp, 1, False)
        build(cp + 2, 1)
        apply(cp + 1, 0, False)
        return carry

    lax.fori_loop(0, (n_pairs - 2) // 2, two, 0)
    apply(n_pairs - 1, 1, False)

    a_full = a_ref[gi,0:1, :]
    a_swap = a_ref[gi,1:2, :]
    zs_ref[...] = pltpu.roll(z_ref[...], SSM_STATE, 1)
    st = jnp.zeros((n_batch, 2 * SSM_STATE), F32)
    sw = jnp.zeros((n_batch, 2 * SSM_STATE), F32)
    for blk in range(n_blocks):
        sl = pl.ds(blk, n_batch, stride=n_blocks)
        s_ref[sl, :] = st
        st, sw = (a_full * st + a_swap * sw + z_ref[sl, :],
                  a_full * sw - a_swap * st + zs_ref[sl, :])

    sb = s_ref[...].astype(BF16)
    y3 = jnp.stack([acc_ref[:, c * SSM_BLOCK:(c + 1) * SSM_BLOCK]
                    + jnp.dot(sb, v_ref[gi, c], preferred_element_type=F32)
                    for c in range(SSM_GROUP)], axis=0)
    y_ref[gi * SSM_GROUP:(gi + 1) * SSM_GROUP, :] = y3.reshape(SSM_GROUP, y_ref.shape[1])


def _ssm_call(u4, kt, wz, vy, acoef, n_batch):
    g, c, rows, blk = u4.shape
    width = c * blk
    gs = SSM_GROUPS_PER_STEP
    return pl.pallas_call(
        functools.partial(_ssm_kernel, n_batch=n_batch),
        grid=(g // gs,),
        in_specs=[pl.BlockSpec((gs, c, rows, blk), lambda i: (i, 0, 0, 0)),
                  pl.BlockSpec((gs, c, c, blk), lambda i: (i, 0, 0, 0)),
                  pl.BlockSpec((gs, width, 2 * SSM_STATE), lambda i: (i, 0, 0)),
                  pl.BlockSpec((gs, c, 2 * SSM_STATE, blk), lambda i: (i, 0, 0, 0)),
                  pl.BlockSpec((gs, 2, 2 * SSM_STATE), lambda i: (i, 0, 0))],
        out_specs=pl.BlockSpec((gs * c, rows * blk), lambda i: (i, 0)),
        out_shape=jax.ShapeDtypeStruct((g * c, rows * blk), F32),
        scratch_shapes=[pltpu.VMEM((rows, width), F32),
                        pltpu.VMEM((rows, 2 * SSM_STATE), F32),
                        pltpu.VMEM((rows, 2 * SSM_STATE), F32),
                        pltpu.VMEM((rows, 2 * SSM_STATE), F32),
                        pltpu.VMEM((2, 2 * blk, width), BF16)],
        name="ssm",
    )(u4, kt, wz, vy, acoef)


def _ssm_tables(lam_re, lam_im, log_dt, b_re, b_im, c_re, c_im, d):
    hi = lax.Precision.HIGHEST
    g = lam_re.shape[0]
    dt = jnp.exp(log_dt)[:, None]
    er, ei = lam_re * dt, lam_im * dt
    cat = jnp.concatenate
    kk = jnp.arange(SSM_BLOCK + 1, dtype=F32)
    kdesc = jnp.arange(SSM_BLOCK - 1, -1, -1, dtype=F32)[None, :, None]
    ppk_m, ppk_a = jnp.exp(er[:, :, None] * kk), ei[:, :, None] * kk
    ppk_r, ppk_i = ppk_m * jnp.cos(ppk_a), ppk_m * jnp.sin(ppk_a)
    prev_m, prev_a = jnp.exp(kdesc * er[:, None, :]), kdesc * ei[:, None, :]
    prr, pir = prev_m * jnp.cos(prev_a), prev_m * jnp.sin(prev_a)
    nr, ni = ppk_r[:, :, 1] - 1.0, ppk_i[:, :, 1]
    den = lam_re * lam_re + lam_im * lam_im
    fr = (nr * lam_re + ni * lam_im) / den
    fi = (ni * lam_re - nr * lam_im) / den
    bbr = jnp.swapaxes(fr[:, :, None] * b_re - fi[:, :, None] * b_im, 1, 2)
    bbi = jnp.swapaxes(fr[:, :, None] * b_im + fi[:, :, None] * b_re, 1, 2)
    cb_r = c_re[:, None] * bbr[:, :, None, :] - c_im[:, None] * bbi[:, :, None, :]
    cb_i = c_re[:, None] * bbi[:, :, None, :] + c_im[:, None] * bbr[:, :, None, :]
    cb = cat([cb_r, -cb_i], axis=-1).reshape(g, SSM_GROUP * SSM_GROUP, 2 * SSM_STATE)
    pk = cat([ppk_r[:, :, :SSM_BLOCK], ppk_i[:, :, :SSM_BLOCK]], axis=1)
    kt = jnp.einsum("gmq,gqk->gmk", cb, pk, precision=hi).reshape(g, SSM_GROUP, SSM_GROUP, SSM_BLOCK)
    dmat = jnp.eye(SSM_GROUP, dtype=F32)[None] * d.reshape(g, 1, SSM_GROUP)
    kt = kt + dmat[..., None] * (jnp.arange(SSM_BLOCK) == 0).astype(F32)
    wz = (cat([prr, prr], -1)[:, None] * cat([bbr, bbi], -1)[:, :, None, :]
          + cat([pir, pir], -1)[:, None] * cat([-bbi, bbr], -1)[:, :, None, :])
    wz = wz.astype(BF16).reshape(g, SSM_GROUP * SSM_BLOCK, 2 * SSM_STATE)
    pt_r, pt_i = ppk_r[:, :, 1:], ppk_i[:, :, 1:]
    vy = (cat([c_re, -c_re], -1)[..., None] * cat([pt_r, pt_i], axis=1)[:, None]
          + cat([-c_im, -c_im], -1)[..., None] * cat([pt_i, pt_r], axis=1)[:, None]).astype(BF16)
    ar, ai = ppk_r[:, :, SSM_BLOCK], ppk_i[:, :, SSM_BLOCK]
    acoef = jnp.stack([cat([ar, ar], -1), cat([-ai, ai], -1)], axis=1)
    return kt, wz, vy, acoef


def _mix_kernel(x_ref, yt_ref, k_ref, v_ref, g_ref, wc_ref, wq_ref, wg_ref, dw_ref, dwb_ref,
                lng_ref, lnb_ref, wpw_ref, wglu_ref, wo_ref, wout_ref, o_ref, vext_ref, vsh_ref):
    ts = x_ref.shape[1]
    d = x_ref.shape[2]
    cw = dw_ref.shape[1]
    x = x_ref[0]
    h = _rms(x, g_ref[...]).astype(BF16)

    ci = jnp.dot(h, wc_ref[...], preferred_element_type=F32)
    v = ci[:, :cw] * _sigmoid(ci[:, cw:])
    q = jnp.dot(h, wq_ref[...], preferred_element_type=F32)

    @pl.when(pl.program_id(1) == 0)
    def _():
        vext_ref[0:CONV_HALO, :] = jnp.zeros((CONV_HALO, cw), F32)

    vext_ref[CONV_HALO:CONV_HALO + ts, :] = v
    acc = jnp.broadcast_to(dwb_ref[...], (ts, cw))
    first = CONV_HALO - (CONV_KERNEL - 1)
    for phase in range(SUBLANES):
        offs = [first + k for k in range(CONV_KERNEL) if (first + k) % SUBLANES == phase]
        if not offs:
            continue
        span = offs[-1] - offs[0] + ts
        vsh_ref[0:span, :] = vext_ref[offs[0]:offs[0] + span, :]
        for off in offs:
            acc = acc + dw_ref[off - first:off - first + 1, :] * vsh_ref[off - offs[0]:off - offs[0] + ts, :]
    vext_ref[0:CONV_HALO, :] = vext_ref[ts:ts + CONV_HALO, :]
    mu = jnp.mean(acc, axis=-1, keepdims=True)
    xc = acc - mu
    var = jnp.mean(xc * xc, axis=-1, keepdims=True)
    ln = xc * lax.rsqrt(var + EPS) * lng_ref[...] + lnb_ref[...]
    sw = ln * _sigmoid(ln)
    y_conv = jnp.dot(sw.astype(BF16), wpw_ref[...], preferred_element_type=F32)
    merged = _sigmoid(jnp.dot(h, wg_ref[:, 0:d], preferred_element_type=F32)) * y_conv

    gy = jax.nn.gelu(yt_ref[...]).astype(BF16)
    z = lax.dot_general(gy, wglu_ref[...], (((0,), (0,)), ((), ())), preferred_element_type=F32)
    y_ssm = z[:, :d] * _sigmoid(z[:, d:])
    merged = merged + _sigmoid(jnp.dot(h, wg_ref[:, d:2 * d], preferred_element_type=F32)) * y_ssm

    kk = k_ref[0]
    vv = v_ref[0]
    outs = []
    for hd in range(HEADS):
        sl = slice(hd * HEAD_DIM, (hd + 1) * HEAD_DIM)
        s = lax.dot_general(q[:, sl].astype(BF16), kk[:, sl], (((1,), (1,)), ((), ())),
                            preferred_element_type=F32) * (HEAD_DIM ** -0.5)
        p = jnp.exp(s - jnp.max(s, axis=-1, keepdims=True))
        den = jnp.sum(p, axis=-1, keepdims=True)
        o = jnp.dot(p.astype(BF16), vv[:, sl], preferred_element_type=F32) / den
        outs.append(o.astype(BF16))
    y_mem = jnp.dot(jnp.concatenate(outs, axis=1), wo_ref[...], preferred_element_type=F32)
    merged = merged + _sigmoid(jnp.dot(h, wg_ref[:, 2 * d:3 * d], preferred_element_type=F32)) * y_mem

    o_ref[0] = x + jnp.dot(merged.astype(BF16), wout_ref[...], preferred_element_type=F32)


def _mix_call(x, yt, kmem, vmem, g_mix, wc, wq, wg, dw, dwb, lng, lnb, wpw, wglu, wo, wout):
    b, s, d = x.shape
    ts = MIX_TILE
    nst = s // ts
    cw = dw.shape[1]
    m = kmem.shape[1]
    consts = [g_mix, wc, wq, wg, dw, dwb, lng, lnb, wpw, wglu, wo, wout]
    return pl.pallas_call(
        _mix_kernel,
        grid=(b, nst),
        in_specs=[pl.BlockSpec((1, ts, d), lambda i, j: (i, j, 0)),
                  pl.BlockSpec((yt.shape[0], ts), lambda i, j: (0, i * nst + j)),
                  pl.BlockSpec((1, m, kmem.shape[2]), lambda i, j: (i, 0, 0)),
                  pl.BlockSpec((1, m, vmem.shape[2]), lambda i, j: (i, 0, 0))]
                 + [_const_spec(c.shape) for c in consts],
        out_specs=pl.BlockSpec((1, ts, d), lambda i, j: (i, j, 0)),
        out_shape=jax.ShapeDtypeStruct((b, s, d), F32),
        scratch_shapes=[pltpu.VMEM((ts + CONV_HALO, cw), F32),
                        pltpu.VMEM((ts + CONV_HALO, cw), F32)],
        compiler_params=pltpu.CompilerParams(
            dimension_semantics=("arbitrary", "arbitrary"), vmem_limit_bytes=VMEM_LIMIT),
        name="mix",
    )(x, yt, kmem, vmem, *consts)


def _route_kernel(x_ref, g_ref, wr_ref, br_ref, xs_ref, info_ref, nch_ref):
    ts = x_ref.shape[0] // ROUTE_TILES
    cpt = xs_ref.shape[0] // ROUTE_TILES
    for sub in range(ROUTE_TILES):
        rows = slice(sub * ts, (sub + 1) * ts)
        xs, info, nch = _route_tile(x_ref[rows, :], g_ref[...], wr_ref, br_ref[...], cpt * CHUNK_ROWS)
        xs_ref[sub * cpt:(sub + 1) * cpt] = xs.reshape(cpt, CHUNK_ROWS, xs.shape[-1])
        info_ref[rows, :] = info
        nch_ref[sub] = nch


def _route_tile(x, g, wr_ref, br, cap):
    ts = x.shape[0]
    h2 = _rms(x, g)
    hb = h2.astype(BF16)
    hl = (h2 - hb.astype(F32)).astype(BF16)
    hw = jnp.dot(hb, wr_ref[...], preferred_element_type=F32)
    logits = (hw[:, :LANES] + hw[:, LANES:]
              + jnp.dot(hl, wr_ref[:, :LANES], preferred_element_type=F32)) + br
    lt = logits.T
    le = lt[0:N_EXPERTS]
    lg = lt[N_EXPERTS:N_EXPERTS + SUBLANES]
    neg = jnp.float32(-1e30)
    big = jnp.float32(1e9)
    g_f = lax.broadcasted_iota(jnp.int32, (SUBLANES, ts), 0).astype(F32)
    e_f = lax.broadcasted_iota(jnp.int32, (N_EXPERTS, ts), 0).astype(F32)

    gmask = g_f < MOE_GROUPS
    gmax = jnp.max(jnp.where(gmask, lg, neg), axis=0, keepdims=True)
    gidx = jnp.min(jnp.where(gmask & (lg == gmax), g_f, big), axis=0, keepdims=True)
    gsum = jnp.sum(jnp.where(gmask, jnp.exp(jnp.minimum(lg - gmax, 0.0)), 0.0), axis=0, keepdims=True)
    p_top = 1.0 / gsum
    emask = jnp.floor(e_f * (1.0 / EXPERTS_PER_GROUP)) == gidx
    m1 = jnp.max(jnp.where(emask, le, neg), axis=0, keepdims=True)
    i1 = jnp.min(jnp.where(emask & (le == m1), e_f, big), axis=0, keepdims=True)
    emask2 = emask & (e_f != i1)
    m2 = jnp.max(jnp.where(emask2, le, neg), axis=0, keepdims=True)
    i2 = jnp.min(jnp.where(emask2 & (le == m2), e_f, big), axis=0, keepdims=True)
    r = jnp.exp(m2 - m1)
    w1 = p_top / (1.0 + r)
    w2 = p_top * r / (1.0 + r)

    sel1 = e_f == i1
    sel2 = e_f == i2
    occ = jnp.where(sel1 | sel2, 1.0, 0.0)
    tr = lax.broadcasted_iota(jnp.int32, (ts, ts), 0)
    tc = lax.broadcasted_iota(jnp.int32, (ts, ts), 1)
    earlier = jnp.where(tr < tc, 1.0, 0.0).astype(BF16)
    rank = jnp.dot(occ.astype(BF16), earlier, preferred_element_type=F32)
    cnt = jnp.sum(occ, axis=1, keepdims=True)
    nch = jnp.floor((cnt + (CHUNK_ROWS - 1)) * (1.0 / CHUNK_ROWS))
    er = lax.broadcasted_iota(jnp.int32, (N_EXPERTS, N_EXPERTS), 0)
    ec = lax.broadcasted_iota(jnp.int32, (N_EXPERTS, N_EXPERTS), 1)
    lower = jnp.where(ec < er, 1.0, 0.0).astype(BF16)
    nch_b = jnp.broadcast_to(nch, (N_EXPERTS, LANES))
    start = jnp.dot(lower, nch_b.astype(BF16), preferred_element_type=F32)[:, 0:1] * CHUNK_ROWS
    slot = start + rank
    pos1 = jnp.sum(jnp.where(sel1, slot, 0.0), axis=0, keepdims=True)
    pos2 = jnp.sum(jnp.where(sel2, slot, 0.0), axis=0, keepdims=True)

    rowid = lax.broadcasted_iota(jnp.int32, (cap, ts), 0)
    p = jnp.where((rowid == pos1.astype(jnp.int32)) | (rowid == pos2.astype(jnp.int32)), 1.0, 0.0)
    xs = jnp.dot(p.astype(BF16), hb, preferred_element_type=F32).astype(BF16)
    sub = lax.broadcasted_iota(jnp.int32, (LANES, ts), 0)
    info_t = jnp.where(sub == 0, pos1, jnp.where(sub == 1, pos2,
                       jnp.where(sub == 2, w1, jnp.where(sub == 3, w2, 0.0))))
    nch_rows = jnp.concatenate([nch_b, jnp.zeros((LANES - N_EXPERTS, LANES), F32)], axis=0).T[0:SUBLANES, :]
    return xs, info_t.T, nch_rows


def _route_call(x1, g_ffn, wr, br, cap):
    t, d = x1.shape
    nt = t // TOKEN_TILE
    ts = TOKEN_TILE * ROUTE_TILES
    return pl.pallas_call(
        _route_kernel,
        grid=(t // ts,),
        in_specs=[pl.BlockSpec((ts, d), lambda i: (i, 0)),
                  pl.BlockSpec((1, d), lambda i: (0, 0)),
                  pl.BlockSpec(wr.shape, lambda i: (0, 0)),
                  pl.BlockSpec((1, LANES), lambda i: (0, 0))],
        out_specs=[pl.BlockSpec((ROUTE_TILES * cap // CHUNK_ROWS, CHUNK_ROWS, d), lambda i: (i, 0, 0)),
                   pl.BlockSpec((ts, LANES), lambda i: (i, 0)),
                   pl.BlockSpec((ROUTE_TILES, SUBLANES, LANES), lambda i: (i, 0, 0))],
        out_shape=[jax.ShapeDtypeStruct((nt * cap // CHUNK_ROWS, CHUNK_ROWS, d), BF16),
                   jax.ShapeDtypeStruct((t, LANES), F32),
                   jax.ShapeDtypeStruct((nt, SUBLANES, LANES), F32)],
        compiler_params=pltpu.CompilerParams(vmem_limit_bytes=VMEM_LIMIT),
        name="route",
    )(x1, g_ffn, wr, br)


def _expert_kernel(te_ref, nu_ref, src_ref, xs_ref, wg_ref, wu_ref, wd_ref, o_ref, xbuf_ref, sem_ref):
    i = pl.program_id(0)
    n_used = nu_ref[0]
    slot = i % 2

    def chunk_copy(tile, c, buf):
        return pltpu.make_async_copy(
            xs_ref.at[src_ref[tile * CHUNKS_PER_ETILE + c]], xbuf_ref.at[buf, c], sem_ref.at[buf])

    def gather(tile, buf):
        for c in range(CHUNKS_PER_ETILE):
            chunk_copy(tile, c, buf).start(priority=c % 2)

    def drain(buf):
        pltpu.make_async_copy(xs_ref.at[pl.ds(0, CHUNKS_PER_ETILE)], xbuf_ref.at[buf], sem_ref.at[buf]).wait()

    @pl.when(i == 0)
    def _():
        gather(0, 0)

    @pl.when(i < n_used)
    def _():
        nxt = jnp.minimum(i + 1, n_used - 1)
        gather(nxt, 1 - slot)
        drain(slot)
        x = xbuf_ref[slot].reshape(EXPERT_TILE, xbuf_ref.shape[-1])
        gate = jnp.dot(x, wg_ref[0].astype(BF16), preferred_element_type=F32)
        up = jnp.dot(x, wu_ref[0].astype(BF16), preferred_element_type=F32)
        act = (gate * _sigmoid(gate) * up).astype(BF16)
        y = jnp.dot(act, wd_ref[0].astype(BF16), preferred_element_type=F32).astype(BF16)
        o_ref[...] = y.reshape(o_ref.shape)

        @pl.when(i == n_used - 1)
        def _():
            drain(1 - slot)

    @pl.when(i >= n_used)
    def _():
        o_ref[...] = jnp.zeros_like(o_ref)


def _expert_call(tile_expert, n_used, chunk_src, xs, wg, wu, wd):
    d = xs.shape[-1]
    de = wg.shape[2]
    tm = EXPERT_TILE
    nt = tile_expert.shape[0]

    def w_map(i, te, nu, src):
        return (te[jnp.maximum(jnp.minimum(i, nu[0] - 1), 0)], 0, 0)

    return pl.pallas_call(
        _expert_kernel,
        grid_spec=pltpu.PrefetchScalarGridSpec(
            num_scalar_prefetch=3,
            grid=(nt,),
            in_specs=[pl.BlockSpec(memory_space=pl.ANY),
                      pl.BlockSpec((1, d, de), w_map),
                      pl.BlockSpec((1, d, de), w_map),
                      pl.BlockSpec((1, de, d), w_map)],
            out_specs=pl.BlockSpec((CHUNKS_PER_ETILE, CHUNK_ROWS, d), lambda i, te, nu, src: (i, 0, 0)),
            scratch_shapes=[pltpu.VMEM((2, CHUNKS_PER_ETILE, CHUNK_ROWS, d), BF16),
                            pltpu.SemaphoreType.DMA((2,))]),
        out_shape=jax.ShapeDtypeStruct((nt * CHUNKS_PER_ETILE, CHUNK_ROWS, d), BF16),
        compiler_params=pltpu.CompilerParams(dimension_semantics=("arbitrary",)),
        name="expert",
    )(tile_expert, n_used, chunk_src, xs, wg, wu, wd)


def _combine_kernel(dst_ref, x_ref, info_ref, g_ref, ys_ref, o_ref, ybuf_ref, sem_ref):
    i = pl.program_id(0)
    n_tiles = pl.num_programs(0)
    ts = x_ref.shape[0] // COMBINE_TILES
    n_chunks = ybuf_ref.shape[1]
    cap = n_chunks // COMBINE_TILES * CHUNK_ROWS
    slot = i % 2

    def chunk_copy(tile, k, buf):
        return pltpu.make_async_copy(ys_ref.at[dst_ref[tile * n_chunks + k]], ybuf_ref.at[buf, k], sem_ref.at[buf])

    def gather(tile, buf):
        for k in range(n_chunks):
            chunk_copy(tile, k, buf).start(priority=k % 2)

    def drain(buf):
        pltpu.make_async_copy(ys_ref.at[pl.ds(0, n_chunks)], ybuf_ref.at[buf], sem_ref.at[buf]).wait()

    @pl.when(i == 0)
    def _():
        gather(0, 0)

    nxt = jnp.minimum(i + 1, n_tiles - 1)
    gather(nxt, 1 - slot)
    drain(slot)

    rowid = lax.broadcasted_iota(jnp.int32, (ts, cap), 1)
    for sub in range(COMBINE_TILES):
        rows = slice(sub * ts, (sub + 1) * ts)
        info = info_ref[rows, :]
        pos1 = info[:, 0:1].astype(jnp.int32)
        pos2 = info[:, 1:2].astype(jnp.int32)
        w1 = info[:, 2:3]
        w2 = info[:, 3:4]
        ys = ybuf_ref[slot, sub * (n_chunks // COMBINE_TILES):(sub + 1) * (n_chunks // COMBINE_TILES)]
        ys = ys.reshape(cap, ybuf_ref.shape[-1])
        pw = jnp.where(rowid == pos1, w1, jnp.where(rowid == pos2, w2, 0.0)).astype(BF16)
        y = jnp.dot(pw, ys, preferred_element_type=F32)
        o_ref[rows, :] = _rms(x_ref[rows, :] + y, g_ref[...])

    @pl.when(i == n_tiles - 1)
    def _():
        drain(1 - slot)


def _combine_call(chunk_dst, x1, info, g_final, ys_sorted, cap):
    t, d = x1.shape
    ts = TOKEN_TILE * COMBINE_TILES
    return pl.pallas_call(
        _combine_kernel,
        grid_spec=pltpu.PrefetchScalarGridSpec(
            num_scalar_prefetch=1,
            grid=(t // ts,),
            in_specs=[pl.BlockSpec((ts, d), lambda i, dst: (i, 0)),
                      pl.BlockSpec((ts, LANES), lambda i, dst: (i, 0)),
                      pl.BlockSpec((1, d), lambda i, dst: (0, 0)),
                      pl.BlockSpec(memory_space=pl.ANY)],
            out_specs=pl.BlockSpec((ts, d), lambda i, dst: (i, 0)),
            scratch_shapes=[pltpu.VMEM((2, COMBINE_TILES * cap // CHUNK_ROWS, CHUNK_ROWS, d), BF16),
                            pltpu.SemaphoreType.DMA((2,))]),
        out_shape=jax.ShapeDtypeStruct((t, d), F32),
        compiler_params=pltpu.CompilerParams(dimension_semantics=("arbitrary",),
                                             vmem_limit_bytes=VMEM_LIMIT),
        name="combine",
    )(chunk_dst, x1, info, g_final, ys_sorted)


def _run_tables(nch, cpt, n_etiles):
    n_tiles, n_exp = nch.shape
    per = CHUNKS_PER_ETILE
    tcum = jnp.cumsum(nch, axis=1)
    toff = tcum - nch
    ecum = jnp.cumsum(nch, axis=0)
    etot = ecum[-1]
    eseg = -(-etot // per) * per
    segcum = jnp.cumsum(eseg)
    ebase = segcum - eseg
    eoff = ebase[None, :] + ecum - nch
    n_used = (segcum[-1] // per).reshape(1)

    k = jnp.arange(cpt, dtype=jnp.int32)[None, :, None]
    in_run = (k >= toff[:, None, :]) & (k < tcum[:, None, :])
    chunk_dst = jnp.sum(jnp.where(in_run, (eoff - toff)[:, None, :] + k, 0), axis=-1)

    first = jnp.arange(n_etiles, dtype=jnp.int32)[:, None] * per
    owner = (first >= ebase[None, :]) & (first < segcum[None, :])
    tile_expert = jnp.sum(jnp.where(owner, jnp.arange(n_exp, dtype=jnp.int32)[None, :], 0), axis=-1)

    pick = lambda a: jnp.sum(jnp.where(owner[:, None, :], a[None, :, :], 0), axis=-1)
    run_lo, run_n = pick(eoff), pick(nch)
    shift = pick(jnp.arange(n_tiles, dtype=jnp.int32)[:, None] * cpt + toff - eoff)
    c = (first + jnp.arange(per, dtype=jnp.int32)[None, :])[:, :, None]
    hit = (c >= run_lo[:, None, :]) & (c < (run_lo + run_n)[:, None, :])
    chunk_src = jnp.sum(jnp.where(hit, shift[:, None, :] + c, 0), axis=-1)
    i32 = lambda a: a.reshape(-1).astype(jnp.int32)
    return i32(chunk_dst), i32(tile_expert), i32(n_used), i32(chunk_src)


def _layer(x, mem, g_mix, w_in, conv_dw, conv_dw_bias, conv_ln_g, conv_ln_b, w_conv_out,
           lam_re, lam_im, log_dt, b_re, b_im, c_re, c_im, ssm_d, w_ssm_glu, g_mem, w_mem_kv,
           w_mem_out, w_out, g_ffn, w_rg, b_rg, w_re, b_re_, w_eg, w_eu, w_ed, g_final):
    b, s, d = x.shape
    t = b * s
    cw = conv_dw.shape[1]
    sw = ssm_d.shape[0]
    qw = w_mem_out.shape[0]
    row = lambda a: a.reshape(1, -1)

    o0, o1, o2 = 2 * cw, 2 * cw + sw, 2 * cw + sw + qw
    wc = w_in[:, :o0].astype(BF16)
    w_ssm_t = w_in[:, o0:o1].T.astype(BF16)
    wq = w_in[:, o1:o2].astype(BF16)
    wg = w_in[:, o2:].astype(BF16)

    kmem, vmem = _kv_call(mem, row(g_mem), w_mem_kv.astype(BF16))

    ut = _ut_call(x.reshape(t, d), row(g_mix), w_ssm_t)
    kt, wz, vy, acoef = _ssm_tables(lam_re, lam_im, log_dt, b_re, b_im, c_re, c_im, ssm_d)
    yt = _ssm_call(ut, kt, wz, vy, acoef, b)

    dw = jnp.concatenate([conv_dw, jnp.zeros((CONV_HALO - CONV_KERNEL, cw), F32)], axis=0)
    x1 = _mix_call(x, yt, kmem, vmem, row(g_mix), wc, wq, wg, dw, row(conv_dw_bias),
                   row(conv_ln_g), row(conv_ln_b), w_conv_out.astype(BF16),
                   w_ssm_glu.astype(BF16), w_mem_out.astype(BF16), w_out.astype(BF16))
    x1 = x1.reshape(t, d)

    pad = LANES - N_EXPERTS - MOE_GROUPS
    wr = jnp.concatenate([w_re, w_rg, jnp.zeros((d, pad), F32)], axis=1)
    wr_hi = wr.astype(BF16)
    wr = jnp.concatenate([wr_hi, (wr - wr_hi.astype(F32)).astype(BF16)], axis=1)
    br = jnp.concatenate([b_re_, b_rg, jnp.zeros((pad,), F32)]).reshape(1, LANES)
    n_tiles = t // TOKEN_TILE
    max_tile_chunks = (2 * TOKEN_TILE + N_EXPERTS * (CHUNK_ROWS - 1)) // CHUNK_ROWS
    cpt = -(-max_tile_chunks // SUBLANES) * SUBLANES
    cap = cpt * CHUNK_ROWS
    xs_tiles, info, nch_f = _route_call(x1, row(g_ffn), wr, br, cap)

    max_chunks = n_tiles * max_tile_chunks + N_EXPERTS * (CHUNKS_PER_ETILE - 1)
    n_etiles = -(-max_chunks // CHUNKS_PER_ETILE)
    nch = nch_f[:, 0, :N_EXPERTS].astype(jnp.int32)
    chunk_dst, tile_expert, n_used, chunk_src = _run_tables(nch, cpt, n_etiles)

    ys_sorted = _expert_call(tile_expert, n_used, chunk_src, xs_tiles, w_eg, w_eu, w_ed)
    out = _combine_call(chunk_dst, x1, info, row(g_final), ys_sorted, cap)
    return out.reshape(b, s, d)


def kernel(x, mem, g_mix, w_in, conv_dw, conv_dw_bias, conv_ln_g, conv_ln_b, w_conv_out, ssm_lambda_re, ssm_lambda_im, ssm_log_dt, ssm_b_re, ssm_b_im, ssm_c_re, ssm_c_im, ssm_d, w_ssm_glu, g_mem, w_mem_kv, w_mem_out, w_out, g_ffn, w_router_group, b_router_group, w_router_expert, b_router_expert, w_exp_gate, w_exp_up, w_exp_down, g_final):
    assert g_mix.shape[0] == 1, "the problem fixes one layer"
    l = 0
    return _layer(
        x, mem, g_mix[l], w_in[l], conv_dw[l], conv_dw_bias[l], conv_ln_g[l], conv_ln_b[l],
        w_conv_out[l], ssm_lambda_re[l], ssm_lambda_im[l], ssm_log_dt[l], ssm_b_re[l],
        ssm_b_im[l], ssm_c_re[l], ssm_c_im[l], ssm_d[l], w_ssm_glu[l], g_mem[l], w_mem_kv[l],
        w_mem_out[l], w_out[l], g_ffn[l], w_router_group[l], b_router_group[l],
        w_router_expert[l], b_router_expert[l], w_exp_gate[l], w_exp_up[l], w_exp_down[l], g_final)
```

```python
import functools

import jax
import jax.numpy as jnp
from jax import lax
from jax.experimental import pallas as pl
from jax.experimental.pallas import tpu as pltpu

F32 = jnp.float32
BF16 = jnp.bfloat16
EPS = 1e-6

LANES = 128
SUBLANES = 8
V7X_VMEM_BYTES = 64 * 1024 * 1024
CHUNK_ROWS = 2 * SUBLANES
SSM_GROUP = 16
SSM_STATE = 64
SSM_BLOCK = LANES
CONV_KERNEL = 31
CONV_HALO = 32
HEADS = 4
HEAD_DIM = 128
MOE_GROUPS = 4
EXPERTS_PER_GROUP = 8
N_EXPERTS = MOE_GROUPS * EXPERTS_PER_GROUP
TOKEN_TILE = 512
MIX_TILE = 1024
COMBINE_TILES = 2
ROUTE_TILES = 2
EXPERT_TILE = 512
CHUNKS_PER_ETILE = EXPERT_TILE // CHUNK_ROWS
UT_TILE = 2048
VMEM_LIMIT = V7X_VMEM_BYTES * 7 // 8


def _rms(x, g):
    return x * lax.rsqrt(jnp.mean(x * x, axis=-1, keepdims=True) + EPS) * g


def _sigmoid(x):
    return 0.5 * jnp.tanh(0.5 * x) + 0.5


def _const_spec(shape):
    zeros = (0,) * len(shape)
    return pl.BlockSpec(shape, lambda *_: zeros, pipeline_mode=pl.Buffered(1))


def _kv_kernel(mem_ref, g_ref, w_ref, k_ref, v_ref):
    width = k_ref.shape[-1]
    mn = _rms(mem_ref[0], g_ref[...]).astype(BF16)
    kv = jnp.dot(mn, w_ref[...], preferred_element_type=F32)
    k_ref[0] = kv[:, :width].astype(BF16)
    v_ref[0] = kv[:, width:].astype(BF16)


def _kv_call(mem, g_mem, w_kv):
    b, m, d = mem.shape
    width = w_kv.shape[1] // 2
    return pl.pallas_call(
        _kv_kernel,
        grid=(b,),
        in_specs=[pl.BlockSpec((1, m, d), lambda i: (i, 0, 0)),
                  pl.BlockSpec((1, d), lambda i: (0, 0)),
                  pl.BlockSpec(w_kv.shape, lambda i: (0, 0))],
        out_specs=[pl.BlockSpec((1, m, width), lambda i: (i, 0, 0)),
                   pl.BlockSpec((1, m, width), lambda i: (i, 0, 0))],
        out_shape=[jax.ShapeDtypeStruct((b, m, width), BF16)] * 2,
        name="kv",
    )(mem, g_mem, w_kv)


def _ut_kernel(x_ref, g_ref, wt_ref, o_ref):
    h = _rms(x_ref[...], g_ref[...]).astype(BF16)
    ut = lax.dot_general(wt_ref[...], h, (((1,), (1,)), ((), ())), preferred_element_type=F32)
    ut = ut.reshape(ut.shape[0], ut.shape[1] // SSM_BLOCK, SSM_BLOCK)
    o_ref[...] = ut.reshape(o_ref.shape).astype(BF16)


def _ut_call(x2, g_mix, w_ssm_t):
    t, d = x2.shape
    c = w_ssm_t.shape[0]
    ts = UT_TILE
    n_groups = c // SSM_GROUP
    return pl.pallas_call(
        _ut_kernel,
        grid=(t // ts,),
        in_specs=[pl.BlockSpec((ts, d), lambda i: (i, 0)),
                  pl.BlockSpec((1, d), lambda i: (0, 0)),
                  pl.BlockSpec((c, d), lambda i: (0, 0))],
        out_specs=pl.BlockSpec((n_groups, SSM_GROUP, ts // SSM_BLOCK, SSM_BLOCK),
                               lambda i: (0, 0, i, 0)),
        out_shape=jax.ShapeDtypeStruct((n_groups, SSM_GROUP, t // SSM_BLOCK, SSM_BLOCK), BF16),
        name="ut",
    )(x2, g_mix, w_ssm_t)


def _ssm_kernel(u_ref, kt_ref, w_ref, v_ref, cab_ref, a_ref, y_ref, acc_ref, z_ref, zs_ref, s_ref,
                slab_ref, *, n_batch):
    rows = u_ref.shape[2]
    n_blocks = rows // n_batch
    ri = lax.broadcasted_iota(jnp.int32, (SSM_BLOCK, SSM_BLOCK), 0)
    ci = lax.broadcasted_iota(jnp.int32, (SSM_BLOCK, SSM_BLOCK), 1)
    causal = ci >= ri

    n_pairs = SSM_GROUP // 2

    def build(cp, slot):
        for half in range(2):
            kt = kt_ref[0, 2 * cp + half]
            for c in range(SSM_GROUP):
                xb = jnp.broadcast_to(kt[c:c + 1, :], (SSM_BLOCK, SSM_BLOCK))
                toe = pltpu.roll(xb, 0, 1, stride=1, stride_axis=0)
                slab_ref[slot, half * SSM_BLOCK:(half + 1) * SSM_BLOCK, c * SSM_BLOCK:(c + 1) * SSM_BLOCK] = (
                    jnp.where(causal, toe, 0.0).astype(BF16))

    def apply(cp, slot, first):
        x2 = jnp.concatenate([u_ref[0, 2 * cp], u_ref[0, 2 * cp + 1]], axis=1)
        part = jnp.dot(x2, slab_ref[slot], preferred_element_type=F32)
        wrow = pl.multiple_of(cp * 2 * SSM_BLOCK, 2 * SSM_BLOCK)
        zpart = jnp.dot(x2, w_ref[0, pl.ds(wrow, 2 * SSM_BLOCK), :], preferred_element_type=F32)
        if first:
            acc_ref[...] = part
            z_ref[...] = zpart
        else:
            acc_ref[...] += part
            z_ref[...] += zpart

    build(0, 0)
    build(1, 1)
    apply(0, 0, True)

    def two(it, carry):
        cp = 2 * it + 1
        build(cp + 1, 0)
        apply(cp, 1, False)
        build(cp + 2, 1)
        apply(cp + 1, 0, False)
        return carry

    lax.fori_loop(0, (n_pairs - 2) // 2, two, 0)
    apply(n_pairs - 1, 1, False)

    a_full = a_ref[0, 0:1, :]
    a_swap = a_ref[0, 1:2, :]
    zs_ref[...] = pltpu.roll(z_ref[...], SSM_STATE, 1)
    st = jnp.zeros((n_batch, 2 * SSM_STATE), F32)
    sw = jnp.zeros((n_batch, 2 * SSM_STATE), F32)
    for blk in range(n_blocks):
        sl = pl.ds(blk, n_batch, stride=n_blocks)
        s_ref[sl, :] = st
        st, sw = (a_full * st + a_swap * sw + z_ref[sl, :],
                  a_full * sw - a_swap * st + zs_ref[sl, :])

    s2 = jnp.concatenate([s_ref[...], s_ref[...]], axis=1)
    y3 = jnp.stack([acc_ref[:, c * SSM_BLOCK:(c + 1) * SSM_BLOCK]
                    + jnp.dot((s2 * cab_ref[0, c:c + 1, :]).astype(BF16), v_ref[0],
                              preferred_element_type=F32)
                    for c in range(SSM_GROUP)], axis=0)
    y_ref[...] = y3.reshape(y_ref.shape)


def _ssm_call(u4, kt, wz, vy, cab, acoef, n_batch):
    g, c, rows, blk = u4.shape
    width = c * blk
    return pl.pallas_call(
        functools.partial(_ssm_kernel, n_batch=n_batch),
        grid=(g,),
        in_specs=[pl.BlockSpec((1, c, rows, blk), lambda i: (i, 0, 0, 0)),
                  pl.BlockSpec((1, c, c, blk), lambda i: (i, 0, 0, 0)),
                  pl.BlockSpec((1, width, 2 * SSM_STATE), lambda i: (i, 0, 0)),
                  pl.BlockSpec((1, 4 * SSM_STATE, blk), lambda i: (i, 0, 0)),
                  pl.BlockSpec((1, c, 4 * SSM_STATE), lambda i: (i, 0, 0)),
                  pl.BlockSpec((1, 2, 2 * SSM_STATE), lambda i: (i, 0, 0))],
        out_specs=pl.BlockSpec((c, rows * blk), lambda i: (i, 0)),
        out_shape=jax.ShapeDtypeStruct((g * c, rows * blk), F32),
        scratch_shapes=[pltpu.VMEM((rows, width), F32),
                        pltpu.VMEM((rows, 2 * SSM_STATE), F32),
                        pltpu.VMEM((rows, 2 * SSM_STATE), F32),
                        pltpu.VMEM((rows, 2 * SSM_STATE), F32),
                        pltpu.VMEM((2, 2 * blk, width), BF16)],
        name="ssm",
    )(u4, kt, wz, vy, cab, acoef)


def _ssm_tables(lam_re, lam_im, log_dt, b_re, b_im, c_re, c_im, d):
    hi = lax.Precision.HIGHEST
    g = lam_re.shape[0]
    dt = jnp.exp(log_dt)[:, None]
    er, ei = lam_re * dt, lam_im * dt
    cat = jnp.concatenate
    kk = jnp.arange(SSM_BLOCK + 1, dtype=F32)
    kdesc = jnp.arange(SSM_BLOCK - 1, -1, -1, dtype=F32)[None, :, None]
    ppk_m, ppk_a = jnp.exp(er[:, :, None] * kk), ei[:, :, None] * kk
    ppk_r, ppk_i = ppk_m * jnp.cos(ppk_a), ppk_m * jnp.sin(ppk_a)
    prev_m, prev_a = jnp.exp(kdesc * er[:, None, :]), kdesc * ei[:, None, :]
    prr, pir = prev_m * jnp.cos(prev_a), prev_m * jnp.sin(prev_a)
    nr, ni = ppk_r[:, :, 1] - 1.0, ppk_i[:, :, 1]
    den = lam_re * lam_re + lam_im * lam_im
    fr = (nr * lam_re + ni * lam_im) / den
    fi = (ni * lam_re - nr * lam_im) / den
    bbr = jnp.swapaxes(fr[:, :, None] * b_re - fi[:, :, None] * b_im, 1, 2)
    bbi = jnp.swapaxes(fr[:, :, None] * b_im + fi[:, :, None] * b_re, 1, 2)
    cb_r = c_re[:, None] * bbr[:, :, None, :] - c_im[:, None] * bbi[:, :, None, :]
    cb_i = c_re[:, None] * bbi[:, :, None, :] + c_im[:, None] * bbr[:, :, None, :]
    cb = cat([cb_r, -cb_i], axis=-1).reshape(g, SSM_GROUP * SSM_GROUP, 2 * SSM_STATE)
    pk = cat([ppk_r[:, :, :SSM_BLOCK], ppk_i[:, :, :SSM_BLOCK]], axis=1)
    kt = jnp.einsum("gmq,gqk->gmk", cb, pk, precision=hi).reshape(g, SSM_GROUP, SSM_GROUP, SSM_BLOCK)
    dmat = jnp.eye(SSM_GROUP, dtype=F32)[None] * d.reshape(g, 1, SSM_GROUP)
    kt = kt + dmat[..., None] * (jnp.arange(SSM_BLOCK) == 0).astype(F32)
    wz = (cat([prr, prr], -1)[:, None] * cat([bbr, bbi], -1)[:, :, None, :]
          + cat([pir, pir], -1)[:, None] * cat([-bbi, bbr], -1)[:, :, None, :])
    wz = wz.astype(BF16).reshape(g, SSM_GROUP * SSM_BLOCK, 2 * SSM_STATE)
    pt_r, pt_i = ppk_r[:, :, 1:], ppk_i[:, :, 1:]
    vy = cat([pt_r, pt_i, pt_i, pt_r], axis=1).astype(BF16)
    cab = cat([c_re, -c_re, -c_im, -c_im], -1)
    ar, ai = ppk_r[:, :, SSM_BLOCK], ppk_i[:, :, SSM_BLOCK]
    acoef = jnp.stack([cat([ar, ar], -1), cat([-ai, ai], -1)], axis=1)
    return kt, wz, vy, cab, acoef


def _mix_kernel(x_ref, yt_ref, k_ref, v_ref, g_ref, wc_ref, wq_ref, wg_ref, dw_ref, dwb_ref,
                lng_ref, lnb_ref, wpw_ref, wglu_ref, wo_ref, wout_ref, o_ref, vext_ref, vsh_ref):
    ts = x_ref.shape[1]
    d = x_ref.shape[2]
    cw = dw_ref.shape[1]
    x = x_ref[0]
    h = _rms(x, g_ref[...]).astype(BF16)

    ci = jnp.dot(h, wc_ref[...], preferred_element_type=F32)
    v = ci[:, :cw] * _sigmoid(ci[:, cw:])
    q = jnp.dot(h, wq_ref[...], preferred_element_type=F32)

    @pl.when(pl.program_id(1) == 0)
    def _():
        vext_ref[0:CONV_HALO, :] = jnp.zeros((CONV_HALO, cw), F32)

    vext_ref[CONV_HALO:CONV_HALO + ts, :] = v
    acc = jnp.broadcast_to(dwb_ref[...], (ts, cw))
    first = CONV_HALO - (CONV_KERNEL - 1)
    for phase in range(SUBLANES):
        offs = [first + k for k in range(CONV_KERNEL) if (first + k) % SUBLANES == phase]
        if not offs:
            continue
        span = offs[-1] - offs[0] + ts
        vsh_ref[0:span, :] = vext_ref[offs[0]:offs[0] + span, :]
        for off in offs:
            acc = acc + dw_ref[off - first:off - first + 1, :] * vsh_ref[off - offs[0]:off - offs[0] + ts, :]
    vext_ref[0:CONV_HALO, :] = vext_ref[ts:ts + CONV_HALO, :]
    mu = jnp.mean(acc, axis=-1, keepdims=True)
    xc = acc - mu
    var = jnp.mean(xc * xc, axis=-1, keepdims=True)
    ln = xc * lax.rsqrt(var + EPS) * lng_ref[...] + lnb_ref[...]
    sw = ln * _sigmoid(ln)
    y_conv = jnp.dot(sw.astype(BF16), wpw_ref[...], preferred_element_type=F32)
    merged = _sigmoid(jnp.dot(h, wg_ref[:, 0:d], preferred_element_type=F32)) * y_conv

    gy = jax.nn.gelu(yt_ref[...]).astype(BF16)
    z = lax.dot_general(gy, wglu_ref[...], (((0,), (0,)), ((), ())), preferred_element_type=F32)
    y_ssm = z[:, :d] * _sigmoid(z[:, d:])
    merged = merged + _sigmoid(jnp.dot(h, wg_ref[:, d:2 * d], preferred_element_type=F32)) * y_ssm

    kk = k_ref[0]
    vv = v_ref[0]
    outs = []
    for hd in range(HEADS):
        sl = slice(hd * HEAD_DIM, (hd + 1) * HEAD_DIM)
        s = lax.dot_general(q[:, sl].astype(BF16), kk[:, sl], (((1,), (1,)), ((), ())),
                            preferred_element_type=F32) * (HEAD_DIM ** -0.5)
        p = jnp.exp(s - jnp.max(s, axis=-1, keepdims=True))
        den = jnp.sum(p, axis=-1, keepdims=True)
        o = jnp.dot(p.astype(BF16), vv[:, sl], preferred_element_type=F32) / den
        outs.append(o.astype(BF16))
    y_mem = jnp.dot(jnp.concatenate(outs, axis=1), wo_ref[...], preferred_element_type=F32)
    merged = merged + _sigmoid(jnp.dot(h, wg_ref[:, 2 * d:3 * d], preferred_element_type=F32)) * y_mem

    o_ref[0] = x + jnp.dot(merged.astype(BF16), wout_ref[...], preferred_element_type=F32)


def _mix_call(x, yt, kmem, vmem, g_mix, wc, wq, wg, dw, dwb, lng, lnb, wpw, wglu, wo, wout):
    b, s, d = x.shape
    ts = MIX_TILE
    nst = s // ts
    cw = dw.shape[1]
    m = kmem.shape[1]
    consts = [g_mix, wc, wq, wg, dw, dwb, lng, lnb, wpw, wglu, wo, wout]
    return pl.pallas_call(
        _mix_kernel,
        grid=(b, nst),
        in_specs=[pl.BlockSpec((1, ts, d), lambda i, j: (i, j, 0)),
                  pl.BlockSpec((yt.shape[0], ts), lambda i, j: (0, i * nst + j)),
                  pl.BlockSpec((1, m, kmem.shape[2]), lambda i, j: (i, 0, 0)),
                  pl.BlockSpec((1, m, vmem.shape[2]), lambda i, j: (i, 0, 0))]
                 + [_const_spec(c.shape) for c in consts],
        out_specs=pl.BlockSpec((1, ts, d), lambda i, j: (i, j, 0)),
        out_shape=jax.ShapeDtypeStruct((b, s, d), F32),
        scratch_shapes=[pltpu.VMEM((ts + CONV_HALO, cw), F32),
                        pltpu.VMEM((ts + CONV_HALO, cw), F32)],
        compiler_params=pltpu.CompilerParams(
            dimension_semantics=("arbitrary", "arbitrary"), vmem_limit_bytes=VMEM_LIMIT),
        name="mix",
    )(x, yt, kmem, vmem, *consts)


def _route_kernel(x_ref, g_ref, wr_ref, br_ref, xs_ref, info_ref, nch_ref):
    ts = x_ref.shape[0] // ROUTE_TILES
    cpt = xs_ref.shape[0] // ROUTE_TILES
    for sub in range(ROUTE_TILES):
        rows = slice(sub * ts, (sub + 1) * ts)
        xs, info, nch = _route_tile(x_ref[rows, :], g_ref[...], wr_ref, br_ref[...], cpt * CHUNK_ROWS)
        xs_ref[sub * cpt:(sub + 1) * cpt] = xs.reshape(cpt, CHUNK_ROWS, xs.shape[-1])
        info_ref[rows, :] = info
        nch_ref[sub] = nch


def _route_tile(x, g, wr_ref, br, cap):
    ts = x.shape[0]
    h2 = _rms(x, g)
    hb = h2.astype(BF16)
    hl = (h2 - hb.astype(F32)).astype(BF16)
    hw = jnp.dot(hb, wr_ref[...], preferred_element_type=F32)
    logits = (hw[:, :LANES] + hw[:, LANES:]
              + jnp.dot(hl, wr_ref[:, :LANES], preferred_element_type=F32)) + br
    lt = logits.T
    le = lt[0:N_EXPERTS]
    lg = lt[N_EXPERTS:N_EXPERTS + SUBLANES]
    neg = jnp.float32(-1e30)
    big = jnp.float32(1e9)
    g_f = lax.broadcasted_iota(jnp.int32, (SUBLANES, ts), 0).astype(F32)
    e_f = lax.broadcasted_iota(jnp.int32, (N_EXPERTS, ts), 0).astype(F32)

    gmask = g_f < MOE_GROUPS
    gmax = jnp.max(jnp.where(gmask, lg, neg), axis=0, keepdims=True)
    gidx = jnp.min(jnp.where(gmask & (lg == gmax), g_f, big), axis=0, keepdims=True)
    gsum = jnp.sum(jnp.where(gmask, jnp.exp(jnp.minimum(lg - gmax, 0.0)), 0.0), axis=0, keepdims=True)
    p_top = 1.0 / gsum
    emask = jnp.floor(e_f * (1.0 / EXPERTS_PER_GROUP)) == gidx
    m1 = jnp.max(jnp.where(emask, le, neg), axis=0, keepdims=True)
    i1 = jnp.min(jnp.where(emask & (le == m1), e_f, big), axis=0, keepdims=True)
    emask2 = emask & (e_f != i1)
    m2 = jnp.max(jnp.where(emask2, le, neg), axis=0, keepdims=True)
    i2 = jnp.min(jnp.where(emask2 & (le == m2), e_f, big), axis=0, keepdims=True)
    r = jnp.exp(m2 - m1)
    w1 = p_top / (1.0 + r)
    w2 = p_top * r / (1.0 + r)

    sel1 = e_f == i1
    sel2 = e_f == i2
    occ = jnp.where(sel1 | sel2, 1.0, 0.0)
    tr = lax.broadcasted_iota(jnp.int32, (ts, ts), 0)
    tc = lax.broadcasted_iota(jnp.int32, (ts, ts), 1)
    earlier = jnp.where(tr < tc, 1.0, 0.0).astype(BF16)
    rank = jnp.dot(occ.astype(BF16), earlier, preferred_element_type=F32)
    cnt = jnp.sum(occ, axis=1, keepdims=True)
    nch = jnp.floor((cnt + (CHUNK_ROWS - 1)) * (1.0 / CHUNK_ROWS))
    er = lax.broadcasted_iota(jnp.int32, (N_EXPERTS, N_EXPERTS), 0)
    ec = lax.broadcasted_iota(jnp.int32, (N_EXPERTS, N_EXPERTS), 1)
    lower = jnp.where(ec < er, 1.0, 0.0).astype(BF16)
    nch_b = jnp.broadcast_to(nch, (N_EXPERTS, LANES))
    start = jnp.dot(lower, nch_b.astype(BF16), preferred_element_type=F32)[:, 0:1] * CHUNK_ROWS
    slot = start + rank
    pos1 = jnp.sum(jnp.where(sel1, slot, 0.0), axis=0, keepdims=True)
    pos2 = jnp.sum(jnp.where(sel2, slot, 0.0), axis=0, keepdims=True)

    rowid = lax.broadcasted_iota(jnp.int32, (cap, ts), 0)
    p = jnp.where((rowid == pos1.astype(jnp.int32)) | (rowid == pos2.astype(jnp.int32)), 1.0, 0.0)
    xs = jnp.dot(p.astype(BF16), hb, preferred_element_type=F32).astype(BF16)
    sub = lax.broadcasted_iota(jnp.int32, (LANES, ts), 0)
    info_t = jnp.where(sub == 0, pos1, jnp.where(sub == 1, pos2,
                       jnp.where(sub == 2, w1, jnp.where(sub == 3, w2, 0.0))))
    nch_rows = jnp.concatenate([nch_b, jnp.zeros((LANES - N_EXPERTS, LANES), F32)], axis=0).T[0:SUBLANES, :]
    return xs, info_t.T, nch_rows


def _route_call(x1, g_ffn, wr, br, cap):
    t, d = x1.shape
    nt = t // TOKEN_TILE
    ts = TOKEN_TILE * ROUTE_TILES
    return pl.pallas_call(
        _route_kernel,
        grid=(t // ts,),
        in_specs=[pl.BlockSpec((ts, d), lambda i: (i, 0)),
                  pl.BlockSpec((1, d), lambda i: (0, 0)),
                  pl.BlockSpec(wr.shape, lambda i: (0, 0)),
                  pl.BlockSpec((1, LANES), lambda i: (0, 0))],
        out_specs=[pl.BlockSpec((ROUTE_TILES * cap // CHUNK_ROWS, CHUNK_ROWS, d), lambda i: (i, 0, 0)),
                   pl.BlockSpec((ts, LANES), lambda i: (i, 0)),
                   pl.BlockSpec((ROUTE_TILES, SUBLANES, LANES), lambda i: (i, 0, 0))],
        out_shape=[jax.ShapeDtypeStruct((nt * cap // CHUNK_ROWS, CHUNK_ROWS, d), BF16),
                   jax.ShapeDtypeStruct((t, LANES), F32),
                   jax.ShapeDtypeStruct((nt, SUBLANES, LANES), F32)],
        compiler_params=pltpu.CompilerParams(vmem_limit_bytes=VMEM_LIMIT),
        name="route",
    )(x1, g_ffn, wr, br)


def _expert_kernel(te_ref, nu_ref, src_ref, xs_ref, wg_ref, wu_ref, wd_ref, o_ref, xbuf_ref, sem_ref):
    i = pl.program_id(0)
    n_used = nu_ref[0]
    slot = i % 2

    def chunk_copy(tile, c, buf):
        return pltpu.make_async_copy(
            xs_ref.at[src_ref[tile * CHUNKS_PER_ETILE + c]], xbuf_ref.at[buf, c], sem_ref.at[buf])

    def gather(tile, buf):
        for c in range(CHUNKS_PER_ETILE):
            chunk_copy(tile, c, buf).start(priority=c % 2)

    def drain(buf):
        pltpu.make_async_copy(xs_ref.at[pl.ds(0, CHUNKS_PER_ETILE)], xbuf_ref.at[buf], sem_ref.at[buf]).wait()

    @pl.when(i == 0)
    def _():
        gather(0, 0)

    @pl.when(i < n_used)
    def _():
        nxt = jnp.minimum(i + 1, n_used - 1)
        gather(nxt, 1 - slot)
        drain(slot)
        x = xbuf_ref[slot].reshape(EXPERT_TILE, xbuf_ref.shape[-1])
        gate = jnp.dot(x, wg_ref[0].astype(BF16), preferred_element_type=F32)
        up = jnp.dot(x, wu_ref[0].astype(BF16), preferred_element_type=F32)
        act = (gate * _sigmoid(gate) * up).astype(BF16)
        y = jnp.dot(act, wd_ref[0].astype(BF16), preferred_element_type=F32).astype(BF16)
        o_ref[...] = y.reshape(o_ref.shape)

        @pl.when(i == n_used - 1)
        def _():
            drain(1 - slot)

    @pl.when(i >= n_used)
    def _():
        o_ref[...] = jnp.zeros_like(o_ref)


def _expert_call(tile_expert, n_used, chunk_src, xs, wg, wu, wd):
    d = xs.shape[-1]
    de = wg.shape[2]
    tm = EXPERT_TILE
    nt = tile_expert.shape[0]

    def w_map(i, te, nu, src):
        return (te[jnp.maximum(jnp.minimum(i, nu[0] - 1), 0)], 0, 0)

    return pl.pallas_call(
        _expert_kernel,
        grid_spec=pltpu.PrefetchScalarGridSpec(
            num_scalar_prefetch=3,
            grid=(nt,),
            in_specs=[pl.BlockSpec(memory_space=pl.ANY),
                      pl.BlockSpec((1, d, de), w_map),
                      pl.BlockSpec((1, d, de), w_map),
                      pl.BlockSpec((1, de, d), w_map)],
            out_specs=pl.BlockSpec((CHUNKS_PER_ETILE, CHUNK_ROWS, d), lambda i, te, nu, src: (i, 0, 0)),
            scratch_shapes=[pltpu.VMEM((2, CHUNKS_PER_ETILE, CHUNK_ROWS, d), BF16),
                            pltpu.SemaphoreType.DMA((2,))]),
        out_shape=jax.ShapeDtypeStruct((nt * CHUNKS_PER_ETILE, CHUNK_ROWS, d), BF16),
        compiler_params=pltpu.CompilerParams(dimension_semantics=("arbitrary",)),
        name="expert",
    )(tile_expert, n_used, chunk_src, xs, wg, wu, wd)


def _combine_kernel(dst_ref, x_ref, info_ref, g_ref, ys_ref, o_ref, ybuf_ref, sem_ref):
    i = pl.program_id(0)
    n_tiles = pl.num_programs(0)
    ts = x_ref.shape[0] // COMBINE_TILES
    n_chunks = ybuf_ref.shape[1]
    cap = n_chunks // COMBINE_TILES * CHUNK_ROWS
    slot = i % 2

    def chunk_copy(tile, k, buf):
        return pltpu.make_async_copy(ys_ref.at[dst_ref[tile * n_chunks + k]], ybuf_ref.at[buf, k], sem_ref.at[buf])

    def gather(tile, buf):
        for k in range(n_chunks):
            chunk_copy(tile, k, buf).start(priority=k % 2)

    def drain(buf):
        pltpu.make_async_copy(ys_ref.at[pl.ds(0, n_chunks)], ybuf_ref.at[buf], sem_ref.at[buf]).wait()

    @pl.when(i == 0)
    def _():
        gather(0, 0)

    nxt = jnp.minimum(i + 1, n_tiles - 1)
    gather(nxt, 1 - slot)
    drain(slot)

    rowid = lax.broadcasted_iota(jnp.int32, (ts, cap), 1)
    for sub in range(COMBINE_TILES):
        rows = slice(sub * ts, (sub + 1) * ts)
        info = info_ref[rows, :]
        pos1 = info[:, 0:1].astype(jnp.int32)
        pos2 = info[:, 1:2].astype(jnp.int32)
        w1 = info[:, 2:3]
        w2 = info[:, 3:4]
        ys = ybuf_ref[slot, sub * (n_chunks // COMBINE_TILES):(sub + 1) * (n_chunks // COMBINE_TILES)]
        ys = ys.reshape(cap, ybuf_ref.shape[-1])
        pw = jnp.where(rowid == pos1, w1, jnp.where(rowid == pos2, w2, 0.0)).astype(BF16)
        y = jnp.dot(pw, ys, preferred_element_type=F32)
        o_ref[rows, :] = _rms(x_ref[rows, :] + y, g_ref[...])

    @pl.when(i == n_tiles - 1)
    def _():
        drain(1 - slot)


def _combine_call(chunk_dst, x1, info, g_final, ys_sorted, cap):
    t, d = x1.shape
    ts = TOKEN_TILE * COMBINE_TILES
    return pl.pallas_call(
        _combine_kernel,
        grid_spec=pltpu.PrefetchScalarGridSpec(
            num_scalar_prefetch=1,
            grid=(t // ts,),
            in_specs=[pl.BlockSpec((ts, d), lambda i, dst: (i, 0)),
                      pl.BlockSpec((ts, LANES), lambda i, dst: (i, 0)),
                      pl.BlockSpec((1, d), lambda i, dst: (0, 0)),
                      pl.BlockSpec(memory_space=pl.ANY)],
            out_specs=pl.BlockSpec((ts, d), lambda i, dst: (i, 0)),
            scratch_shapes=[pltpu.VMEM((2, COMBINE_TILES * cap // CHUNK_ROWS, CHUNK_ROWS, d), BF16),
                            pltpu.SemaphoreType.DMA((2,))]),
        out_shape=jax.ShapeDtypeStruct((t, d), F32),
        compiler_params=pltpu.CompilerParams(dimension_semantics=("arbitrary",),
                                             vmem_limit_bytes=VMEM_LIMIT),
        name="combine",
    )(chunk_dst, x1, info, g_final, ys_sorted)


def _run_tables(nch, cpt, n_etiles):
    n_tiles, n_exp = nch.shape
    per = CHUNKS_PER_ETILE
    tcum = jnp.cumsum(nch, axis=1)
    toff = tcum - nch
    ecum = jnp.cumsum(nch, axis=0)
    etot = ecum[-1]
    eseg = -(-etot // per) * per
    segcum = jnp.cumsum(eseg)
    ebase = segcum - eseg
    eoff = ebase[None, :] + ecum - nch
    n_used = (segcum[-1] // per).reshape(1)

    k = jnp.arange(cpt, dtype=jnp.int32)[None, :, None]
    in_run = (k >= toff[:, None, :]) & (k < tcum[:, None, :])
    chunk_dst = jnp.sum(jnp.where(in_run, (eoff - toff)[:, None, :] + k, 0), axis=-1)

    first = jnp.arange(n_etiles, dtype=jnp.int32)[:, None] * per
    owner = (first >= ebase[None, :]) & (first < segcum[None, :])
    tile_expert = jnp.sum(jnp.where(owner, jnp.arange(n_exp, dtype=jnp.int32)[None, :], 0), axis=-1)

    pick = lambda a: jnp.sum(jnp.where(owner[:, None, :], a[None, :, :], 0), axis=-1)
    run_lo, run_n = pick(eoff), pick(nch)
    shift = pick(jnp.arange(n_tiles, dtype=jnp.int32)[:, None] * cpt + toff - eoff)
    c = (first + jnp.arange(per, dtype=jnp.int32)[None, :])[:, :, None]
    hit = (c >= run_lo[:, None, :]) & (c < (run_lo + run_n)[:, None, :])
    chunk_src = jnp.sum(jnp.where(hit, shift[:, None, :] + c, 0), axis=-1)
    i32 = lambda a: a.reshape(-1).astype(jnp.int32)
    return i32(chunk_dst), i32(tile_expert), i32(n_used), i32(chunk_src)


def _layer(x, mem, g_mix, w_in, conv_dw, conv_dw_bias, conv_ln_g, conv_ln_b, w_conv_out,
           lam_re, lam_im, log_dt, b_re, b_im, c_re, c_im, ssm_d, w_ssm_glu, g_mem, w_mem_kv,
           w_mem_out, w_out, g_ffn, w_rg, b_rg, w_re, b_re_, w_eg, w_eu, w_ed, g_final):
    b, s, d = x.shape
    t = b * s
    cw = conv_dw.shape[1]
    sw = ssm_d.shape[0]
    qw = w_mem_out.shape[0]
    row = lambda a: a.reshape(1, -1)

    o0, o1, o2 = 2 * cw, 2 * cw + sw, 2 * cw + sw + qw
    wc = w_in[:, :o0].astype(BF16)
    w_ssm_t = w_in[:, o0:o1].T.astype(BF16)
    wq = w_in[:, o1:o2].astype(BF16)
    wg = w_in[:, o2:].astype(BF16)

    kmem, vmem = _kv_call(mem, row(g_mem), w_mem_kv.astype(BF16))

    ut = _ut_call(x.reshape(t, d), row(g_mix), w_ssm_t)
    kt, wz, vy, cab, acoef = _ssm_tables(lam_re, lam_im, log_dt, b_re, b_im, c_re, c_im, ssm_d)
    yt = _ssm_call(ut, kt, wz, vy, cab, acoef, b)

    dw = jnp.concatenate([conv_dw, jnp.zeros((CONV_HALO - CONV_KERNEL, cw), F32)], axis=0)
    x1 = _mix_call(x, yt, kmem, vmem, row(g_mix), wc, wq, wg, dw, row(conv_dw_bias),
                   row(conv_ln_g), row(conv_ln_b), w_conv_out.astype(BF16),
                   w_ssm_glu.astype(BF16), w_mem_out.astype(BF16), w_out.astype(BF16))
    x1 = x1.reshape(t, d)

    pad = LANES - N_EXPERTS - MOE_GROUPS
    wr = jnp.concatenate([w_re, w_rg, jnp.zeros((d, pad), F32)], axis=1)
    wr_hi = wr.astype(BF16)
    wr = jnp.concatenate([wr_hi, (wr - wr_hi.astype(F32)).astype(BF16)], axis=1)
    br = jnp.concatenate([b_re_, b_rg, jnp.zeros((pad,), F32)]).reshape(1, LANES)
    n_tiles = t // TOKEN_TILE
    max_tile_chunks = (2 * TOKEN_TILE + N_EXPERTS * (CHUNK_ROWS - 1)) // CHUNK_ROWS
    cpt = -(-max_tile_chunks // SUBLANES) * SUBLANES
    cap = cpt * CHUNK_ROWS
    xs_tiles, info, nch_f = _route_call(x1, row(g_ffn), wr, br, cap)

    max_chunks = n_tiles * max_tile_chunks + N_EXPERTS * (CHUNKS_PER_ETILE - 1)
    n_etiles = -(-max_chunks // CHUNKS_PER_ETILE)
    nch = nch_f[:, 0, :N_EXPERTS].astype(jnp.int32)
    chunk_dst, tile_expert, n_used, chunk_src = _run_tables(nch, cpt, n_etiles)

    ys_sorted = _expert_call(tile_expert, n_used, chunk_src, xs_tiles, w_eg, w_eu, w_ed)
    out = _combine_call(chunk_dst, x1, info, row(g_final), ys_sorted, cap)
    return out.reshape(b, s, d)


def kernel(x, mem, g_mix, w_in, conv_dw, conv_dw_bias, conv_ln_g, conv_ln_b, w_conv_out, ssm_lambda_re, ssm_lambda_im, ssm_log_dt, ssm_b_re, ssm_b_im, ssm_c_re, ssm_c_im, ssm_d, w_ssm_glu, g_mem, w_mem_kv, w_mem_out, w_out, g_ffn, w_router_group, b_router_group, w_router_expert, b_router_expert, w_exp_gate, w_exp_up, w_exp_down, g_final):
    assert g_mix.shape[0] == 1, "the problem fixes one layer"
    l = 0
    return _layer(
        x, mem, g_mix[l], w_in[l], conv_dw[l], conv_dw_bias[l], conv_ln_g[l], conv_ln_b[l],
        w_conv_out[l], ssm_lambda_re[l], ssm_lambda_im[l], ssm_log_dt[l], ssm_b_re[l],
        ssm_b_im[l], ssm_c_re[l], ssm_c_im[l], ssm_d[l], w_ssm_glu[l], g_mem[l], w_mem_kv[l],
        w_mem_out[l], w_out[l], g_ffn[l], w_router_group[l], b_router_group[l],
        w_router_expert[l], b_router_expert[l], w_exp_gate[l], w_exp_up[l], w_exp_down[l], g_final)
```

```python
import functools

import jax
import jax.numpy as jnp
from jax import lax
from jax.experimental import pallas as pl
from jax.experimental.pallas import tpu as pltpu

F32 = jnp.float32
BF16 = jnp.bfloat16
EPS = 1e-6

LANES = 128
SUBLANES = 8
V7X_VMEM_BYTES = 64 * 1024 * 1024
CHUNK_ROWS = 2 * SUBLANES
SSM_GROUP = 16
SSM_STATE = 64
SSM_BLOCK = LANES
CONV_KERNEL = 31
CONV_HALO = 32
HEADS = 4
HEAD_DIM = 128
MOE_GROUPS = 4
EXPERTS_PER_GROUP = 8
N_EXPERTS = MOE_GROUPS * EXPERTS_PER_GROUP
TOKEN_TILE = 512
MIX_TILE = 1024
COMBINE_TILES = 2
ROUTE_TILES = 2
EXPERT_TILE = 512
CHUNKS_PER_ETILE = EXPERT_TILE // CHUNK_ROWS
UT_TILE = 2048
VMEM_LIMIT = V7X_VMEM_BYTES * 7 // 8


def _rms(x, g):
    return x * lax.rsqrt(jnp.mean(x * x, axis=-1, keepdims=True) + EPS) * g


def _sigmoid(x):
    return 0.5 * jnp.tanh(0.5 * x) + 0.5


def _const_spec(shape):
    zeros = (0,) * len(shape)
    return pl.BlockSpec(shape, lambda *_: zeros, pipeline_mode=pl.Buffered(1))


def _kv_kernel(mem_ref, g_ref, w_ref, k_ref, v_ref):
    width = k_ref.shape[-1]
    mn = _rms(mem_ref[0], g_ref[...]).astype(BF16)
    kv = jnp.dot(mn, w_ref[...], preferred_element_type=F32)
    k_ref[0] = kv[:, :width].astype(BF16)
    v_ref[0] = kv[:, width:].astype(BF16)


def _kv_call(mem, g_mem, w_kv):
    b, m, d = mem.shape
    width = w_kv.shape[1] // 2
    return pl.pallas_call(
        _kv_kernel,
        grid=(b,),
        in_specs=[pl.BlockSpec((1, m, d), lambda i: (i, 0, 0)),
                  pl.BlockSpec((1, d), lambda i: (0, 0)),
                  pl.BlockSpec(w_kv.shape, lambda i: (0, 0))],
        out_specs=[pl.BlockSpec((1, m, width), lambda i: (i, 0, 0)),
                   pl.BlockSpec((1, m, width), lambda i: (i, 0, 0))],
        out_shape=[jax.ShapeDtypeStruct((b, m, width), BF16)] * 2,
        name="kv",
    )(mem, g_mem, w_kv)


def _ut_kernel(x_ref, g_ref, wt_ref, o_ref):
    h = _rms(x_ref[...], g_ref[...]).astype(BF16)
    ut = lax.dot_general(wt_ref[...], h, (((1,), (1,)), ((), ())), preferred_element_type=F32)
    ut = ut.reshape(ut.shape[0], ut.shape[1] // SSM_BLOCK, SSM_BLOCK)
    o_ref[...] = ut.reshape(o_ref.shape).astype(BF16)


def _ut_call(x2, g_mix, w_ssm_t):
    t, d = x2.shape
    c = w_ssm_t.shape[0]
    ts = UT_TILE
    n_groups = c // SSM_GROUP
    return pl.pallas_call(
        _ut_kernel,
        grid=(t // ts,),
        in_specs=[pl.BlockSpec((ts, d), lambda i: (i, 0)),
                  pl.BlockSpec((1, d), lambda i: (0, 0)),
                  pl.BlockSpec((c, d), lambda i: (0, 0))],
        out_specs=pl.BlockSpec((n_groups, SSM_GROUP, ts // SSM_BLOCK, SSM_BLOCK),
                               lambda i: (0, 0, i, 0)),
        out_shape=jax.ShapeDtypeStruct((n_groups, SSM_GROUP, t // SSM_BLOCK, SSM_BLOCK), BF16),
        name="ut",
    )(x2, g_mix, w_ssm_t)


def _ssm_kernel(u_ref, kt_ref, ktn_ref, w_ref, v_ref, cab_ref, a_ref, y_ref, acc_ref, z_ref, zs_ref, s_ref,
                slab_ref, *, n_batch):
    rows = u_ref.shape[2]
    n_blocks = rows // n_batch
    ri = lax.broadcasted_iota(jnp.int32, (SSM_BLOCK, SSM_BLOCK), 0)
    ci = lax.broadcasted_iota(jnp.int32, (SSM_BLOCK, SSM_BLOCK), 1)
    causal = ci >= ri

    n_pairs = SSM_GROUP // 2

    def build(cp, slot, src_ref=kt_ref):
        for half in range(2):
            kt = src_ref[0, 2 * cp + half]
            for c in range(SSM_GROUP):
                xb = jnp.broadcast_to(kt[c:c + 1, :], (SSM_BLOCK, SSM_BLOCK))
                toe = pltpu.roll(xb, 0, 1, stride=1, stride_axis=0)
                slab_ref[slot, half * SSM_BLOCK:(half + 1) * SSM_BLOCK, c * SSM_BLOCK:(c + 1) * SSM_BLOCK] = (
                    jnp.where(causal, toe, 0.0).astype(BF16))

    def apply(cp, slot, first):
        x2 = jnp.concatenate([u_ref[0, 2 * cp], u_ref[0, 2 * cp + 1]], axis=1)
        part = jnp.dot(x2, slab_ref[slot], preferred_element_type=F32)
        wrow = pl.multiple_of(cp * 2 * SSM_BLOCK, 2 * SSM_BLOCK)
        zpart = jnp.dot(x2, w_ref[0, pl.ds(wrow, 2 * SSM_BLOCK), :], preferred_element_type=F32)
        if first:
            acc_ref[...] = part
            z_ref[...] = zpart
        else:
            acc_ref[...] += part
            z_ref[...] += zpart

    @pl.when(pl.program_id(0) == 0)
    def _():
        build(0, 0)
        build(1, 1)

    apply(0, 0, True)

    def two(it, carry):
        cp = 2 * it + 1
        build(cp + 1, 0)
        apply(cp, 1, False)
        build(cp + 2, 1)
        apply(cp + 1, 0, False)
        return carry

    lax.fori_loop(0, (n_pairs - 2) // 2, two, 0)
    build(0, 0, ktn_ref)
    apply(n_pairs - 1, 1, False)
    build(1, 1, ktn_ref)

    a_full = a_ref[0, 0:1, :]
    a_swap = a_ref[0, 1:2, :]
    zs_ref[...] = pltpu.roll(z_ref[...], SSM_STATE, 1)
    st = jnp.zeros((n_batch, 2 * SSM_STATE), F32)
    sw = jnp.zeros((n_batch, 2 * SSM_STATE), F32)
    for blk in range(n_blocks):
        sl = pl.ds(blk, n_batch, stride=n_blocks)
        s_ref[sl, :] = st
        st, sw = (a_full * st + a_swap * sw + z_ref[sl, :],
                  a_full * sw - a_swap * st + zs_ref[sl, :])

    s2 = jnp.concatenate([s_ref[...], s_ref[...]], axis=1)
    y3 = jnp.stack([acc_ref[:, c * SSM_BLOCK:(c + 1) * SSM_BLOCK]
                    + jnp.dot((s2 * cab_ref[0, c:c + 1, :]).astype(BF16), v_ref[0],
                              preferred_element_type=F32)
                    for c in range(SSM_GROUP)], axis=0)
    y_ref[...] = y3.reshape(y_ref.shape)


def _ssm_call(u4, kt, wz, vy, cab, acoef, n_batch):
    g, c, rows, blk = u4.shape
    width = c * blk
    return pl.pallas_call(
        functools.partial(_ssm_kernel, n_batch=n_batch),
        grid=(g,),
        in_specs=[pl.BlockSpec((1, c, rows, blk), lambda i: (i, 0, 0, 0)),
                  pl.BlockSpec((1, c, c, blk), lambda i: (i, 0, 0, 0)),
                  pl.BlockSpec((1, c, c, blk), lambda i: (jnp.minimum(i + 1, g - 1), 0, 0, 0)),
                  pl.BlockSpec((1, width, 2 * SSM_STATE), lambda i: (i, 0, 0)),
                  pl.BlockSpec((1, 4 * SSM_STATE, blk), lambda i: (i, 0, 0)),
                  pl.BlockSpec((1, c, 4 * SSM_STATE), lambda i: (i, 0, 0)),
                  pl.BlockSpec((1, 2, 2 * SSM_STATE), lambda i: (i, 0, 0))],
        out_specs=pl.BlockSpec((c, rows * blk), lambda i: (i, 0)),
        out_shape=jax.ShapeDtypeStruct((g * c, rows * blk), F32),
        scratch_shapes=[pltpu.VMEM((rows, width), F32),
                        pltpu.VMEM((rows, 2 * SSM_STATE), F32),
                        pltpu.VMEM((rows, 2 * SSM_STATE), F32),
                        pltpu.VMEM((rows, 2 * SSM_STATE), F32),
                        pltpu.VMEM((2, 2 * blk, width), BF16)],
        compiler_params=pltpu.CompilerParams(dimension_semantics=("arbitrary",)),
        name="ssm",
    )(u4, kt, kt, wz, vy, cab, acoef)


def _ssm_tables(lam_re, lam_im, log_dt, b_re, b_im, c_re, c_im, d):
    hi = lax.Precision.HIGHEST
    g = lam_re.shape[0]
    dt = jnp.exp(log_dt)[:, None]
    er, ei = lam_re * dt, lam_im * dt
    cat = jnp.concatenate
    kk = jnp.arange(SSM_BLOCK + 1, dtype=F32)
    kdesc = jnp.arange(SSM_BLOCK - 1, -1, -1, dtype=F32)[None, :, None]
    ppk_m, ppk_a = jnp.exp(er[:, :, None] * kk), ei[:, :, None] * kk
    ppk_r, ppk_i = ppk_m * jnp.cos(ppk_a), ppk_m * jnp.sin(ppk_a)
    prev_m, prev_a = jnp.exp(kdesc * er[:, None, :]), kdesc * ei[:, None, :]
    prr, pir = prev_m * jnp.cos(prev_a), prev_m * jnp.sin(prev_a)
    nr, ni = ppk_r[:, :, 1] - 1.0, ppk_i[:, :, 1]
    den = lam_re * lam_re + lam_im * lam_im
    fr = (nr * lam_re + ni * lam_im) / den
    fi = (ni * lam_re - nr * lam_im) / den
    bbr = jnp.swapaxes(fr[:, :, None] * b_re - fi[:, :, None] * b_im, 1, 2)
    bbi = jnp.swapaxes(fr[:, :, None] * b_im + fi[:, :, None] * b_re, 1, 2)
    cb_r = c_re[:, None] * bbr[:, :, None, :] - c_im[:, None] * bbi[:, :, None, :]
    cb_i = c_re[:, None] * bbi[:, :, None, :] + c_im[:, None] * bbr[:, :, None, :]
    cb = cat([cb_r, -cb_i], axis=-1).reshape(g, SSM_GROUP * SSM_GROUP, 2 * SSM_STATE)
    pk = cat([ppk_r[:, :, :SSM_BLOCK], ppk_i[:, :, :SSM_BLOCK]], axis=1)
    kt = jnp.einsum("gmq,gqk->gmk", cb, pk, precision=hi).reshape(g, SSM_GROUP, SSM_GROUP, SSM_BLOCK)
    dmat = jnp.eye(SSM_GROUP, dtype=F32)[None] * d.reshape(g, 1, SSM_GROUP)
    kt = kt + dmat[..., None] * (jnp.arange(SSM_BLOCK) == 0).astype(F32)
    wz = (cat([prr, prr], -1)[:, None] * cat([bbr, bbi], -1)[:, :, None, :]
          + cat([pir, pir], -1)[:, None] * cat([-bbi, bbr], -1)[:, :, None, :])
    wz = wz.astype(BF16).reshape(g, SSM_GROUP * SSM_BLOCK, 2 * SSM_STATE)
    pt_r, pt_i = ppk_r[:, :, 1:], ppk_i[:, :, 1:]
    vy = cat([pt_r, pt_i, pt_i, pt_r], axis=1).astype(BF16)
    cab = cat([c_re, -c_re, -c_im, -c_im], -1)
    ar, ai = ppk_r[:, :, SSM_BLOCK], ppk_i[:, :, SSM_BLOCK]
    acoef = jnp.stack([cat([ar, ar], -1), cat([-ai, ai], -1)], axis=1)
    return kt, wz, vy, cab, acoef


def _mix_kernel(x_ref, yt_ref, k_ref, v_ref, g_ref, wc_ref, wq_ref, wg_ref, dw_ref, dwb_ref,
                lng_ref, lnb_ref, wpw_ref, wglu_ref, wo_ref, wout_ref, o_ref, vext_ref, vsh_ref):
    ts = x_ref.shape[1]
    d = x_ref.shape[2]
    cw = dw_ref.shape[1]
    x = x_ref[0]
    h = _rms(x, g_ref[...]).astype(BF16)

    ci = jnp.dot(h, wc_ref[...], preferred_element_type=F32)
    v = ci[:, :cw] * _sigmoid(ci[:, cw:])
    q = jnp.dot(h, wq_ref[...], preferred_element_type=F32)

    @pl.when(pl.program_id(1) == 0)
    def _():
        vext_ref[0:CONV_HALO, :] = jnp.zeros((CONV_HALO, cw), F32)

    vext_ref[CONV_HALO:CONV_HALO + ts, :] = v
    acc = jnp.broadcast_to(dwb_ref[...], (ts, cw))
    first = CONV_HALO - (CONV_KERNEL - 1)
    for phase in range(SUBLANES):
        offs = [first + k for k in range(CONV_KERNEL) if (first + k) % SUBLANES == phase]
        if not offs:
            continue
        span = offs[-1] - offs[0] + ts
        vsh_ref[0:span, :] = vext_ref[offs[0]:offs[0] + span, :]
        for off in offs:
            acc = acc + dw_ref[off - first:off - first + 1, :] * vsh_ref[off - offs[0]:off - offs[0] + ts, :]
    vext_ref[0:CONV_HALO, :] = vext_ref[ts:ts + CONV_HALO, :]
    mu = jnp.mean(acc, axis=-1, keepdims=True)
    xc = acc - mu
    var = jnp.mean(xc * xc, axis=-1, keepdims=True)
    ln = xc * lax.rsqrt(var + EPS) * lng_ref[...] + lnb_ref[...]
    sw = ln * _sigmoid(ln)
    y_conv = jnp.dot(sw.astype(BF16), wpw_ref[...], preferred_element_type=F32)
    merged = _sigmoid(jnp.dot(h, wg_ref[:, 0:d], preferred_element_type=F32)) * y_conv

    gy = jax.nn.gelu(yt_ref[...]).astype(BF16)
    z = lax.dot_general(gy, wglu_ref[...], (((0,), (0,)), ((), ())), preferred_element_type=F32)
    y_ssm = z[:, :d] * _sigmoid(z[:, d:])
    merged = merged + _sigmoid(jnp.dot(h, wg_ref[:, d:2 * d], preferred_element_type=F32)) * y_ssm

    kk = k_ref[0]
    vv = v_ref[0]
    outs = []
    for hd in range(HEADS):
        sl = slice(hd * HEAD_DIM, (hd + 1) * HEAD_DIM)
        s = lax.dot_general(q[:, sl].astype(BF16), kk[:, sl], (((1,), (1,)), ((), ())),
                            preferred_element_type=F32) * (HEAD_DIM ** -0.5)
        p = jnp.exp(s - jnp.max(s, axis=-1, keepdims=True))
        den = jnp.sum(p, axis=-1, keepdims=True)
        o = jnp.dot(p.astype(BF16), vv[:, sl], preferred_element_type=F32) / den
        outs.append(o.astype(BF16))
    y_mem = jnp.dot(jnp.concatenate(outs, axis=1), wo_ref[...], preferred_element_type=F32)
    merged = merged + _sigmoid(jnp.dot(h, wg_ref[:, 2 * d:3 * d], preferred_element_type=F32)) * y_mem

    o_ref[0] = x + jnp.dot(merged.astype(BF16), wout_ref[...], preferred_element_type=F32)


def _mix_call(x, yt, kmem, vmem, g_mix, wc, wq, wg, dw, dwb, lng, lnb, wpw, wglu, wo, wout):
    b, s, d = x.shape
    ts = MIX_TILE
    nst = s // ts
    cw = dw.shape[1]
    m = kmem.shape[1]
    consts = [g_mix, wc, wq, wg, dw, dwb, lng, lnb, wpw, wglu, wo, wout]
    return pl.pallas_call(
        _mix_kernel,
        grid=(b, nst),
        in_specs=[pl.BlockSpec((1, ts, d), lambda i, j: (i, j, 0)),
                  pl.BlockSpec((yt.shape[0], ts), lambda i, j: (0, i * nst + j)),
                  pl.BlockSpec((1, m, kmem.shape[2]), lambda i, j: (i, 0, 0)),
                  pl.BlockSpec((1, m, vmem.shape[2]), lambda i, j: (i, 0, 0))]
                 + [_const_spec(c.shape) for c in consts],
        out_specs=pl.BlockSpec((1, ts, d), lambda i, j: (i, j, 0)),
        out_shape=jax.ShapeDtypeStruct((b, s, d), F32),
        scratch_shapes=[pltpu.VMEM((ts + CONV_HALO, cw), F32),
                        pltpu.VMEM((ts + CONV_HALO, cw), F32)],
        compiler_params=pltpu.CompilerParams(
            dimension_semantics=("arbitrary", "arbitrary"), vmem_limit_bytes=VMEM_LIMIT),
        name="mix",
    )(x, yt, kmem, vmem, *consts)


def _route_kernel(x_ref, g_ref, wr_ref, br_ref, xs_ref, info_ref, nch_ref):
    ts = x_ref.shape[0] // ROUTE_TILES
    cpt = xs_ref.shape[0] // ROUTE_TILES
    for sub in range(ROUTE_TILES):
        rows = slice(sub * ts, (sub + 1) * ts)
        xs, info, nch = _route_tile(x_ref[rows, :], g_ref[...], wr_ref, br_ref[...], cpt * CHUNK_ROWS)
        xs_ref[sub * cpt:(sub + 1) * cpt] = xs.reshape(cpt, CHUNK_ROWS, xs.shape[-1])
        info_ref[rows, :] = info
        nch_ref[sub] = nch


def _route_tile(x, g, wr_ref, br, cap):
    ts = x.shape[0]
    h2 = _rms(x, g)
    hb = h2.astype(BF16)
    hl = (h2 - hb.astype(F32)).astype(BF16)
    hw = jnp.dot(hb, wr_ref[...], preferred_element_type=F32)
    logits = (hw[:, :LANES] + hw[:, LANES:]
              + jnp.dot(hl, wr_ref[:, :LANES], preferred_element_type=F32)) + br
    lt = logits.T
    le = lt[0:N_EXPERTS]
    lg = lt[N_EXPERTS:N_EXPERTS + SUBLANES]
    neg = jnp.float32(-1e30)
    big = jnp.float32(1e9)
    g_f = lax.broadcasted_iota(jnp.int32, (SUBLANES, ts), 0).astype(F32)
    e_f = lax.broadcasted_iota(jnp.int32, (N_EXPERTS, ts), 0).astype(F32)

    gmask = g_f < MOE_GROUPS
    gmax = jnp.max(jnp.where(gmask, lg, neg), axis=0, keepdims=True)
    gidx = jnp.min(jnp.where(gmask & (lg == gmax), g_f, big), axis=0, keepdims=True)
    gsum = jnp.sum(jnp.where(gmask, jnp.exp(jnp.minimum(lg - gmax, 0.0)), 0.0), axis=0, keepdims=True)
    p_top = 1.0 / gsum
    emask = jnp.floor(e_f * (1.0 / EXPERTS_PER_GROUP)) == gidx
    m1 = jnp.max(jnp.where(emask, le, neg), axis=0, keepdims=True)
    i1 = jnp.min(jnp.where(emask & (le == m1), e_f, big), axis=0, keepdims=True)
    emask2 = emask & (e_f != i1)
    m2 = jnp.max(jnp.where(emask2, le, neg), axis=0, keepdims=True)
    i2 = jnp.min(jnp.where(emask2 & (le == m2), e_f, big), axis=0, keepdims=True)
    r = jnp.exp(m2 - m1)
    w1 = p_top / (1.0 + r)
    w2 = p_top * r / (1.0 + r)

    sel1 = e_f == i1
    sel2 = e_f == i2
    occ = jnp.where(sel1 | sel2, 1.0, 0.0)
    tr = lax.broadcasted_iota(jnp.int32, (ts, ts), 0)
    tc = lax.broadcasted_iota(jnp.int32, (ts, ts), 1)
    earlier = jnp.where(tr < tc, 1.0, 0.0).astype(BF16)
    rank = jnp.dot(occ.astype(BF16), earlier, preferred_element_type=F32)
    cnt = jnp.sum(occ, axis=1, keepdims=True)
    nch = jnp.floor((cnt + (CHUNK_ROWS - 1)) * (1.0 / CHUNK_ROWS))
    er = lax.broadcasted_iota(jnp.int32, (N_EXPERTS, N_EXPERTS), 0)
    ec = lax.broadcasted_iota(jnp.int32, (N_EXPERTS, N_EXPERTS), 1)
    lower = jnp.where(ec < er, 1.0, 0.0).astype(BF16)
    nch_b = jnp.broadcast_to(nch, (N_EXPERTS, LANES))
    start = jnp.dot(lower, nch_b.astype(BF16), preferred_element_type=F32)[:, 0:1] * CHUNK_ROWS
    slot = start + rank
    pos1 = jnp.sum(jnp.where(sel1, slot, 0.0), axis=0, keepdims=True)
    pos2 = jnp.sum(jnp.where(sel2, slot, 0.0), axis=0, keepdims=True)

    rowid = lax.broadcasted_iota(jnp.int32, (cap, ts), 0)
    p = jnp.where((rowid == pos1.astype(jnp.int32)) | (rowid == pos2.astype(jnp.int32)), 1.0, 0.0)
    xs = jnp.dot(p.astype(BF16), hb, preferred_element_type=F32).astype(BF16)
    sub = lax.broadcasted_iota(jnp.int32, (LANES, ts), 0)
    info_t = jnp.where(sub == 0, pos1, jnp.where(sub == 1, pos2,
                       jnp.where(sub == 2, w1, jnp.where(sub == 3, w2, 0.0))))
    nch_rows = jnp.concatenate([nch_b, jnp.zeros((LANES - N_EXPERTS, LANES), F32)], axis=0).T[0:SUBLANES, :]
    return xs, info_t.T, nch_rows


def _route_call(x1, g_ffn, wr, br, cap):
    t, d = x1.shape
    nt = t // TOKEN_TILE
    ts = TOKEN_TILE * ROUTE_TILES
    return pl.pallas_call(
        _route_kernel,
        grid=(t // ts,),
        in_specs=[pl.BlockSpec((ts, d), lambda i: (i, 0)),
                  pl.BlockSpec((1, d), lambda i: (0, 0)),
                  pl.BlockSpec(wr.shape, lambda i: (0, 0)),
                  pl.BlockSpec((1, LANES), lambda i: (0, 0))],
        out_specs=[pl.BlockSpec((ROUTE_TILES * cap // CHUNK_ROWS, CHUNK_ROWS, d), lambda i: (i, 0, 0)),
                   pl.BlockSpec((ts, LANES), lambda i: (i, 0)),
                   pl.BlockSpec((ROUTE_TILES, SUBLANES, LANES), lambda i: (i, 0, 0))],
        out_shape=[jax.ShapeDtypeStruct((nt * cap // CHUNK_ROWS, CHUNK_ROWS, d), BF16),
                   jax.ShapeDtypeStruct((t, LANES), F32),
                   jax.ShapeDtypeStruct((nt, SUBLANES, LANES), F32)],
        compiler_params=pltpu.CompilerParams(vmem_limit_bytes=VMEM_LIMIT),
        name="route",
    )(x1, g_ffn, wr, br)


def _expert_kernel(te_ref, nu_ref, src_ref, xs_ref, wg_ref, wu_ref, wd_ref, o_ref, xbuf_ref, sem_ref):
    i = pl.program_id(0)
    n_used = nu_ref[0]
    slot = i % 2

    def chunk_copy(tile, c, buf):
        return pltpu.make_async_copy(
            xs_ref.at[src_ref[tile * CHUNKS_PER_ETILE + c]], xbuf_ref.at[buf, c], sem_ref.at[buf])

    def gather(tile, buf):
        for c in range(CHUNKS_PER_ETILE):
            chunk_copy(tile, c, buf).start(priority=c % 2)

    def drain(buf):
        pltpu.make_async_copy(xs_ref.at[pl.ds(0, CHUNKS_PER_ETILE)], xbuf_ref.at[buf], sem_ref.at[buf]).wait()

    @pl.when(i == 0)
    def _():
        gather(0, 0)

    @pl.when(i < n_used)
    def _():
        nxt = jnp.minimum(i + 1, n_used - 1)
        gather(nxt, 1 - slot)
        drain(slot)
        x = xbuf_ref[slot].reshape(EXPERT_TILE, xbuf_ref.shape[-1])
        gate = jnp.dot(x, wg_ref[0].astype(BF16), preferred_element_type=F32)
        up = jnp.dot(x, wu_ref[0].astype(BF16), preferred_element_type=F32)
        act = (gate * _sigmoid(gate) * up).astype(BF16)
        y = jnp.dot(act, wd_ref[0].astype(BF16), preferred_element_type=F32).astype(BF16)
        o_ref[...] = y.reshape(o_ref.shape)

        @pl.when(i == n_used - 1)
        def _():
            drain(1 - slot)

    @pl.when(i >= n_used)
    def _():
        o_ref[...] = jnp.zeros_like(o_ref)


def _expert_call(tile_expert, n_used, chunk_src, xs, wg, wu, wd):
    d = xs.shape[-1]
    de = wg.shape[2]
    tm = EXPERT_TILE
    nt = tile_expert.shape[0]

    def w_map(i, te, nu, src):
        return (te[jnp.maximum(jnp.minimum(i, nu[0] - 1), 0)], 0, 0)

    return pl.pallas_call(
        _expert_kernel,
        grid_spec=pltpu.PrefetchScalarGridSpec(
            num_scalar_prefetch=3,
            grid=(nt,),
            in_specs=[pl.BlockSpec(memory_space=pl.ANY),
                      pl.BlockSpec((1, d, de), w_map),
                      pl.BlockSpec((1, d, de), w_map),
                      pl.BlockSpec((1, de, d), w_map)],
            out_specs=pl.BlockSpec((CHUNKS_PER_ETILE, CHUNK_ROWS, d), lambda i, te, nu, src: (i, 0, 0)),
            scratch_shapes=[pltpu.VMEM((2, CHUNKS_PER_ETILE, CHUNK_ROWS, d), BF16),
                            pltpu.SemaphoreType.DMA((2,))]),
        out_shape=jax.ShapeDtypeStruct((nt * CHUNKS_PER_ETILE, CHUNK_ROWS, d), BF16),
        compiler_params=pltpu.CompilerParams(dimension_semantics=("arbitrary",)),
        name="expert",
    )(tile_expert, n_used, chunk_src, xs, wg, wu, wd)


def _combine_kernel(dst_ref, x_ref, info_ref, g_ref, ys_ref, o_ref, ybuf_ref, sem_ref):
    i = pl.program_id(0)
    n_tiles = pl.num_programs(0)
    ts = x_ref.shape[0] // COMBINE_TILES
    n_chunks = ybuf_ref.shape[1]
    cap = n_chunks // COMBINE_TILES * CHUNK_ROWS
    slot = i % 2

    def chunk_copy(tile, k, buf):
        return pltpu.make_async_copy(ys_ref.at[dst_ref[tile * n_chunks + k]], ybuf_ref.at[buf, k], sem_ref.at[buf])

    def gather(tile, buf):
        for k in range(n_chunks):
            chunk_copy(tile, k, buf).start(priority=k % 2)

    def drain(buf):
        pltpu.make_async_copy(ys_ref.at[pl.ds(0, n_chunks)], ybuf_ref.at[buf], sem_ref.at[buf]).wait()

    @pl.when(i == 0)
    def _():
        gather(0, 0)

    nxt = jnp.minimum(i + 1, n_tiles - 1)
    gather(nxt, 1 - slot)
    drain(slot)

    rowid = lax.broadcasted_iota(jnp.int32, (ts, cap), 1)
    for sub in range(COMBINE_TILES):
        rows = slice(sub * ts, (sub + 1) * ts)
        info = info_ref[rows, :]
        pos1 = info[:, 0:1].astype(jnp.int32)
        pos2 = info[:, 1:2].astype(jnp.int32)
        w1 = info[:, 2:3]
        w2 = info[:, 3:4]
        ys = ybuf_ref[slot, sub * (n_chunks // COMBINE_TILES):(sub + 1) * (n_chunks // COMBINE_TILES)]
        ys = ys.reshape(cap, ybuf_ref.shape[-1])
        pw = jnp.where(rowid == pos1, w1, jnp.where(rowid == pos2, w2, 0.0)).astype(BF16)
        y = jnp.dot(pw, ys, preferred_element_type=F32)
        o_ref[rows, :] = _rms(x_ref[rows, :] + y, g_ref[...])

    @pl.when(i == n_tiles - 1)
    def _():
        drain(1 - slot)


def _combine_call(chunk_dst, x1, info, g_final, ys_sorted, cap):
    t, d = x1.shape
    ts = TOKEN_TILE * COMBINE_TILES
    return pl.pallas_call(
        _combine_kernel,
        grid_spec=pltpu.PrefetchScalarGridSpec(
            num_scalar_prefetch=1,
            grid=(t // ts,),
            in_specs=[pl.BlockSpec((ts, d), lambda i, dst: (i, 0)),
                      pl.BlockSpec((ts, LANES), lambda i, dst: (i, 0)),
                      pl.BlockSpec((1, d), lambda i, dst: (0, 0)),
                      pl.BlockSpec(memory_space=pl.ANY)],
            out_specs=pl.BlockSpec((ts, d), lambda i, dst: (i, 0)),
            scratch_shapes=[pltpu.VMEM((2, COMBINE_TILES * cap // CHUNK_ROWS, CHUNK_ROWS, d), BF16),
                            pltpu.SemaphoreType.DMA((2,))]),
        out_shape=jax.ShapeDtypeStruct((t, d), F32),
        compiler_params=pltpu.CompilerParams(dimension_semantics=("arbitrary",),
                                             vmem_limit_bytes=VMEM_LIMIT),
        name="combine",
    )(chunk_dst, x1, info, g_final, ys_sorted)


def _run_tables(nch, cpt, n_etiles):
    n_tiles, n_exp = nch.shape
    per = CHUNKS_PER_ETILE
    tcum = jnp.cumsum(nch, axis=1)
    toff = tcum - nch
    ecum = jnp.cumsum(nch, axis=0)
    etot = ecum[-1]
    eseg = -(-etot // per) * per
    segcum = jnp.cumsum(eseg)
    ebase = segcum - eseg
    eoff = ebase[None, :] + ecum - nch
    n_used = (segcum[-1] // per).reshape(1)

    k = jnp.arange(cpt, dtype=jnp.int32)[None, :, None]
    in_run = (k >= toff[:, None, :]) & (k < tcum[:, None, :])
    chunk_dst = jnp.sum(jnp.where(in_run, (eoff - toff)[:, None, :] + k, 0), axis=-1)

    first = jnp.arange(n_etiles, dtype=jnp.int32)[:, None] * per
    owner = (first >= ebase[None, :]) & (first < segcum[None, :])
    tile_expert = jnp.sum(jnp.where(owner, jnp.arange(n_exp, dtype=jnp.int32)[None, :], 0), axis=-1)

    pick = lambda a: jnp.sum(jnp.where(owner[:, None, :], a[None, :, :], 0), axis=-1)
    run_lo, run_n = pick(eoff), pick(nch)
    shift = pick(jnp.arange(n_tiles, dtype=jnp.int32)[:, None] * cpt + toff - eoff)
    c = (first + jnp.arange(per, dtype=jnp.int32)[None, :])[:, :, None]
    hit = (c >= run_lo[:, None, :]) & (c < (run_lo + run_n)[:, None, :])
    chunk_src = jnp.sum(jnp.where(hit, shift[:, None, :] + c, 0), axis=-1)
    i32 = lambda a: a.reshape(-1).astype(jnp.int32)
    return i32(chunk_dst), i32(tile_expert), i32(n_used), i32(chunk_src)


def _layer(x, mem, g_mix, w_in, conv_dw, conv_dw_bias, conv_ln_g, conv_ln_b, w_conv_out,
           lam_re, lam_im, log_dt, b_re, b_im, c_re, c_im, ssm_d, w_ssm_glu, g_mem, w_mem_kv,
           w_mem_out, w_out, g_ffn, w_rg, b_rg, w_re, b_re_, w_eg, w_eu, w_ed, g_final):
    b, s, d = x.shape
    t = b * s
    cw = conv_dw.shape[1]
    sw = ssm_d.shape[0]
    qw = w_mem_out.shape[0]
    row = lambda a: a.reshape(1, -1)

    o0, o1, o2 = 2 * cw, 2 * cw + sw, 2 * cw + sw + qw
    wc = w_in[:, :o0].astype(BF16)
    w_ssm_t = w_in[:, o0:o1].T.astype(BF16)
    wq = w_in[:, o1:o2].astype(BF16)
    wg = w_in[:, o2:].astype(BF16)

    kmem, vmem = _kv_call(mem, row(g_mem), w_mem_kv.astype(BF16))

    ut = _ut_call(x.reshape(t, d), row(g_mix), w_ssm_t)
    kt, wz, vy, cab, acoef = _ssm_tables(lam_re, lam_im, log_dt, b_re, b_im, c_re, c_im, ssm_d)
    yt = _ssm_call(ut, kt, wz, vy, cab, acoef, b)

    dw = jnp.concatenate([conv_dw, jnp.zeros((CONV_HALO - CONV_KERNEL, cw), F32)], axis=0)
    x1 = _mix_call(x, yt, kmem, vmem, row(g_mix), wc, wq, wg, dw, row(conv_dw_bias),
                   row(conv_ln_g), row(conv_ln_b), w_conv_out.astype(BF16),
                   w_ssm_glu.astype(BF16), w_mem_out.astype(BF16), w_out.astype(BF16))
    x1 = x1.reshape(t, d)

    pad = LANES - N_EXPERTS - MOE_GROUPS
    wr = jnp.concatenate([w_re, w_rg, jnp.zeros((d, pad), F32)], axis=1)
    wr_hi = wr.astype(BF16)
    wr = jnp.concatenate([wr_hi, (wr - wr_hi.astype(F32)).astype(BF16)], axis=1)
    br = jnp.concatenate([b_re_, b_rg, jnp.zeros((pad,), F32)]).reshape(1, LANES)
    n_tiles = t // TOKEN_TILE
    max_tile_chunks = (2 * TOKEN_TILE + N_EXPERTS * (CHUNK_ROWS - 1)) // CHUNK_ROWS
    cpt = -(-max_tile_chunks // SUBLANES) * SUBLANES
    cap = cpt * CHUNK_ROWS
    xs_tiles, info, nch_f = _route_call(x1, row(g_ffn), wr, br, cap)

    max_chunks = n_tiles * max_tile_chunks + N_EXPERTS * (CHUNKS_PER_ETILE - 1)
    n_etiles = -(-max_chunks // CHUNKS_PER_ETILE)
    nch = nch_f[:, 0, :N_EXPERTS].astype(jnp.int32)
    chunk_dst, tile_expert, n_used, chunk_src = _run_tables(nch, cpt, n_etiles)

    ys_sorted = _expert_call(tile_expert, n_used, chunk_src, xs_tiles, w_eg, w_eu, w_ed)
    out = _combine_call(chunk_dst, x1, info, row(g_final), ys_sorted, cap)
    return out.reshape(b, s, d)


def kernel(x, mem, g_mix, w_in, conv_dw, conv_dw_bias, conv_ln_g, conv_ln_b, w_conv_out, ssm_lambda_re, ssm_lambda_im, ssm_log_dt, ssm_b_re, ssm_b_im, ssm_c_re, ssm_c_im, ssm_d, w_ssm_glu, g_mem, w_mem_kv, w_mem_out, w_out, g_ffn, w_router_group, b_router_group, w_router_expert, b_router_expert, w_exp_gate, w_exp_up, w_exp_down, g_final):
    assert g_mix.shape[0] == 1, "the problem fixes one layer"
    l = 0
    return _layer(
        x, mem, g_mix[l], w_in[l], conv_dw[l], conv_dw_bias[l], conv_ln_g[l], conv_ln_b[l],
        w_conv_out[l], ssm_lambda_re[l], ssm_lambda_im[l], ssm_log_dt[l], ssm_b_re[l],
        ssm_b_im[l], ssm_c_re[l], ssm_c_im[l], ssm_d[l], w_ssm_glu[l], g_mem[l], w_mem_kv[l],
        w_mem_out[l], w_out[l], g_ffn[l], w_router_group[l], b_router_group[l],
        w_router_expert[l], b_router_expert[l], w_exp_gate[l], w_exp_up[l], w_exp_down[l], g_final)
```

```python
import functools

import jax
import jax.numpy as jnp
from jax import lax
from jax.experimental import pallas as pl
from jax.experimental.pallas import tpu as pltpu

F32 = jnp.float32
BF16 = jnp.bfloat16
EPS = 1e-6

LANES = 128
SUBLANES = 8
V7X_VMEM_BYTES = 64 * 1024 * 1024
CHUNK_ROWS = 2 * SUBLANES
SSM_GROUP = 16
SSM_STATE = 64
SSM_BLOCK = LANES
CONV_KERNEL = 31
CONV_HALO = 32
HEADS = 4
HEAD_DIM = 128
MOE_GROUPS = 4
EXPERTS_PER_GROUP = 8
N_EXPERTS = MOE_GROUPS * EXPERTS_PER_GROUP
TOKEN_TILE = 512
MIX_TILE = 1024
COMBINE_TILES = 2
ROUTE_TILES = 4
EXPERT_TILE = 512
CHUNKS_PER_ETILE = EXPERT_TILE // CHUNK_ROWS
UT_TILE = 2048
VMEM_LIMIT = V7X_VMEM_BYTES * 7 // 8


def _rms(x, g):
    return x * lax.rsqrt(jnp.mean(x * x, axis=-1, keepdims=True) + EPS) * g


def _sigmoid(x):
    return 0.5 * jnp.tanh(0.5 * x) + 0.5


def _const_spec(shape):
    zeros = (0,) * len(shape)
    return pl.BlockSpec(shape, lambda *_: zeros, pipeline_mode=pl.Buffered(1))


def _kv_kernel(mem_ref, g_ref, w_ref, k_ref, v_ref):
    width = k_ref.shape[-1]
    mn = _rms(mem_ref[0], g_ref[...]).astype(BF16)
    kv = jnp.dot(mn, w_ref[...], preferred_element_type=F32)
    k_ref[0] = kv[:, :width].astype(BF16)
    v_ref[0] = kv[:, width:].astype(BF16)


def _kv_call(mem, g_mem, w_kv):
    b, m, d = mem.shape
    width = w_kv.shape[1] // 2
    return pl.pallas_call(
        _kv_kernel,
        grid=(b,),
        in_specs=[pl.BlockSpec((1, m, d), lambda i: (i, 0, 0)),
                  pl.BlockSpec((1, d), lambda i: (0, 0)),
                  pl.BlockSpec(w_kv.shape, lambda i: (0, 0))],
        out_specs=[pl.BlockSpec((1, m, width), lambda i: (i, 0, 0)),
                   pl.BlockSpec((1, m, width), lambda i: (i, 0, 0))],
        out_shape=[jax.ShapeDtypeStruct((b, m, width), BF16)] * 2,
        name="kv",
    )(mem, g_mem, w_kv)


def _ut_kernel(x_ref, g_ref, wt_ref, o_ref):
    h = _rms(x_ref[...], g_ref[...]).astype(BF16)
    ut = lax.dot_general(wt_ref[...], h, (((1,), (1,)), ((), ())), preferred_element_type=F32)
    ut = ut.reshape(ut.shape[0], ut.shape[1] // SSM_BLOCK, SSM_BLOCK)
    o_ref[...] = ut.reshape(o_ref.shape).astype(BF16)


def _ut_call(x2, g_mix, w_ssm_t):
    t, d = x2.shape
    c = w_ssm_t.shape[0]
    ts = UT_TILE
    n_groups = c // SSM_GROUP
    return pl.pallas_call(
        _ut_kernel,
        grid=(t // ts,),
        in_specs=[pl.BlockSpec((ts, d), lambda i: (i, 0)),
                  pl.BlockSpec((1, d), lambda i: (0, 0)),
                  pl.BlockSpec((c, d), lambda i: (0, 0))],
        out_specs=pl.BlockSpec((n_groups, SSM_GROUP, ts // SSM_BLOCK, SSM_BLOCK),
                               lambda i: (0, 0, i, 0)),
        out_shape=jax.ShapeDtypeStruct((n_groups, SSM_GROUP, t // SSM_BLOCK, SSM_BLOCK), BF16),
        name="ut",
    )(x2, g_mix, w_ssm_t)


def _ssm_kernel(u_ref, kt_ref, w_ref, v_ref, cab_ref, a_ref, y_ref, acc_ref, z_ref, zs_ref, s_ref,
                slab_ref, *, n_batch):
    rows = u_ref.shape[2]
    n_blocks = rows // n_batch
    ri = lax.broadcasted_iota(jnp.int32, (SSM_BLOCK, SSM_BLOCK), 0)
    ci = lax.broadcasted_iota(jnp.int32, (SSM_BLOCK, SSM_BLOCK), 1)
    causal = ci >= ri

    n_pairs = SSM_GROUP // 2

    def build(cp, slot):
        for half in range(2):
            kt = kt_ref[0, 2 * cp + half]
            for c in range(SSM_GROUP):
                xb = jnp.broadcast_to(kt[c:c + 1, :], (SSM_BLOCK, SSM_BLOCK))
                toe = pltpu.roll(xb, 0, 1, stride=1, stride_axis=0)
                slab_ref[slot, half * SSM_BLOCK:(half + 1) * SSM_BLOCK, c * SSM_BLOCK:(c + 1) * SSM_BLOCK] = (
                    jnp.where(causal, toe, 0.0).astype(BF16))

    def apply(cp, slot, first):
        x2 = jnp.concatenate([u_ref[0, 2 * cp], u_ref[0, 2 * cp + 1]], axis=1)
        part = jnp.dot(x2, slab_ref[slot], preferred_element_type=F32)
        wrow = pl.multiple_of(cp * 2 * SSM_BLOCK, 2 * SSM_BLOCK)
        zpart = jnp.dot(x2, w_ref[0, pl.ds(wrow, 2 * SSM_BLOCK), :], preferred_element_type=F32)
        if first:
            acc_ref[...] = part
            z_ref[...] = zpart
        else:
            acc_ref[...] += part
            z_ref[...] += zpart

    build(0, 0)
    build(1, 1)
    apply(0, 0, True)

    def two(it, carry):
        cp = 2 * it + 1
        build(cp + 1, 0)
        apply(cp, 1, False)
        build(cp + 2, 1)
        apply(cp + 1, 0, False)
        return carry

    lax.fori_loop(0, (n_pairs - 2) // 2, two, 0)
    apply(n_pairs - 1, 1, False)

    a_full = a_ref[0, 0:1, :]
    a_swap = a_ref[0, 1:2, :]
    zs_ref[...] = pltpu.roll(z_ref[...], SSM_STATE, 1)
    st = jnp.zeros((n_batch, 2 * SSM_STATE), F32)
    sw = jnp.zeros((n_batch, 2 * SSM_STATE), F32)
    for blk in range(n_blocks):
        sl = pl.ds(blk, n_batch, stride=n_blocks)
        s_ref[sl, :] = st
        st, sw = (a_full * st + a_swap * sw + z_ref[sl, :],
                  a_full * sw - a_swap * st + zs_ref[sl, :])

    s2 = jnp.concatenate([s_ref[...], s_ref[...]], axis=1)
    y3 = jnp.stack([acc_ref[:, c * SSM_BLOCK:(c + 1) * SSM_BLOCK]
                    + jnp.dot((s2 * cab_ref[0, c:c + 1, :]).astype(BF16), v_ref[0],
                              preferred_element_type=F32)
                    for c in range(SSM_GROUP)], axis=0)
    y_ref[...] = y3.reshape(y_ref.shape)


def _ssm_call(u4, kt, wz, vy, cab, acoef, n_batch):
    g, c, rows, blk = u4.shape
    width = c * blk
    return pl.pallas_call(
        functools.partial(_ssm_kernel, n_batch=n_batch),
        grid=(g,),
        in_specs=[pl.BlockSpec((1, c, rows, blk), lambda i: (i, 0, 0, 0)),
                  pl.BlockSpec((1, c, c, blk), lambda i: (i, 0, 0, 0)),
                  pl.BlockSpec((1, width, 2 * SSM_STATE), lambda i: (i, 0, 0)),
                  pl.BlockSpec((1, 4 * SSM_STATE, blk), lambda i: (i, 0, 0)),
                  pl.BlockSpec((1, c, 4 * SSM_STATE), lambda i: (i, 0, 0)),
                  pl.BlockSpec((1, 2, 2 * SSM_STATE), lambda i: (i, 0, 0))],
        out_specs=pl.BlockSpec((c, rows * blk), lambda i: (i, 0)),
        out_shape=jax.ShapeDtypeStruct((g * c, rows * blk), F32),
        scratch_shapes=[pltpu.VMEM((rows, width), F32),
                        pltpu.VMEM((rows, 2 * SSM_STATE), F32),
                        pltpu.VMEM((rows, 2 * SSM_STATE), F32),
                        pltpu.VMEM((rows, 2 * SSM_STATE), F32),
                        pltpu.VMEM((2, 2 * blk, width), BF16)],
        name="ssm",
    )(u4, kt, wz, vy, cab, acoef)


def _ssm_tables(lam_re, lam_im, log_dt, b_re, b_im, c_re, c_im, d):
    hi = lax.Precision.HIGHEST
    g = lam_re.shape[0]
    dt = jnp.exp(log_dt)[:, None]
    er, ei = lam_re * dt, lam_im * dt
    cat = jnp.concatenate
    kk = jnp.arange(SSM_BLOCK + 1, dtype=F32)
    kdesc = jnp.arange(SSM_BLOCK - 1, -1, -1, dtype=F32)[None, :, None]
    ppk_m, ppk_a = jnp.exp(er[:, :, None] * kk), ei[:, :, None] * kk
    ppk_r, ppk_i = ppk_m * jnp.cos(ppk_a), ppk_m * jnp.sin(ppk_a)
    prev_m, prev_a = jnp.exp(kdesc * er[:, None, :]), kdesc * ei[:, None, :]
    prr, pir = prev_m * jnp.cos(prev_a), prev_m * jnp.sin(prev_a)
    nr, ni = ppk_r[:, :, 1] - 1.0, ppk_i[:, :, 1]
    den = lam_re * lam_re + lam_im * lam_im
    fr = (nr * lam_re + ni * lam_im) / den
    fi = (ni * lam_re - nr * lam_im) / den
    bbr = jnp.swapaxes(fr[:, :, None] * b_re - fi[:, :, None] * b_im, 1, 2)
    bbi = jnp.swapaxes(fr[:, :, None] * b_im + fi[:, :, None] * b_re, 1, 2)
    cb_r = c_re[:, None] * bbr[:, :, None, :] - c_im[:, None] * bbi[:, :, None, :]
    cb_i = c_re[:, None] * bbi[:, :, None, :] + c_im[:, None] * bbr[:, :, None, :]
    cb = cat([cb_r, -cb_i], axis=-1).reshape(g, SSM_GROUP * SSM_GROUP, 2 * SSM_STATE)
    pk = cat([ppk_r[:, :, :SSM_BLOCK], ppk_i[:, :, :SSM_BLOCK]], axis=1)
    kt = jnp.einsum("gmq,gqk->gmk", cb, pk, precision=hi).reshape(g, SSM_GROUP, SSM_GROUP, SSM_BLOCK)
    dmat = jnp.eye(SSM_GROUP, dtype=F32)[None] * d.reshape(g, 1, SSM_GROUP)
    kt = kt + dmat[..., None] * (jnp.arange(SSM_BLOCK) == 0).astype(F32)
    wz = (cat([prr, prr], -1)[:, None] * cat([bbr, bbi], -1)[:, :, None, :]
          + cat([pir, pir], -1)[:, None] * cat([-bbi, bbr], -1)[:, :, None, :])
    wz = wz.astype(BF16).reshape(g, SSM_GROUP * SSM_BLOCK, 2 * SSM_STATE)
    pt_r, pt_i = ppk_r[:, :, 1:], ppk_i[:, :, 1:]
    vy = cat([pt_r, pt_i, pt_i, pt_r], axis=1).astype(BF16)
    cab = cat([c_re, -c_re, -c_im, -c_im], -1)
    ar, ai = ppk_r[:, :, SSM_BLOCK], ppk_i[:, :, SSM_BLOCK]
    acoef = jnp.stack([cat([ar, ar], -1), cat([-ai, ai], -1)], axis=1)
    return kt, wz, vy, cab, acoef


def _mix_kernel(x_ref, yt_ref, k_ref, v_ref, g_ref, wc_ref, wq_ref, wg_ref, dw_ref, dwb_ref,
                lng_ref, lnb_ref, wpw_ref, wglu_ref, wo_ref, wout_ref, o_ref, vext_ref, vsh_ref):
    ts = x_ref.shape[1]
    d = x_ref.shape[2]
    cw = dw_ref.shape[1]
    x = x_ref[0]
    h = _rms(x, g_ref[...]).astype(BF16)

    ci = jnp.dot(h, wc_ref[...], preferred_element_type=F32)
    v = ci[:, :cw] * _sigmoid(ci[:, cw:])
    q = jnp.dot(h, wq_ref[...], preferred_element_type=F32)

    @pl.when(pl.program_id(1) == 0)
    def _():
        vext_ref[0:CONV_HALO, :] = jnp.zeros((CONV_HALO, cw), F32)

    vext_ref[CONV_HALO:CONV_HALO + ts, :] = v
    acc = jnp.broadcast_to(dwb_ref[...], (ts, cw))
    first = CONV_HALO - (CONV_KERNEL - 1)
    for phase in range(SUBLANES):
        offs = [first + k for k in range(CONV_KERNEL) if (first + k) % SUBLANES == phase]
        if not offs:
            continue
        span = offs[-1] - offs[0] + ts
        vsh_ref[0:span, :] = vext_ref[offs[0]:offs[0] + span, :]
        for off in offs:
            acc = acc + dw_ref[off - first:off - first + 1, :] * vsh_ref[off - offs[0]:off - offs[0] + ts, :]
    vext_ref[0:CONV_HALO, :] = vext_ref[ts:ts + CONV_HALO, :]
    mu = jnp.mean(acc, axis=-1, keepdims=True)
    xc = acc - mu
    var = jnp.mean(xc * xc, axis=-1, keepdims=True)
    ln = xc * lax.rsqrt(var + EPS) * lng_ref[...] + lnb_ref[...]
    sw = ln * _sigmoid(ln)
    y_conv = jnp.dot(sw.astype(BF16), wpw_ref[...], preferred_element_type=F32)
    merged = _sigmoid(jnp.dot(h, wg_ref[:, 0:d], preferred_element_type=F32)) * y_conv

    gy = jax.nn.gelu(yt_ref[...]).astype(BF16)
    z = lax.dot_general(gy, wglu_ref[...], (((0,), (0,)), ((), ())), preferred_element_type=F32)
    y_ssm = z[:, :d] * _sigmoid(z[:, d:])
    merged = merged + _sigmoid(jnp.dot(h, wg_ref[:, d:2 * d], preferred_element_type=F32)) * y_ssm

    kk = k_ref[0]
    vv = v_ref[0]
    outs = []
    for hd in range(HEADS):
        sl = slice(hd * HEAD_DIM, (hd + 1) * HEAD_DIM)
        s = lax.dot_general(q[:, sl].astype(BF16), kk[:, sl], (((1,), (1,)), ((), ())),
                            preferred_element_type=F32) * (HEAD_DIM ** -0.5)
        p = jnp.exp(s - jnp.max(s, axis=-1, keepdims=True))
        den = jnp.sum(p, axis=-1, keepdims=True)
        o = jnp.dot(p.astype(BF16), vv[:, sl], preferred_element_type=F32) / den
        outs.append(o.astype(BF16))
    y_mem = jnp.dot(jnp.concatenate(outs, axis=1), wo_ref[...], preferred_element_type=F32)
    merged = merged + _sigmoid(jnp.dot(h, wg_ref[:, 2 * d:3 * d], preferred_element_type=F32)) * y_mem

    o_ref[0] = x + jnp.dot(merged.astype(BF16), wout_ref[...], preferred_element_type=F32)


def _mix_call(x, yt, kmem, vmem, g_mix, wc, wq, wg, dw, dwb, lng, lnb, wpw, wglu, wo, wout):
    b, s, d = x.shape
    ts = MIX_TILE
    nst = s // ts
    cw = dw.shape[1]
    m = kmem.shape[1]
    consts = [g_mix, wc, wq, wg, dw, dwb, lng, lnb, wpw, wglu, wo, wout]
    return pl.pallas_call(
        _mix_kernel,
        grid=(b, nst),
        in_specs=[pl.BlockSpec((1, ts, d), lambda i, j: (i, j, 0)),
                  pl.BlockSpec((yt.shape[0], ts), lambda i, j: (0, i * nst + j)),
                  pl.BlockSpec((1, m, kmem.shape[2]), lambda i, j: (i, 0, 0)),
                  pl.BlockSpec((1, m, vmem.shape[2]), lambda i, j: (i, 0, 0))]
                 + [_const_spec(c.shape) for c in consts],
        out_specs=pl.BlockSpec((1, ts, d), lambda i, j: (i, j, 0)),
        out_shape=jax.ShapeDtypeStruct((b, s, d), F32),
        scratch_shapes=[pltpu.VMEM((ts + CONV_HALO, cw), F32),
                        pltpu.VMEM((ts + CONV_HALO, cw), F32)],
        compiler_params=pltpu.CompilerParams(
            dimension_semantics=("arbitrary", "arbitrary"), vmem_limit_bytes=VMEM_LIMIT),
        name="mix",
    )(x, yt, kmem, vmem, *consts)


def _route_kernel(x_ref, g_ref, wr_ref, br_ref, xs_ref, info_ref, nch_ref):
    ts = x_ref.shape[0] // ROUTE_TILES
    cpt = xs_ref.shape[0] // ROUTE_TILES
    for sub in range(ROUTE_TILES):
        rows = slice(sub * ts, (sub + 1) * ts)
        xs, info, nch = _route_tile(x_ref[rows, :], g_ref[...], wr_ref, br_ref[...], cpt * CHUNK_ROWS)
        xs_ref[sub * cpt:(sub + 1) * cpt] = xs.reshape(cpt, CHUNK_ROWS, xs.shape[-1])
        info_ref[rows, :] = info
        nch_ref[sub] = nch


def _route_tile(x, g, wr_ref, br, cap):
    ts = x.shape[0]
    h2 = _rms(x, g)
    hb = h2.astype(BF16)
    hl = (h2 - hb.astype(F32)).astype(BF16)
    hw = jnp.dot(hb, wr_ref[...], preferred_element_type=F32)
    logits = (hw[:, :LANES] + hw[:, LANES:]
              + jnp.dot(hl, wr_ref[:, :LANES], preferred_element_type=F32)) + br
    lt = logits.T
    le = lt[0:N_EXPERTS]
    lg = lt[N_EXPERTS:N_EXPERTS + SUBLANES]
    neg = jnp.float32(-1e30)
    big = jnp.float32(1e9)
    g_f = lax.broadcasted_iota(jnp.int32, (SUBLANES, ts), 0).astype(F32)
    e_f = lax.broadcasted_iota(jnp.int32, (N_EXPERTS, ts), 0).astype(F32)

    gmask = g_f < MOE_GROUPS
    gmax = jnp.max(jnp.where(gmask, lg, neg), axis=0, keepdims=True)
    gidx = jnp.min(jnp.where(gmask & (lg == gmax), g_f, big), axis=0, keepdims=True)
    gsum = jnp.sum(jnp.where(gmask, jnp.exp(jnp.minimum(lg - gmax, 0.0)), 0.0), axis=0, keepdims=True)
    p_top = 1.0 / gsum
    emask = jnp.floor(e_f * (1.0 / EXPERTS_PER_GROUP)) == gidx
    m1 = jnp.max(jnp.where(emask, le, neg), axis=0, keepdims=True)
    i1 = jnp.min(jnp.where(emask & (le == m1), e_f, big), axis=0, keepdims=True)
    emask2 = emask & (e_f != i1)
    m2 = jnp.max(jnp.where(emask2, le, neg), axis=0, keepdims=True)
    i2 = jnp.min(jnp.where(emask2 & (le == m2), e_f, big), axis=0, keepdims=True)
    r = jnp.exp(m2 - m1)
    w1 = p_top / (1.0 + r)
    w2 = p_top * r / (1.0 + r)

    sel1 = e_f == i1
    sel2 = e_f == i2
    occ = jnp.where(sel1 | sel2, 1.0, 0.0)
    tr = lax.broadcasted_iota(jnp.int32, (ts, ts), 0)
    tc = lax.broadcasted_iota(jnp.int32, (ts, ts), 1)
    earlier = jnp.where(tr < tc, 1.0, 0.0).astype(BF16)
    rank = jnp.dot(occ.astype(BF16), earlier, preferred_element_type=F32)
    cnt = jnp.sum(occ, axis=1, keepdims=True)
    nch = jnp.floor((cnt + (CHUNK_ROWS - 1)) * (1.0 / CHUNK_ROWS))
    er = lax.broadcasted_iota(jnp.int32, (N_EXPERTS, N_EXPERTS), 0)
    ec = lax.broadcasted_iota(jnp.int32, (N_EXPERTS, N_EXPERTS), 1)
    lower = jnp.where(ec < er, 1.0, 0.0).astype(BF16)
    nch_b = jnp.broadcast_to(nch, (N_EXPERTS, LANES))
    start = jnp.dot(lower, nch_b.astype(BF16), preferred_element_type=F32)[:, 0:1] * CHUNK_ROWS
    slot = start + rank
    pos1 = jnp.sum(jnp.where(sel1, slot, 0.0), axis=0, keepdims=True)
    pos2 = jnp.sum(jnp.where(sel2, slot, 0.0), axis=0, keepdims=True)

    rowid = lax.broadcasted_iota(jnp.int32, (cap, ts), 0)
    p = jnp.where((rowid == pos1.astype(jnp.int32)) | (rowid == pos2.astype(jnp.int32)), 1.0, 0.0)
    xs = jnp.dot(p.astype(BF16), hb, preferred_element_type=F32).astype(BF16)
    sub = lax.broadcasted_iota(jnp.int32, (LANES, ts), 0)
    info_t = jnp.where(sub == 0, pos1, jnp.where(sub == 1, pos2,
                       jnp.where(sub == 2, w1, jnp.where(sub == 3, w2, 0.0))))
    nch_rows = jnp.concatenate([nch_b, jnp.zeros((LANES - N_EXPERTS, LANES), F32)], axis=0).T[0:SUBLANES, :]
    return xs, info_t.T, nch_rows


def _route_call(x1, g_ffn, wr, br, cap):
    t, d = x1.shape
    nt = t // TOKEN_TILE
    ts = TOKEN_TILE * ROUTE_TILES
    return pl.pallas_call(
        _route_kernel,
        grid=(t // ts,),
        in_specs=[pl.BlockSpec((ts, d), lambda i: (i, 0)),
                  pl.BlockSpec((1, d), lambda i: (0, 0)),
                  pl.BlockSpec(wr.shape, lambda i: (0, 0)),
                  pl.BlockSpec((1, LANES), lambda i: (0, 0))],
        out_specs=[pl.BlockSpec((ROUTE_TILES * cap // CHUNK_ROWS, CHUNK_ROWS, d), lambda i: (i, 0, 0)),
                   pl.BlockSpec((ts, LANES), lambda i: (i, 0)),
                   pl.BlockSpec((ROUTE_TILES, SUBLANES, LANES), lambda i: (i, 0, 0))],
        out_shape=[jax.ShapeDtypeStruct((nt * cap // CHUNK_ROWS, CHUNK_ROWS, d), BF16),
                   jax.ShapeDtypeStruct((t, LANES), F32),
                   jax.ShapeDtypeStruct((nt, SUBLANES, LANES), F32)],
        compiler_params=pltpu.CompilerParams(vmem_limit_bytes=VMEM_LIMIT),
        name="route",
    )(x1, g_ffn, wr, br)


def _expert_kernel(te_ref, nu_ref, src_ref, xs_ref, wg_ref, wu_ref, wd_ref, o_ref, xbuf_ref, sem_ref):
    i = pl.program_id(0)
    n_used = nu_ref[0]
    slot = i % 2

    def chunk_copy(tile, c, buf):
        return pltpu.make_async_copy(
            xs_ref.at[src_ref[tile * CHUNKS_PER_ETILE + c]], xbuf_ref.at[buf, c], sem_ref.at[buf])

    def gather(tile, buf):
        for c in range(CHUNKS_PER_ETILE):
            chunk_copy(tile, c, buf).start(priority=c % 2)

    def drain(buf):
        pltpu.make_async_copy(xs_ref.at[pl.ds(0, CHUNKS_PER_ETILE)], xbuf_ref.at[buf], sem_ref.at[buf]).wait()

    @pl.when(i == 0)
    def _():
        gather(0, 0)

    @pl.when(i < n_used)
    def _():
        nxt = jnp.minimum(i + 1, n_used - 1)
        gather(nxt, 1 - slot)
        drain(slot)
        x = xbuf_ref[slot].reshape(EXPERT_TILE, xbuf_ref.shape[-1])
        gate = jnp.dot(x, wg_ref[0].astype(BF16), preferred_element_type=F32)
        up = jnp.dot(x, wu_ref[0].astype(BF16), preferred_element_type=F32)
        act = (gate * _sigmoid(gate) * up).astype(BF16)
        y = jnp.dot(act, wd_ref[0].astype(BF16), preferred_element_type=F32).astype(BF16)
        o_ref[...] = y.reshape(o_ref.shape)

        @pl.when(i == n_used - 1)
        def _():
            drain(1 - slot)

    @pl.when(i >= n_used)
    def _():
        o_ref[...] = jnp.zeros_like(o_ref)


def _expert_call(tile_expert, n_used, chunk_src, xs, wg, wu, wd):
    d = xs.shape[-1]
    de = wg.shape[2]
    tm = EXPERT_TILE
    nt = tile_expert.shape[0]

    def w_map(i, te, nu, src):
        return (te[jnp.maximum(jnp.minimum(i, nu[0] - 1), 0)], 0, 0)

    return pl.pallas_call(
        _expert_kernel,
        grid_spec=pltpu.PrefetchScalarGridSpec(
            num_scalar_prefetch=3,
            grid=(nt,),
            in_specs=[pl.BlockSpec(memory_space=pl.ANY),
                      pl.BlockSpec((1, d, de), w_map),
                      pl.BlockSpec((1, d, de), w_map),
                      pl.BlockSpec((1, de, d), w_map)],
            out_specs=pl.BlockSpec((CHUNKS_PER_ETILE, CHUNK_ROWS, d), lambda i, te, nu, src: (i, 0, 0)),
            scratch_shapes=[pltpu.VMEM((2, CHUNKS_PER_ETILE, CHUNK_ROWS, d), BF16),
                            pltpu.SemaphoreType.DMA((2,))]),
        out_shape=jax.ShapeDtypeStruct((nt * CHUNKS_PER_ETILE, CHUNK_ROWS, d), BF16),
        compiler_params=pltpu.CompilerParams(dimension_semantics=("arbitrary",)),
        name="expert",
    )(tile_expert, n_used, chunk_src, xs, wg, wu, wd)


def _combine_kernel(dst_ref, x_ref, info_ref, g_ref, ys_ref, o_ref, ybuf_ref, sem_ref):
    i = pl.program_id(0)
    n_tiles = pl.num_programs(0)
    ts = x_ref.shape[0] // COMBINE_TILES
    n_chunks = ybuf_ref.shape[1]
    cap = n_chunks // COMBINE_TILES * CHUNK_ROWS
    slot = i % 2

    def chunk_copy(tile, k, buf):
        return pltpu.make_async_copy(ys_ref.at[dst_ref[tile * n_chunks + k]], ybuf_ref.at[buf, k], sem_ref.at[buf])

    def gather(tile, buf):
        for k in range(n_chunks):
            chunk_copy(tile, k, buf).start(priority=k % 2)

    def drain(buf):
        pltpu.make_async_copy(ys_ref.at[pl.ds(0, n_chunks)], ybuf_ref.at[buf], sem_ref.at[buf]).wait()

    @pl.when(i == 0)
    def _():
        gather(0, 0)

    nxt = jnp.minimum(i + 1, n_tiles - 1)
    gather(nxt, 1 - slot)
    drain(slot)

    rowid = lax.broadcasted_iota(jnp.int32, (ts, cap), 1)
    for sub in range(COMBINE_TILES):
        rows = slice(sub * ts, (sub + 1) * ts)
        info = info_ref[rows, :]
        pos1 = info[:, 0:1].astype(jnp.int32)
        pos2 = info[:, 1:2].astype(jnp.int32)
        w1 = info[:, 2:3]
        w2 = info[:, 3:4]
        ys = ybuf_ref[slot, sub * (n_chunks // COMBINE_TILES):(sub + 1) * (n_chunks // COMBINE_TILES)]
        ys = ys.reshape(cap, ybuf_ref.shape[-1])
        pw = jnp.where(rowid == pos1, w1, jnp.where(rowid == pos2, w2, 0.0)).astype(BF16)
        y = jnp.dot(pw, ys, preferred_element_type=F32)
        o_ref[rows, :] = _rms(x_ref[rows, :] + y, g_ref[...])

    @pl.when(i == n_tiles - 1)
    def _():
        drain(1 - slot)


def _combine_call(chunk_dst, x1, info, g_final, ys_sorted, cap):
    t, d = x1.shape
    ts = TOKEN_TILE * COMBINE_TILES
    return pl.pallas_call(
        _combine_kernel,
        grid_spec=pltpu.PrefetchScalarGridSpec(
            num_scalar_prefetch=1,
            grid=(t // ts,),
            in_specs=[pl.BlockSpec((ts, d), lambda i, dst: (i, 0)),
                      pl.BlockSpec((ts, LANES), lambda i, dst: (i, 0)),
                      pl.BlockSpec((1, d), lambda i, dst: (0, 0)),
                      pl.BlockSpec(memory_space=pl.ANY)],
            out_specs=pl.BlockSpec((ts, d), lambda i, dst: (i, 0)),
            scratch_shapes=[pltpu.VMEM((2, COMBINE_TILES * cap // CHUNK_ROWS, CHUNK_ROWS, d), BF16),
                            pltpu.SemaphoreType.DMA((2,))]),
        out_shape=jax.ShapeDtypeStruct((t, d), F32),
        compiler_params=pltpu.CompilerParams(dimension_semantics=("arbitrary",),
                                             vmem_limit_bytes=VMEM_LIMIT),
        name="combine",
    )(chunk_dst, x1, info, g_final, ys_sorted)


def _run_tables(nch, cpt, n_etiles):
    n_tiles, n_exp = nch.shape
    per = CHUNKS_PER_ETILE
    tcum = jnp.cumsum(nch, axis=1)
    toff = tcum - nch
    ecum = jnp.cumsum(nch, axis=0)
    etot = ecum[-1]
    eseg = -(-etot // per) * per
    segcum = jnp.cumsum(eseg)
    ebase = segcum - eseg
    eoff = ebase[None, :] + ecum - nch
    n_used = (segcum[-1] // per).reshape(1)

    k = jnp.arange(cpt, dtype=jnp.int32)[None, :, None]
    in_run = (k >= toff[:, None, :]) & (k < tcum[:, None, :])
    chunk_dst = jnp.sum(jnp.where(in_run, (eoff - toff)[:, None, :] + k, 0), axis=-1)

    first = jnp.arange(n_etiles, dtype=jnp.int32)[:, None] * per
    owner = (first >= ebase[None, :]) & (first < segcum[None, :])
    tile_expert = jnp.sum(jnp.where(owner, jnp.arange(n_exp, dtype=jnp.int32)[None, :], 0), axis=-1)

    pick = lambda a: jnp.sum(jnp.where(owner[:, None, :], a[None, :, :], 0), axis=-1)
    run_lo, run_n = pick(eoff), pick(nch)
    shift = pick(jnp.arange(n_tiles, dtype=jnp.int32)[:, None] * cpt + toff - eoff)
    c = (first + jnp.arange(per, dtype=jnp.int32)[None, :])[:, :, None]
    hit = (c >= run_lo[:, None, :]) & (c < (run_lo + run_n)[:, None, :])
    chunk_src = jnp.sum(jnp.where(hit, shift[:, None, :] + c, 0), axis=-1)
    i32 = lambda a: a.reshape(-1).astype(jnp.int32)
    return i32(chunk_dst), i32(tile_expert), i32(n_used), i32(chunk_src)


def _layer(x, mem, g_mix, w_in, conv_dw, conv_dw_bias, conv_ln_g, conv_ln_b, w_conv_out,
           lam_re, lam_im, log_dt, b_re, b_im, c_re, c_im, ssm_d, w_ssm_glu, g_mem, w_mem_kv,
           w_mem_out, w_out, g_ffn, w_rg, b_rg, w_re, b_re_, w_eg, w_eu, w_ed, g_final):
    b, s, d = x.shape
    t = b * s
    cw = conv_dw.shape[1]
    sw = ssm_d.shape[0]
    qw = w_mem_out.shape[0]
    row = lambda a: a.reshape(1, -1)

    o0, o1, o2 = 2 * cw, 2 * cw + sw, 2 * cw + sw + qw
    wc = w_in[:, :o0].astype(BF16)
    w_ssm_t = w_in[:, o0:o1].T.astype(BF16)
    wq = w_in[:, o1:o2].astype(BF16)
    wg = w_in[:, o2:].astype(BF16)

    kmem, vmem = _kv_call(mem, row(g_mem), w_mem_kv.astype(BF16))

    ut = _ut_call(x.reshape(t, d), row(g_mix), w_ssm_t)
    kt, wz, vy, cab, acoef = _ssm_tables(lam_re, lam_im, log_dt, b_re, b_im, c_re, c_im, ssm_d)
    yt = _ssm_call(ut, kt, wz, vy, cab, acoef, b)

    dw = jnp.concatenate([conv_dw, jnp.zeros((CONV_HALO - CONV_KERNEL, cw), F32)], axis=0)
    x1 = _mix_call(x, yt, kmem, vmem, row(g_mix), wc, wq, wg, dw, row(conv_dw_bias),
                   row(conv_ln_g), row(conv_ln_b), w_conv_out.astype(BF16),
                   w_ssm_glu.astype(BF16), w_mem_out.astype(BF16), w_out.astype(BF16))
    x1 = x1.reshape(t, d)

    pad = LANES - N_EXPERTS - MOE_GROUPS
    wr = jnp.concatenate([w_re, w_rg, jnp.zeros((d, pad), F32)], axis=1)
    wr_hi = wr.astype(BF16)
    wr = jnp.concatenate([wr_hi, (wr - wr_hi.astype(F32)).astype(BF16)], axis=1)
    br = jnp.concatenate([b_re_, b_rg, jnp.zeros((pad,), F32)]).reshape(1, LANES)
    n_tiles = t // TOKEN_TILE
    max_tile_chunks = (2 * TOKEN_TILE + N_EXPERTS * (CHUNK_ROWS - 1)) // CHUNK_ROWS
    cpt = -(-max_tile_chunks // SUBLANES) * SUBLANES
    cap = cpt * CHUNK_ROWS
    xs_tiles, info, nch_f = _route_call(x1, row(g_ffn), wr, br, cap)

    max_chunks = n_tiles * max_tile_chunks + N_EXPERTS * (CHUNKS_PER_ETILE - 1)
    n_etiles = -(-max_chunks // CHUNKS_PER_ETILE)
    nch = nch_f[:, 0, :N_EXPERTS].astype(jnp.int32)
    chunk_dst, tile_expert, n_used, chunk_src = _run_tables(nch, cpt, n_etiles)

    ys_sorted = _expert_call(tile_expert, n_used, chunk_src, xs_tiles, w_eg, w_eu, w_ed)
    out = _combine_call(chunk_dst, x1, info, row(g_final), ys_sorted, cap)
    return out.reshape(b, s, d)


def kernel(x, mem, g_mix, w_in, conv_dw, conv_dw_bias, conv_ln_g, conv_ln_b, w_conv_out, ssm_lambda_re, ssm_lambda_im, ssm_log_dt, ssm_b_re, ssm_b_im, ssm_c_re, ssm_c_im, ssm_d, w_ssm_glu, g_mem, w_mem_kv, w_mem_out, w_out, g_ffn, w_router_group, b_router_group, w_router_expert, b_router_expert, w_exp_gate, w_exp_up, w_exp_down, g_final):
    assert g_mix.shape[0] == 1, "the problem fixes one layer"
    l = 0
    return _layer(
        x, mem, g_mix[l], w_in[l], conv_dw[l], conv_dw_bias[l], conv_ln_g[l], conv_ln_b[l],
        w_conv_out[l], ssm_lambda_re[l], ssm_lambda_im[l], ssm_log_dt[l], ssm_b_re[l],
        ssm_b_im[l], ssm_c_re[l], ssm_c_im[l], ssm_d[l], w_ssm_glu[l], g_mem[l], w_mem_kv[l],
        w_mem_out[l], w_out[l], g_ffn[l], w_router_group[l], b_router_group[l],
        w_router_expert[l], b_router_expert[l], w_exp_gate[l], w_exp_up[l], w_exp_down[l], g_final)
```
